```python
import jax, jax.numpy as jnp
from jax import lax
import numpy as np

D_MODEL = 1024
BATCH = 32
SEQ = 256
DEPTH = 2
DEC_BATCH = 8
DEC_SEQ = 1024
PAST_LEN = 256

GRID_W = 64
Q_BLOCK = 128
ROPE_THETA = 10000.0
EPS = 1e-6
F_FLOOR = 1e-30

H_A = 8
KV_A = 2
HD_A = 64
H_B = 8
Q_RANK = 384
KV_RANK = 256
NOPE_B = 64
ROPE_B = 32
V_B = 64
H_C = 8
DK_C = 64
DV_C = 64
CHUNK_C = 16
N_BRANCH = 3
BRANCH_W = 512
N_GROUPS = 4
E_PER_GROUP = 4
N_EXPERTS = N_GROUPS * E_PER_GROUP
TOP_K = 2
D_EXPERT = 512

IN_SPLITS = (H_A * HD_A, KV_A * HD_A, KV_A * HD_A,
             Q_RANK, KV_RANK, ROPE_B,
             H_C * DK_C, H_C * DK_C, H_C * DK_C,
             H_C * DV_C, H_C * DV_C,
             N_BRANCH * D_MODEL)
IN_COLS = sum(IN_SPLITS)

kernel_name = "hybrid_gqa_mla_hgrn2_hmoe_diffusion_step"


def _split_points():
    return np.cumsum(IN_SPLITS)[:-1].tolist()


def rms_norm(x, g):
    xf = x.astype(jnp.float32)
    y = xf * lax.rsqrt(jnp.mean(xf * xf, axis=-1, keepdims=True) + EPS)
    return (y * g.astype(jnp.float32)).astype(x.dtype)


def grid_positions(n):
    rows = n // GRID_W
    row = jnp.repeat(jnp.arange(rows, dtype=jnp.int32), GRID_W)
    col = jnp.tile(jnp.arange(GRID_W, dtype=jnp.int32), rows)
    return row, col


def axial_rope(x, row, col):
    half = x.shape[-1] // 2

    def rotate(xs, pos):
        inv = ROPE_THETA ** (-jnp.arange(0, half, 2, dtype=jnp.float32) / half)
        ang = pos.astype(jnp.float32)[:, None, None] * inv
        cos, sin = jnp.cos(ang), jnp.sin(ang)
        x1, x2 = xs[..., : half // 2], xs[..., half // 2:]
        return jnp.concatenate([x1 * cos - x2 * sin, x1 * sin + x2 * cos], axis=-1)

    xf = x.astype(jnp.float32)
    return jnp.concatenate([rotate(xf[..., :half], row), rotate(xf[..., half:], col)], axis=-1).astype(x.dtype)


def blocked_attention(q, k, v, scale):
    b, nq, g, r, d = q.shape
    nb = nq // Q_BLOCK
    qb = q.reshape(b, nb, Q_BLOCK, g, r, d).transpose(1, 0, 2, 3, 4, 5)

    def one_block(qblk):
        s = jnp.einsum('bqgrd,bkgd->bgrqk', qblk, k, preferred_element_type=jnp.float32) * scale
        p = jax.nn.softmax(s, axis=-1)
        return jnp.einsum('bgrqk,bkge->bqgre', p.astype(v.dtype), v)

    o = lax.map(one_block, qb)
    return o.transpose(1, 0, 2, 3, 4, 5).reshape(b, nq, g, r, v.shape[-1])


def mla_keys(ckv, k_rope, w_kv):
    b, n, _ = ckv.shape
    kv = (ckv @ w_kv).reshape(b, n, H_B, NOPE_B + V_B)
    k = jnp.concatenate([kv[..., :NOPE_B], jnp.broadcast_to(k_rope, (b, n, H_B, ROPE_B))], axis=-1)
    return k, kv[..., NOPE_B:]


def log_forget(pre, lb):
    f = lb + (1.0 - lb) * jax.nn.sigmoid(pre.astype(jnp.float32))
    return jnp.log(jnp.maximum(f, F_FLOOR))


def hgrn2_chunk_scan(q, k, v, log_f, s0):
    b, h, n, _ = q.shape
    nc = n // CHUNK_C
    ch = lambda t: t.reshape(b, h, nc, CHUNK_C, t.shape[-1]).astype(jnp.float32)
    q, k, v, log_f = ch(q), ch(k), ch(v), ch(log_f)
    cum = jnp.cumsum(log_f, axis=3)
    last = cum[:, :, :, -1:, :]
    causal = jnp.tril(jnp.ones((CHUNK_C, CHUNK_C), dtype=bool))[:, :, None]
    diff = cum[:, :, :, :, None, :] - cum[:, :, :, None, :, :]
    decay = jnp.where(causal, jnp.exp(jnp.where(causal, diff, 0.0)), 0.0)
    scores = jnp.einsum('bhctk,bhcsk,bhctsk->bhcts', q, k, decay)
    o_intra = jnp.einsum('bhcts,bhcsv->bhctv', scores, v)
    u = jnp.einsum('bhcsk,bhcsv->bhckv', k * jnp.exp(last - cum), v)
    a = jnp.exp(last[:, :, :, 0, :])

    def step(s, inp):
        a_c, u_c = inp
        return a_c[..., None] * s + u_c, s

    s_final, s_prev = lax.scan(step, s0.astype(jnp.float32),
                               (a.transpose(2, 0, 1, 3), u.transpose(2, 0, 1, 3, 4)))
    s_prev = s_prev.transpose(1, 2, 0, 3, 4)
    o_inter = jnp.einsum('bhctk,bhckv->bhctv', q * jnp.exp(cum), s_prev)
    return (o_intra + o_inter).reshape(b, h, n, v.shape[-1]), s_final


def token_mixer(h, ctx, lb, w_in, a_qn, a_kn, b_qn, b_wq, b_kvn, b_wkv, c_on, w_branch, w_out):
    b, n, _ = h.shape
    latent = ctx is not None
    (qa, ka, va, qra, kva, kra, fpre_f, fpre_b, qc, ic, gc, gpre) = jnp.split(h @ w_in, _split_points(), axis=-1)
    if latent:
        row, col = grid_positions(n)

    q_a = rms_norm(qa.reshape(b, n, H_A, HD_A), a_qn)
    k_a = rms_norm(ka.reshape(b, n, KV_A, HD_A), a_kn)
    v_a = va.reshape(b, n, KV_A, HD_A)
    if latent:
        q_a = axial_rope(q_a, row, col)
        keys_a = jnp.concatenate([axial_rope(k_a, row, col), ctx[0]], axis=1)
        vals_a = jnp.concatenate([v_a, ctx[1]], axis=1)
    else:
        keys_a, vals_a = k_a, v_a
    o_a = blocked_attention(q_a.reshape(b, n, KV_A, H_A // KV_A, HD_A), keys_a, vals_a,
                            HD_A ** -0.5).reshape(b, n, BRANCH_W)

    qb = (rms_norm(qra, b_qn) @ b_wq).reshape(b, n, H_B, NOPE_B + ROPE_B)
    q_nope, q_rope = qb[..., :NOPE_B], qb[..., NOPE_B:]
    ckv = rms_norm(kva, b_kvn)
    k_rope = kra[:, :, None, :]
    if latent:
        q_rope = axial_rope(q_rope, row, col)
        k_rope = axial_rope(k_rope, row, col)
    keys_b, vals_b = mla_keys(ckv, k_rope, b_wkv)
    if latent:
        ck_b, cv_b = mla_keys(ctx[2], ctx[3][:, :, None, :], b_wkv)
        keys_b = jnp.concatenate([keys_b, ck_b], axis=1)
        vals_b = jnp.concatenate([vals_b, cv_b], axis=1)
    q_b = jnp.concatenate([q_nope, q_rope], axis=-1)[:, :, :, None, :]
    o_b = blocked_attention(q_b, keys_b, vals_b, (NOPE_B + ROPE_B) ** -0.5).reshape(b, n, BRANCH_W)

    heads = lambda t: t.reshape(b, n, H_C, -1).transpose(0, 2, 1, 3)
    flip = lambda t: jnp.flip(t, axis=2)
    logf_f = heads(log_forget(fpre_f, lb[0]))
    logf_b = heads(log_forget(fpre_b, lb[1]))
    q_c, v_c = heads(qc), heads(ic)
    if latent:
        s0_f, s0_b = ctx[4], ctx[5]
    else:
        s0_f = s0_b = jnp.zeros((b, H_C, DK_C, DV_C), jnp.float32)
    o_f, s_f = hgrn2_chunk_scan(q_c, -jnp.expm1(logf_f), v_c, logf_f, s0_f)
    o_r, s_b = hgrn2_chunk_scan(flip(q_c), flip(-jnp.expm1(logf_b)), flip(v_c), flip(logf_b), s0_b)
    o_c = (o_f + flip(o_r)).transpose(0, 2, 1, 3).astype(h.dtype)
    o_c = (rms_norm(o_c, c_on) * jax.nn.silu(gc.reshape(b, n, H_C, DV_C))).reshape(b, n, BRANCH_W)

    gates = jax.nn.sigmoid(gpre.reshape(b, n, N_BRANCH, D_MODEL))
    proj = jnp.einsum('bnjw,jwd->bnjd', jnp.stack([o_a, o_b, o_c], axis=2), w_branch)
    out = jnp.sum(gates * proj, axis=2) @ w_out
    return out, (k_a, v_a, ckv, kra, s_f, s_b)


def hier_moe(h, wg, bg, we, be, e_gate, e_up, e_down):
    b, n, d = h.shape
    t = h.reshape(b * n, d)
    glog = (t @ wg + bg).astype(jnp.float32)
    _, gidx = lax.top_k(glog, 1)
    gsel = jax.nn.one_hot(gidx[:, 0], N_GROUPS, dtype=jnp.float32)
    gp = jnp.sum(jax.nn.softmax(glog, axis=-1) * gsel, axis=-1, keepdims=True)
    elog = (t @ we + be).astype(jnp.float32).reshape(-1, N_GROUPS, E_PER_GROUP)
    elog_g = jnp.einsum('tge,tg->te', elog, gsel)
    ev, ei = lax.top_k(elog_g, TOP_K)
    ew = jax.nn.softmax(ev, axis=-1) * gp
    eid = gidx * E_PER_GROUP + ei
    combine = jnp.einsum('tk,tke->te', ew, jax.nn.one_hot(eid, N_EXPERTS, dtype=jnp.float32))
    hg = jnp.einsum('td,edf->tef', t, e_gate)
    hu = jnp.einsum('td,edf->tef', t, e_up)
    act = jax.nn.silu(hg) * hu * combine.astype(t.dtype)[..., None]
    return jnp.einsum('tef,efd->td', act, e_down).reshape(b, n, d)


def trunk_layer(x, cvec, ctx, lb, lp):
    (w_mod, b_mod, n1, n2, w_in, a_qn, a_kn, b_qn, b_wq, b_kvn, b_wkv, c_on, w_branch, w_out,
     wg, bg, we, be, e_gate, e_up, e_down) = lp
    mod = jax.nn.silu(cvec) @ w_mod + b_mod
    sh1, sc1, g1, sh2, sc2, g2 = jnp.split(mod[..., None, :], 6, axis=-1)
    h = rms_norm(x, n1) * (1 + sc1) + sh1
    mix, ctx_out = token_mixer(h, ctx, lb, w_in, a_qn, a_kn, b_qn, b_wq, b_kvn, b_wkv, c_on, w_branch, w_out)
    x = x + g1 * mix
    h = rms_norm(x, n2) * (1 + sc2) + sh2
    x = x + g2 * hier_moe(h, wg, bg, we, be, e_gate, e_up, e_down)
    return x, ctx_out


def setup_inputs(seed: int = 0) -> dict:
    key = jax.random.key(seed)
    ks = iter(jax.random.split(key, 40))
    nrm = lambda shape, s=1.0: s * jax.random.normal(next(ks), shape, jnp.float32)
    gain = lambda shape: 1.0 + 0.01 * jax.random.normal(next(ks), shape, jnp.float32)
    d = D_MODEL
    return {
        "x_prompt": nrm((BATCH, SEQ, d)),
        "x_sample": nrm((DEC_BATCH, DEC_SEQ, d)),
        "cache_gqa_k": nrm((DEC_BATCH, DEPTH, PAST_LEN, KV_A, HD_A)),
        "cache_gqa_v": nrm((DEC_BATCH, DEPTH, PAST_LEN, KV_A, HD_A)),
        "cache_mla_ckv": nrm((DEC_BATCH, DEPTH, PAST_LEN, KV_RANK)),
        "cache_mla_krope": nrm((DEC_BATCH, DEPTH, PAST_LEN, ROPE_B)),
        "state_hgrn": nrm((DEC_BATCH, DEPTH, 2, H_C, DK_C, DV_C), 0.5),
        "c": nrm((DEC_BATCH, d)),
        "c_ctx": nrm((d,)),
        "w_mod": nrm((DEPTH, d, 6 * d), 0.5 * d ** -0.5),
        "b_mod": nrm((DEPTH, 6 * d), 0.01),
        "norm1_g": gain((DEPTH, d)),
        "norm2_g": gain((DEPTH, d)),
        "w_in": nrm((DEPTH, d, IN_COLS), d ** -0.5),
        "a_qnorm": gain((DEPTH, HD_A)),
        "a_knorm": gain((DEPTH, HD_A)),
        "b_qnorm": gain((DEPTH, Q_RANK)),
        "b_wq": nrm((DEPTH, Q_RANK, H_B * (NOPE_B + ROPE_B)), Q_RANK ** -0.5),
        "b_kvnorm": gain((DEPTH, KV_RANK)),
        "b_wkv": nrm((DEPTH, KV_RANK, H_B * (NOPE_B + V_B)), KV_RANK ** -0.5),
        "c_lb_logits": nrm((DEPTH, 2, H_C * DK_C)),
        "c_onorm": gain((DEPTH, DV_C)),
        "w_branch": nrm((DEPTH, N_BRANCH, BRANCH_W, d), BRANCH_W ** -0.5),
        "w_out": nrm((DEPTH, d, d), d ** -0.5),
        "r_group_w": nrm((DEPTH, d, N_GROUPS), d ** -0.5),
        "r_group_b": nrm((DEPTH, N_GROUPS), 0.01),
        "r_expert_w": nrm((DEPTH, d, N_EXPERTS), d ** -0.5),
        "r_expert_b": nrm((DEPTH, N_EXPERTS), 0.01),
        "e_gate": nrm((DEPTH, N_EXPERTS, d, D_EXPERT), d ** -0.5),
        "e_up": nrm((DEPTH, N_EXPERTS, d, D_EXPERT), d ** -0.5),
        "e_down": nrm((DEPTH, N_EXPERTS, D_EXPERT, d), D_EXPERT ** -0.5),
        "final_g": gain((d,)),
    }


def reference(x_prompt, x_sample, cache_gqa_k, cache_gqa_v, cache_mla_ckv, cache_mla_krope, state_hgrn,
              c, c_ctx, w_mod, b_mod, norm1_g, norm2_g, w_in, a_qnorm, a_knorm, b_qnorm, b_wq, b_kvnorm,
              b_wkv, c_lb_logits, c_onorm, w_branch, w_out, r_group_w, r_group_b, r_expert_w, r_expert_b,
              e_gate, e_up, e_down, final_g):
    p = jax.nn.softmax(c_lb_logits.astype(jnp.float32), axis=0)
    lbs = jnp.cumsum(p, axis=0) - p[0:1]
    xp, xs = x_prompt, x_sample
    new_k, new_v, new_ckv, new_kr, new_s = [], [], [], [], []
    for l in range(DEPTH):
        lp = (w_mod[l], b_mod[l], norm1_g[l], norm2_g[l], w_in[l], a_qnorm[l], a_knorm[l], b_qnorm[l],
              b_wq[l], b_kvnorm[l], b_wkv[l], c_onorm[l], w_branch[l], w_out[l], r_group_w[l], r_group_b[l],
              r_expert_w[l], r_expert_b[l], e_gate[l], e_up[l], e_down[l])
        xp, (k_a, v_a, ckv, kr, s_f, s_b) = trunk_layer(xp, c_ctx, None, lbs[l], lp)
        new_k.append(k_a)
        new_v.append(v_a)
        new_ckv.append(ckv)
        new_kr.append(kr)
        new_s.append(jnp.stack([s_f, s_b], axis=1))
        ctx_l = (cache_gqa_k[:, l], cache_gqa_v[:, l], cache_mla_ckv[:, l], cache_mla_krope[:, l],
                 state_hgrn[:, l, 0], state_hgrn[:, l, 1])
        xs, _ = trunk_layer(xs, c, ctx_l, lbs[l], lp)
    y_prompt = rms_norm(xp, final_g)
    y_sample = rms_norm(xs, final_g)
    return (y_prompt, y_sample, jnp.stack(new_k, axis=1), jnp.stack(new_v, axis=1),
            jnp.stack(new_ckv, axis=1), jnp.stack(new_kr, axis=1), jnp.stack(new_s, axis=1))
```

```python
import functools

import numpy as np
import jax
import jax.numpy as jnp
from jax import lax
from jax.experimental import pallas as pl
from jax.experimental.pallas import tpu as pltpu

D = 1024
BATCH, SEQ = 32, 256
DEC_BATCH, DEC_SEQ = 8, 1024
PAST = 256
DEPTH = 2
GRID_W = 64
THETA = 10000.0
EPS = 1e-6
F_FLOOR = 1e-30
H_A, KV_A, HD_A = 8, 2, 64
H_B, Q_RANK, KV_RANK, NOPE_B, ROPE_B, V_B = 8, 384, 256, 64, 32, 64
H_C, DK_C, DV_C = 8, 64, 64
BRANCH_W = 512
N_GROUPS, E_PER_GROUP, N_EXPERTS, D_EXPERT = 4, 4, 16, 512

T_CTX = BATCH * SEQ
T_LAT = DEC_BATCH * DEC_SEQ
T_ALL = T_CTX + T_LAT
TM = 256
N_TILES = T_ALL // TM
CTX_TILES = T_CTX // TM
LAT_TILES_PER_SEQ = DEC_SEQ // TM
MOD_ROWS = 16
CTX_MOD_ROW = DEC_BATCH
LANE = 128
VMEM_LIMIT = 56 * 1024 * 1024

C_QA, C_KA, C_VA, C_QRA, C_KVA, C_KR = 0, 512, 640, 768, 1152, 1408
C_FF, C_FB, C_QC, C_IC, C_GC, C_END = 1536, 2048, 2560, 3072, 3584, 4096
R_QA, R_KA, R_VA, R_QRA, R_KVA, R_KR = 0, 512, 640, 768, 1152, 1408
R_FF, R_FB, R_QC, R_IC, R_GC, R_GATE, R_END = 1440, 1952, 2464, 2976, 3488, 4000, 7072

F32 = jnp.float32
BF16 = jnp.bfloat16


def _cparams(sem):
    return pltpu.CompilerParams(dimension_semantics=sem, vmem_limit_bytes=VMEM_LIMIT)


def _mod_row(i):
    return jnp.where(i < CTX_TILES, CTX_MOD_ROW, (i - CTX_TILES) // LAT_TILES_PER_SEQ)


def _pos_block(i):
    return jnp.where(i < CTX_TILES, LAT_TILES_PER_SEQ, (i - CTX_TILES) % LAT_TILES_PER_SEQ)


def _split_hi_lo(x):
    hi = x.astype(BF16)
    lo = (x - hi.astype(F32)).astype(BF16)
    return hi, lo


def _group_mean(x2, ones_blk, width):
    n = ones_blk.shape[0]
    outs = []
    for j in range(x2.shape[-1] // n):
        blk = x2[:, j * n:(j + 1) * n]
        hi, lo = _split_hi_lo(blk)
        s = jnp.dot(hi, ones_blk, preferred_element_type=F32) + jnp.dot(lo, ones_blk, preferred_element_type=F32)
        outs.append(s)
    s = outs[0] if len(outs) == 1 else jnp.concatenate(outs, axis=-1)
    return s * (1.0 / width)


def _rope(x, tab_ref, shift, period):
    c, s1, s2 = tab_ref[0], tab_ref[1], tab_ref[2]
    outs = []
    for j in range(x.shape[-1] // period):
        blk = x[:, j * period:(j + 1) * period]
        outs.append(blk * c + pltpu.roll(blk, shift, 1) * s1 + pltpu.roll(blk, period - shift, 1) * s2)
    return outs[0] if len(outs) == 1 else jnp.concatenate(outs, axis=-1)


def _mod_kernel(c_ref, w_ref, b_ref, o_ref):
    c = c_ref[...]
    a = c * jax.nn.sigmoid(c)
    o_ref[...] = jnp.dot(a, w_ref[...], preferred_element_type=F32, precision=lax.Precision.HIGHEST) + b_ref[...]


def _mod_table(cvec, w_mod, b_mod):
    nt = 1024
    return pl.pallas_call(
        _mod_kernel,
        grid=(DEPTH, 6 * D // nt),
        in_specs=[
            pl.BlockSpec((MOD_ROWS, D), lambda l, j: (0, 0)),
            pl.BlockSpec((None, D, nt), lambda l, j: (l, 0, j)),
            pl.BlockSpec((None, 1, nt), lambda l, j: (l, 0, j)),
        ],
        out_specs=pl.BlockSpec((None, MOD_ROWS, nt), lambda l, j: (l, 0, j)),
        out_shape=jax.ShapeDtypeStruct((DEPTH, MOD_ROWS, 6 * D), F32),
        compiler_params=_cparams(("arbitrary", "arbitrary")),
        name="mod_table",
    )(cvec, w_mod, b_mod.reshape(DEPTH, 1, 6 * D))


def _inproj_kernel(layer, x_ref, mod_ref, n1_ref, w_ref, aq_ref, ak_ref, bq_ref, wq_ref, bkv_ref, wkv_ref,
                   lbl_ref, ones_ref, taba_ref, tabb_ref, tabk_ref,
                   qa_o, kan_o, ka_o, va_o, qb_o, ckv_o, kvb_o, kr_o, kre_o,
                   lff_o, lfb_o, kff_o, kfb_o, qc_o, vc_o, sg_o):
    x = x_ref[...]
    mod = mod_ref[...]
    xn = x * lax.rsqrt(jnp.mean(x * x, axis=-1, keepdims=True) + EPS) * n1_ref[...]
    h = (xn * (1.0 + mod[:, D:2 * D]) + mod[:, 0:D]).astype(BF16)
    y = jnp.dot(h, w_ref[...], preferred_element_type=F32)
    ones = ones_ref[...]

    qa = y[:, C_QA:C_KA]
    qa = qa * lax.rsqrt(_group_mean(qa * qa, ones, HD_A) + EPS) * aq_ref[...]
    qa_o[...] = (_rope(qa, taba_ref, 16, LANE) * (HD_A ** -0.5)).astype(BF16)
    ka = y[:, C_KA:C_VA]
    ka = ka * lax.rsqrt(_group_mean(ka * ka, ones[:LANE, :LANE], HD_A) + EPS) * ak_ref[...]
    kan_o[...] = ka
    ka_o[...] = _rope(ka, taba_ref, 16, LANE).astype(BF16)
    va_o[...] = y[:, C_VA:C_QRA]

    qr = y[:, C_QRA:C_KVA]
    qr = qr * lax.rsqrt(jnp.mean(qr * qr, axis=-1, keepdims=True) + EPS) * bq_ref[...]
    qb = jnp.dot(qr.astype(BF16), wq_ref[...], preferred_element_type=F32)
    qb_o[...] = (_rope(qb, tabb_ref, 8, 2 * LANE) * ((NOPE_B + ROPE_B) ** -0.5)).astype(BF16)
    kv = y[:, C_KVA:C_KR]
    ckv = kv * lax.rsqrt(jnp.mean(kv * kv, axis=-1, keepdims=True) + EPS) * bkv_ref[...]
    ckv_o[...] = ckv
    kvb_o[...] = jnp.dot(ckv.astype(BF16), wkv_ref[...], preferred_element_type=F32).astype(BF16)
    kr = y[:, C_KR:C_FF]
    kr_o[...] = kr
    kre_o[...] = _rope(kr, tabk_ref, 8, LANE).astype(BF16)

    lbl = lbl_ref[...]
    e = jnp.exp(lbl - jnp.max(lbl, axis=0, keepdims=True))
    p = e / jnp.sum(e, axis=0, keepdims=True)
    lb = p[0] * 0.0
    for i in range(1, layer + 1):
        lb = lb + p[i]
    for d, (c0, lf_o, kf_o) in enumerate(((C_FF, lff_o, kff_o), (C_FB, lfb_o, kfb_o))):
        pre = y[:, c0:c0 + 512]
        lbd = lb[d:d + 1, :]
        f = jnp.maximum(lbd + (1.0 - lbd) * jax.nn.sigmoid(pre), F_FLOOR)
        lf_o[...] = jnp.log(f)
        kf_o[...] = 1.0 - f
    qc_o[...] = y[:, C_QC:C_IC].astype(BF16)
    vc_o[...] = y[:, C_IC:C_GC].astype(BF16)
    gc = y[:, C_GC:C_END]
    sg_o[...] = (gc * jax.nn.sigmoid(gc)).astype(BF16)


def _const_spec(shape):
    nd = len(shape)
    return pl.BlockSpec(shape, lambda i: (0,) * nd)


def _tile_spec(width):
    return pl.BlockSpec((TM, width), lambda i: (i, 0))


def _inproj(layer, x, mod, n1, w_in_p, aq, ak, bq, wq_p, bkv, wkv_p, lbl, ones, taba, tabb, tabk):
    outs = [
        (512, BF16), (128, F32), (128, BF16), (128, F32), (1024, BF16), (256, F32), (1024, BF16),
        (128, F32), (128, BF16), (512, F32), (512, F32), (512, F32), (512, F32), (512, BF16), (512, BF16),
        (512, BF16),
    ]
    tab_spec = lambda w: pl.BlockSpec((3, TM, w), lambda i: (0, _pos_block(i), 0))
    return pl.pallas_call(
        functools.partial(_inproj_kernel, layer),
        grid=(N_TILES,),
        in_specs=[
            _tile_spec(D),
            pl.BlockSpec((None, 1, 6 * D), lambda i: (_mod_row(i), 0, 0)),
            _const_spec((1, D)),
            _const_spec((D, C_END)),
            _const_spec((1, 512)), _const_spec((1, 128)), _const_spec((1, Q_RANK)),
            _const_spec((Q_RANK, 1024)), _const_spec((1, KV_RANK)), _const_spec((KV_RANK, 1024)),
            _const_spec((DEPTH, 2, 512)), _const_spec((256, 256)),
            tab_spec(LANE), tab_spec(2 * LANE), tab_spec(LANE),
        ],
        out_specs=[_tile_spec(w) for w, _ in outs],
        out_shape=[jax.ShapeDtypeStruct((T_ALL, w), dt) for w, dt in outs],
        compiler_params=_cparams(("parallel",)),
        name="inproj",
    )(x, mod.reshape(MOD_ROWS, 1, 6 * D), n1, w_in_p, aq, ak, bq, wq_p, bkv, wkv_p, lbl, ones, taba, tabb, tabk)


def _pack_w_in(w):
    z = jnp.zeros((D, 32), w.dtype)
    kr = w[:, R_KR:R_FF]
    main = jnp.concatenate([w[:, :R_KR], kr, z, kr, z, w[:, R_FF:R_GATE]], axis=1)
    return main.astype(BF16), w[:, R_GATE:].astype(BF16)


def _pack_wq(wq):
    w = wq.reshape(Q_RANK, H_B, NOPE_B + ROPE_B)
    nope, rope = w[..., :NOPE_B], w[..., NOPE_B:]
    z = jnp.zeros((Q_RANK, H_B, 32), wq.dtype)
    even = jnp.concatenate([rope, z, nope], axis=-1)
    odd = jnp.concatenate([nope, rope, z], axis=-1)
    is_even = (jnp.arange(H_B) % 2 == 0)[None, :, None]
    return jnp.where(is_even, even, odd).reshape(Q_RANK, H_B * LANE).astype(BF16)


def _pack_wkv(wkv):
    w = wkv.reshape(KV_RANK, H_B, NOPE_B + V_B)
    nope, v = w[..., :NOPE_B], w[..., NOPE_B:]
    is_even = (jnp.arange(H_B) % 2 == 0)[None, :, None]
    return jnp.where(is_even, jnp.concatenate([v, nope], -1), jnp.concatenate([nope, v], -1)).reshape(
        KV_RANK, H_B * LANE).astype(BF16)


def _rope_tables():
    pos = np.arange(DEC_SEQ)
    row, col = pos // GRID_W, pos % GRID_W

    def pattern(half):
        quarter = half // 2
        inv = THETA ** (-np.arange(0, half, 2, dtype=np.float64) / half)
        ang = np.concatenate([row[:, None] * inv, row[:, None] * inv, col[:, None] * inv, col[:, None] * inv], 1)
        is_x2 = np.tile(np.concatenate([np.zeros(quarter), np.ones(quarter)]), 2)[None, :]
        c = np.cos(ang)
        s1 = np.sin(ang) * is_x2
        s2 = -np.sin(ang) * (1 - is_x2)
        return c, s1, s2

    def assemble(width, spans, half):
        c, s1, s2 = pattern(half)
        tc = np.ones((DEC_SEQ + TM, width))
        t1 = np.zeros((DEC_SEQ + TM, width))
        t2 = np.zeros((DEC_SEQ + TM, width))
        for start in spans:
            tc[:DEC_SEQ, start:start + 2 * half] = c
            t1[:DEC_SEQ, start:start + 2 * half] = s1
            t2[:DEC_SEQ, start:start + 2 * half] = s2
        return jnp.asarray(np.stack([tc, t1, t2]), F32)

    taba = assemble(LANE, (0, 64), 32)
    tabb = assemble(2 * LANE, (0, 128 + 64), 16)
    tabk = assemble(LANE, (0, 64), 16)
    return taba, tabb, tabk


def _ones_block(n, width):
    g = np.arange(n) // width
    return jnp.asarray(g[:, None] == g[None, :], BF16)


def _ctxkv_kernel(c_ref, w_ref, o_ref):
    o_ref[...] = jnp.dot(c_ref[...].astype(BF16), w_ref[...], preferred_element_type=F32).astype(BF16)


def _ctx_kv(ckv_cache, wkv_p):
    rows = ckv_cache.shape[0]
    return pl.pallas_call(
        _ctxkv_kernel,
        grid=(rows // TM,),
        in_specs=[_tile_spec(KV_RANK), _const_spec((KV_RANK, 1024))],
        out_specs=_tile_spec(1024),
        out_shape=jax.ShapeDtypeStruct((rows, 1024), BF16),
        compiler_params=_cparams(("parallel",)),
        name="ctx_kv",
    )(ckv_cache, wkv_p)


_NT = (((1,), (1,)), ((), ()))


def _softmax_pv(scores, values):
    m = scores[0].max(axis=-1, keepdims=True)
    for s in scores[1:]:
        m = jnp.maximum(m, s.max(axis=-1, keepdims=True))
    acc, den = None, None
    for s, v in zip(scores, values):
        p = jnp.exp(s - m)
        l = p.sum(axis=-1, keepdims=True)
        o = jnp.dot(p.astype(BF16), v, preferred_element_type=F32)
        acc = o if acc is None else acc + o
        den = l if den is None else den + l
    return acc / den


def _attn_kernel(n_pieces, qa_ref, qb_ref, *refs):
    kv_refs = refs[:4 * n_pieces]
    oa_ref, ob_ref = refs[-2], refs[-1]
    lane = lax.broadcasted_iota(jnp.int32, (1, LANE), 1)
    lo = lane < 64

    ka = [kv_refs[4 * i][...].astype(F32) for i in range(n_pieces)]
    va = [kv_refs[4 * i + 1][...].astype(F32) for i in range(n_pieces)]

    def place(x, g, parity):
        if g != parity:
            x = pltpu.roll(x, 64, 1)
        keep = lo if parity == 0 else jnp.logical_not(lo)
        return jnp.where(keep, x, 0.0).astype(BF16)

    placed = {(g, parity): ([place(k, g, parity) for k in ka], [place(v, g, parity) for v in va])
              for g in range(KV_A) for parity in range(2)}
    for pair in range(H_A // 2):
        g = (2 * pair) // (H_A // KV_A)
        q = qa_ref[:, pair * LANE:(pair + 1) * LANE]
        acc = None
        for parity in range(2):
            ks, vs = placed[(g, parity)]
            scores = [lax.dot_general(q, k, _NT, preferred_element_type=F32) for k in ks]
            o = _softmax_pv(scores, vs)
            acc = o if acc is None else acc + o
        oa_ref[:, pair * LANE:(pair + 1) * LANE] = acc.astype(BF16)

    for pair in range(H_B // 2):
        acc = None
        for parity in range(2):
            h = 2 * pair + parity
            q = qb_ref[:, h * LANE:(h + 1) * LANE]
            nope = jnp.logical_not(lo) if parity == 0 else lo
            scores, vs = [], []
            for i in range(n_pieces):
                kvb = kv_refs[4 * i + 2][:, h * LANE:(h + 1) * LANE]
                kre = kv_refs[4 * i + 3][...].astype(BF16)
                k = jnp.where(nope, kvb, kre)
                scores.append(lax.dot_general(q, k, _NT, preferred_element_type=F32))
                vs.append(jnp.where(nope, jnp.zeros_like(kvb), kvb))
            o = _softmax_pv(scores, vs)
            acc = o if acc is None else acc + o
        ob_ref[:, pair * LANE:(pair + 1) * LANE] = acc.astype(BF16)


def _attention(qa, qb, ka, va, kvb, kre, prev=None, cache=None):
    if cache is None:
        nb, nqt, nk, q_blk0, k_blk0 = BATCH, 1, SEQ, 0, 0
    else:
        nb, nqt, nk, q_blk0, k_blk0 = DEC_BATCH, DEC_SEQ // TM, DEC_SEQ, CTX_TILES, T_CTX // DEC_SEQ
    qspec = lambda w: pl.BlockSpec((TM, w), lambda b, j: (q_blk0 + b * nqt + j, 0))
    kspec = lambda w: pl.BlockSpec((nk, w), lambda b, j: (k_blk0 + b, 0))
    in_specs = [qspec(512), qspec(1024), kspec(128), kspec(128), kspec(1024), kspec(128)]
    args = [qa, qb, ka, va, kvb, kre]
    n_pieces = 1
    aliases = {}
    if cache is not None:
        cspec = lambda w: pl.BlockSpec((PAST, w), lambda b, j: (b, 0))
        in_specs += [cspec(128), cspec(128), cspec(1024), cspec(128)]
        args += list(cache)
        n_pieces = 2
        in_specs += [pl.BlockSpec(memory_space=pl.ANY)] * 2
        args += list(prev)
        aliases = {len(args) - 2: 0, len(args) - 1: 1}

    def body(*refs):
        if cache is not None:
            refs = refs[:2 + 4 * n_pieces] + refs[-2:]
        _attn_kernel(n_pieces, *refs)

    return pl.pallas_call(
        body,
        grid=(nb, nqt),
        in_specs=in_specs,
        out_specs=[qspec(512), qspec(512)],
        out_shape=[jax.ShapeDtypeStruct((T_ALL, 512), BF16)] * 2,
        input_output_aliases=aliases,
        compiler_params=_cparams(("parallel", "arbitrary")),
        name="attention_ctx" if cache is None else "attention_lat",
    )(*args)


HL = 256


def _hgrn_dir(q, k, lf, v, st_ref, rev):
    row = lax.broadcasted_iota(jnp.int32, (HL, LANE), 0)
    lane = lax.broadcasted_iota(jnp.int32, (1, LANE), 1)
    lo = lane < 64

    b = lf
    d = 1
    while d < HL:
        if rev:
            b = b + jnp.where(row < HL - d, pltpu.roll(b, HL - d, 0), 0.0)
        else:
            b = b + jnp.where(row >= d, pltpu.roll(b, d, 0), 0.0)
        d *= 2

    st = st_ref[...]
    o_int = lax.dot_general((q * jnp.exp(b)).astype(BF16), st.astype(BF16), _NT, preferred_element_type=F32)
    b_edge = b[0:1, :] if rev else b[HL - 1:HL, :]
    kh = (k * jnp.exp(b_edge - b)).astype(BF16)
    upd = lax.dot_general(v, kh, (((0,), (0,)), ((), ())), preferred_element_type=F32)
    r128 = lax.broadcasted_iota(jnp.int32, (LANE, LANE), 0)
    c128 = lax.broadcasted_iota(jnp.int32, (LANE, LANE), 1)
    st_ref[...] = st * jnp.exp(b_edge) + jnp.where((r128 >> 6) == (c128 >> 6), upd, 0.0)

    srow = lax.broadcasted_iota(jnp.int32, (HL, HL), 0)
    scol = lax.broadcasted_iota(jnp.int32, (HL, HL), 1)
    probs = []
    for parity in range(2):
        def dup(x):
            xs = pltpu.roll(x, 64, 1)
            return jnp.where(lo, x, xs) if parity == 0 else jnp.where(lo, xs, x)
        qd, kd, bd = dup(q), dup(k), dup(b)

        dg = row & 3
        if rev:
            dg = 3 - dg
        e = [None]
        for delta in range(1, 4):
            shifted = pltpu.roll(bd, delta if rev else HL - delta, 0)
            e.append(jnp.exp(jnp.minimum(shifted - bd, 0.0)))
        qp, kp = [], []
        for c1, c2 in ((0, 1), (2, 3)):
            cv = jnp.where(lo, c1, c2)
            dl = cv - dg
            fac = jnp.where(dl == 0, 1.0, jnp.where(dl == 1, e[1], jnp.where(dl == 2, e[2],
                            jnp.where(dl == 3, e[3], 0.0))))
            kp.append((kd * fac).astype(BF16))
            qp.append(jnp.where(dg == cv, qd, 0.0).astype(BF16))
        s = lax.dot_general(jnp.concatenate(qp, axis=1), jnp.concatenate(kp, axis=1), _NT,
                            preferred_element_type=F32)
        tot = jnp.where((srow >> 2) == (scol >> 2), s, 0.0)

        for lev in range(1, 4):
            g = 4 ** lev
            par = 4 * g
            shape3 = (HL // par, par, LANE)
            rid = lax.broadcasted_iota(jnp.int32, shape3, 1)
            dg3 = rid >> (2 * lev)
            if rev:
                dg3 = 3 - dg3
            b3, q3, k3 = bd.reshape(shape3), qd.reshape(shape3), kd.reshape(shape3)
            lo3 = lo.reshape(1, 1, LANE)
            qp, kp = [], []
            for c1, c2 in ((1, 2), (3, None)):
                idx = lambda c: (4 - c) * g if rev else c * g - 1
                i1 = idx(c1)
                i2 = idx(c2) if c2 is not None else i1
                ridx = jnp.where(lo3, i1, i2)
                ref = jnp.sum(jnp.where(rid == ridx, b3, 0.0), axis=1, keepdims=True)
                cvk = jnp.where(lo3, c1, c2 if c2 is not None else 0)
                cvq = jnp.where(lo3, c1, c2 if c2 is not None else -1)
                kk = jnp.where(dg3 < cvk, k3 * jnp.exp(jnp.minimum(ref - b3, 0.0)), 0.0)
                qq = jnp.where(dg3 == cvq, q3 * jnp.exp(jnp.minimum(b3 - ref, 0.0)), 0.0)
                kp.append(kk.reshape(HL, LANE).astype(BF16))
                qp.append(qq.reshape(HL, LANE).astype(BF16))
            s = lax.dot_general(jnp.concatenate(qp, axis=1), jnp.concatenate(kp, axis=1), _NT,
                                preferred_element_type=F32)
            sh = 2 * lev + 2
            tot = tot + (jnp.where((srow >> sh) == (scol >> sh), s, 0.0) if par < HL else s)
        probs.append(tot.astype(BF16))

    o_e = jnp.dot(probs[0], v, preferred_element_type=F32)
    o_o = jnp.dot(probs[1], v, preferred_element_type=F32)
    return jnp.where(lo, o_e, o_o) + o_int


def _hgrn_kernel(has_s0, nt, *refs):
    if has_s0:
        (qf_ref, vf_ref, kf_ref, lf_ref, qb_ref, vb_ref, kb_ref, lb_ref, s0_ref,
         of_ref, ob_ref, so_ref, st_scr) = refs
    else:
        (qf_ref, vf_ref, kf_ref, lf_ref, qb_ref, vb_ref, kb_ref, lb_ref,
         of_ref, ob_ref, so_ref, st_scr) = refs
    j = pl.program_id(2)

    @pl.when(j == 0)
    def _():
        if has_s0:
            st_scr[...] = s0_ref[...]
        else:
            st_scr[...] = jnp.zeros_like(st_scr)

    of_ref[...] = _hgrn_dir(qf_ref[...].astype(F32), kf_ref[...], lf_ref[...], vf_ref[...], st_scr.at[0], False)
    ob_ref[...] = _hgrn_dir(qb_ref[...].astype(F32), kb_ref[...], lb_ref[...], vb_ref[...], st_scr.at[1], True)

    @pl.when(j == nt - 1)
    def _():
        so_ref[...] = st_scr[...]


def _hgrn(qc, vc, kff, lff, kfb, lfb, s0=None, prev=None):
    if s0 is None:
        nb, nt, blk0 = BATCH, SEQ // HL, 0
    else:
        nb, nt, blk0 = DEC_BATCH, DEC_SEQ // HL, T_CTX // HL
    fspec = pl.BlockSpec((HL, LANE), lambda b, p, j: (blk0 + b * nt + j, p))
    bspec = pl.BlockSpec((HL, LANE), lambda b, p, j: (blk0 + b * nt + nt - 1 - j, p))
    sspec = pl.BlockSpec((None, None, 2, LANE, LANE), lambda b, p, j: (b, p, 0, 0, 0))
    in_specs = [fspec] * 4 + [bspec] * 4
    args = [qc, vc, kff, lff, qc, vc, kfb, lfb]
    aliases = {}
    if s0 is not None:
        in_specs += [sspec]
        args += [s0]
        in_specs += [pl.BlockSpec(memory_space=pl.ANY)] * 2
        args += list(prev)
        aliases = {len(args) - 2: 0, len(args) - 1: 1}

    def body(*refs):
        if s0 is not None:
            refs = refs[:9] + refs[11:]
        _hgrn_kernel(s0 is not None, nt, *refs)

    return pl.pallas_call(
        body,
        grid=(nb, H_C // 2, nt),
        in_specs=in_specs,
        out_specs=[fspec, bspec, sspec],
        out_shape=[jax.ShapeDtypeStruct((T_ALL, 512), F32), jax.ShapeDtypeStruct((T_ALL, 512), F32),
                   jax.ShapeDtypeStruct((nb, H_C // 2, 2, LANE, LANE), F32)],
        scratch_shapes=[pltpu.VMEM((2, LANE, LANE), F32)],
        input_output_aliases=aliases,
        compiler_params=_cparams(("parallel", "parallel", "arbitrary")),
        name="hgrn_ctx" if s0 is None else "hgrn_lat",
    )(*args)


def _merge_kernel(x_ref, mod_ref, n1_ref, wg_ref, oa_ref, ob_ref, ocf_ref, ocb_ref, sg_ref, con_ref, ones_ref,
                  wbr_ref, wout_ref, n2_ref, wr_ref, br_ref, x1_o, h2_o, comb_o):
    x = x_ref[...]
    mod = mod_ref[...]
    xn = x * lax.rsqrt(jnp.mean(x * x, axis=-1, keepdims=True) + EPS) * n1_ref[...]
    h = (xn * (1.0 + mod[:, D:2 * D]) + mod[:, 0:D]).astype(BF16)

    oc = ocf_ref[...] + ocb_ref[...]
    oc = oc * lax.rsqrt(_group_mean(oc * oc, ones_ref[...], DV_C) + EPS) * con_ref[...]
    oc = (oc * sg_ref[...].astype(F32)).astype(BF16)
    branches = (oa_ref[...], ob_ref[...], oc)
    mix = None
    for jb in range(3):
        gate = jax.nn.sigmoid(jnp.dot(h, wg_ref[:, jb * D:(jb + 1) * D], preferred_element_type=F32))
        t = gate * jnp.dot(branches[jb], wbr_ref[jb], preferred_element_type=F32)
        mix = t if mix is None else mix + t
    out = jnp.dot(mix.astype(BF16), wout_ref[...], preferred_element_type=F32)
    x1 = x + mod[:, 2 * D:3 * D] * out
    x1_o[...] = x1

    x1n = x1 * lax.rsqrt(jnp.mean(x1 * x1, axis=-1, keepdims=True) + EPS) * n2_ref[...]
    h2 = x1n * (1.0 + mod[:, 4 * D:5 * D]) + mod[:, 3 * D:4 * D]
    h2_o[...] = h2.astype(BF16)

    logits = jnp.dot(h2, wr_ref[...], preferred_element_type=F32, precision=lax.Precision.HIGHEST) + br_ref[...]
    lane = lax.broadcasted_iota(jnp.int32, logits.shape, 1)
    neg = -jnp.inf
    is_g = lane < N_GROUPS
    gl = jnp.where(is_g, logits, neg)
    gmax = gl.max(axis=-1, keepdims=True)
    gidx = jnp.min(jnp.where(gl == gmax, lane, LANE), axis=-1, keepdims=True)
    gp = 1.0 / jnp.sum(jnp.where(is_g, jnp.exp(gl - gmax), 0.0), axis=-1, keepdims=True)
    eid = lane - N_GROUPS
    in_grp = (eid >= 0) & (eid < N_EXPERTS) & ((eid >> 2) == gidx)
    el = jnp.where(in_grp, logits, neg)
    v1 = el.max(axis=-1, keepdims=True)
    i1 = jnp.min(jnp.where(el == v1, lane, LANE), axis=-1, keepdims=True)
    el2 = jnp.where(lane == i1, neg, el)
    v2 = el2.max(axis=-1, keepdims=True)
    i2 = jnp.min(jnp.where(el2 == v2, lane, LANE), axis=-1, keepdims=True)
    e2 = jnp.exp(v2 - v1)
    w1 = gp / (1.0 + e2)
    w2 = gp * e2 / (1.0 + e2)
    comb = jnp.where(lane == i1, w1, 0.0) + jnp.where(lane == i2, w2, 0.0)
    comb_o[...] = pltpu.roll(comb, LANE - N_GROUPS, 1)


def _merge(x, mod, n1, wgate, oa, ob, ocf, ocb, sg, con, ones, wbr, wout, n2, wr, br):
    return pl.pallas_call(
        _merge_kernel,
        grid=(N_TILES,),
        in_specs=[
            _tile_spec(D),
            pl.BlockSpec((None, 1, 6 * D), lambda i: (_mod_row(i), 0, 0)),
            _const_spec((1, D)), _const_spec((D, 3 * D)),
            _tile_spec(512), _tile_spec(512), _tile_spec(512), _tile_spec(512), _tile_spec(512),
            _const_spec((1, 512)), _const_spec((256, 256)),
            _const_spec((3, BRANCH_W, D)), _const_spec((D, D)), _const_spec((1, D)),
            _const_spec((D, LANE)), _const_spec((1, LANE)),
        ],
        out_specs=[_tile_spec(D), _tile_spec(D), _tile_spec(LANE)],
        out_shape=[jax.ShapeDtypeStruct((T_ALL, D), F32), jax.ShapeDtypeStruct((T_ALL, D), BF16),
                   jax.ShapeDtypeStruct((T_ALL, LANE), F32)],
        compiler_params=_cparams(("parallel",)),
        name="merge",
    )(x, mod.reshape(MOD_ROWS, 1, 6 * D), n1, wgate, oa, ob, ocf, ocb, sg, con, ones, wbr, wout, n2, wr, br)


TMM = 512


def _moe_kernel(final, h2_ref, comb_ref, x1_ref, mod_ref, eg_ref, eu_ref, ed_ref, fg_ref, o_ref, acc_ref):
    e = pl.program_id(1)

    @pl.when(e == 0)
    def _():
        acc_ref[...] = jnp.zeros_like(acc_ref)

    h2 = h2_ref[...]
    comb = comb_ref[...]
    lane = lax.broadcasted_iota(jnp.int32, comb.shape, 1)
    w = jnp.sum(jnp.where(lane == e, comb, 0.0), axis=-1, keepdims=True)
    hg = jnp.dot(h2, eg_ref[...], preferred_element_type=F32)
    hu = jnp.dot(h2, eu_ref[...], preferred_element_type=F32)
    act = (hg * jax.nn.sigmoid(hg) * hu * w).astype(BF16)
    acc_ref[...] += jnp.dot(act, ed_ref[...], preferred_element_type=F32)

    @pl.when(e == N_EXPERTS - 1)
    def _():
        x2 = x1_ref[...] + mod_ref[:, 5 * D:6 * D] * acc_ref[...]
        if final:
            x2 = x2 * lax.rsqrt(jnp.mean(x2 * x2, axis=-1, keepdims=True) + EPS) * fg_ref[...]
        o_ref[...] = x2


def _moe(final, h2, comb, x1, mod, eg, eu, ed, fg):
    tiles = T_ALL // TMM
    ctx_tiles = T_CTX // TMM
    per_seq = DEC_SEQ // TMM
    mrow = lambda i: jnp.where(i < ctx_tiles, CTX_MOD_ROW, (i - ctx_tiles) // per_seq)
    tspec = lambda w: pl.BlockSpec((TMM, w), lambda i, e: (i, 0))
    return pl.pallas_call(
        functools.partial(_moe_kernel, final),
        grid=(tiles, N_EXPERTS),
        in_specs=[
            tspec(D), tspec(LANE), tspec(D),
            pl.BlockSpec((None, 1, 6 * D), lambda i, e: (mrow(i), 0, 0)),
            pl.BlockSpec((None, D, D_EXPERT), lambda i, e: (e, 0, 0)),
            pl.BlockSpec((None, D, D_EXPERT), lambda i, e: (e, 0, 0)),
            pl.BlockSpec((None, D_EXPERT, D), lambda i, e: (e, 0, 0)),
            pl.BlockSpec((1, D), lambda i, e: (0, 0)),
        ],
        out_specs=tspec(D),
        out_shape=jax.ShapeDtypeStruct((T_ALL, D), F32),
        scratch_shapes=[pltpu.VMEM((TMM, D), F32)],
        compiler_params=_cparams(("parallel", "arbitrary")),
        name="moe",
    )(h2, comb, x1, mod.reshape(MOD_ROWS, 1, 6 * D), eg, eu, ed, fg)


def _state_to_blockdiag(s):
    st = jnp.swapaxes(s, -1, -2).reshape(s.shape[0], H_C // 2, 2, DV_C, DK_C)
    z = jnp.zeros_like(st[:, :, 0])
    top = jnp.concatenate([st[:, :, 0], z], axis=-1)
    bot = jnp.concatenate([z, st[:, :, 1]], axis=-1)
    return jnp.concatenate([top, bot], axis=-2)


def _blockdiag_to_state(sb):
    even = sb[:, :, :DV_C, :DK_C]
    odd = sb[:, :, DV_C:, DK_C:]
    st = jnp.stack([even, odd], axis=2).reshape(sb.shape[0], H_C, DV_C, DK_C)
    return jnp.swapaxes(st, -1, -2)


def kernel(x_prompt, x_sample, cache_gqa_k, cache_gqa_v, cache_mla_ckv, cache_mla_krope, state_hgrn, c, c_ctx,
           w_mod, b_mod, norm1_g, norm2_g, w_in, a_qnorm, a_knorm, b_qnorm, b_wq, b_kvnorm, b_wkv, c_lb_logits,
           c_onorm, w_branch, w_out, r_group_w, r_group_b, r_expert_w, r_expert_b, e_gate, e_up, e_down, final_g):
    x = jnp.concatenate([x_prompt.reshape(T_CTX, D), x_sample.reshape(T_LAT, D)], axis=0)
    cvec = jnp.concatenate([c, c_ctx[None, :], jnp.zeros((MOD_ROWS - DEC_BATCH - 1, D), F32)], axis=0)
    mod_all = _mod_table(cvec, w_mod, b_mod)
    taba, tabb, tabk = _rope_tables()
    ones = _ones_block(256, 64)
    lbl = c_lb_logits.reshape(DEPTH, 2, H_C * DK_C)
    zpad = jnp.zeros((DEC_BATCH * PAST, 32), F32)

    new_k, new_v, new_ckv, new_kr, new_s = [], [], [], [], []
    for l in range(DEPTH):
        mod = mod_all[l]
        w_in_p, w_gate = _pack_w_in(w_in[l])
        wq_p, wkv_p = _pack_wq(b_wq[l]), _pack_wkv(b_wkv[l])
        (qa, kan, ka, va, qb, ckv, kvb, kr, kre, lff, lfb, kff, kfb, qc, vc, sg) = _inproj(
            l, x, mod, norm1_g[l][None, :], w_in_p, jnp.tile(a_qnorm[l], H_A)[None, :],
            jnp.tile(a_knorm[l], KV_A)[None, :], b_qnorm[l][None, :], wq_p, b_kvnorm[l][None, :], wkv_p,
            lbl, ones, taba, tabb, tabk)
        new_k.append(kan[:T_CTX].reshape(BATCH, SEQ, KV_A, HD_A))
        new_v.append(va[:T_CTX].reshape(BATCH, SEQ, KV_A, HD_A))
        new_ckv.append(ckv[:T_CTX].reshape(BATCH, SEQ, KV_RANK))
        new_kr.append(kr[:T_CTX, :ROPE_B].reshape(BATCH, SEQ, ROPE_B))

        ck = cache_gqa_k[:, l].reshape(DEC_BATCH * PAST, KV_A * HD_A)
        cv = cache_gqa_v[:, l].reshape(DEC_BATCH * PAST, KV_A * HD_A)
        ckvb = _ctx_kv(cache_mla_ckv[:, l].reshape(DEC_BATCH * PAST, KV_RANK), wkv_p)
        ckr = cache_mla_krope[:, l].reshape(DEC_BATCH * PAST, ROPE_B)
        ckre = jnp.concatenate([ckr, zpad, ckr, zpad], axis=1)
        oa, ob = _attention(qa, qb, ka, va, kvb, kre)
        oa, ob = _attention(qa, qb, ka, va, kvb, kre, prev=(oa, ob), cache=(ck, cv, ckvb, ckre))

        ocf, ocb, s_ctx = _hgrn(qc, vc, kff, lff, kfb, lfb)
        s0 = jnp.stack([_state_to_blockdiag(state_hgrn[:, l, 0]), _state_to_blockdiag(state_hgrn[:, l, 1])], axis=2)
        ocf, ocb, _ = _hgrn(qc, vc, kff, lff, kfb, lfb, s0=s0, prev=(ocf, ocb))
        new_s.append(jnp.stack([_blockdiag_to_state(s_ctx[:, :, 0]), _blockdiag_to_state(s_ctx[:, :, 1])], axis=1))

        wr = jnp.concatenate([r_group_w[l], r_expert_w[l], jnp.zeros((D, LANE - N_GROUPS - N_EXPERTS), F32)], axis=1)
        br = jnp.concatenate([r_group_b[l], r_expert_b[l], jnp.zeros((LANE - N_GROUPS - N_EXPERTS,), F32)])[None, :]
        x1, h2, comb = _merge(x, mod, norm1_g[l][None, :], w_gate, oa, ob, ocf, ocb, sg,
                              jnp.tile(c_onorm[l], H_C)[None, :], ones, w_branch[l].astype(BF16),
                              w_out[l].astype(BF16), norm2_g[l][None, :], wr, br)
        x = _moe(l == DEPTH - 1, h2, comb, x1, mod, e_gate[l].astype(BF16), e_up[l].astype(BF16),
                 e_down[l].astype(BF16), final_g[None, :])

    y_prompt = x[:T_CTX].reshape(BATCH, SEQ, D)
    y_sample = x[T_CTX:].reshape(DEC_BATCH, DEC_SEQ, D)
    return (y_prompt, y_sample, jnp.stack(new_k, axis=1), jnp.stack(new_v, axis=1), jnp.stack(new_ckv, axis=1),
            jnp.stack(new_kr, axis=1), jnp.stack(new_s, axis=1))
```

```python
import functools

import numpy as np
import jax
import jax.numpy as jnp
from jax import lax
from jax.experimental import pallas as pl
from jax.experimental.pallas import tpu as pltpu

D = 1024
BATCH, SEQ = 32, 256
DEC_BATCH, DEC_SEQ = 8, 1024
PAST = 256
DEPTH = 2
GRID_W = 64
THETA = 10000.0
EPS = 1e-6
F_FLOOR = 1e-30
H_A, KV_A, HD_A = 8, 2, 64
H_B, Q_RANK, KV_RANK, NOPE_B, ROPE_B, V_B = 8, 384, 256, 64, 32, 64
H_C, DK_C, DV_C = 8, 64, 64
BRANCH_W = 512
N_GROUPS, E_PER_GROUP, N_EXPERTS, D_EXPERT = 4, 4, 16, 512

T_CTX = BATCH * SEQ
T_LAT = DEC_BATCH * DEC_SEQ
T_ALL = T_CTX + T_LAT
TM = 256
N_TILES = T_ALL // TM
CTX_TILES = T_CTX // TM
LAT_TILES_PER_SEQ = DEC_SEQ // TM
MOD_ROWS = 16
CTX_MOD_ROW = DEC_BATCH
LANE = 128
VMEM_LIMIT = 56 * 1024 * 1024

C_QA, C_KA, C_VA, C_QRA, C_KVA, C_KR = 0, 512, 640, 768, 1152, 1408
C_FF, C_FB, C_QC, C_IC, C_GC, C_END = 1536, 2048, 2560, 3072, 3584, 4096
R_QA, R_KA, R_VA, R_QRA, R_KVA, R_KR = 0, 512, 640, 768, 1152, 1408
R_FF, R_FB, R_QC, R_IC, R_GC, R_GATE, R_END = 1440, 1952, 2464, 2976, 3488, 4000, 7072

F32 = jnp.float32
BF16 = jnp.bfloat16


def _cparams(sem):
    return pltpu.CompilerParams(dimension_semantics=sem, vmem_limit_bytes=VMEM_LIMIT)


def _mod_row(i):
    return jnp.where(i < CTX_TILES, CTX_MOD_ROW, (i - CTX_TILES) // LAT_TILES_PER_SEQ)


def _pos_block(i):
    return jnp.where(i < CTX_TILES, LAT_TILES_PER_SEQ, (i - CTX_TILES) % LAT_TILES_PER_SEQ)


def _split_hi_lo(x):
    hi = x.astype(BF16)
    lo = (x - hi.astype(F32)).astype(BF16)
    return hi, lo


def _group_mean(x2, ones_blk, width):
    n = ones_blk.shape[0]
    outs = []
    for j in range(x2.shape[-1] // n):
        blk = x2[:, j * n:(j + 1) * n]
        hi, lo = _split_hi_lo(blk)
        s = jnp.dot(hi, ones_blk, preferred_element_type=F32) + jnp.dot(lo, ones_blk, preferred_element_type=F32)
        outs.append(s)
    s = outs[0] if len(outs) == 1 else jnp.concatenate(outs, axis=-1)
    return s * (1.0 / width)


def _rope(x, tab_ref, shift, period):
    c, s1, s2 = tab_ref[0], tab_ref[1], tab_ref[2]
    outs = []
    for j in range(x.shape[-1] // period):
        blk = x[:, j * period:(j + 1) * period]
        outs.append(blk * c + pltpu.roll(blk, shift, 1) * s1 + pltpu.roll(blk, period - shift, 1) * s2)
    return outs[0] if len(outs) == 1 else jnp.concatenate(outs, axis=-1)


def _mod_kernel(c_ref, w_ref, b_ref, o_ref):
    c = c_ref[...]
    a = c * jax.nn.sigmoid(c)
    o_ref[...] = jnp.dot(a, w_ref[...], preferred_element_type=F32, precision=lax.Precision.HIGHEST) + b_ref[...]


def _mod_table(cvec, w_mod, b_mod):
    nt = 1024
    return pl.pallas_call(
        _mod_kernel,
        grid=(DEPTH, 6 * D // nt),
        in_specs=[
            pl.BlockSpec((MOD_ROWS, D), lambda l, j: (0, 0)),
            pl.BlockSpec((None, D, nt), lambda l, j: (l, 0, j)),
            pl.BlockSpec((None, 1, nt), lambda l, j: (l, 0, j)),
        ],
        out_specs=pl.BlockSpec((None, MOD_ROWS, nt), lambda l, j: (l, 0, j)),
        out_shape=jax.ShapeDtypeStruct((DEPTH, MOD_ROWS, 6 * D), F32),
        compiler_params=_cparams(("arbitrary", "arbitrary")),
        name="mod_table",
    )(cvec, w_mod, b_mod.reshape(DEPTH, 1, 6 * D))


def _inproj_kernel(layer, x_ref, mod_ref, n1_ref, w_ref, aq_ref, ak_ref, bq_ref, wq_ref, bkv_ref, wkv_ref,
                   lbl_ref, ones_ref, taba_ref, tabb_ref, tabk_ref,
                   qa_o, kan_o, ka_o, va_o, qb_o, ckv_o, kvb_o, kr_o, kre_o,
                   lff_o, lfb_o, kff_o, kfb_o, qc_o, vc_o, sg_o):
    x = x_ref[...]
    mod = mod_ref[...]
    xn = x * lax.rsqrt(jnp.mean(x * x, axis=-1, keepdims=True) + EPS) * n1_ref[...]
    h = (xn * (1.0 + mod[:, D:2 * D]) + mod[:, 0:D]).astype(BF16)
    y = jnp.dot(h, w_ref[...], preferred_element_type=F32)
    ones = ones_ref[...]

    qa = y[:, C_QA:C_KA]
    qa = qa * lax.rsqrt(_group_mean(qa * qa, ones, HD_A) + EPS) * aq_ref[...]
    qa_o[...] = (_rope(qa, taba_ref, 16, LANE) * (HD_A ** -0.5)).astype(BF16)
    ka = y[:, C_KA:C_VA]
    ka = ka * lax.rsqrt(_group_mean(ka * ka, ones[:LANE, :LANE], HD_A) + EPS) * ak_ref[...]
    kan_o[...] = ka
    ka_o[...] = _rope(ka, taba_ref, 16, LANE).astype(BF16)
    va_o[...] = y[:, C_VA:C_QRA]

    qr = y[:, C_QRA:C_KVA]
    qr = qr * lax.rsqrt(jnp.mean(qr * qr, axis=-1, keepdims=True) + EPS) * bq_ref[...]
    qb = jnp.dot(qr.astype(BF16), wq_ref[...], preferred_element_type=F32)
    qb_o[...] = (_rope(qb, tabb_ref, 8, 2 * LANE) * ((NOPE_B + ROPE_B) ** -0.5)).astype(BF16)
    kv = y[:, C_KVA:C_KR]
    ckv = kv * lax.rsqrt(jnp.mean(kv * kv, axis=-1, keepdims=True) + EPS) * bkv_ref[...]
    ckv_o[...] = ckv
    kvb_o[...] = jnp.dot(ckv.astype(BF16), wkv_ref[...], preferred_element_type=F32).astype(BF16)
    kr = y[:, C_KR:C_FF]
    kr_o[...] = kr
    kre_o[...] = _rope(kr, tabk_ref, 8, LANE).astype(BF16)

    lbl = lbl_ref[...]
    e = jnp.exp(lbl - jnp.max(lbl, axis=0, keepdims=True))
    p = e / jnp.sum(e, axis=0, keepdims=True)
    lb = p[0] * 0.0
    for i in range(1, layer + 1):
        lb = lb + p[i]
    for d, (c0, lf_o, kf_o) in enumerate(((C_FF, lff_o, kff_o), (C_FB, lfb_o, kfb_o))):
        pre = y[:, c0:c0 + 512]
        lbd = lb[d:d + 1, :]
        f = jnp.maximum(lbd + (1.0 - lbd) * jax.nn.sigmoid(pre), F_FLOOR)
        lf_o[...] = jnp.log(f)
        kf_o[...] = 1.0 - f
    qc_o[...] = y[:, C_QC:C_IC].astype(BF16)
    vc_o[...] = y[:, C_IC:C_GC].astype(BF16)
    gc = y[:, C_GC:C_END]
    sg_o[...] = (gc * jax.nn.sigmoid(gc)).astype(BF16)


def _const_spec(shape):
    nd = len(shape)
    return pl.BlockSpec(shape, lambda i: (0,) * nd)


def _tile_spec(width):
    return pl.BlockSpec((TM, width), lambda i: (i, 0))


def _inproj(layer, x, mod, n1, w_in_p, aq, ak, bq, wq_p, bkv, wkv_p, lbl, ones, taba, tabb, tabk):
    outs = [
        (512, BF16), (128, F32), (128, BF16), (128, F32), (1024, BF16), (256, F32), (1024, BF16),
        (128, F32), (128, BF16), (512, F32), (512, F32), (512, F32), (512, F32), (512, BF16), (512, BF16),
        (512, BF16),
    ]
    tab_spec = lambda w: pl.BlockSpec((3, TM, w), lambda i: (0, _pos_block(i), 0))
    return pl.pallas_call(
        functools.partial(_inproj_kernel, layer),
        grid=(N_TILES,),
        in_specs=[
            _tile_spec(D),
            pl.BlockSpec((None, 1, 6 * D), lambda i: (_mod_row(i), 0, 0)),
            _const_spec((1, D)),
            _const_spec((D, C_END)),
            _const_spec((1, 512)), _const_spec((1, 128)), _const_spec((1, Q_RANK)),
            _const_spec((Q_RANK, 1024)), _const_spec((1, KV_RANK)), _const_spec((KV_RANK, 1024)),
            _const_spec((DEPTH, 2, 512)), _const_spec((256, 256)),
            tab_spec(LANE), tab_spec(2 * LANE), tab_spec(LANE),
        ],
        out_specs=[_tile_spec(w) for w, _ in outs],
        out_shape=[jax.ShapeDtypeStruct((T_ALL, w), dt) for w, dt in outs],
        compiler_params=_cparams(("parallel",)),
        name="inproj",
    )(x, mod.reshape(MOD_ROWS, 1, 6 * D), n1, w_in_p, aq, ak, bq, wq_p, bkv, wkv_p, lbl, ones, taba, tabb, tabk)


def _pack_w_in(w):
    z = jnp.zeros((D, 32), w.dtype)
    kr = w[:, R_KR:R_FF]
    main = jnp.concatenate([w[:, :R_KR], kr, z, kr, z, w[:, R_FF:R_GATE]], axis=1)
    return main.astype(BF16), w[:, R_GATE:].astype(BF16)


def _pack_wq(wq):
    w = wq.reshape(Q_RANK, H_B, NOPE_B + ROPE_B)
    nope, rope = w[..., :NOPE_B], w[..., NOPE_B:]
    z = jnp.zeros((Q_RANK, H_B, 32), wq.dtype)
    even = jnp.concatenate([rope, z, nope], axis=-1)
    odd = jnp.concatenate([nope, rope, z], axis=-1)
    is_even = (jnp.arange(H_B) % 2 == 0)[None, :, None]
    return jnp.where(is_even, even, odd).reshape(Q_RANK, H_B * LANE).astype(BF16)


def _pack_wkv(wkv):
    w = wkv.reshape(KV_RANK, H_B, NOPE_B + V_B)
    nope, v = w[..., :NOPE_B], w[..., NOPE_B:]
    is_even = (jnp.arange(H_B) % 2 == 0)[None, :, None]
    return jnp.where(is_even, jnp.concatenate([v, nope], -1), jnp.concatenate([nope, v], -1)).reshape(
        KV_RANK, H_B * LANE).astype(BF16)


def _rope_tables():
    pos = np.arange(DEC_SEQ)
    row, col = pos // GRID_W, pos % GRID_W

    def pattern(half):
        quarter = half // 2
        inv = THETA ** (-np.arange(0, half, 2, dtype=np.float64) / half)
        ang = np.concatenate([row[:, None] * inv, row[:, None] * inv, col[:, None] * inv, col[:, None] * inv], 1)
        is_x2 = np.tile(np.concatenate([np.zeros(quarter), np.ones(quarter)]), 2)[None, :]
        c = np.cos(ang)
        s1 = np.sin(ang) * is_x2
        s2 = -np.sin(ang) * (1 - is_x2)
        return c, s1, s2

    def assemble(width, spans, half):
        c, s1, s2 = pattern(half)
        tc = np.ones((DEC_SEQ + TM, width))
        t1 = np.zeros((DEC_SEQ + TM, width))
        t2 = np.zeros((DEC_SEQ + TM, width))
        for start in spans:
            tc[:DEC_SEQ, start:start + 2 * half] = c
            t1[:DEC_SEQ, start:start + 2 * half] = s1
            t2[:DEC_SEQ, start:start + 2 * half] = s2
        return jnp.asarray(np.stack([tc, t1, t2]), F32)

    taba = assemble(LANE, (0, 64), 32)
    tabb = assemble(2 * LANE, (0, 128 + 64), 16)
    tabk = assemble(LANE, (0, 64), 16)
    return taba, tabb, tabk


def _ones_block(n, width):
    g = np.arange(n) // width
    return jnp.asarray(g[:, None] == g[None, :], BF16)


def _ctxkv_kernel(c_ref, w_ref, o_ref):
    o_ref[...] = jnp.dot(c_ref[...].astype(BF16), w_ref[...], preferred_element_type=F32).astype(BF16)


def _ctx_kv(ckv_cache, wkv_p):
    rows = ckv_cache.shape[0]
    return pl.pallas_call(
        _ctxkv_kernel,
        grid=(rows // TM,),
        in_specs=[_tile_spec(KV_RANK), _const_spec((KV_RANK, 1024))],
        out_specs=_tile_spec(1024),
        out_shape=jax.ShapeDtypeStruct((rows, 1024), BF16),
        compiler_params=_cparams(("parallel",)),
        name="ctx_kv",
    )(ckv_cache, wkv_p)


_NT = (((1,), (1,)), ((), ()))


def _softmax_pv(scores, values):
    m = scores[0].max(axis=-1, keepdims=True)
    for s in scores[1:]:
        m = jnp.maximum(m, s.max(axis=-1, keepdims=True))
    acc, den = None, None
    for s, v in zip(scores, values):
        p = jnp.exp(s - m)
        l = p.sum(axis=-1, keepdims=True)
        o = jnp.dot(p.astype(BF16), v, preferred_element_type=F32)
        acc = o if acc is None else acc + o
        den = l if den is None else den + l
    return acc / den


def _attn_kernel(n_pieces, qa_ref, qb_ref, *refs):
    kv_refs = refs[:4 * n_pieces]
    oa_ref, ob_ref = refs[-2], refs[-1]
    lane = lax.broadcasted_iota(jnp.int32, (1, LANE), 1)
    lo = lane < 64

    ka = [kv_refs[4 * i][...].astype(F32) for i in range(n_pieces)]
    va = [kv_refs[4 * i + 1][...].astype(F32) for i in range(n_pieces)]

    def place(x, g, parity):
        if g != parity:
            x = pltpu.roll(x, 64, 1)
        keep = lo if parity == 0 else jnp.logical_not(lo)
        return jnp.where(keep, x, 0.0).astype(BF16)

    placed = {(g, parity): ([place(k, g, parity) for k in ka], [place(v, g, parity) for v in va])
              for g in range(KV_A) for parity in range(2)}
    for pair in range(H_A // 2):
        g = (2 * pair) // (H_A // KV_A)
        q = qa_ref[:, pair * LANE:(pair + 1) * LANE]
        acc = None
        for parity in range(2):
            ks, vs = placed[(g, parity)]
            scores = [lax.dot_general(q, k, _NT, preferred_element_type=F32) for k in ks]
            o = _softmax_pv(scores, vs)
            acc = o if acc is None else acc + o
        oa_ref[:, pair * LANE:(pair + 1) * LANE] = acc.astype(BF16)

    for pair in range(H_B // 2):
        acc = None
        for parity in range(2):
            h = 2 * pair + parity
            q = qb_ref[:, h * LANE:(h + 1) * LANE]
            nope = jnp.logical_not(lo) if parity == 0 else lo
            scores, vs = [], []
            for i in range(n_pieces):
                kvb = kv_refs[4 * i + 2][:, h * LANE:(h + 1) * LANE]
                kre = kv_refs[4 * i + 3][...].astype(BF16)
                k = jnp.where(nope, kvb, kre)
                scores.append(lax.dot_general(q, k, _NT, preferred_element_type=F32))
                vs.append(jnp.where(nope, jnp.zeros_like(kvb), kvb))
            o = _softmax_pv(scores, vs)
            acc = o if acc is None else acc + o
        ob_ref[:, pair * LANE:(pair + 1) * LANE] = acc.astype(BF16)


def _attention(qa, qb, ka, va, kvb, kre, prev=None, cache=None):
    if cache is None:
        nb, nqt, nk, q_blk0, k_blk0 = BATCH, 1, SEQ, 0, 0
    else:
        nb, nqt, nk, q_blk0, k_blk0 = DEC_BATCH, DEC_SEQ // TM, DEC_SEQ, CTX_TILES, T_CTX // DEC_SEQ
    qspec = lambda w: pl.BlockSpec((TM, w), lambda b, j: (q_blk0 + b * nqt + j, 0))
    kspec = lambda w: pl.BlockSpec((nk, w), lambda b, j: (k_blk0 + b, 0))
    in_specs = [qspec(512), qspec(1024), kspec(128), kspec(128), kspec(1024), kspec(128)]
    args = [qa, qb, ka, va, kvb, kre]
    n_pieces = 1
    aliases = {}
    if cache is not None:
        cspec = lambda w: pl.BlockSpec((PAST, w), lambda b, j: (b, 0))
        in_specs += [cspec(128), cspec(128), cspec(1024), cspec(128)]
        args += list(cache)
        n_pieces = 2
        in_specs += [pl.BlockSpec(memory_space=pl.ANY)] * 2
        args += list(prev)
        aliases = {len(args) - 2: 0, len(args) - 1: 1}

    def body(*refs):
        if cache is not None:
            refs = refs[:2 + 4 * n_pieces] + refs[-2:]
        _attn_kernel(n_pieces, *refs)

    return pl.pallas_call(
        body,
        grid=(nb, nqt),
        in_specs=in_specs,
        out_specs=[qspec(512), qspec(512)],
        out_shape=[jax.ShapeDtypeStruct((T_ALL, 512), BF16)] * 2,
        input_output_aliases=aliases,
        compiler_params=_cparams(("parallel", "arbitrary")),
        name="attention_ctx" if cache is None else "attention_lat",
    )(*args)


HL = 256
HS = 128
HG = 64
N_PAIRS = H_C // 2
FAST_DECAY_LIMIT = 80.0


def _hgrn_bottom_exact(q, k, c, lo, rev):
    row = lax.broadcasted_iota(jnp.int32, (HS, LANE), 0)
    srow = lax.broadcasted_iota(jnp.int32, (HS, HS), 0)
    scol = lax.broadcasted_iota(jnp.int32, (HS, HS), 1)
    out = []
    for parity in range(2):
        def dup(x):
            xs = pltpu.roll(x, 64, 1)
            return jnp.where(lo, x, xs) if parity == 0 else jnp.where(lo, xs, x)
        qd, kd, bd = dup(q), dup(k), dup(c)

        dg = row & 3
        if rev:
            dg = 3 - dg
        e = [None]
        for delta in range(1, 4):
            shifted = pltpu.roll(bd, delta if rev else HS - delta, 0)
            e.append(jnp.exp(jnp.minimum(shifted - bd, 0.0)))
        qp, kp = [], []
        for c1, c2 in ((0, 1), (2, 3)):
            cv = jnp.where(lo, c1, c2)
            dl = cv - dg
            fac = jnp.where(dl == 0, 1.0, jnp.where(dl == 1, e[1], jnp.where(dl == 2, e[2],
                            jnp.where(dl == 3, e[3], 0.0))))
            kp.append((kd * fac).astype(BF16))
            qp.append(jnp.where(dg == cv, qd, 0.0).astype(BF16))
        s = lax.dot_general(jnp.concatenate(qp, axis=1), jnp.concatenate(kp, axis=1), _NT,
                            preferred_element_type=F32)
        tot = jnp.where((srow >> 2) == (scol >> 2), s, 0.0)

        for lev in range(1, 3):
            g = 4 ** lev
            par = 4 * g
            shape3 = (HS // par, par, LANE)
            rid = lax.broadcasted_iota(jnp.int32, shape3, 1)
            dg3 = rid >> (2 * lev)
            if rev:
                dg3 = 3 - dg3
            b3, q3, k3 = bd.reshape(shape3), qd.reshape(shape3), kd.reshape(shape3)
            lo3 = lo.reshape(1, 1, LANE)
            qp, kp = [], []
            for c1, c2 in ((1, 2), (3, None)):
                idx = lambda cc: (4 - cc) * g if rev else cc * g - 1
                i1 = idx(c1)
                i2 = idx(c2) if c2 is not None else i1
                ridx = jnp.where(lo3, i1, i2)
                ref = jnp.sum(jnp.where(rid == ridx, b3, 0.0), axis=1, keepdims=True)
                cvk = jnp.where(lo3, c1, c2 if c2 is not None else 0)
                cvq = jnp.where(lo3, c1, c2 if c2 is not None else -1)
                kk = jnp.where(dg3 < cvk, k3 * jnp.exp(jnp.minimum(ref - b3, 0.0)), 0.0)
                qq = jnp.where(dg3 == cvq, q3 * jnp.exp(jnp.minimum(b3 - ref, 0.0)), 0.0)
                kp.append(kk.reshape(HS, LANE).astype(BF16))
                qp.append(qq.reshape(HS, LANE).astype(BF16))
            s = lax.dot_general(jnp.concatenate(qp, axis=1), jnp.concatenate(kp, axis=1), _NT,
                                preferred_element_type=F32)
            sh = 2 * lev + 2
            tot = tot + jnp.where((srow >> sh) == (scol >> sh), s, 0.0)
        out.append(tot)
    return out


def _hgrn_head(q, k, lf, v, st_ref, rev):
    row = lax.broadcasted_iota(jnp.int32, (HS, LANE), 0)
    lane = lax.broadcasted_iota(jnp.int32, (1, LANE), 1)
    lo = lane < 64
    hi = jnp.logical_not(lo)
    grow = row & (HG - 1)
    in_g1 = row >= HG

    c = lf
    d = 1
    while d < HG:
        if rev:
            c = c + jnp.where(grow < HG - d, pltpu.roll(c, HS - d, 0), 0.0)
        else:
            c = c + jnp.where(grow >= d, pltpu.roll(c, d, 0), 0.0)
        d *= 2
    if rev:
        t0, t1 = c[0:1, :], c[HG:HG + 1, :]
    else:
        t0, t1 = c[HG - 1:HG, :], c[HS - 1:HS, :]
    et0, et1 = jnp.exp(t0), jnp.exp(t1)
    qe = q * jnp.exp(c)
    e_out = jnp.exp(jnp.where(in_g1, t1, t0) - c)
    ke = k * e_out

    if rev:
        qb = qe * jnp.where(in_g1, 1.0, et1)
        kh = ke * jnp.where(in_g1, et0, 1.0)
    else:
        qb = qe * jnp.where(in_g1, et0, 1.0)
        kh = ke * jnp.where(in_g1, 1.0, et1)
    st = st_ref[...]
    o_int = lax.dot_general(qb.astype(BF16), st.astype(BF16), _NT, preferred_element_type=F32)
    upd = lax.dot_general(v, kh.astype(BF16), (((0,), (0,)), ((), ())), preferred_element_type=F32)
    r128 = lax.broadcasted_iota(jnp.int32, (LANE, LANE), 0)
    c128 = lax.broadcasted_iota(jnp.int32, (LANE, LANE), 1)
    st_ref[...] = st * (et0 * et1) + jnp.where((r128 >> 6) == (c128 >> 6), upd, 0.0)

    q_late = in_g1 if not rev else jnp.logical_not(in_g1)
    q_top = jnp.where(q_late, qe, 0.0)
    k_top = jnp.where(q_late, 0.0, ke).astype(BF16)
    top = [lax.dot_general(jnp.where(m, q_top, 0.0).astype(BF16), k_top, _NT, preferred_element_type=F32)
           for m in (lo, hi)]
    return c, qe, top, o_int, jnp.minimum(t0, t1)


def _hgrn_steps(jobs, bot_ref):
    lane = lax.broadcasted_iota(jnp.int32, (1, LANE), 1)
    lo = lane < 64
    hi = jnp.logical_not(lo)
    heads = [_hgrn_head(*job) for job in jobs]
    tmin = heads[0][4]
    for h in heads[1:]:
        tmin = jnp.minimum(tmin, h[4])
    fast = jnp.min(tmin) >= -FAST_DECAY_LIMIT

    @pl.when(fast)
    def _():
        srow = lax.broadcasted_iota(jnp.int32, (HS, HS), 0)
        scol = lax.broadcasted_iota(jnp.int32, (HS, HS), 1)
        same = (srow >> 6) == (scol >> 6)
        for ji, (job, (c, qe, _, _, _)) in enumerate(zip(jobs, heads)):
            keep = same & ((scol >= srow) if job[5] else (scol <= srow))
            kf = (job[1] * jnp.exp(-c)).astype(BF16)
            for parity, m in enumerate((lo, hi)):
                s = lax.dot_general(jnp.where(m, qe, 0.0).astype(BF16), kf, _NT, preferred_element_type=F32)
                bot_ref[ji, parity] = jnp.where(keep, s, 0.0)

    @pl.when(jnp.logical_not(fast))
    def _():
        for ji, (job, (c, _, _, _, _)) in enumerate(zip(jobs, heads)):
            for parity, s in enumerate(_hgrn_bottom_exact(job[0], job[1], c, lo, job[5])):
                bot_ref[ji, parity] = s

    outs = []
    for ji, (job, (_, _, top, o_int, _)) in enumerate(zip(jobs, heads)):
        v = job[3]
        probs = jnp.concatenate([(bot_ref[ji, 0] + top[0]).astype(BF16), (bot_ref[ji, 1] + top[1]).astype(BF16)],
                                axis=1)
        vv = jnp.concatenate([jnp.where(lo, v, jnp.zeros_like(v)), jnp.where(hi, v, jnp.zeros_like(v))], axis=0)
        outs.append(jnp.dot(probs, vv, preferred_element_type=F32) + o_int)
    return outs


PAIRS_PER_ITER = 2


def _hgrn_kernel(has_s0, nt, *refs):
    if has_s0:
        (qf_ref, vf_ref, kf_ref, lf_ref, qb_ref, vb_ref, kb_ref, lb_ref, s0_ref,
         of_ref, ob_ref, so_ref, st_scr, bot_scr) = refs
    else:
        (qf_ref, vf_ref, kf_ref, lf_ref, qb_ref, vb_ref, kb_ref, lb_ref,
         of_ref, ob_ref, so_ref, st_scr, bot_scr) = refs
    j = pl.program_id(1)

    @pl.when(j == 0)
    def _():
        if has_s0:
            st_scr[...] = s0_ref[...]
        else:
            st_scr[...] = jnp.zeros_like(st_scr)

    n_sub = HL // HS

    def pair_body(it, carry):
        for step in range(n_sub):
            jobs, dests = [], []
            for pp in range(PAIRS_PER_ITER):
                p = it * PAIRS_PER_ITER + pp
                cols = pl.ds(pl.multiple_of(p * LANE, LANE), LANE)
                rf = pl.ds(step * HS, HS)
                rb = pl.ds((n_sub - 1 - step) * HS, HS)
                jobs.append((qf_ref[rf, cols].astype(F32), kf_ref[rf, cols], lf_ref[rf, cols], vf_ref[rf, cols],
                             st_scr.at[p, 0], False))
                dests.append((of_ref, rf, cols))
                jobs.append((qb_ref[rb, cols].astype(F32), kb_ref[rb, cols], lb_ref[rb, cols], vb_ref[rb, cols],
                             st_scr.at[p, 1], True))
                dests.append((ob_ref, rb, cols))
            for (ref, rows, cols), o in zip(dests, _hgrn_steps(jobs, bot_scr)):
                ref[rows, cols] = o
        return carry

    lax.fori_loop(0, N_PAIRS // PAIRS_PER_ITER, pair_body, 0)

    @pl.when(j == nt - 1)
    def _():
        so_ref[...] = st_scr[...]


def _hgrn(qc, vc, kff, lff, kfb, lfb, s0=None, prev=None):
    if s0 is None:
        nb, nt, blk0 = BATCH, SEQ // HL, 0
    else:
        nb, nt, blk0 = DEC_BATCH, DEC_SEQ // HL, T_CTX // HL
    fspec = pl.BlockSpec((HL, 512), lambda b, j: (blk0 + b * nt + j, 0))
    bspec = pl.BlockSpec((HL, 512), lambda b, j: (blk0 + b * nt + nt - 1 - j, 0))
    sspec = pl.BlockSpec((None, N_PAIRS, 2, LANE, LANE), lambda b, j: (b, 0, 0, 0, 0))
    in_specs = [fspec] * 4 + [bspec] * 4
    args = [qc, vc, kff, lff, qc, vc, kfb, lfb]
    aliases = {}
    if s0 is not None:
        in_specs += [sspec]
        args += [s0]
        in_specs += [pl.BlockSpec(memory_space=pl.ANY)] * 2
        args += list(prev)
        aliases = {len(args) - 2: 0, len(args) - 1: 1}

    def body(*refs):
        if s0 is not None:
            refs = refs[:9] + refs[11:]
        _hgrn_kernel(s0 is not None, nt, *refs)

    return pl.pallas_call(
        body,
        grid=(nb, nt),
        in_specs=in_specs,
        out_specs=[fspec, bspec, sspec],
        out_shape=[jax.ShapeDtypeStruct((T_ALL, 512), F32), jax.ShapeDtypeStruct((T_ALL, 512), F32),
                   jax.ShapeDtypeStruct((nb, N_PAIRS, 2, LANE, LANE), F32)],
        scratch_shapes=[pltpu.VMEM((N_PAIRS, 2, LANE, LANE), F32),
                        pltpu.VMEM((2 * PAIRS_PER_ITER, 2, HS, HS), F32)],
        input_output_aliases=aliases,
        compiler_params=_cparams(("parallel", "arbitrary")),
        name="hgrn_ctx" if s0 is None else "hgrn_lat",
    )(*args)


def _merge_kernel(x_ref, mod_ref, n1_ref, wg_ref, oa_ref, ob_ref, ocf_ref, ocb_ref, sg_ref, con_ref, ones_ref,
                  wbr_ref, wout_ref, n2_ref, wr_ref, br_ref, x1_o, h2_o, comb_o):
    x = x_ref[...]
    mod = mod_ref[...]
    xn = x * lax.rsqrt(jnp.mean(x * x, axis=-1, keepdims=True) + EPS) * n1_ref[...]
    h = (xn * (1.0 + mod[:, D:2 * D]) + mod[:, 0:D]).astype(BF16)

    oc = ocf_ref[...] + ocb_ref[...]
    oc = oc * lax.rsqrt(_group_mean(oc * oc, ones_ref[...], DV_C) + EPS) * con_ref[...]
    oc = (oc * sg_ref[...].astype(F32)).astype(BF16)
    branches = (oa_ref[...], ob_ref[...], oc)
    mix = None
    for jb in range(3):
        gate = jax.nn.sigmoid(jnp.dot(h, wg_ref[:, jb * D:(jb + 1) * D], preferred_element_type=F32))
        t = gate * jnp.dot(branches[jb], wbr_ref[jb], preferred_element_type=F32)
        mix = t if mix is None else mix + t
    out = jnp.dot(mix.astype(BF16), wout_ref[...], preferred_element_type=F32)
    x1 = x + mod[:, 2 * D:3 * D] * out
    x1_o[...] = x1

    x1n = x1 * lax.rsqrt(jnp.mean(x1 * x1, axis=-1, keepdims=True) + EPS) * n2_ref[...]
    h2 = x1n * (1.0 + mod[:, 4 * D:5 * D]) + mod[:, 3 * D:4 * D]
    h2_o[...] = h2.astype(BF16)

    logits = jnp.dot(h2, wr_ref[...], preferred_element_type=F32, precision=lax.Precision.HIGHEST) + br_ref[...]
    lane = lax.broadcasted_iota(jnp.int32, logits.shape, 1)
    neg = -jnp.inf
    is_g = lane < N_GROUPS
    gl = jnp.where(is_g, logits, neg)
    gmax = gl.max(axis=-1, keepdims=True)
    gidx = jnp.min(jnp.where(gl == gmax, lane, LANE), axis=-1, keepdims=True)
    gp = 1.0 / jnp.sum(jnp.where(is_g, jnp.exp(gl - gmax), 0.0), axis=-1, keepdims=True)
    eid = lane - N_GROUPS
    in_grp = (eid >= 0) & (eid < N_EXPERTS) & ((eid >> 2) == gidx)
    el = jnp.where(in_grp, logits, neg)
    v1 = el.max(axis=-1, keepdims=True)
    i1 = jnp.min(jnp.where(el == v1, lane, LANE), axis=-1, keepdims=True)
    el2 = jnp.where(lane == i1, neg, el)
    v2 = el2.max(axis=-1, keepdims=True)
    i2 = jnp.min(jnp.where(el2 == v2, lane, LANE), axis=-1, keepdims=True)
    e2 = jnp.exp(v2 - v1)
    w1 = gp / (1.0 + e2)
    w2 = gp * e2 / (1.0 + e2)
    comb = jnp.where(lane == i1, w1, 0.0) + jnp.where(lane == i2, w2, 0.0)
    comb_o[...] = pltpu.roll(comb, LANE - N_GROUPS, 1)


def _merge(x, mod, n1, wgate, oa, ob, ocf, ocb, sg, con, ones, wbr, wout, n2, wr, br):
    return pl.pallas_call(
        _merge_kernel,
        grid=(N_TILES,),
        in_specs=[
            _tile_spec(D),
            pl.BlockSpec((None, 1, 6 * D), lambda i: (_mod_row(i), 0, 0)),
            _const_spec((1, D)), _const_spec((D, 3 * D)),
            _tile_spec(512), _tile_spec(512), _tile_spec(512), _tile_spec(512), _tile_spec(512),
            _const_spec((1, 512)), _const_spec((256, 256)),
            _const_spec((3, BRANCH_W, D)), _const_spec((D, D)), _const_spec((1, D)),
            _const_spec((D, LANE)), _const_spec((1, LANE)),
        ],
        out_specs=[_tile_spec(D), _tile_spec(D), _tile_spec(LANE)],
        out_shape=[jax.ShapeDtypeStruct((T_ALL, D), F32), jax.ShapeDtypeStruct((T_ALL, D), BF16),
                   jax.ShapeDtypeStruct((T_ALL, LANE), F32)],
        compiler_params=_cparams(("parallel",)),
        name="merge",
    )(x, mod.reshape(MOD_ROWS, 1, 6 * D), n1, wgate, oa, ob, ocf, ocb, sg, con, ones, wbr, wout, n2, wr, br)


NB = 2048
N_BLK = T_ALL // NB
SBK = 256
N_SB = NB // SBK
WIN = 64
FT = 128
SEG_ALIGN = 16
STG = 2 * NB + N_EXPERTS * SEG_ALIGN + 256
QUAD = 4


def _route_kernel(comb_ref, tri_ref, upper_ref, rank_ref, carry_ref, tab_ref, carry_scr):
    s = pl.program_id(1)

    @pl.when(s == 0)
    def _():
        carry_scr[...] = jnp.zeros_like(carry_scr)

    routed = comb_ref[...] > 0.0
    ind = jnp.where(routed, 1.0, 0.0)
    carry = carry_scr[...]
    rank = jnp.dot(tri_ref[...], ind.astype(BF16), preferred_element_type=F32) + carry
    rank_ref[...] = jnp.where(routed, rank, -1.0)
    carry_ref[...] = carry
    count = carry + jnp.sum(ind, axis=0, keepdims=True)
    carry_scr[...] = count

    @pl.when(s == N_SB - 1)
    def _():
        seg = jnp.floor((count + (SEG_ALIGN - 1.0)) * (1.0 / SEG_ALIGN)) * SEG_ALIGN
        off = jnp.dot(jnp.broadcast_to(seg, (8, LANE)), upper_ref[...], preferred_element_type=F32,
                      precision=lax.Precision.HIGHEST)
        tab_ref[0:1, :] = count
        tab_ref[1:2, :] = off[0:1, :]


def _route(comb):
    tri = jnp.asarray(np.tril(np.ones((SBK, SBK)), -1), BF16)
    upper = jnp.asarray(np.triu(np.ones((LANE, LANE)), 1), F32)
    return pl.pallas_call(
        _route_kernel,
        grid=(N_BLK, N_SB),
        in_specs=[
            pl.BlockSpec((SBK, LANE), lambda b, s: (b * N_SB + s, 0)),
            pl.BlockSpec((SBK, SBK), lambda b, s: (0, 0)),
            pl.BlockSpec((LANE, LANE), lambda b, s: (0, 0)),
        ],
        out_specs=[
            pl.BlockSpec((SBK, LANE), lambda b, s: (b * N_SB + s, 0)),
            pl.BlockSpec((None, None, 1, LANE), lambda b, s: (b, s, 0, 0)),
            pl.BlockSpec((None, 2, LANE), lambda b, s: (b, 0, 0)),
        ],
        out_shape=[jax.ShapeDtypeStruct((T_ALL, LANE), F32),
                   jax.ShapeDtypeStruct((N_BLK, N_SB, 1, LANE), F32),
                   jax.ShapeDtypeStruct((N_BLK, 2, LANE), F32)],
        scratch_shapes=[pltpu.VMEM((1, LANE), F32)],
        compiler_params=_cparams(("parallel", "arbitrary")),
        name="moe_route",
    )(comb, tri, upper)


def _moe_kernel(cnt_s, off_s, car_s, h2_ref, rank_ref, comb_ref, offv_ref, eg_ref, eu_ref, ed_ref, o_ref,
                stin_ref, stout_ref, acc_ref):
    blk = pl.program_id(0)
    e = pl.program_id(1)
    lane = lax.broadcasted_iota(jnp.int32, (1, LANE), 1)
    lane_lo = (lane & (2 * WIN - 1)) < WIN
    slot = (lane & (WIN - 1)).astype(F32)
    srow = lax.broadcasted_iota(jnp.int32, (WIN, SBK), 0).astype(F32)

    def windows(s):
        out = []
        for ex in range(N_EXPERTS):
            start = off_s[blk, ex] + car_s[blk, s, ex]
            length = car_s[blk, s + 1, ex] - car_s[blk, s, ex]
            ws = (start // SEG_ALIGN) * SEG_ALIGN
            out.append((ws, (start - ws + length + (WIN - 1)) // WIN))
        return out, functools.reduce(jnp.maximum, [w[1] for w in out])

    def positions(s):
        rows = pl.ds(pl.multiple_of(s * SBK, SBK), SBK)
        rank = rank_ref[rows, :]
        return rows, jnp.where(rank >= 0.0, rank + offv_ref[1:2, :], -1.0e6)

    @pl.when(e == 0)
    def _():
        stin_ref[...] = jnp.zeros_like(stin_ref)
        stout_ref[...] = jnp.zeros_like(stout_ref)

        def sub_body(s, carry):
            rows, pos = positions(s)
            pos_t = pos.T
            h2 = h2_ref[rows, :]
            wins, nmax = windows(s)

            def chunk_body(c, carry2):
                for quad in range(N_EXPERTS // QUAD):
                    blocks = []
                    for ex in range(quad * QUAD, (quad + 1) * QUAD):
                        base = (wins[ex][0] + c * WIN).astype(F32)
                        hit = (pos_t[ex:ex + 1, :] - base) == srow
                        blocks.append(jnp.where(hit, 1.0, 0.0).astype(BF16))
                    moved = jnp.dot(jnp.concatenate(blocks, axis=0), h2, preferred_element_type=F32).astype(BF16)
                    for i in range(QUAD):
                        ws, nch = wins[quad * QUAD + i]

                        @pl.when(c < nch)
                        def _():
                            dst = pl.ds(pl.multiple_of(ws + c * WIN, SEG_ALIGN), WIN)
                            stin_ref[dst, :] = stin_ref[dst, :] + moved[i * WIN:(i + 1) * WIN, :]
                return carry2

            lax.fori_loop(0, nmax, chunk_body, 0)
            return carry

        lax.fori_loop(0, N_SB, sub_body, 0)

    def ffn_tile(first, n_rows):
        rows = pl.ds(pl.multiple_of(first, SEG_ALIGN), n_rows)
        xs = stin_ref[rows, :]
        hg = jnp.dot(xs, eg_ref[...], preferred_element_type=F32)
        hu = jnp.dot(xs, eu_ref[...], preferred_element_type=F32)
        act = (hg * jax.nn.sigmoid(hg) * hu).astype(BF16)
        stout_ref[rows, :] = jnp.dot(act, ed_ref[...], preferred_element_type=F32).astype(BF16)

    count = cnt_s[blk, e]
    seg0 = off_s[blk, e]
    n_big = (count + (FT - 1)) // (2 * FT)

    def ffn_body(t, carry):
        ffn_tile(seg0 + t * (2 * FT), 2 * FT)
        return carry

    lax.fori_loop(0, n_big, ffn_body, 0)

    @pl.when(count > n_big * (2 * FT))
    def _():
        ffn_tile(seg0 + n_big * (2 * FT), FT)

    @pl.when(e == N_EXPERTS - 1)
    def _():
        def sub_body(s, carry):
            rows, pos = positions(s)
            wts = comb_ref[rows, :]
            wins, nmax = windows(s)
            acc_ref[...] = jnp.zeros_like(acc_ref)

            def chunk_body(c, carry2):
                sel, srcs = [], []
                for ex0 in range(0, N_EXPERTS, 2):
                    b0 = (wins[ex0][0] + c * WIN).astype(F32)
                    b1 = (wins[ex0 + 1][0] + c * WIN).astype(F32)
                    rel = jnp.where(lane_lo, pos[:, ex0:ex0 + 1] - b0, pos[:, ex0 + 1:ex0 + 2] - b1)
                    w = jnp.where(lane_lo, wts[:, ex0:ex0 + 1], wts[:, ex0 + 1:ex0 + 2])
                    sel.append(jnp.where(rel == slot, w, 0.0).astype(BF16))
                for ex in range(N_EXPERTS):
                    first = jnp.minimum(wins[ex][0] + c * WIN, STG - WIN)
                    srcs.append(stout_ref[pl.ds(pl.multiple_of(first, SEG_ALIGN), WIN), :])
                acc_ref[...] += jnp.dot(jnp.concatenate(sel, axis=1), jnp.concatenate(srcs, axis=0),
                                        preferred_element_type=F32)
                return carry2

            lax.fori_loop(0, nmax, chunk_body, 0)
            o_ref[rows, :] = acc_ref[...].astype(BF16)
            return carry

        lax.fori_loop(0, N_SB, sub_body, 0)


def _moe(h2, comb, eg, eu, ed):
    rank, carry, tab = _route(comb)
    cnt_i = tab[:, 0, :N_EXPERTS].astype(jnp.int32)
    off_i = tab[:, 1, :N_EXPERTS].astype(jnp.int32)
    car_i = jnp.concatenate([carry[:, :, 0, :N_EXPERTS].astype(jnp.int32), cnt_i[:, None, :]], axis=1)
    bspec = lambda w: pl.BlockSpec((NB, w), lambda b, e, *_: (b, 0))
    bspec1 = lambda w: pl.BlockSpec((NB, w), lambda b, e, *_: (b, 0), pipeline_mode=pl.Buffered(1))
    grid_spec = pltpu.PrefetchScalarGridSpec(
        num_scalar_prefetch=3,
        grid=(N_BLK, N_EXPERTS),
        in_specs=[
            bspec1(D), bspec1(LANE), bspec1(LANE),
            pl.BlockSpec((None, 2, LANE), lambda b, e, *_: (b, 0, 0)),
            pl.BlockSpec((None, D, D_EXPERT), lambda b, e, *_: (e, 0, 0)),
            pl.BlockSpec((None, D, D_EXPERT), lambda b, e, *_: (e, 0, 0)),
            pl.BlockSpec((None, D_EXPERT, D), lambda b, e, *_: (e, 0, 0)),
        ],
        out_specs=bspec(D),
        scratch_shapes=[pltpu.VMEM((STG, D), BF16), pltpu.VMEM((STG, D), BF16), pltpu.VMEM((SBK, D), F32)],
    )
    return pl.pallas_call(
        _moe_kernel,
        grid_spec=grid_spec,
        out_shape=jax.ShapeDtypeStruct((T_ALL, D), BF16),
        compiler_params=_cparams(("parallel", "arbitrary")),
        name="moe",
    )(cnt_i, off_i, car_i, h2, rank, comb, tab, eg, eu, ed)


def _residual_kernel(final, x1_ref, moe_ref, mod_ref, fg_ref, o_ref):
    x2 = x1_ref[...] + mod_ref[:, 5 * D:6 * D] * moe_ref[...].astype(F32)
    if final:
        x2 = x2 * lax.rsqrt(jnp.mean(x2 * x2, axis=-1, keepdims=True) + EPS) * fg_ref[...]
    o_ref[...] = x2


def _residual(final, x1, moe, mod, fg):
    return pl.pallas_call(
        functools.partial(_residual_kernel, final),
        grid=(N_TILES,),
        in_specs=[_tile_spec(D), _tile_spec(D),
                  pl.BlockSpec((None, 1, 6 * D), lambda i: (_mod_row(i), 0, 0)), _const_spec((1, D))],
        out_specs=_tile_spec(D),
        out_shape=jax.ShapeDtypeStruct((T_ALL, D), F32),
        compiler_params=_cparams(("parallel",)),
        name="residual",
    )(x1, moe, mod.reshape(MOD_ROWS, 1, 6 * D), fg)


def _state_to_blockdiag(s):
    st = jnp.swapaxes(s, -1, -2).reshape(s.shape[0], H_C // 2, 2, DV_C, DK_C)
    z = jnp.zeros_like(st[:, :, 0])
    top = jnp.concatenate([st[:, :, 0], z], axis=-1)
    bot = jnp.concatenate([z, st[:, :, 1]], axis=-1)
    return jnp.concatenate([top, bot], axis=-2)


def _blockdiag_to_state(sb):
    even = sb[:, :, :DV_C, :DK_C]
    odd = sb[:, :, DV_C:, DK_C:]
    st = jnp.stack([even, odd], axis=2).reshape(sb.shape[0], H_C, DV_C, DK_C)
    return jnp.swapaxes(st, -1, -2)


def kernel(x_prompt, x_sample, cache_gqa_k, cache_gqa_v, cache_mla_ckv, cache_mla_krope, state_hgrn, c, c_ctx,
           w_mod, b_mod, norm1_g, norm2_g, w_in, a_qnorm, a_knorm, b_qnorm, b_wq, b_kvnorm, b_wkv, c_lb_logits,
           c_onorm, w_branch, w_out, r_group_w, r_group_b, r_expert_w, r_expert_b, e_gate, e_up, e_down, final_g):
    x = jnp.concatenate([x_prompt.reshape(T_CTX, D), x_sample.reshape(T_LAT, D)], axis=0)
    cvec = jnp.concatenate([c, c_ctx[None, :], jnp.zeros((MOD_ROWS - DEC_BATCH - 1, D), F32)], axis=0)
    mod_all = _mod_table(cvec, w_mod, b_mod)
    taba, tabb, tabk = _rope_tables()
    ones = _ones_block(256, 64)
    lbl = c_lb_logits.reshape(DEPTH, 2, H_C * DK_C)
    zpad = jnp.zeros((DEC_BATCH * PAST, 32), F32)

    new_k, new_v, new_ckv, new_kr, new_s = [], [], [], [], []
    for l in range(DEPTH):
        mod = mod_all[l]
        w_in_p, w_gate = _pack_w_in(w_in[l])
        wq_p, wkv_p = _pack_wq(b_wq[l]), _pack_wkv(b_wkv[l])
        (qa, kan, ka, va, qb, ckv, kvb, kr, kre, lff, lfb, kff, kfb, qc, vc, sg) = _inproj(
            l, x, mod, norm1_g[l][None, :], w_in_p, jnp.tile(a_qnorm[l], H_A)[None, :],
            jnp.tile(a_knorm[l], KV_A)[None, :], b_qnorm[l][None, :], wq_p, b_kvnorm[l][None, :], wkv_p,
            lbl, ones, taba, tabb, tabk)
        new_k.append(kan[:T_CTX].reshape(BATCH, SEQ, KV_A, HD_A))
        new_v.append(va[:T_CTX].reshape(BATCH, SEQ, KV_A, HD_A))
        new_ckv.append(ckv[:T_CTX].reshape(BATCH, SEQ, KV_RANK))
        new_kr.append(kr[:T_CTX, :ROPE_B].reshape(BATCH, SEQ, ROPE_B))

        ck = cache_gqa_k[:, l].reshape(DEC_BATCH * PAST, KV_A * HD_A)
        cv = cache_gqa_v[:, l].reshape(DEC_BATCH * PAST, KV_A * HD_A)
        ckvb = _ctx_kv(cache_mla_ckv[:, l].reshape(DEC_BATCH * PAST, KV_RANK), wkv_p)
        ckr = cache_mla_krope[:, l].reshape(DEC_BATCH * PAST, ROPE_B)
        ckre = jnp.concatenate([ckr, zpad, ckr, zpad], axis=1)
        oa, ob = _attention(qa, qb, ka, va, kvb, kre)
        oa, ob = _attention(qa, qb, ka, va, kvb, kre, prev=(oa, ob), cache=(ck, cv, ckvb, ckre))

        ocf, ocb, s_ctx = _hgrn(qc, vc, kff, lff, kfb, lfb)
        s0 = jnp.stack([_state_to_blockdiag(state_hgrn[:, l, 0]), _state_to_blockdiag(state_hgrn[:, l, 1])], axis=2)
        ocf, ocb, _ = _hgrn(qc, vc, kff, lff, kfb, lfb, s0=s0, prev=(ocf, ocb))
        new_s.append(jnp.stack([_blockdiag_to_state(s_ctx[:, :, 0]), _blockdiag_to_state(s_ctx[:, :, 1])], axis=1))

        wr = jnp.concatenate([r_group_w[l], r_expert_w[l], jnp.zeros((D, LANE - N_GROUPS - N_EXPERTS), F32)], axis=1)
        br = jnp.concatenate([r_group_b[l], r_expert_b[l], jnp.zeros((LANE - N_GROUPS - N_EXPERTS,), F32)])[None, :]
        x1, h2, comb = _merge(x, mod, norm1_g[l][None, :], w_gate, oa, ob, ocf, ocb, sg,
                              jnp.tile(c_onorm[l], H_C)[None, :], ones, w_branch[l].astype(BF16),
                              w_out[l].astype(BF16), norm2_g[l][None, :], wr, br)
        moe = _moe(h2, comb, e_gate[l].astype(BF16), e_up[l].astype(BF16), e_down[l].astype(BF16))
        x = _residual(l == DEPTH - 1, x1, moe, mod, final_g[None, :])

    y_prompt = x[:T_CTX].reshape(BATCH, SEQ, D)
    y_sample = x[T_CTX:].reshape(DEC_BATCH, DEC_SEQ, D)
    return (y_prompt, y_sample, jnp.stack(new_k, axis=1), jnp.stack(new_v, axis=1), jnp.stack(new_ckv, axis=1),
            jnp.stack(new_kr, axis=1), jnp.stack(new_s, axis=1))
```

```python
import functools

import numpy as np
import jax
import jax.numpy as jnp
from jax import lax
from jax.experimental import pallas as pl
from jax.experimental.pallas import tpu as pltpu

D = 1024
BATCH, SEQ = 32, 256
DEC_BATCH, DEC_SEQ = 8, 1024
PAST = 256
DEPTH = 2
GRID_W = 64
THETA = 10000.0
EPS = 1e-6
F_FLOOR = 1e-30
H_A, KV_A, HD_A = 8, 2, 64
H_B, Q_RANK, KV_RANK, NOPE_B, ROPE_B, V_B = 8, 384, 256, 64, 32, 64
H_C, DK_C, DV_C = 8, 64, 64
BRANCH_W = 512
N_GROUPS, E_PER_GROUP, N_EXPERTS, D_EXPERT = 4, 4, 16, 512

T_CTX = BATCH * SEQ
T_LAT = DEC_BATCH * DEC_SEQ
T_ALL = T_CTX + T_LAT
TM = 256
N_TILES = T_ALL // TM
CTX_TILES = T_CTX // TM
LAT_TILES_PER_SEQ = DEC_SEQ // TM
MOD_ROWS = 16
CTX_MOD_ROW = DEC_BATCH
LANE = 128
VMEM_LIMIT = 56 * 1024 * 1024

C_QA, C_KA, C_VA, C_QRA, C_KVA, C_KR = 0, 512, 640, 768, 1152, 1408
C_FF, C_FB, C_QC, C_IC, C_GC, C_END = 1536, 2048, 2560, 3072, 3584, 4096
R_QA, R_KA, R_VA, R_QRA, R_KVA, R_KR = 0, 512, 640, 768, 1152, 1408
R_FF, R_FB, R_QC, R_IC, R_GC, R_GATE, R_END = 1440, 1952, 2464, 2976, 3488, 4000, 7072

F32 = jnp.float32
BF16 = jnp.bfloat16


def _cparams(sem):
    return pltpu.CompilerParams(dimension_semantics=sem, vmem_limit_bytes=VMEM_LIMIT)


def _mod_row(i):
    return jnp.where(i < CTX_TILES, CTX_MOD_ROW, (i - CTX_TILES) // LAT_TILES_PER_SEQ)


def _pos_block(i):
    return jnp.where(i < CTX_TILES, LAT_TILES_PER_SEQ, (i - CTX_TILES) % LAT_TILES_PER_SEQ)


def _split_hi_lo(x):
    hi = x.astype(BF16)
    lo = (x - hi.astype(F32)).astype(BF16)
    return hi, lo


def _group_mean(x2, ones_blk, width):
    n = ones_blk.shape[0]
    outs = []
    for j in range(x2.shape[-1] // n):
        blk = x2[:, j * n:(j + 1) * n]
        hi, lo = _split_hi_lo(blk)
        s = jnp.dot(hi, ones_blk, preferred_element_type=F32) + jnp.dot(lo, ones_blk, preferred_element_type=F32)
        outs.append(s)
    s = outs[0] if len(outs) == 1 else jnp.concatenate(outs, axis=-1)
    return s * (1.0 / width)


def _rope(x, tab_ref, shift, period):
    c, s1, s2 = tab_ref[0], tab_ref[1], tab_ref[2]
    outs = []
    for j in range(x.shape[-1] // period):
        blk = x[:, j * period:(j + 1) * period]
        outs.append(blk * c + pltpu.roll(blk, shift, 1) * s1 + pltpu.roll(blk, period - shift, 1) * s2)
    return outs[0] if len(outs) == 1 else jnp.concatenate(outs, axis=-1)


def _mod_kernel(c_ref, w_ref, b_ref, o_ref):
    c = c_ref[...]
    a = c * jax.nn.sigmoid(c)
    o_ref[...] = jnp.dot(a, w_ref[...], preferred_element_type=F32, precision=lax.Precision.HIGHEST) + b_ref[...]


def _mod_table(cvec, w_mod, b_mod):
    nt = 1024
    return pl.pallas_call(
        _mod_kernel,
        grid=(DEPTH, 6 * D // nt),
        in_specs=[
            pl.BlockSpec((MOD_ROWS, D), lambda l, j: (0, 0)),
            pl.BlockSpec((None, D, nt), lambda l, j: (l, 0, j)),
            pl.BlockSpec((None, 1, nt), lambda l, j: (l, 0, j)),
        ],
        out_specs=pl.BlockSpec((None, MOD_ROWS, nt), lambda l, j: (l, 0, j)),
        out_shape=jax.ShapeDtypeStruct((DEPTH, MOD_ROWS, 6 * D), F32),
        compiler_params=_cparams(("arbitrary", "arbitrary")),
        name="mod_table",
    )(cvec, w_mod, b_mod.reshape(DEPTH, 1, 6 * D))


def _inproj_kernel(layer, x_ref, mod_ref, n1_ref, w_ref, aq_ref, ak_ref, bq_ref, wq_ref, bkv_ref, wkv_ref,
                   lbl_ref, ones_ref, taba_ref, tabb_ref, tabk_ref,
                   qa_o, kan_o, ka_o, va_o, qb_o, ckv_o, kvb_o, kr_o, kre_o,
                   lff_o, lfb_o, kff_o, kfb_o, qc_o, vc_o, sg_o):
    x = x_ref[...]
    mod = mod_ref[...]
    xn = x * lax.rsqrt(jnp.mean(x * x, axis=-1, keepdims=True) + EPS) * n1_ref[...]
    h = (xn * (1.0 + mod[:, D:2 * D]) + mod[:, 0:D]).astype(BF16)
    y = jnp.dot(h, w_ref[...], preferred_element_type=F32)
    ones = ones_ref[...]

    qa = y[:, C_QA:C_KA]
    qa = qa * lax.rsqrt(_group_mean(qa * qa, ones, HD_A) + EPS) * aq_ref[...]
    qa_o[...] = (_rope(qa, taba_ref, 16, LANE) * (HD_A ** -0.5)).astype(BF16)
    ka = y[:, C_KA:C_VA]
    ka = ka * lax.rsqrt(_group_mean(ka * ka, ones[:LANE, :LANE], HD_A) + EPS) * ak_ref[...]
    kan_o[...] = ka
    ka_o[...] = _rope(ka, taba_ref, 16, LANE).astype(BF16)
    va_o[...] = y[:, C_VA:C_QRA]

    qr = y[:, C_QRA:C_KVA]
    qr = qr * lax.rsqrt(jnp.mean(qr * qr, axis=-1, keepdims=True) + EPS) * bq_ref[...]
    qb = jnp.dot(qr.astype(BF16), wq_ref[...], preferred_element_type=F32)
    qb_o[...] = (_rope(qb, tabb_ref, 8, 2 * LANE) * ((NOPE_B + ROPE_B) ** -0.5)).astype(BF16)
    kv = y[:, C_KVA:C_KR]
    ckv = kv * lax.rsqrt(jnp.mean(kv * kv, axis=-1, keepdims=True) + EPS) * bkv_ref[...]
    ckv_o[...] = ckv
    kvb_o[...] = jnp.dot(ckv.astype(BF16), wkv_ref[...], preferred_element_type=F32).astype(BF16)
    kr = y[:, C_KR:C_FF]
    kr_o[...] = kr
    kre_o[...] = _rope(kr, tabk_ref, 8, LANE).astype(BF16)

    lbl = lbl_ref[...]
    e = jnp.exp(lbl - jnp.max(lbl, axis=0, keepdims=True))
    p = e / jnp.sum(e, axis=0, keepdims=True)
    lb = p[0] * 0.0
    for i in range(1, layer + 1):
        lb = lb + p[i]
    for d, (c0, lf_o, kf_o) in enumerate(((C_FF, lff_o, kff_o), (C_FB, lfb_o, kfb_o))):
        pre = y[:, c0:c0 + 512]
        lbd = lb[d:d + 1, :]
        f = jnp.maximum(lbd + (1.0 - lbd) * jax.nn.sigmoid(pre), F_FLOOR)
        lf_o[...] = jnp.log(f)
        kf_o[...] = 1.0 - f
    qc_o[...] = y[:, C_QC:C_IC].astype(BF16)
    vc_o[...] = y[:, C_IC:C_GC].astype(BF16)
    gc = y[:, C_GC:C_END]
    sg_o[...] = (gc * jax.nn.sigmoid(gc)).astype(BF16)


def _const_spec(shape):
    nd = len(shape)
    return pl.BlockSpec(shape, lambda i: (0,) * nd)


def _tile_spec(width):
    return pl.BlockSpec((TM, width), lambda i: (i, 0))


def _inproj(layer, x, mod, n1, w_in_p, aq, ak, bq, wq_p, bkv, wkv_p, lbl, ones, taba, tabb, tabk):
    outs = [
        (512, BF16), (128, F32), (128, BF16), (128, F32), (1024, BF16), (256, F32), (1024, BF16),
        (128, F32), (128, BF16), (512, F32), (512, F32), (512, F32), (512, F32), (512, BF16), (512, BF16),
        (512, BF16),
    ]
    tab_spec = lambda w: pl.BlockSpec((3, TM, w), lambda i: (0, _pos_block(i), 0))
    return pl.pallas_call(
        functools.partial(_inproj_kernel, layer),
        grid=(N_TILES,),
        in_specs=[
            _tile_spec(D),
            pl.BlockSpec((None, 1, 6 * D), lambda i: (_mod_row(i), 0, 0)),
            _const_spec((1, D)),
            _const_spec((D, C_END)),
            _const_spec((1, 512)), _const_spec((1, 128)), _const_spec((1, Q_RANK)),
            _const_spec((Q_RANK, 1024)), _const_spec((1, KV_RANK)), _const_spec((KV_RANK, 1024)),
            _const_spec((DEPTH, 2, 512)), _const_spec((256, 256)),
            tab_spec(LANE), tab_spec(2 * LANE), tab_spec(LANE),
        ],
        out_specs=[_tile_spec(w) for w, _ in outs],
        out_shape=[jax.ShapeDtypeStruct((T_ALL, w), dt) for w, dt in outs],
        compiler_params=_cparams(("parallel",)),
        name="inproj",
    )(x, mod.reshape(MOD_ROWS, 1, 6 * D), n1, w_in_p, aq, ak, bq, wq_p, bkv, wkv_p, lbl, ones, taba, tabb, tabk)


def _pack_w_in(w):
    z = jnp.zeros((D, 32), w.dtype)
    kr = w[:, R_KR:R_FF]
    main = jnp.concatenate([w[:, :R_KR], kr, z, kr, z, w[:, R_FF:R_GATE]], axis=1)
    return main.astype(BF16), w[:, R_GATE:].astype(BF16)


def _pack_wq(wq):
    w = wq.reshape(Q_RANK, H_B, NOPE_B + ROPE_B)
    nope, rope = w[..., :NOPE_B], w[..., NOPE_B:]
    z = jnp.zeros((Q_RANK, H_B, 32), wq.dtype)
    even = jnp.concatenate([rope, z, nope], axis=-1)
    odd = jnp.concatenate([nope, rope, z], axis=-1)
    is_even = (jnp.arange(H_B) % 2 == 0)[None, :, None]
    return jnp.where(is_even, even, odd).reshape(Q_RANK, H_B * LANE).astype(BF16)


def _pack_wkv(wkv):
    w = wkv.reshape(KV_RANK, H_B, NOPE_B + V_B)
    nope, v = w[..., :NOPE_B], w[..., NOPE_B:]
    is_even = (jnp.arange(H_B) % 2 == 0)[None, :, None]
    return jnp.where(is_even, jnp.concatenate([v, nope], -1), jnp.concatenate([nope, v], -1)).reshape(
        KV_RANK, H_B * LANE).astype(BF16)


def _rope_tables():
    pos = np.arange(DEC_SEQ)
    row, col = pos // GRID_W, pos % GRID_W

    def pattern(half):
        quarter = half // 2
        inv = THETA ** (-np.arange(0, half, 2, dtype=np.float64) / half)
        ang = np.concatenate([row[:, None] * inv, row[:, None] * inv, col[:, None] * inv, col[:, None] * inv], 1)
        is_x2 = np.tile(np.concatenate([np.zeros(quarter), np.ones(quarter)]), 2)[None, :]
        c = np.cos(ang)
        s1 = np.sin(ang) * is_x2
        s2 = -np.sin(ang) * (1 - is_x2)
        return c, s1, s2

    def assemble(width, spans, half):
        c, s1, s2 = pattern(half)
        tc = np.ones((DEC_SEQ + TM, width))
        t1 = np.zeros((DEC_SEQ + TM, width))
        t2 = np.zeros((DEC_SEQ + TM, width))
        for start in spans:
            tc[:DEC_SEQ, start:start + 2 * half] = c
            t1[:DEC_SEQ, start:start + 2 * half] = s1
            t2[:DEC_SEQ, start:start + 2 * half] = s2
        return jnp.asarray(np.stack([tc, t1, t2]), F32)

    taba = assemble(LANE, (0, 64), 32)
    tabb = assemble(2 * LANE, (0, 128 + 64), 16)
    tabk = assemble(LANE, (0, 64), 16)
    return taba, tabb, tabk


def _ones_block(n, width):
    g = np.arange(n) // width
    return jnp.asarray(g[:, None] == g[None, :], BF16)


def _ctxkv_kernel(c_ref, w_ref, o_ref):
    o_ref[...] = jnp.dot(c_ref[...].astype(BF16), w_ref[...], preferred_element_type=F32).astype(BF16)


def _ctx_kv(ckv_cache, wkv_p):
    rows = ckv_cache.shape[0]
    return pl.pallas_call(
        _ctxkv_kernel,
        grid=(rows // TM,),
        in_specs=[_tile_spec(KV_RANK), _const_spec((KV_RANK, 1024))],
        out_specs=_tile_spec(1024),
        out_shape=jax.ShapeDtypeStruct((rows, 1024), BF16),
        compiler_params=_cparams(("parallel",)),
        name="ctx_kv",
    )(ckv_cache, wkv_p)


_NT = (((1,), (1,)), ((), ()))


def _softmax_pv(scores, values, parity):
    lane = lax.broadcasted_iota(jnp.int32, (1, LANE), 1)
    den_lane = 64 if parity == 0 else 0
    keep = (lane < 64) if parity == 0 else (lane >= 64)
    m = scores[0].max(axis=-1, keepdims=True)
    for s in scores[1:]:
        m = jnp.maximum(m, s.max(axis=-1, keepdims=True))
    acc = None
    for s, v in zip(scores, values):
        p = jnp.exp((s - m).astype(BF16))
        o = jnp.dot(p, jnp.where(lane == den_lane, jnp.ones_like(v), v), preferred_element_type=F32)
        acc = o if acc is None else acc + o
    return jnp.where(keep, acc / acc[:, den_lane:den_lane + 1], 0.0)


def _attn_kernel(n_pieces, qa_ref, qb_ref, *refs):
    kv_refs = refs[:4 * n_pieces]
    oa_ref, ob_ref = refs[-2], refs[-1]
    lane = lax.broadcasted_iota(jnp.int32, (1, LANE), 1)
    lo = lane < 64

    ka = [kv_refs[4 * i][...].astype(F32) for i in range(n_pieces)]
    va = [kv_refs[4 * i + 1][...].astype(F32) for i in range(n_pieces)]

    def place(x, g, parity):
        if g != parity:
            x = pltpu.roll(x, 64, 1)
        keep = lo if parity == 0 else jnp.logical_not(lo)
        return jnp.where(keep, x, 0.0).astype(BF16)

    placed = {(g, parity): ([place(k, g, parity) for k in ka], [place(v, g, parity) for v in va])
              for g in range(KV_A) for parity in range(2)}
    for pair in range(H_A // 2):
        g = (2 * pair) // (H_A // KV_A)
        q = qa_ref[:, pair * LANE:(pair + 1) * LANE]
        acc = None
        for parity in range(2):
            ks, vs = placed[(g, parity)]
            scores = [lax.dot_general(q, k, _NT, preferred_element_type=F32) for k in ks]
            o = _softmax_pv(scores, vs, parity)
            acc = o if acc is None else acc + o
        oa_ref[:, pair * LANE:(pair + 1) * LANE] = acc.astype(BF16)

    for pair in range(H_B // 2):
        acc = None
        for parity in range(2):
            h = 2 * pair + parity
            q = qb_ref[:, h * LANE:(h + 1) * LANE]
            nope = jnp.logical_not(lo) if parity == 0 else lo
            scores, vs = [], []
            for i in range(n_pieces):
                kvb = kv_refs[4 * i + 2][:, h * LANE:(h + 1) * LANE]
                kre = kv_refs[4 * i + 3][...].astype(BF16)
                k = jnp.where(nope, kvb, kre)
                scores.append(lax.dot_general(q, k, _NT, preferred_element_type=F32))
                vs.append(jnp.where(nope, jnp.zeros_like(kvb), kvb))
            o = _softmax_pv(scores, vs, parity)
            acc = o if acc is None else acc + o
        ob_ref[:, pair * LANE:(pair + 1) * LANE] = acc.astype(BF16)


def _attention(qa, qb, ka, va, kvb, kre, prev=None, cache=None):
    if cache is None:
        nb, nqt, nk, q_blk0, k_blk0 = BATCH, 1, SEQ, 0, 0
    else:
        nb, nqt, nk, q_blk0, k_blk0 = DEC_BATCH, DEC_SEQ // TM, DEC_SEQ, CTX_TILES, T_CTX // DEC_SEQ
    qspec = lambda w: pl.BlockSpec((TM, w), lambda b, j: (q_blk0 + b * nqt + j, 0))
    kspec = lambda w: pl.BlockSpec((nk, w), lambda b, j: (k_blk0 + b, 0))
    in_specs = [qspec(512), qspec(1024), kspec(128), kspec(128), kspec(1024), kspec(128)]
    args = [qa, qb, ka, va, kvb, kre]
    n_pieces = 1
    aliases = {}
    if cache is not None:
        cspec = lambda w: pl.BlockSpec((PAST, w), lambda b, j: (b, 0))
        in_specs += [cspec(128), cspec(128), cspec(1024), cspec(128)]
        args += list(cache)
        n_pieces = 2
        in_specs += [pl.BlockSpec(memory_space=pl.ANY)] * 2
        args += list(prev)
        aliases = {len(args) - 2: 0, len(args) - 1: 1}

    def body(*refs):
        if cache is not None:
            refs = refs[:2 + 4 * n_pieces] + refs[-2:]
        _attn_kernel(n_pieces, *refs)

    return pl.pallas_call(
        body,
        grid=(nb, nqt),
        in_specs=in_specs,
        out_specs=[qspec(512), qspec(512)],
        out_shape=[jax.ShapeDtypeStruct((T_ALL, 512), BF16)] * 2,
        input_output_aliases=aliases,
        compiler_params=_cparams(("parallel", "arbitrary")),
        name="attention_ctx" if cache is None else "attention_lat",
    )(*args)


HL = 256
HS = 128
HG = 64
N_PAIRS = H_C // 2
FAST_DECAY_LIMIT = 80.0


def _hgrn_bottom_exact(q, k, c, lo, rev):
    row = lax.broadcasted_iota(jnp.int32, (HS, LANE), 0)
    srow = lax.broadcasted_iota(jnp.int32, (HS, HS), 0)
    scol = lax.broadcasted_iota(jnp.int32, (HS, HS), 1)
    out = []
    for parity in range(2):
        def dup(x):
            xs = pltpu.roll(x, 64, 1)
            return jnp.where(lo, x, xs) if parity == 0 else jnp.where(lo, xs, x)
        qd, kd, bd = dup(q), dup(k), dup(c)

        dg = row & 3
        if rev:
            dg = 3 - dg
        e = [None]
        for delta in range(1, 4):
            shifted = pltpu.roll(bd, delta if rev else HS - delta, 0)
            e.append(jnp.exp(jnp.minimum(shifted - bd, 0.0)))
        qp, kp = [], []
        for c1, c2 in ((0, 1), (2, 3)):
            cv = jnp.where(lo, c1, c2)
            dl = cv - dg
            fac = jnp.where(dl == 0, 1.0, jnp.where(dl == 1, e[1], jnp.where(dl == 2, e[2],
                            jnp.where(dl == 3, e[3], 0.0))))
            kp.append((kd * fac).astype(BF16))
            qp.append(jnp.where(dg == cv, qd, 0.0).astype(BF16))
        s = lax.dot_general(jnp.concatenate(qp, axis=1), jnp.concatenate(kp, axis=1), _NT,
                            preferred_element_type=F32)
        tot = jnp.where((srow >> 2) == (scol >> 2), s, 0.0)

        for lev in range(1, 3):
            g = 4 ** lev
            par = 4 * g
            shape3 = (HS // par, par, LANE)
            rid = lax.broadcasted_iota(jnp.int32, shape3, 1)
            dg3 = rid >> (2 * lev)
            if rev:
                dg3 = 3 - dg3
            b3, q3, k3 = bd.reshape(shape3), qd.reshape(shape3), kd.reshape(shape3)
            lo3 = lo.reshape(1, 1, LANE)
            qp, kp = [], []
            for c1, c2 in ((1, 2), (3, None)):
                idx = lambda cc: (4 - cc) * g if rev else cc * g - 1
                i1 = idx(c1)
                i2 = idx(c2) if c2 is not None else i1
                ridx = jnp.where(lo3, i1, i2)
                ref = jnp.sum(jnp.where(rid == ridx, b3, 0.0), axis=1, keepdims=True)
                cvk = jnp.where(lo3, c1, c2 if c2 is not None else 0)
                cvq = jnp.where(lo3, c1, c2 if c2 is not None else -1)
                kk = jnp.where(dg3 < cvk, k3 * jnp.exp(jnp.minimum(ref - b3, 0.0)), 0.0)
                qq = jnp.where(dg3 == cvq, q3 * jnp.exp(jnp.minimum(b3 - ref, 0.0)), 0.0)
                kp.append(kk.reshape(HS, LANE).astype(BF16))
                qp.append(qq.reshape(HS, LANE).astype(BF16))
            s = lax.dot_general(jnp.concatenate(qp, axis=1), jnp.concatenate(kp, axis=1), _NT,
                                preferred_element_type=F32)
            sh = 2 * lev + 2
            tot = tot + jnp.where((srow >> sh) == (scol >> sh), s, 0.0)
        out.append(tot)
    return out


def _hgrn_head(q, k, lf, v, st_ref, rev):
    row = lax.broadcasted_iota(jnp.int32, (HS, LANE), 0)
    lane = lax.broadcasted_iota(jnp.int32, (1, LANE), 1)
    lo = lane < 64
    hi = jnp.logical_not(lo)
    grow = row & (HG - 1)
    in_g1 = row >= HG

    c = lf
    d = 1
    while d < HG:
        if rev:
            c = c + jnp.where(grow < HG - d, pltpu.roll(c, HS - d, 0), 0.0)
        else:
            c = c + jnp.where(grow >= d, pltpu.roll(c, d, 0), 0.0)
        d *= 2
    if rev:
        t0, t1 = c[0:1, :], c[HG:HG + 1, :]
    else:
        t0, t1 = c[HG - 1:HG, :], c[HS - 1:HS, :]
    et0, et1 = jnp.exp(t0), jnp.exp(t1)
    qe = q * jnp.exp(c)
    e_out = jnp.exp(jnp.where(in_g1, t1, t0) - c)
    ke = k * e_out

    if rev:
        qb = qe * jnp.where(in_g1, 1.0, et1)
        kh = ke * jnp.where(in_g1, et0, 1.0)
    else:
        qb = qe * jnp.where(in_g1, et0, 1.0)
        kh = ke * jnp.where(in_g1, 1.0, et1)
    st = st_ref[...]
    o_int = lax.dot_general(qb.astype(BF16), st.astype(BF16), _NT, preferred_element_type=F32)
    upd = lax.dot_general(v, kh.astype(BF16), (((0,), (0,)), ((), ())), preferred_element_type=F32)
    r128 = lax.broadcasted_iota(jnp.int32, (LANE, LANE), 0)
    c128 = lax.broadcasted_iota(jnp.int32, (LANE, LANE), 1)
    st_ref[...] = st * (et0 * et1) + jnp.where((r128 >> 6) == (c128 >> 6), upd, 0.0)

    q_late = in_g1 if not rev else jnp.logical_not(in_g1)
    q_top = jnp.where(q_late, qe, 0.0)
    k_top = jnp.where(q_late, 0.0, ke).astype(BF16)
    top = [lax.dot_general(jnp.where(m, q_top, 0.0).astype(BF16), k_top, _NT, preferred_element_type=F32)
           for m in (lo, hi)]
    mid = HG // 2 if rev else HG // 2 - 1
    cm = c - jnp.where(in_g1, c[HG + mid:HG + mid + 1, :], c[mid:mid + 1, :])
    return c, top, o_int, cm, jnp.max(jnp.abs(cm))


def _hgrn_steps(jobs, bot_ref):
    lane = lax.broadcasted_iota(jnp.int32, (1, LANE), 1)
    lo = lane < 64
    hi = jnp.logical_not(lo)
    heads = [_hgrn_head(*job) for job in jobs]
    fast = functools.reduce(jnp.maximum, [h[4] for h in heads]) <= FAST_DECAY_LIMIT

    @pl.when(fast)
    def _():
        srow = lax.broadcasted_iota(jnp.int32, (HS, HS), 0)
        scol = lax.broadcasted_iota(jnp.int32, (HS, HS), 1)
        same = (srow >> 6) == (scol >> 6)
        for ji, (job, (_, _, _, cm, _)) in enumerate(zip(jobs, heads)):
            keep = same & ((scol >= srow) if job[5] else (scol <= srow))
            qf = job[0] * jnp.exp(cm)
            kf = (job[1] * jnp.exp(-cm)).astype(BF16)
            for parity, m in enumerate((lo, hi)):
                s = lax.dot_general(jnp.where(m, qf, 0.0).astype(BF16), kf, _NT, preferred_element_type=F32)
                bot_ref[ji, parity] = jnp.where(keep, s, 0.0)

    @pl.when(jnp.logical_not(fast))
    def _():
        for ji, (job, (c, _, _, _, _)) in enumerate(zip(jobs, heads)):
            for parity, s in enumerate(_hgrn_bottom_exact(job[0], job[1], c, lo, job[5])):
                bot_ref[ji, parity] = s

    outs = []
    for ji, (job, (_, top, o_int, _, _)) in enumerate(zip(jobs, heads)):
        v = job[3]
        probs = jnp.concatenate([(bot_ref[ji, 0] + top[0]).astype(BF16), (bot_ref[ji, 1] + top[1]).astype(BF16)],
                                axis=1)
        vv = jnp.concatenate([jnp.where(lo, v, jnp.zeros_like(v)), jnp.where(hi, v, jnp.zeros_like(v))], axis=0)
        outs.append(jnp.dot(probs, vv, preferred_element_type=F32) + o_int)
    return outs


PAIRS_PER_ITER = 2


def _hgrn_kernel(has_s0, nt, *refs):
    if has_s0:
        (qf_ref, vf_ref, kf_ref, lf_ref, qb_ref, vb_ref, kb_ref, lb_ref, s0_ref,
         of_ref, ob_ref, so_ref, st_scr, bot_scr) = refs
    else:
        (qf_ref, vf_ref, kf_ref, lf_ref, qb_ref, vb_ref, kb_ref, lb_ref,
         of_ref, ob_ref, so_ref, st_scr, bot_scr) = refs
    j = pl.program_id(1)

    @pl.when(j == 0)
    def _():
        if has_s0:
            st_scr[...] = s0_ref[...]
        else:
            st_scr[...] = jnp.zeros_like(st_scr)

    n_sub = HL // HS

    def pair_body(it, carry):
        for step in range(n_sub):
            jobs, dests = [], []
            for pp in range(PAIRS_PER_ITER):
                p = it * PAIRS_PER_ITER + pp
                cols = pl.ds(pl.multiple_of(p * LANE, LANE), LANE)
                rf = pl.ds(step * HS, HS)
                rb = pl.ds((n_sub - 1 - step) * HS, HS)
                jobs.append((qf_ref[rf, cols].astype(F32), kf_ref[rf, cols], lf_ref[rf, cols], vf_ref[rf, cols],
                             st_scr.at[p, 0], False))
                dests.append((of_ref, rf, cols))
                jobs.append((qb_ref[rb, cols].astype(F32), kb_ref[rb, cols], lb_ref[rb, cols], vb_ref[rb, cols],
                             st_scr.at[p, 1], True))
                dests.append((ob_ref, rb, cols))
            for (ref, rows, cols), o in zip(dests, _hgrn_steps(jobs, bot_scr)):
                ref[rows, cols] = o
        return carry

    lax.fori_loop(0, N_PAIRS // PAIRS_PER_ITER, pair_body, 0)

    @pl.when(j == nt - 1)
    def _():
        so_ref[...] = st_scr[...]


def _hgrn(qc, vc, kff, lff, kfb, lfb, s0=None, prev=None):
    if s0 is None:
        nb, nt, blk0 = BATCH, SEQ // HL, 0
    else:
        nb, nt, blk0 = DEC_BATCH, DEC_SEQ // HL, T_CTX // HL
    fspec = pl.BlockSpec((HL, 512), lambda b, j: (blk0 + b * nt + j, 0))
    bspec = pl.BlockSpec((HL, 512), lambda b, j: (blk0 + b * nt + nt - 1 - j, 0))
    sspec = pl.BlockSpec((None, N_PAIRS, 2, LANE, LANE), lambda b, j: (b, 0, 0, 0, 0))
    in_specs = [fspec] * 4 + [bspec] * 4
    args = [qc, vc, kff, lff, qc, vc, kfb, lfb]
    aliases = {}
    if s0 is not None:
        in_specs += [sspec]
        args += [s0]
        in_specs += [pl.BlockSpec(memory_space=pl.ANY)] * 2
        args += list(prev)
        aliases = {len(args) - 2: 0, len(args) - 1: 1}

    def body(*refs):
        if s0 is not None:
            refs = refs[:9] + refs[11:]
        _hgrn_kernel(s0 is not None, nt, *refs)

    return pl.pallas_call(
        body,
        grid=(nb, nt),
        in_specs=in_specs,
        out_specs=[fspec, bspec, sspec],
        out_shape=[jax.ShapeDtypeStruct((T_ALL, 512), F32), jax.ShapeDtypeStruct((T_ALL, 512), F32),
                   jax.ShapeDtypeStruct((nb, N_PAIRS, 2, LANE, LANE), F32)],
        scratch_shapes=[pltpu.VMEM((N_PAIRS, 2, LANE, LANE), F32),
                        pltpu.VMEM((2 * PAIRS_PER_ITER, 2, HS, HS), F32)],
        input_output_aliases=aliases,
        compiler_params=_cparams(("parallel", "arbitrary")),
        name="hgrn_ctx" if s0 is None else "hgrn_lat",
    )(*args)


def _merge_kernel(x_ref, mod_ref, n1_ref, wg_ref, oa_ref, ob_ref, ocf_ref, ocb_ref, sg_ref, con_ref, ones_ref,
                  wbr_ref, wout_ref, n2_ref, wrh_ref, wrl_ref, br_ref, x1_o, h2_o, comb_o):
    x = x_ref[...]
    mod = mod_ref[...]
    xn = x * lax.rsqrt(jnp.mean(x * x, axis=-1, keepdims=True) + EPS) * n1_ref[...]
    h = (xn * (1.0 + mod[:, D:2 * D]) + mod[:, 0:D]).astype(BF16)

    oc = ocf_ref[...] + ocb_ref[...]
    oc = oc * lax.rsqrt(_group_mean(oc * oc, ones_ref[...], DV_C) + EPS) * con_ref[...]
    oc = (oc * sg_ref[...].astype(F32)).astype(BF16)
    branches = (oa_ref[...], ob_ref[...], oc)
    mix = None
    for jb in range(3):
        gate = jax.nn.sigmoid(jnp.dot(h, wg_ref[:, jb * D:(jb + 1) * D], preferred_element_type=F32))
        t = gate * jnp.dot(branches[jb], wbr_ref[jb], preferred_element_type=F32)
        mix = t if mix is None else mix + t
    out = jnp.dot(mix.astype(BF16), wout_ref[...], preferred_element_type=F32)
    x1 = x + mod[:, 2 * D:3 * D] * out
    x1_o[...] = x1

    x1n = x1 * lax.rsqrt(jnp.mean(x1 * x1, axis=-1, keepdims=True) + EPS) * n2_ref[...]
    h2 = x1n * (1.0 + mod[:, 4 * D:5 * D]) + mod[:, 3 * D:4 * D]
    h2_o[...] = h2.astype(BF16)

    h2h, h2l = _split_hi_lo(h2)
    logits = (jnp.dot(h2h, wrh_ref[...], preferred_element_type=F32)
              + jnp.dot(h2l, wrh_ref[...], preferred_element_type=F32)
              + jnp.dot(h2h, wrl_ref[...], preferred_element_type=F32)) + br_ref[...]
    lane = lax.broadcasted_iota(jnp.int32, logits.shape, 1)
    neg = -jnp.inf
    is_g = lane < N_GROUPS
    gl = jnp.where(is_g, logits, neg)
    gmax = gl.max(axis=-1, keepdims=True)
    gidx = jnp.min(jnp.where(gl == gmax, lane, LANE), axis=-1, keepdims=True)
    gp = 1.0 / jnp.sum(jnp.where(is_g, jnp.exp(gl - gmax), 0.0), axis=-1, keepdims=True)
    eid = lane - N_GROUPS
    in_grp = (eid >= 0) & (eid < N_EXPERTS) & ((eid >> 2) == gidx)
    el = jnp.where(in_grp, logits, neg)
    v1 = el.max(axis=-1, keepdims=True)
    i1 = jnp.min(jnp.where(el == v1, lane, LANE), axis=-1, keepdims=True)
    el2 = jnp.where(lane == i1, neg, el)
    v2 = el2.max(axis=-1, keepdims=True)
    i2 = jnp.min(jnp.where(el2 == v2, lane, LANE), axis=-1, keepdims=True)
    e2 = jnp.exp(v2 - v1)
    w1 = gp / (1.0 + e2)
    w2 = gp * e2 / (1.0 + e2)
    comb = jnp.where(lane == i1, w1, 0.0) + jnp.where(lane == i2, w2, 0.0)
    comb_o[...] = pltpu.roll(comb, LANE - N_GROUPS, 1)


TMG = 512


def _merge(x, mod, n1, wgate, oa, ob, ocf, ocb, sg, con, ones, wbr, wout, n2, wr, br):
    ctx_tiles, per_seq = T_CTX // TMG, DEC_SEQ // TMG
    mrow = lambda i: jnp.where(i < ctx_tiles, CTX_MOD_ROW, (i - ctx_tiles) // per_seq)
    tspec = lambda w: pl.BlockSpec((TMG, w), lambda i: (i, 0))
    wspec = lambda shape: pl.BlockSpec(shape, lambda i: (0,) * len(shape), pipeline_mode=pl.Buffered(1))
    wr_hi, wr_lo = _split_hi_lo(wr)
    return pl.pallas_call(
        _merge_kernel,
        grid=(T_ALL // TMG,),
        in_specs=[
            tspec(D),
            pl.BlockSpec((None, 1, 6 * D), lambda i: (mrow(i), 0, 0)),
            _const_spec((1, D)), wspec((D, 3 * D)),
            tspec(512), tspec(512), tspec(512), tspec(512), tspec(512),
            _const_spec((1, 512)), _const_spec((256, 256)),
            wspec((3, BRANCH_W, D)), wspec((D, D)), _const_spec((1, D)),
            _const_spec((D, LANE)), _const_spec((D, LANE)), _const_spec((1, LANE)),
        ],
        out_specs=[tspec(D), tspec(D), tspec(LANE)],
        out_shape=[jax.ShapeDtypeStruct((T_ALL, D), F32), jax.ShapeDtypeStruct((T_ALL, D), BF16),
                   jax.ShapeDtypeStruct((T_ALL, LANE), F32)],
        compiler_params=_cparams(("parallel",)),
        name="merge",
    )(x, mod.reshape(MOD_ROWS, 1, 6 * D), n1, wgate, oa, ob, ocf, ocb, sg, con, ones, wbr, wout, n2,
      wr_hi, wr_lo, br)


NB = 2048
N_BLK = T_ALL // NB
SBK = 256
N_SB = NB // SBK
WIN_SHIFT, FT_SHIFT, SEG_SHIFT = 6, 7, 4
WIN = 1 << WIN_SHIFT
FT = 1 << FT_SHIFT
SEG_ALIGN = 1 << SEG_SHIFT
STG = 2 * NB + N_EXPERTS * SEG_ALIGN + 256
QUAD = 4


def _route_kernel(comb_ref, tri_ref, upper_ref, rank_ref, carry_ref, tab_ref, carry_scr):
    s = pl.program_id(1)

    @pl.when(s == 0)
    def _():
        carry_scr[...] = jnp.zeros_like(carry_scr)

    routed = comb_ref[...] > 0.0
    ind = jnp.where(routed, 1.0, 0.0)
    carry = carry_scr[...]
    rank = jnp.dot(tri_ref[...], ind.astype(BF16), preferred_element_type=F32) + carry
    rank_ref[...] = jnp.where(routed, rank, -1.0)
    carry_ref[...] = carry
    count = carry + jnp.sum(ind, axis=0, keepdims=True)
    carry_scr[...] = count

    @pl.when(s == N_SB - 1)
    def _():
        seg = jnp.floor((count + (SEG_ALIGN - 1.0)) * (1.0 / SEG_ALIGN)) * SEG_ALIGN
        off = jnp.dot(jnp.broadcast_to(seg, (8, LANE)), upper_ref[...], preferred_element_type=F32,
                      precision=lax.Precision.HIGHEST)
        tab_ref[0:1, :] = count
        tab_ref[1:2, :] = off[0:1, :]


def _route(comb):
    tri = jnp.asarray(np.tril(np.ones((SBK, SBK)), -1), BF16)
    upper = jnp.asarray(np.triu(np.ones((LANE, LANE)), 1), F32)
    return pl.pallas_call(
        _route_kernel,
        grid=(N_BLK, N_SB),
        in_specs=[
            pl.BlockSpec((SBK, LANE), lambda b, s: (b * N_SB + s, 0)),
            pl.BlockSpec((SBK, SBK), lambda b, s: (0, 0)),
            pl.BlockSpec((LANE, LANE), lambda b, s: (0, 0)),
        ],
        out_specs=[
            pl.BlockSpec((SBK, LANE), lambda b, s: (b * N_SB + s, 0)),
            pl.BlockSpec((None, None, 1, LANE), lambda b, s: (b, s, 0, 0)),
            pl.BlockSpec((None, 2, LANE), lambda b, s: (b, 0, 0)),
        ],
        out_shape=[jax.ShapeDtypeStruct((T_ALL, LANE), F32),
                   jax.ShapeDtypeStruct((N_BLK, N_SB, 1, LANE), F32),
                   jax.ShapeDtypeStruct((N_BLK, 2, LANE), F32)],
        scratch_shapes=[pltpu.VMEM((1, LANE), F32)],
        compiler_params=_cparams(("parallel", "arbitrary")),
        name="moe_route",
    )(comb, tri, upper)


def _moe_kernel(cnt_s, off_s, car_s, h2_ref, rank_ref, comb_ref, offv_ref, eg_ref, eu_ref, ed_ref, o_ref,
                stin_ref, stout_ref, acc_ref):
    blk = pl.program_id(0)
    e = pl.program_id(1)
    lane = lax.broadcasted_iota(jnp.int32, (1, LANE), 1)
    lane_lo = (lane & (2 * WIN - 1)) < WIN
    slot = (lane & (WIN - 1)).astype(F32)
    srow = lax.broadcasted_iota(jnp.int32, (WIN, SBK), 0).astype(F32)

    def windows(s):
        out = []
        for ex in range(N_EXPERTS):
            start = off_s[blk, ex] + car_s[blk, s, ex]
            length = car_s[blk, s + 1, ex] - car_s[blk, s, ex]
            ws = (start >> SEG_SHIFT) << SEG_SHIFT
            out.append((ws, (start - ws + length + (WIN - 1)) >> WIN_SHIFT))
        return out, functools.reduce(jnp.maximum, [w[1] for w in out])

    def positions(s):
        rows = pl.ds(pl.multiple_of(s * SBK, SBK), SBK)
        rank = rank_ref[rows, :]
        return rows, jnp.where(rank >= 0.0, rank + offv_ref[1:2, :], -1.0e6)

    @pl.when(e == 0)
    def _():
        stin_ref[...] = jnp.zeros_like(stin_ref)
        stout_ref[...] = jnp.zeros_like(stout_ref)

        def sub_body(s, carry):
            rows, pos = positions(s)
            pos_t = pos.T
            h2 = h2_ref[rows, :]
            wins, nmax = windows(s)

            def chunk_body(c, carry2):
                for quad in range(N_EXPERTS // QUAD):
                    blocks = []
                    for ex in range(quad * QUAD, (quad + 1) * QUAD):
                        base = (wins[ex][0] + c * WIN).astype(F32)
                        hit = (pos_t[ex:ex + 1, :] - base) == srow
                        blocks.append(jnp.where(hit, 1.0, 0.0).astype(BF16))
                    moved = jnp.dot(jnp.concatenate(blocks, axis=0), h2, preferred_element_type=F32).astype(BF16)
                    for i in range(QUAD):
                        ws, nch = wins[quad * QUAD + i]

                        @pl.when(c < nch)
                        def _():
                            dst = pl.ds(pl.multiple_of(ws + c * WIN, SEG_ALIGN), WIN)
                            stin_ref[dst, :] = stin_ref[dst, :] + moved[i * WIN:(i + 1) * WIN, :]
                return carry2

            lax.fori_loop(0, nmax, chunk_body, 0)
            return carry

        lax.fori_loop(0, N_SB, sub_body, 0)

    def ffn_tile(first, n_rows):
        rows = pl.ds(pl.multiple_of(first, SEG_ALIGN), n_rows)
        xs = stin_ref[rows, :]
        hg = jnp.dot(xs, eg_ref[...], preferred_element_type=F32)
        hu = jnp.dot(xs, eu_ref[...], preferred_element_type=F32)
        act = (hg * jax.nn.sigmoid(hg) * hu).astype(BF16)
        stout_ref[rows, :] = jnp.dot(act, ed_ref[...], preferred_element_type=F32).astype(BF16)

    count = cnt_s[blk, e]
    seg0 = off_s[blk, e]
    n_big = (count + (FT - 1)) >> (FT_SHIFT + 1)

    def ffn_body(t, carry):
        ffn_tile(seg0 + t * (2 * FT), 2 * FT)
        return carry

    lax.fori_loop(0, n_big, ffn_body, 0)

    @pl.when(count > n_big * (2 * FT))
    def _():
        ffn_tile(seg0 + n_big * (2 * FT), FT)

    @pl.when(e == N_EXPERTS - 1)
    def _():
        def sub_body(s, carry):
            rows, pos = positions(s)
            wts = comb_ref[rows, :]
            wins, nmax = windows(s)
            acc_ref[...] = jnp.zeros_like(acc_ref)

            def chunk_body(c, carry2):
                sel, srcs = [], []
                for ex0 in range(0, N_EXPERTS, 2):
                    b0 = (wins[ex0][0] + c * WIN).astype(F32)
                    b1 = (wins[ex0 + 1][0] + c * WIN).astype(F32)
                    rel = jnp.where(lane_lo, pos[:, ex0:ex0 + 1] - b0, pos[:, ex0 + 1:ex0 + 2] - b1)
                    w = jnp.where(lane_lo, wts[:, ex0:ex0 + 1], wts[:, ex0 + 1:ex0 + 2])
                    sel.append(jnp.where(rel == slot, w, 0.0).astype(BF16))
                for ex in range(N_EXPERTS):
                    first = jnp.minimum(wins[ex][0] + c * WIN, STG - WIN)
                    srcs.append(stout_ref[pl.ds(pl.multiple_of(first, SEG_ALIGN), WIN), :])
                acc_ref[...] += jnp.dot(jnp.concatenate(sel, axis=1), jnp.concatenate(srcs, axis=0),
                                        preferred_element_type=F32)
                return carry2

            lax.fori_loop(0, nmax, chunk_body, 0)
            o_ref[rows, :] = acc_ref[...].astype(BF16)
            return carry

        lax.fori_loop(0, N_SB, sub_body, 0)


def _moe(h2, comb, eg, eu, ed):
    rank, carry, tab = _route(comb)
    cnt_i = tab[:, 0, :N_EXPERTS].astype(jnp.int32)
    off_i = tab[:, 1, :N_EXPERTS].astype(jnp.int32)
    car_i = jnp.concatenate([carry[:, :, 0, :N_EXPERTS].astype(jnp.int32), cnt_i[:, None, :]], axis=1)
    bspec = lambda w: pl.BlockSpec((NB, w), lambda b, e, *_: (b, 0))
    bspec1 = lambda w: pl.BlockSpec((NB, w), lambda b, e, *_: (b, 0), pipeline_mode=pl.Buffered(1))
    grid_spec = pltpu.PrefetchScalarGridSpec(
        num_scalar_prefetch=3,
        grid=(N_BLK, N_EXPERTS),
        in_specs=[
            bspec1(D), bspec1(LANE), bspec1(LANE),
            pl.BlockSpec((None, 2, LANE), lambda b, e, *_: (b, 0, 0)),
            pl.BlockSpec((None, D, D_EXPERT), lambda b, e, *_: (e, 0, 0)),
            pl.BlockSpec((None, D, D_EXPERT), lambda b, e, *_: (e, 0, 0)),
            pl.BlockSpec((None, D_EXPERT, D), lambda b, e, *_: (e, 0, 0)),
        ],
        out_specs=bspec(D),
        scratch_shapes=[pltpu.VMEM((STG, D), BF16), pltpu.VMEM((STG, D), BF16), pltpu.VMEM((SBK, D), F32)],
    )
    return pl.pallas_call(
        _moe_kernel,
        grid_spec=grid_spec,
        out_shape=jax.ShapeDtypeStruct((T_ALL, D), BF16),
        compiler_params=_cparams(("parallel", "arbitrary")),
        name="moe",
    )(cnt_i, off_i, car_i, h2, rank, comb, tab, eg, eu, ed)


def _residual_kernel(final, x1_ref, moe_ref, mod_ref, fg_ref, o_ref):
    x2 = x1_ref[...] + mod_ref[:, 5 * D:6 * D] * moe_ref[...].astype(F32)
    if final:
        x2 = x2 * lax.rsqrt(jnp.mean(x2 * x2, axis=-1, keepdims=True) + EPS) * fg_ref[...]
    o_ref[...] = x2


def _residual(final, x1, moe, mod, fg):
    return pl.pallas_call(
        functools.partial(_residual_kernel, final),
        grid=(N_TILES,),
        in_specs=[_tile_spec(D), _tile_spec(D),
                  pl.BlockSpec((None, 1, 6 * D), lambda i: (_mod_row(i), 0, 0)), _const_spec((1, D))],
        out_specs=_tile_spec(D),
        out_shape=jax.ShapeDtypeStruct((T_ALL, D), F32),
        compiler_params=_cparams(("parallel",)),
        name="residual",
    )(x1, moe, mod.reshape(MOD_ROWS, 1, 6 * D), fg)


def _state_to_blockdiag(s):
    st = jnp.swapaxes(s, -1, -2).reshape(s.shape[0], H_C // 2, 2, DV_C, DK_C)
    z = jnp.zeros_like(st[:, :, 0])
    top = jnp.concatenate([st[:, :, 0], z], axis=-1)
    bot = jnp.concatenate([z, st[:, :, 1]], axis=-1)
    return jnp.concatenate([top, bot], axis=-2)


def _blockdiag_to_state(sb):
    even = sb[:, :, :DV_C, :DK_C]
    odd = sb[:, :, DV_C:, DK_C:]
    st = jnp.stack([even, odd], axis=2).reshape(sb.shape[0], H_C, DV_C, DK_C)
    return jnp.swapaxes(st, -1, -2)


def kernel(x_prompt, x_sample, cache_gqa_k, cache_gqa_v, cache_mla_ckv, cache_mla_krope, state_hgrn, c, c_ctx,
           w_mod, b_mod, norm1_g, norm2_g, w_in, a_qnorm, a_knorm, b_qnorm, b_wq, b_kvnorm, b_wkv, c_lb_logits,
           c_onorm, w_branch, w_out, r_group_w, r_group_b, r_expert_w, r_expert_b, e_gate, e_up, e_down, final_g):
    x = jnp.concatenate([x_prompt.reshape(T_CTX, D), x_sample.reshape(T_LAT, D)], axis=0)
    cvec = jnp.concatenate([c, c_ctx[None, :], jnp.zeros((MOD_ROWS - DEC_BATCH - 1, D), F32)], axis=0)
    mod_all = _mod_table(cvec, w_mod, b_mod)
    taba, tabb, tabk = _rope_tables()
    ones = _ones_block(256, 64)
    lbl = c_lb_logits.reshape(DEPTH, 2, H_C * DK_C)
    zpad = jnp.zeros((DEC_BATCH * PAST, 32), F32)

    new_k, new_v, new_ckv, new_kr, new_s = [], [], [], [], []
    for l in range(DEPTH):
        mod = mod_all[l]
        w_in_p, w_gate = _pack_w_in(w_in[l])
        wq_p, wkv_p = _pack_wq(b_wq[l]), _pack_wkv(b_wkv[l])
        (qa, kan, ka, va, qb, ckv, kvb, kr, kre, lff, lfb, kff, kfb, qc, vc, sg) = _inproj(
            l, x, mod, norm1_g[l][None, :], w_in_p, jnp.tile(a_qnorm[l], H_A)[None, :],
            jnp.tile(a_knorm[l], KV_A)[None, :], b_qnorm[l][None, :], wq_p, b_kvnorm[l][None, :], wkv_p,
            lbl, ones, taba, tabb, tabk)
        new_k.append(kan[:T_CTX].reshape(BATCH, SEQ, KV_A, HD_A))
        new_v.append(va[:T_CTX].reshape(BATCH, SEQ, KV_A, HD_A))
        new_ckv.append(ckv[:T_CTX].reshape(BATCH, SEQ, KV_RANK))
        new_kr.append(kr[:T_CTX, :ROPE_B].reshape(BATCH, SEQ, ROPE_B))

        ck = cache_gqa_k[:, l].reshape(DEC_BATCH * PAST, KV_A * HD_A)
        cv = cache_gqa_v[:, l].reshape(DEC_BATCH * PAST, KV_A * HD_A)
        ckvb = _ctx_kv(cache_mla_ckv[:, l].reshape(DEC_BATCH * PAST, KV_RANK), wkv_p)
        ckr = cache_mla_krope[:, l].reshape(DEC_BATCH * PAST, ROPE_B)
        ckre = jnp.concatenate([ckr, zpad, ckr, zpad], axis=1)
        oa, ob = _attention(qa, qb, ka, va, kvb, kre)
        oa, ob = _attention(qa, qb, ka, va, kvb, kre, prev=(oa, ob), cache=(ck, cv, ckvb, ckre))

        ocf, ocb, s_ctx = _hgrn(qc, vc, kff, lff, kfb, lfb)
        s0 = jnp.stack([_state_to_blockdiag(state_hgrn[:, l, 0]), _state_to_blockdiag(state_hgrn[:, l, 1])], axis=2)
        ocf, ocb, _ = _hgrn(qc, vc, kff, lff, kfb, lfb, s0=s0, prev=(ocf, ocb))
        new_s.append(jnp.stack([_blockdiag_to_state(s_ctx[:, :, 0]), _blockdiag_to_state(s_ctx[:, :, 1])], axis=1))

        wr = jnp.concatenate([r_group_w[l], r_expert_w[l], jnp.zeros((D, LANE - N_GROUPS - N_EXPERTS), F32)], axis=1)
        br = jnp.concatenate([r_group_b[l], r_expert_b[l], jnp.zeros((LANE - N_GROUPS - N_EXPERTS,), F32)])[None, :]
        x1, h2, comb = _merge(x, mod, norm1_g[l][None, :], w_gate, oa, ob, ocf, ocb, sg,
                              jnp.tile(c_onorm[l], H_C)[None, :], ones, w_branch[l].astype(BF16),
                              w_out[l].astype(BF16), norm2_g[l][None, :], wr, br)
        moe = _moe(h2, comb, e_gate[l].astype(BF16), e_up[l].astype(BF16), e_down[l].astype(BF16))
        x = _residual(l == DEPTH - 1, x1, moe, mod, final_g[None, :])

    y_prompt = x[:T_CTX].reshape(BATCH, SEQ, D)
    y_sample = x[T_CTX:].reshape(DEC_BATCH, DEC_SEQ, D)
    return (y_prompt, y_sample, jnp.stack(new_k, axis=1), jnp.stack(new_v, axis=1), jnp.stack(new_ckv, axis=1),
            jnp.stack(new_kr, axis=1), jnp.stack(new_s, axis=1))
```

```python
import functools

import numpy as np
import jax
import jax.numpy as jnp
from jax import lax
from jax.experimental import pallas as pl
from jax.experimental.pallas import tpu as pltpu

D = 1024
BATCH, SEQ = 32, 256
DEC_BATCH, DEC_SEQ = 8, 1024
PAST = 256
DEPTH = 2
GRID_W = 64
THETA = 10000.0
EPS = 1e-6
F_FLOOR = 1e-30
H_A, KV_A, HD_A = 8, 2, 64
H_B, Q_RANK, KV_RANK, NOPE_B, ROPE_B, V_B = 8, 384, 256, 64, 32, 64
H_C, DK_C, DV_C = 8, 64, 64
BRANCH_W = 512
N_GROUPS, E_PER_GROUP, N_EXPERTS, D_EXPERT = 4, 4, 16, 512

T_CTX = BATCH * SEQ
T_LAT = DEC_BATCH * DEC_SEQ
T_ALL = T_CTX + T_LAT
TM = 256
N_TILES = T_ALL // TM
CTX_TILES = T_CTX // TM
LAT_TILES_PER_SEQ = DEC_SEQ // TM
MOD_ROWS = 16
CTX_MOD_ROW = DEC_BATCH
LANE = 128
VMEM_LIMIT = 56 * 1024 * 1024

C_QA, C_KA, C_VA, C_QRA, C_KVA, C_KR = 0, 512, 640, 768, 1152, 1408
C_FF, C_FB, C_QC, C_IC, C_GC, C_END = 1536, 2048, 2560, 3072, 3584, 4096
R_QA, R_KA, R_VA, R_QRA, R_KVA, R_KR = 0, 512, 640, 768, 1152, 1408
R_FF, R_FB, R_QC, R_IC, R_GC, R_GATE, R_END = 1440, 1952, 2464, 2976, 3488, 4000, 7072

F32 = jnp.float32
BF16 = jnp.bfloat16


def _cparams(sem):
    return pltpu.CompilerParams(dimension_semantics=sem, vmem_limit_bytes=VMEM_LIMIT)


def _mod_row(i):
    return jnp.where(i < CTX_TILES, CTX_MOD_ROW, (i - CTX_TILES) // LAT_TILES_PER_SEQ)


def _pos_block(i):
    return jnp.where(i < CTX_TILES, LAT_TILES_PER_SEQ, (i - CTX_TILES) % LAT_TILES_PER_SEQ)


def _split_hi_lo(x):
    hi = x.astype(BF16)
    lo = (x - hi.astype(F32)).astype(BF16)
    return hi, lo


def _group_mean(x2, ones_blk, width):
    n = ones_blk.shape[0]
    outs = []
    for j in range(x2.shape[-1] // n):
        blk = x2[:, j * n:(j + 1) * n]
        hi, lo = _split_hi_lo(blk)
        s = jnp.dot(hi, ones_blk, preferred_element_type=F32) + jnp.dot(lo, ones_blk, preferred_element_type=F32)
        outs.append(s)
    s = outs[0] if len(outs) == 1 else jnp.concatenate(outs, axis=-1)
    return s * (1.0 / width)


def _rope(x, tab_ref, shift, period):
    c, s1, s2 = tab_ref[0], tab_ref[1], tab_ref[2]
    outs = []
    for j in range(x.shape[-1] // period):
        blk = x[:, j * period:(j + 1) * period]
        outs.append(blk * c + pltpu.roll(blk, shift, 1) * s1 + pltpu.roll(blk, period - shift, 1) * s2)
    return outs[0] if len(outs) == 1 else jnp.concatenate(outs, axis=-1)


def _mod_kernel(c_ref, w_ref, b_ref, o_ref):
    c = c_ref[...]
    a = c * jax.nn.sigmoid(c)
    o_ref[...] = jnp.dot(a, w_ref[...], preferred_element_type=F32, precision=lax.Precision.HIGHEST) + b_ref[...]


def _mod_table(cvec, w_mod, b_mod):
    nt = 1024
    return pl.pallas_call(
        _mod_kernel,
        grid=(DEPTH, 6 * D // nt),
        in_specs=[
            pl.BlockSpec((MOD_ROWS, D), lambda l, j: (0, 0)),
            pl.BlockSpec((None, D, nt), lambda l, j: (l, 0, j)),
            pl.BlockSpec((None, 1, nt), lambda l, j: (l, 0, j)),
        ],
        out_specs=pl.BlockSpec((None, MOD_ROWS, nt), lambda l, j: (l, 0, j)),
        out_shape=jax.ShapeDtypeStruct((DEPTH, MOD_ROWS, 6 * D), F32),
        compiler_params=_cparams(("arbitrary", "arbitrary")),
        name="mod_table",
    )(cvec, w_mod, b_mod.reshape(DEPTH, 1, 6 * D))


def _x_pair(x):
    if isinstance(x, tuple):
        return x[0], x[1], 0
    return x, x, T_CTX


def _x_specs(x, tile):
    _, _, lat_off = _x_pair(x)
    ctx = T_CTX // tile
    return [pl.BlockSpec((tile, D), lambda i: (jnp.minimum(i, ctx - 1), 0)),
            pl.BlockSpec((tile, D), lambda i: (jnp.maximum(i - ctx, 0) + lat_off // tile, 0))]


def _x_tile(xc_ref, xl_ref, tile):
    return jnp.where(pl.program_id(0) < T_CTX // tile, xc_ref[...], xl_ref[...])


def _inproj_kernel(layer, xc_ref, xl_ref, mod_ref, n1_ref, w_ref, aq_ref, ak_ref, bq_ref, wq_ref, bkv_ref, wkv_ref,
                   lbl_ref, ones_ref, taba_ref, tabb_ref, tabk_ref,
                   qa_o, kan_o, ka_o, va_o, qb_o, ckv_o, kvb_o, kr_o, kre_o,
                   lff_o, lfb_o, kff_o, kfb_o, qc_o, vc_o, sg_o):
    x = _x_tile(xc_ref, xl_ref, TM)
    mod = mod_ref[...]
    xn = x * lax.rsqrt(jnp.mean(x * x, axis=-1, keepdims=True) + EPS) * n1_ref[...]
    h = (xn * (1.0 + mod[:, D:2 * D]) + mod[:, 0:D]).astype(BF16)
    y = jnp.dot(h, w_ref[...], preferred_element_type=F32)
    ones = ones_ref[...]

    qa = y[:, C_QA:C_KA]
    qa = qa * lax.rsqrt(_group_mean(qa * qa, ones, HD_A) + EPS) * aq_ref[...]
    qa_o[...] = (_rope(qa, taba_ref, 16, LANE) * (HD_A ** -0.5)).astype(BF16)
    ka = y[:, C_KA:C_VA]
    ka = ka * lax.rsqrt(_group_mean(ka * ka, ones[:LANE, :LANE], HD_A) + EPS) * ak_ref[...]
    kan_o[...] = ka
    ka_o[...] = _rope(ka, taba_ref, 16, LANE).astype(BF16)
    va_o[...] = y[:, C_VA:C_QRA]

    qr = y[:, C_QRA:C_KVA]
    qr = qr * lax.rsqrt(jnp.mean(qr * qr, axis=-1, keepdims=True) + EPS) * bq_ref[...]
    qb = jnp.dot(qr.astype(BF16), wq_ref[...], preferred_element_type=F32)
    qb_o[...] = (_rope(qb, tabb_ref, 8, 2 * LANE) * ((NOPE_B + ROPE_B) ** -0.5)).astype(BF16)
    kv = y[:, C_KVA:C_KR]
    ckv = kv * lax.rsqrt(jnp.mean(kv * kv, axis=-1, keepdims=True) + EPS) * bkv_ref[...]
    ckv_o[...] = ckv
    kvb_o[...] = jnp.dot(ckv.astype(BF16), wkv_ref[...], preferred_element_type=F32).astype(BF16)
    kr = y[:, C_KR:C_FF]
    kr_o[...] = kr
    kre_o[...] = _rope(kr, tabk_ref, 8, LANE).astype(BF16)

    lbl = lbl_ref[...]
    e = jnp.exp(lbl - jnp.max(lbl, axis=0, keepdims=True))
    p = e / jnp.sum(e, axis=0, keepdims=True)
    lb = p[0] * 0.0
    for i in range(1, layer + 1):
        lb = lb + p[i]
    for d, (c0, lf_o, kf_o) in enumerate(((C_FF, lff_o, kff_o), (C_FB, lfb_o, kfb_o))):
        pre = y[:, c0:c0 + 512]
        lbd = lb[d:d + 1, :]
        f = jnp.maximum(lbd + (1.0 - lbd) * jax.nn.sigmoid(pre), F_FLOOR)
        lf_o[...] = jnp.log(f)
        kf_o[...] = 1.0 - f
    qc_o[...] = y[:, C_QC:C_IC].astype(BF16)
    vc_o[...] = y[:, C_IC:C_GC].astype(BF16)
    gc = y[:, C_GC:C_END]
    sg_o[...] = (gc * jax.nn.sigmoid(gc)).astype(BF16)


def _const_spec(shape):
    nd = len(shape)
    return pl.BlockSpec(shape, lambda i: (0,) * nd)


def _tile_spec(width):
    return pl.BlockSpec((TM, width), lambda i: (i, 0))


def _inproj(layer, x, mod, n1, w_in_p, aq, ak, bq, wq_p, bkv, wkv_p, lbl, ones, taba, tabb, tabk):
    outs = [
        (512, BF16), (128, F32), (128, BF16), (128, F32), (1024, BF16), (256, F32), (1024, BF16),
        (128, F32), (128, BF16), (512, F32), (512, F32), (512, F32), (512, F32), (512, BF16), (512, BF16),
        (512, BF16),
    ]
    tab_spec = lambda w: pl.BlockSpec((3, TM, w), lambda i: (0, _pos_block(i), 0))
    return pl.pallas_call(
        functools.partial(_inproj_kernel, layer),
        grid=(N_TILES,),
        in_specs=_x_specs(x, TM) + [
            pl.BlockSpec((None, 1, 6 * D), lambda i: (_mod_row(i), 0, 0)),
            _const_spec((1, D)),
            _const_spec((D, C_END)),
            _const_spec((1, 512)), _const_spec((1, 128)), _const_spec((1, Q_RANK)),
            _const_spec((Q_RANK, 1024)), _const_spec((1, KV_RANK)), _const_spec((KV_RANK, 1024)),
            _const_spec((DEPTH, 2, 512)), _const_spec((256, 256)),
            tab_spec(LANE), tab_spec(2 * LANE), tab_spec(LANE),
        ],
        out_specs=[_tile_spec(w) for w, _ in outs],
        out_shape=[jax.ShapeDtypeStruct((T_ALL, w), dt) for w, dt in outs],
        compiler_params=_cparams(("parallel",)),
        name="inproj",
    )(*_x_pair(x)[:2], mod.reshape(MOD_ROWS, 1, 6 * D), n1, w_in_p, aq, ak, bq, wq_p, bkv, wkv_p, lbl, ones,
      taba, tabb, tabk)


def _pack_w_in(w):
    z = jnp.zeros((D, 32), w.dtype)
    kr = w[:, R_KR:R_FF]
    main = jnp.concatenate([w[:, :R_KR], kr, z, kr, z, w[:, R_FF:R_GATE]], axis=1)
    return main.astype(BF16), w[:, R_GATE:].astype(BF16)


def _pack_wq(wq):
    w = wq.reshape(Q_RANK, H_B, NOPE_B + ROPE_B)
    nope, rope = w[..., :NOPE_B], w[..., NOPE_B:]
    z = jnp.zeros((Q_RANK, H_B, 32), wq.dtype)
    even = jnp.concatenate([rope, z, nope], axis=-1)
    odd = jnp.concatenate([nope, rope, z], axis=-1)
    is_even = (jnp.arange(H_B) % 2 == 0)[None, :, None]
    return jnp.where(is_even, even, odd).reshape(Q_RANK, H_B * LANE).astype(BF16)


def _pack_wkv(wkv):
    w = wkv.reshape(KV_RANK, H_B, NOPE_B + V_B)
    nope, v = w[..., :NOPE_B], w[..., NOPE_B:]
    is_even = (jnp.arange(H_B) % 2 == 0)[None, :, None]
    return jnp.where(is_even, jnp.concatenate([v, nope], -1), jnp.concatenate([nope, v], -1)).reshape(
        KV_RANK, H_B * LANE).astype(BF16)


def _rope_tables():
    pos = np.arange(DEC_SEQ)
    row, col = pos // GRID_W, pos % GRID_W

    def pattern(half):
        quarter = half // 2
        inv = THETA ** (-np.arange(0, half, 2, dtype=np.float64) / half)
        ang = np.concatenate([row[:, None] * inv, row[:, None] * inv, col[:, None] * inv, col[:, None] * inv], 1)
        is_x2 = np.tile(np.concatenate([np.zeros(quarter), np.ones(quarter)]), 2)[None, :]
        c = np.cos(ang)
        s1 = np.sin(ang) * is_x2
        s2 = -np.sin(ang) * (1 - is_x2)
        return c, s1, s2

    def assemble(width, spans, half):
        c, s1, s2 = pattern(half)
        tc = np.ones((DEC_SEQ + TM, width))
        t1 = np.zeros((DEC_SEQ + TM, width))
        t2 = np.zeros((DEC_SEQ + TM, width))
        for start in spans:
            tc[:DEC_SEQ, start:start + 2 * half] = c
            t1[:DEC_SEQ, start:start + 2 * half] = s1
            t2[:DEC_SEQ, start:start + 2 * half] = s2
        return jnp.asarray(np.stack([tc, t1, t2]), F32)

    taba = assemble(LANE, (0, 64), 32)
    tabb = assemble(2 * LANE, (0, 128 + 64), 16)
    tabk = assemble(LANE, (0, 64), 16)
    return taba, tabb, tabk


def _ones_block(n, width):
    g = np.arange(n) // width
    return jnp.asarray(g[:, None] == g[None, :], BF16)


def _ctxkv_kernel(c_ref, w_ref, o_ref):
    o_ref[...] = jnp.dot(c_ref[...].astype(BF16), w_ref[...], preferred_element_type=F32).astype(BF16)


def _ctx_kv(ckv_cache, wkv_p):
    rows = ckv_cache.shape[0]
    return pl.pallas_call(
        _ctxkv_kernel,
        grid=(rows // TM,),
        in_specs=[_tile_spec(KV_RANK), _const_spec((KV_RANK, 1024))],
        out_specs=_tile_spec(1024),
        out_shape=jax.ShapeDtypeStruct((rows, 1024), BF16),
        compiler_params=_cparams(("parallel",)),
        name="ctx_kv",
    )(ckv_cache, wkv_p)


_NT = (((1,), (1,)), ((), ()))


def _softmax_pv(scores, values, parity):
    lane = lax.broadcasted_iota(jnp.int32, (1, LANE), 1)
    den_lane = 64 if parity == 0 else 0
    keep = (lane < 64) if parity == 0 else (lane >= 64)
    m = scores[0].max(axis=-1, keepdims=True)
    for s in scores[1:]:
        m = jnp.maximum(m, s.max(axis=-1, keepdims=True))
    if len(scores) == 1:
        p = jnp.exp(scores[0] - m)
        o = jnp.dot(p.astype(BF16), values[0], preferred_element_type=F32)
        return o / p.sum(axis=-1, keepdims=True)
    acc = None
    for s, v in zip(scores, values):
        p = jnp.exp((s - m).astype(BF16))
        o = jnp.dot(p, jnp.where(lane == den_lane, jnp.ones_like(v), v), preferred_element_type=F32)
        acc = o if acc is None else acc + o
    return jnp.where(keep, acc / acc[:, den_lane:den_lane + 1], 0.0)


def _attn_kernel(n_pieces, qa_ref, qb_ref, *refs):
    kv_refs = refs[:4 * n_pieces]
    oa_ref, ob_ref = refs[-2], refs[-1]
    lane = lax.broadcasted_iota(jnp.int32, (1, LANE), 1)
    lo = lane < 64

    ka = [kv_refs[4 * i][...].astype(F32) for i in range(n_pieces)]
    va = [kv_refs[4 * i + 1][...].astype(F32) for i in range(n_pieces)]

    def place(x, g, parity):
        if g != parity:
            x = pltpu.roll(x, 64, 1)
        keep = lo if parity == 0 else jnp.logical_not(lo)
        return jnp.where(keep, x, 0.0).astype(BF16)

    placed = {(g, parity): ([place(k, g, parity) for k in ka], [place(v, g, parity) for v in va])
              for g in range(KV_A) for parity in range(2)}
    for pair in range(H_A // 2):
        g = (2 * pair) // (H_A // KV_A)
        q = qa_ref[:, pair * LANE:(pair + 1) * LANE]
        acc = None
        for parity in range(2):
            ks, vs = placed[(g, parity)]
            scores = [lax.dot_general(q, k, _NT, preferred_element_type=F32) for k in ks]
            o = _softmax_pv(scores, vs, parity)
            acc = o if acc is None else acc + o
        oa_ref[:, pair * LANE:(pair + 1) * LANE] = acc.astype(BF16)

    for pair in range(H_B // 2):
        acc = None
        for parity in range(2):
            h = 2 * pair + parity
            q = qb_ref[:, h * LANE:(h + 1) * LANE]
            nope = jnp.logical_not(lo) if parity == 0 else lo
            scores, vs = [], []
            for i in range(n_pieces):
                kvb = kv_refs[4 * i + 2][:, h * LANE:(h + 1) * LANE]
                kre = kv_refs[4 * i + 3][...].astype(BF16)
                k = jnp.where(nope, kvb, kre)
                scores.append(lax.dot_general(q, k, _NT, preferred_element_type=F32))
                vs.append(jnp.where(nope, jnp.zeros_like(kvb), kvb))
            o = _softmax_pv(scores, vs, parity)
            acc = o if acc is None else acc + o
        ob_ref[:, pair * LANE:(pair + 1) * LANE] = acc.astype(BF16)


def _attention(qa, qb, ka, va, kvb, kre, prev=None, cache=None):
    if cache is None:
        nb, nqt, nk, q_blk0, k_blk0 = BATCH, 1, SEQ, 0, 0
    else:
        nb, nqt, nk, q_blk0, k_blk0 = DEC_BATCH, DEC_SEQ // TM, DEC_SEQ, CTX_TILES, T_CTX // DEC_SEQ
    qspec = lambda w: pl.BlockSpec((TM, w), lambda b, j: (q_blk0 + b * nqt + j, 0))
    kspec = lambda w: pl.BlockSpec((nk, w), lambda b, j: (k_blk0 + b, 0))
    in_specs = [qspec(512), qspec(1024), kspec(128), kspec(128), kspec(1024), kspec(128)]
    args = [qa, qb, ka, va, kvb, kre]
    n_pieces = 1
    aliases = {}
    if cache is not None:
        cspec = lambda w: pl.BlockSpec((PAST, w), lambda b, j: (b, 0))
        in_specs += [cspec(128), cspec(128), cspec(1024), cspec(128)]
        args += list(cache)
        n_pieces = 2
        in_specs += [pl.BlockSpec(memory_space=pl.ANY)] * 2
        args += list(prev)
        aliases = {len(args) - 2: 0, len(args) - 1: 1}

    def body(*refs):
        if cache is not None:
            refs = refs[:2 + 4 * n_pieces] + refs[-2:]
        _attn_kernel(n_pieces, *refs)

    return pl.pallas_call(
        body,
        grid=(nb, nqt),
        in_specs=in_specs,
        out_specs=[qspec(512), qspec(512)],
        out_shape=[jax.ShapeDtypeStruct((T_ALL, 512), BF16)] * 2,
        input_output_aliases=aliases,
        compiler_params=_cparams(("parallel", "arbitrary")),
        name="attention_ctx" if cache is None else "attention_lat",
    )(*args)


HL = 256
HS = 128
HG = 64
N_PAIRS = H_C // 2
FAST_DECAY_LIMIT = 80.0


def _hgrn_bottom_exact(q, k, c, lo, rev):
    row = lax.broadcasted_iota(jnp.int32, (HS, LANE), 0)
    srow = lax.broadcasted_iota(jnp.int32, (HS, HS), 0)
    scol = lax.broadcasted_iota(jnp.int32, (HS, HS), 1)
    out = []
    for parity in range(2):
        def dup(x):
            xs = pltpu.roll(x, 64, 1)
            return jnp.where(lo, x, xs) if parity == 0 else jnp.where(lo, xs, x)
        qd, kd, bd = dup(q), dup(k), dup(c)

        dg = row & 3
        if rev:
            dg = 3 - dg
        e = [None]
        for delta in range(1, 4):
            shifted = pltpu.roll(bd, delta if rev else HS - delta, 0)
            e.append(jnp.exp(jnp.minimum(shifted - bd, 0.0)))
        qp, kp = [], []
        for c1, c2 in ((0, 1), (2, 3)):
            cv = jnp.where(lo, c1, c2)
            dl = cv - dg
            fac = jnp.where(dl == 0, 1.0, jnp.where(dl == 1, e[1], jnp.where(dl == 2, e[2],
                            jnp.where(dl == 3, e[3], 0.0))))
            kp.append((kd * fac).astype(BF16))
            qp.append(jnp.where(dg == cv, qd, 0.0).astype(BF16))
        s = lax.dot_general(jnp.concatenate(qp, axis=1), jnp.concatenate(kp, axis=1), _NT,
                            preferred_element_type=F32)
        tot = jnp.where((srow >> 2) == (scol >> 2), s, 0.0)

        for lev in range(1, 3):
            g = 4 ** lev
            par = 4 * g
            shape3 = (HS // par, par, LANE)
            rid = lax.broadcasted_iota(jnp.int32, shape3, 1)
            dg3 = rid >> (2 * lev)
            if rev:
                dg3 = 3 - dg3
            b3, q3, k3 = bd.reshape(shape3), qd.reshape(shape3), kd.reshape(shape3)
            lo3 = lo.reshape(1, 1, LANE)
            qp, kp = [], []
            for c1, c2 in ((1, 2), (3, None)):
                idx = lambda cc: (4 - cc) * g if rev else cc * g - 1
                i1 = idx(c1)
                i2 = idx(c2) if c2 is not None else i1
                ridx = jnp.where(lo3, i1, i2)
                ref = jnp.sum(jnp.where(rid == ridx, b3, 0.0), axis=1, keepdims=True)
                cvk = jnp.where(lo3, c1, c2 if c2 is not None else 0)
                cvq = jnp.where(lo3, c1, c2 if c2 is not None else -1)
                kk = jnp.where(dg3 < cvk, k3 * jnp.exp(jnp.minimum(ref - b3, 0.0)), 0.0)
                qq = jnp.where(dg3 == cvq, q3 * jnp.exp(jnp.minimum(b3 - ref, 0.0)), 0.0)
                kp.append(kk.reshape(HS, LANE).astype(BF16))
                qp.append(qq.reshape(HS, LANE).astype(BF16))
            s = lax.dot_general(jnp.concatenate(qp, axis=1), jnp.concatenate(kp, axis=1), _NT,
                                preferred_element_type=F32)
            sh = 2 * lev + 2
            tot = tot + jnp.where((srow >> sh) == (scol >> sh), s, 0.0)
        out.append(tot)
    return out


def _hgrn_head(q, k, lf, v, st_ref, rev):
    row = lax.broadcasted_iota(jnp.int32, (HS, LANE), 0)
    lane = lax.broadcasted_iota(jnp.int32, (1, LANE), 1)
    lo = lane < 64
    hi = jnp.logical_not(lo)
    grow = row & (HG - 1)
    in_g1 = row >= HG

    c = lf
    d = 1
    while d < HG:
        if rev:
            c = c + jnp.where(grow < HG - d, pltpu.roll(c, HS - d, 0), 0.0)
        else:
            c = c + jnp.where(grow >= d, pltpu.roll(c, d, 0), 0.0)
        d *= 2
    if rev:
        t0, t1 = c[0:1, :], c[HG:HG + 1, :]
    else:
        t0, t1 = c[HG - 1:HG, :], c[HS - 1:HS, :]
    et0, et1 = jnp.exp(t0), jnp.exp(t1)
    qe = q * jnp.exp(c)
    e_out = jnp.exp(jnp.where(in_g1, t1, t0) - c)
    ke = k * e_out

    if rev:
        qb = qe * jnp.where(in_g1, 1.0, et1)
        kh = ke * jnp.where(in_g1, et0, 1.0)
    else:
        qb = qe * jnp.where(in_g1, et0, 1.0)
        kh = ke * jnp.where(in_g1, 1.0, et1)
    st = st_ref[...]
    o_int = lax.dot_general(qb.astype(BF16), st.astype(BF16), _NT, preferred_element_type=F32)
    upd = lax.dot_general(v, kh.astype(BF16), (((0,), (0,)), ((), ())), preferred_element_type=F32)
    r128 = lax.broadcasted_iota(jnp.int32, (LANE, LANE), 0)
    c128 = lax.broadcasted_iota(jnp.int32, (LANE, LANE), 1)
    st_ref[...] = st * (et0 * et1) + jnp.where((r128 >> 6) == (c128 >> 6), upd, 0.0)

    q_late = in_g1 if not rev else jnp.logical_not(in_g1)
    q_top = jnp.where(q_late, qe, 0.0)
    k_top = jnp.where(q_late, 0.0, ke).astype(BF16)
    top = [lax.dot_general(jnp.where(m, q_top, 0.0).astype(BF16), k_top, _NT, preferred_element_type=F32)
           for m in (lo, hi)]
    mid = HG // 2 if rev else HG // 2 - 1
    cm = c - jnp.where(in_g1, c[HG + mid:HG + mid + 1, :], c[mid:mid + 1, :])
    return c, top, o_int, cm, jnp.max(jnp.abs(cm))


def _hgrn_steps(jobs, bot_ref):
    lane = lax.broadcasted_iota(jnp.int32, (1, LANE), 1)
    lo = lane < 64
    hi = jnp.logical_not(lo)
    heads = [_hgrn_head(*job) for job in jobs]
    fast = functools.reduce(jnp.maximum, [h[4] for h in heads]) <= FAST_DECAY_LIMIT

    @pl.when(fast)
    def _():
        srow = lax.broadcasted_iota(jnp.int32, (HS, HS), 0)
        scol = lax.broadcasted_iota(jnp.int32, (HS, HS), 1)
        same = (srow >> 6) == (scol >> 6)
        for ji, (job, (_, _, _, cm, _)) in enumerate(zip(jobs, heads)):
            keep = same & ((scol >= srow) if job[5] else (scol <= srow))
            qf = job[0] * jnp.exp(cm)
            kf = (job[1] * jnp.exp(-cm)).astype(BF16)
            for parity, m in enumerate((lo, hi)):
                s = lax.dot_general(jnp.where(m, qf, 0.0).astype(BF16), kf, _NT, preferred_element_type=F32)
                bot_ref[ji, parity] = jnp.where(keep, s, 0.0)

    @pl.when(jnp.logical_not(fast))
    def _():
        for ji, (job, (c, _, _, _, _)) in enumerate(zip(jobs, heads)):
            for parity, s in enumerate(_hgrn_bottom_exact(job[0], job[1], c, lo, job[5])):
                bot_ref[ji, parity] = s

    outs = []
    for ji, (job, (_, top, o_int, _, _)) in enumerate(zip(jobs, heads)):
        v = job[3]
        probs = jnp.concatenate([(bot_ref[ji, 0] + top[0]).astype(BF16), (bot_ref[ji, 1] + top[1]).astype(BF16)],
                                axis=1)
        vv = jnp.concatenate([jnp.where(lo, v, jnp.zeros_like(v)), jnp.where(hi, v, jnp.zeros_like(v))], axis=0)
        outs.append(jnp.dot(probs, vv, preferred_element_type=F32) + o_int)
    return outs


PAIRS_PER_ITER = 2


def _hgrn_kernel(has_s0, nt, *refs):
    if has_s0:
        (qf_ref, vf_ref, kf_ref, lf_ref, qb_ref, vb_ref, kb_ref, lb_ref, s0_ref,
         of_ref, ob_ref, so_ref, st_scr, bot_scr) = refs
    else:
        (qf_ref, vf_ref, kf_ref, lf_ref, qb_ref, vb_ref, kb_ref, lb_ref,
         of_ref, ob_ref, so_ref, st_scr, bot_scr) = refs
    j = pl.program_id(1)

    @pl.when(j == 0)
    def _():
        if has_s0:
            st_scr[...] = s0_ref[...]
        else:
            st_scr[...] = jnp.zeros_like(st_scr)

    n_sub = HL // HS

    def pair_body(it, carry):
        for step in range(n_sub):
            jobs, dests = [], []
            for pp in range(PAIRS_PER_ITER):
                p = it * PAIRS_PER_ITER + pp
                cols = pl.ds(pl.multiple_of(p * LANE, LANE), LANE)
                rf = pl.ds(step * HS, HS)
                rb = pl.ds((n_sub - 1 - step) * HS, HS)
                jobs.append((qf_ref[rf, cols].astype(F32), kf_ref[rf, cols], lf_ref[rf, cols], vf_ref[rf, cols],
                             st_scr.at[p, 0], False))
                dests.append((of_ref, rf, cols))
                jobs.append((qb_ref[rb, cols].astype(F32), kb_ref[rb, cols], lb_ref[rb, cols], vb_ref[rb, cols],
                             st_scr.at[p, 1], True))
                dests.append((ob_ref, rb, cols))
            for (ref, rows, cols), o in zip(dests, _hgrn_steps(jobs, bot_scr)):
                ref[rows, cols] = o
        return carry

    lax.fori_loop(0, N_PAIRS // PAIRS_PER_ITER, pair_body, 0)

    @pl.when(j == nt - 1)
    def _():
        so_ref[...] = st_scr[...]


def _hgrn(qc, vc, kff, lff, kfb, lfb, s0=None, prev=None):
    if s0 is None:
        nb, nt, blk0 = BATCH, SEQ // HL, 0
    else:
        nb, nt, blk0 = DEC_BATCH, DEC_SEQ // HL, T_CTX // HL
    fspec = pl.BlockSpec((HL, 512), lambda b, j: (blk0 + b * nt + j, 0))
    bspec = pl.BlockSpec((HL, 512), lambda b, j: (blk0 + b * nt + nt - 1 - j, 0))
    sspec = pl.BlockSpec((None, N_PAIRS, 2, LANE, LANE), lambda b, j: (b, 0, 0, 0, 0))
    in_specs = [fspec] * 4 + [bspec] * 4
    args = [qc, vc, kff, lff, qc, vc, kfb, lfb]
    aliases = {}
    if s0 is not None:
        in_specs += [sspec]
        args += [s0]
        in_specs += [pl.BlockSpec(memory_space=pl.ANY)] * 2
        args += list(prev)
        aliases = {len(args) - 2: 0, len(args) - 1: 1}

    def body(*refs):
        if s0 is not None:
            refs = refs[:9] + refs[11:]
        _hgrn_kernel(s0 is not None, nt, *refs)

    return pl.pallas_call(
        body,
        grid=(nb, nt),
        in_specs=in_specs,
        out_specs=[fspec, bspec, sspec],
        out_shape=[jax.ShapeDtypeStruct((T_ALL, 512), F32), jax.ShapeDtypeStruct((T_ALL, 512), F32),
                   jax.ShapeDtypeStruct((nb, N_PAIRS, 2, LANE, LANE), F32)],
        scratch_shapes=[pltpu.VMEM((N_PAIRS, 2, LANE, LANE), F32),
                        pltpu.VMEM((2 * PAIRS_PER_ITER, 2, HS, HS), F32)],
        input_output_aliases=aliases,
        compiler_params=_cparams(("parallel", "arbitrary")),
        name="hgrn_ctx" if s0 is None else "hgrn_lat",
    )(*args)


def _merge_kernel(xc_ref, xl_ref, mod_ref, n1_ref, wg_ref, oa_ref, ob_ref, ocf_ref, ocb_ref, sg_ref, con_ref,
                  ones_ref, wbr_ref, wout_ref, n2_ref, wrh_ref, wrl_ref, br_ref, x1_o, h2_o, comb_o):
    x = _x_tile(xc_ref, xl_ref, TMG)
    mod = mod_ref[...]
    xn = x * lax.rsqrt(jnp.mean(x * x, axis=-1, keepdims=True) + EPS) * n1_ref[...]
    h = (xn * (1.0 + mod[:, D:2 * D]) + mod[:, 0:D]).astype(BF16)

    oc = ocf_ref[...] + ocb_ref[...]
    oc = oc * lax.rsqrt(_group_mean(oc * oc, ones_ref[...], DV_C) + EPS) * con_ref[...]
    oc = (oc * sg_ref[...].astype(F32)).astype(BF16)
    branches = (oa_ref[...], ob_ref[...], oc)
    mix = None
    for jb in range(3):
        gate = jax.nn.sigmoid(jnp.dot(h, wg_ref[:, jb * D:(jb + 1) * D], preferred_element_type=F32))
        t = gate * jnp.dot(branches[jb], wbr_ref[jb], preferred_element_type=F32)
        mix = t if mix is None else mix + t
    out = jnp.dot(mix.astype(BF16), wout_ref[...], preferred_element_type=F32)
    x1 = x + mod[:, 2 * D:3 * D] * out
    x1_o[...] = x1

    x1n = x1 * lax.rsqrt(jnp.mean(x1 * x1, axis=-1, keepdims=True) + EPS) * n2_ref[...]
    h2 = x1n * (1.0 + mod[:, 4 * D:5 * D]) + mod[:, 3 * D:4 * D]
    h2_o[...] = h2.astype(BF16)

    h2h, h2l = _split_hi_lo(h2)
    logits = (jnp.dot(h2h, wrh_ref[...], preferred_element_type=F32)
              + jnp.dot(h2l, wrh_ref[...], preferred_element_type=F32)
              + jnp.dot(h2h, wrl_ref[...], preferred_element_type=F32)) + br_ref[...]
    lane = lax.broadcasted_iota(jnp.int32, logits.shape, 1)
    neg = -jnp.inf
    is_g = lane < N_GROUPS
    gl = jnp.where(is_g, logits, neg)
    gmax = gl.max(axis=-1, keepdims=True)
    gidx = jnp.min(jnp.where(gl == gmax, lane, LANE), axis=-1, keepdims=True)
    gp = 1.0 / jnp.sum(jnp.where(is_g, jnp.exp(gl - gmax), 0.0), axis=-1, keepdims=True)
    eid = lane - N_GROUPS
    in_grp = (eid >= 0) & (eid < N_EXPERTS) & ((eid >> 2) == gidx)
    el = jnp.where(in_grp, logits, neg)
    v1 = el.max(axis=-1, keepdims=True)
    i1 = jnp.min(jnp.where(el == v1, lane, LANE), axis=-1, keepdims=True)
    el2 = jnp.where(lane == i1, neg, el)
    v2 = el2.max(axis=-1, keepdims=True)
    i2 = jnp.min(jnp.where(el2 == v2, lane, LANE), axis=-1, keepdims=True)
    e2 = jnp.exp(v2 - v1)
    w1 = gp / (1.0 + e2)
    w2 = gp * e2 / (1.0 + e2)
    comb = jnp.where(lane == i1, w1, 0.0) + jnp.where(lane == i2, w2, 0.0)
    comb_o[...] = pltpu.roll(comb, LANE - N_GROUPS, 1)


TMG = 512


def _merge(x, mod, n1, wgate, oa, ob, ocf, ocb, sg, con, ones, wbr, wout, n2, wr, br):
    ctx_tiles, per_seq = T_CTX // TMG, DEC_SEQ // TMG
    mrow = lambda i: jnp.where(i < ctx_tiles, CTX_MOD_ROW, (i - ctx_tiles) // per_seq)
    tspec = lambda w: pl.BlockSpec((TMG, w), lambda i: (i, 0))
    wspec = lambda shape: pl.BlockSpec(shape, lambda i: (0,) * len(shape), pipeline_mode=pl.Buffered(1))
    wr_hi, wr_lo = _split_hi_lo(wr)
    return pl.pallas_call(
        _merge_kernel,
        grid=(T_ALL // TMG,),
        in_specs=_x_specs(x, TMG) + [
            pl.BlockSpec((None, 1, 6 * D), lambda i: (mrow(i), 0, 0)),
            _const_spec((1, D)), wspec((D, 3 * D)),
            tspec(512), tspec(512), tspec(512), tspec(512), tspec(512),
            _const_spec((1, 512)), _const_spec((256, 256)),
            wspec((3, BRANCH_W, D)), wspec((D, D)), _const_spec((1, D)),
            _const_spec((D, LANE)), _const_spec((D, LANE)), _const_spec((1, LANE)),
        ],
        out_specs=[tspec(D), tspec(D), tspec(LANE)],
        out_shape=[jax.ShapeDtypeStruct((T_ALL, D), F32), jax.ShapeDtypeStruct((T_ALL, D), BF16),
                   jax.ShapeDtypeStruct((T_ALL, LANE), F32)],
        compiler_params=_cparams(("parallel",)),
        name="merge",
    )(*_x_pair(x)[:2], mod.reshape(MOD_ROWS, 1, 6 * D), n1, wgate, oa, ob, ocf, ocb, sg, con, ones, wbr, wout, n2,
      wr_hi, wr_lo, br)


NB = 2048
N_BLK = T_ALL // NB
SBK = 256
N_SB = NB // SBK
WIN_SHIFT, FT_SHIFT, SEG_SHIFT = 6, 7, 4
WIN = 1 << WIN_SHIFT
FT = 1 << FT_SHIFT
SEG_ALIGN = 1 << SEG_SHIFT
STG = 2 * NB + N_EXPERTS * SEG_ALIGN + 256
QUAD = 4
E_STEP = 2


def _route_kernel(comb_ref, tri_ref, upper_ref, rank_ref, carry_ref, tab_ref, carry_scr):
    s = pl.program_id(1)

    @pl.when(s == 0)
    def _():
        carry_scr[...] = jnp.zeros_like(carry_scr)

    routed = comb_ref[...] > 0.0
    ind = jnp.where(routed, 1.0, 0.0)
    carry = carry_scr[...]
    rank = jnp.dot(tri_ref[...], ind.astype(BF16), preferred_element_type=F32) + carry
    rank_ref[...] = jnp.where(routed, rank, -1.0)
    carry_ref[...] = carry
    count = carry + jnp.sum(ind, axis=0, keepdims=True)
    carry_scr[...] = count

    @pl.when(s == N_SB - 1)
    def _():
        seg = jnp.floor((count + (SEG_ALIGN - 1.0)) * (1.0 / SEG_ALIGN)) * SEG_ALIGN
        off = jnp.dot(jnp.broadcast_to(seg, (8, LANE)), upper_ref[...], preferred_element_type=F32,
                      precision=lax.Precision.HIGHEST)
        tab_ref[0:1, :] = count
        tab_ref[1:2, :] = off[0:1, :]


def _route(comb):
    tri = jnp.asarray(np.tril(np.ones((SBK, SBK)), -1), BF16)
    upper = jnp.asarray(np.triu(np.ones((LANE, LANE)), 1), F32)
    return pl.pallas_call(
        _route_kernel,
        grid=(N_BLK, N_SB),
        in_specs=[
            pl.BlockSpec((SBK, LANE), lambda b, s: (b * N_SB + s, 0)),
            pl.BlockSpec((SBK, SBK), lambda b, s: (0, 0)),
            pl.BlockSpec((LANE, LANE), lambda b, s: (0, 0)),
        ],
        out_specs=[
            pl.BlockSpec((SBK, LANE), lambda b, s: (b * N_SB + s, 0)),
            pl.BlockSpec((None, None, 1, LANE), lambda b, s: (b, s, 0, 0)),
            pl.BlockSpec((None, 2, LANE), lambda b, s: (b, 0, 0)),
        ],
        out_shape=[jax.ShapeDtypeStruct((T_ALL, LANE), F32),
                   jax.ShapeDtypeStruct((N_BLK, N_SB, 1, LANE), F32),
                   jax.ShapeDtypeStruct((N_BLK, 2, LANE), F32)],
        scratch_shapes=[pltpu.VMEM((1, LANE), F32)],
        compiler_params=_cparams(("parallel", "arbitrary")),
        name="moe_route",
    )(comb, tri, upper)


def _moe_kernel(cnt_s, off_s, car_s, h2_ref, rank_ref, comb_ref, offv_ref, eg_ref, eu_ref, ed_ref, o_ref,
                stg_ref, acc_ref):
    blk = pl.program_id(0)
    step = pl.program_id(1)
    lane = lax.broadcasted_iota(jnp.int32, (1, LANE), 1)
    lane_lo = (lane & (2 * WIN - 1)) < WIN
    slot = (lane & (WIN - 1)).astype(F32)
    srow = lax.broadcasted_iota(jnp.int32, (WIN, SBK), 0).astype(F32)

    def windows(s):
        out = []
        for ex in range(N_EXPERTS):
            start = off_s[blk, ex] + car_s[blk, s, ex]
            length = car_s[blk, s + 1, ex] - car_s[blk, s, ex]
            ws = (start >> SEG_SHIFT) << SEG_SHIFT
            out.append((ws, (start - ws + length + (WIN - 1)) >> WIN_SHIFT))
        return out, functools.reduce(jnp.maximum, [w[1] for w in out])

    def positions(s):
        rows = pl.ds(pl.multiple_of(s * SBK, SBK), SBK)
        rank = rank_ref[rows, :]
        return rows, jnp.where(rank >= 0.0, rank + offv_ref[1:2, :], -1.0e6)

    @pl.when(step == 0)
    def _():
        stg_ref[...] = jnp.zeros_like(stg_ref)

        def sub_body(s, carry):
            rows, pos = positions(s)
            pos_t = pos.T
            h2 = h2_ref[rows, :]
            wins, nmax = windows(s)

            def chunk_body(c, carry2):
                for quad in range(N_EXPERTS // QUAD):
                    blocks = []
                    for ex in range(quad * QUAD, (quad + 1) * QUAD):
                        base = (wins[ex][0] + c * WIN).astype(F32)
                        hit = (pos_t[ex:ex + 1, :] - base) == srow
                        blocks.append(jnp.where(hit, 1.0, 0.0).astype(BF16))
                    moved = jnp.dot(jnp.concatenate(blocks, axis=0), h2, preferred_element_type=F32).astype(BF16)
                    for i in range(QUAD):
                        ws, nch = wins[quad * QUAD + i]

                        @pl.when(c < nch)
                        def _():
                            dst = pl.ds(pl.multiple_of(ws + c * WIN, SEG_ALIGN), WIN)
                            stg_ref[dst, :] = stg_ref[dst, :] + moved[i * WIN:(i + 1) * WIN, :]
                return carry2

            lax.fori_loop(0, nmax, chunk_body, 0)
            return carry

        lax.fori_loop(0, N_SB, sub_body, 0)

    def ffn_tile(j, first, n_rows, n_valid):
        rows = pl.ds(pl.multiple_of(first, SEG_ALIGN), n_rows)
        xs = stg_ref[rows, :]
        hg = jnp.dot(xs, eg_ref[j], preferred_element_type=F32)
        hu = jnp.dot(xs, eu_ref[j], preferred_element_type=F32)
        act = (hg * jax.nn.sigmoid(hg) * hu).astype(BF16)
        y = jnp.dot(act, ed_ref[j], preferred_element_type=F32).astype(BF16)
        rid = lax.broadcasted_iota(jnp.int32, (n_rows, 1), 0)
        stg_ref[rows, :] = jnp.where(rid < n_valid, y, xs)

    for j in range(E_STEP):
        ex = step * E_STEP + j
        count = cnt_s[blk, ex]
        seg0 = off_s[blk, ex]
        n_big = (count + (FT - 1)) >> (FT_SHIFT + 1)

        def ffn_body(t, carry, j=j, count=count, seg0=seg0):
            ffn_tile(j, seg0 + t * (2 * FT), 2 * FT, count - t * (2 * FT))
            return carry

        lax.fori_loop(0, n_big, ffn_body, 0)

        @pl.when(count > n_big * (2 * FT))
        def _(j=j, count=count, seg0=seg0, n_big=n_big):
            ffn_tile(j, seg0 + n_big * (2 * FT), FT, count - n_big * (2 * FT))

    @pl.when(step == N_EXPERTS // E_STEP - 1)
    def _():
        def sub_body(s, carry):
            rows, pos = positions(s)
            wts = comb_ref[rows, :]
            wins, nmax = windows(s)
            acc_ref[...] = jnp.zeros_like(acc_ref)

            def chunk_body(c, carry2):
                sel, srcs = [], []
                for ex0 in range(0, N_EXPERTS, 2):
                    b0 = (wins[ex0][0] + c * WIN).astype(F32)
                    b1 = (wins[ex0 + 1][0] + c * WIN).astype(F32)
                    rel = jnp.where(lane_lo, pos[:, ex0:ex0 + 1] - b0, pos[:, ex0 + 1:ex0 + 2] - b1)
                    w = jnp.where(lane_lo, wts[:, ex0:ex0 + 1], wts[:, ex0 + 1:ex0 + 2])
                    sel.append(jnp.where(rel == slot, w, 0.0).astype(BF16))
                for ex in range(N_EXPERTS):
                    first = jnp.minimum(wins[ex][0] + c * WIN, STG - WIN)
                    srcs.append(stg_ref[pl.ds(pl.multiple_of(first, SEG_ALIGN), WIN), :])
                acc_ref[...] += jnp.dot(jnp.concatenate(sel, axis=1), jnp.concatenate(srcs, axis=0),
                                        preferred_element_type=F32)
                return carry2

            lax.fori_loop(0, nmax, chunk_body, 0)
            o_ref[rows, :] = acc_ref[...].astype(BF16)
            return carry

        lax.fori_loop(0, N_SB, sub_body, 0)


def _moe(h2, comb, eg, eu, ed):
    rank, carry, tab = _route(comb)
    cnt_i = tab[:, 0, :N_EXPERTS].astype(jnp.int32)
    off_i = tab[:, 1, :N_EXPERTS].astype(jnp.int32)
    car_i = jnp.concatenate([carry[:, :, 0, :N_EXPERTS].astype(jnp.int32), cnt_i[:, None, :]], axis=1)
    bspec = lambda w: pl.BlockSpec((NB, w), lambda b, e, *_: (b, 0))
    grid_spec = pltpu.PrefetchScalarGridSpec(
        num_scalar_prefetch=3,
        grid=(N_BLK, N_EXPERTS // E_STEP),
        in_specs=[
            bspec(D), bspec(LANE), bspec(LANE),
            pl.BlockSpec((None, 2, LANE), lambda b, e, *_: (b, 0, 0)),
            pl.BlockSpec((E_STEP, D, D_EXPERT), lambda b, e, *_: (e, 0, 0)),
            pl.BlockSpec((E_STEP, D, D_EXPERT), lambda b, e, *_: (e, 0, 0)),
            pl.BlockSpec((E_STEP, D_EXPERT, D), lambda b, e, *_: (e, 0, 0)),
        ],
        out_specs=bspec(D),
        scratch_shapes=[pltpu.VMEM((STG, D), BF16), pltpu.VMEM((SBK, D), F32)],
    )
    return pl.pallas_call(
        _moe_kernel,
        grid_spec=grid_spec,
        out_shape=jax.ShapeDtypeStruct((T_ALL, D), BF16),
        compiler_params=_cparams(("parallel", "arbitrary")),
        name="moe",
    )(cnt_i, off_i, car_i, h2, rank, comb, tab, eg, eu, ed)


def _residual_kernel(final, x1_ref, moe_ref, mod_ref, fg_ref, o_ref):
    x2 = x1_ref[...] + mod_ref[:, 5 * D:6 * D] * moe_ref[...].astype(F32)
    if final:
        x2 = x2 * lax.rsqrt(jnp.mean(x2 * x2, axis=-1, keepdims=True) + EPS) * fg_ref[...]
    o_ref[...] = x2


def _residual(final, x1, moe, mod, fg, tile0=0, n_tiles=N_TILES):
    src = lambda: pl.BlockSpec((TM, D), lambda i: (tile0 + i, 0))
    return pl.pallas_call(
        functools.partial(_residual_kernel, final),
        grid=(n_tiles,),
        in_specs=[src(), src(),
                  pl.BlockSpec((None, 1, 6 * D), lambda i: (_mod_row(tile0 + i), 0, 0)), _const_spec((1, D))],
        out_specs=_tile_spec(D),
        out_shape=jax.ShapeDtypeStruct((n_tiles * TM, D), F32),
        compiler_params=_cparams(("parallel",)),
        name="residual",
    )(x1, moe, mod.reshape(MOD_ROWS, 1, 6 * D), fg)


def _state_to_blockdiag(s):
    st = jnp.swapaxes(s, -1, -2).reshape(s.shape[0], H_C // 2, 2, DV_C, DK_C)
    z = jnp.zeros_like(st[:, :, 0])
    top = jnp.concatenate([st[:, :, 0], z], axis=-1)
    bot = jnp.concatenate([z, st[:, :, 1]], axis=-1)
    return jnp.concatenate([top, bot], axis=-2)


def _blockdiag_to_state(sb):
    even = sb[:, :, :DV_C, :DK_C]
    odd = sb[:, :, DV_C:, DK_C:]
    st = jnp.stack([even, odd], axis=2).reshape(sb.shape[0], H_C, DV_C, DK_C)
    return jnp.swapaxes(st, -1, -2)


def kernel(x_prompt, x_sample, cache_gqa_k, cache_gqa_v, cache_mla_ckv, cache_mla_krope, state_hgrn, c, c_ctx,
           w_mod, b_mod, norm1_g, norm2_g, w_in, a_qnorm, a_knorm, b_qnorm, b_wq, b_kvnorm, b_wkv, c_lb_logits,
           c_onorm, w_branch, w_out, r_group_w, r_group_b, r_expert_w, r_expert_b, e_gate, e_up, e_down, final_g):
    x = (x_prompt.reshape(T_CTX, D), x_sample.reshape(T_LAT, D))
    cvec = jnp.concatenate([c, c_ctx[None, :], jnp.zeros((MOD_ROWS - DEC_BATCH - 1, D), F32)], axis=0)
    mod_all = _mod_table(cvec, w_mod, b_mod)
    taba, tabb, tabk = _rope_tables()
    ones = _ones_block(256, 64)
    lbl = c_lb_logits.reshape(DEPTH, 2, H_C * DK_C)
    zpad = jnp.zeros((DEC_BATCH * PAST, 32), F32)

    new_k, new_v, new_ckv, new_kr, new_s = [], [], [], [], []
    for l in range(DEPTH):
        mod = mod_all[l]
        w_in_p, w_gate = _pack_w_in(w_in[l])
        wq_p, wkv_p = _pack_wq(b_wq[l]), _pack_wkv(b_wkv[l])
        (qa, kan, ka, va, qb, ckv, kvb, kr, kre, lff, lfb, kff, kfb, qc, vc, sg) = _inproj(
            l, x, mod, norm1_g[l][None, :], w_in_p, jnp.tile(a_qnorm[l], H_A)[None, :],
            jnp.tile(a_knorm[l], KV_A)[None, :], b_qnorm[l][None, :], wq_p, b_kvnorm[l][None, :], wkv_p,
            lbl, ones, taba, tabb, tabk)
        new_k.append(kan[:T_CTX].reshape(BATCH, SEQ, KV_A, HD_A))
        new_v.append(va[:T_CTX].reshape(BATCH, SEQ, KV_A, HD_A))
        new_ckv.append(ckv[:T_CTX].reshape(BATCH, SEQ, KV_RANK))
        new_kr.append(kr[:T_CTX, :ROPE_B].reshape(BATCH, SEQ, ROPE_B))

        ck = cache_gqa_k[:, l].reshape(DEC_BATCH * PAST, KV_A * HD_A)
        cv = cache_gqa_v[:, l].reshape(DEC_BATCH * PAST, KV_A * HD_A)
        ckvb = _ctx_kv(cache_mla_ckv[:, l].reshape(DEC_BATCH * PAST, KV_RANK), wkv_p)
        ckr = cache_mla_krope[:, l].reshape(DEC_BATCH * PAST, ROPE_B)
        ckre = jnp.concatenate([ckr, zpad, ckr, zpad], axis=1)
        oa, ob = _attention(qa, qb, ka, va, kvb, kre)
        oa, ob = _attention(qa, qb, ka, va, kvb, kre, prev=(oa, ob), cache=(ck, cv, ckvb, ckre))

        ocf, ocb, s_ctx = _hgrn(qc, vc, kff, lff, kfb, lfb)
        s0 = jnp.stack([_state_to_blockdiag(state_hgrn[:, l, 0]), _state_to_blockdiag(state_hgrn[:, l, 1])], axis=2)
        ocf, ocb, _ = _hgrn(qc, vc, kff, lff, kfb, lfb, s0=s0, prev=(ocf, ocb))
        new_s.append(jnp.stack([_blockdiag_to_state(s_ctx[:, :, 0]), _blockdiag_to_state(s_ctx[:, :, 1])], axis=1))

        wr = jnp.concatenate([r_group_w[l], r_expert_w[l], jnp.zeros((D, LANE - N_GROUPS - N_EXPERTS), F32)], axis=1)
        br = jnp.concatenate([r_group_b[l], r_expert_b[l], jnp.zeros((LANE - N_GROUPS - N_EXPERTS,), F32)])[None, :]
        x1, h2, comb = _merge(x, mod, norm1_g[l][None, :], w_gate, oa, ob, ocf, ocb, sg,
                              jnp.tile(c_onorm[l], H_C)[None, :], ones, w_branch[l].astype(BF16),
                              w_out[l].astype(BF16), norm2_g[l][None, :], wr, br)
        moe = _moe(h2, comb, e_gate[l].astype(BF16), e_up[l].astype(BF16), e_down[l].astype(BF16))
        if l < DEPTH - 1:
            x = _residual(False, x1, moe, mod, final_g[None, :])

    y_prompt = _residual(True, x1, moe, mod, final_g[None, :], 0, CTX_TILES).reshape(BATCH, SEQ, D)
    y_sample = _residual(True, x1, moe, mod, final_g[None, :], CTX_TILES, N_TILES - CTX_TILES).reshape(
        DEC_BATCH, DEC_SEQ, D)
    return (y_prompt, y_sample, jnp.stack(new_k, axis=1), jnp.stack(new_v, axis=1), jnp.stack(new_ckv, axis=1),
            jnp.stack(new_kr, axis=1), jnp.stack(new_s, axis=1))
```

```python
import functools

import numpy as np
import jax
import jax.numpy as jnp
from jax import lax
from jax.experimental import pallas as pl
from jax.experimental.pallas import tpu as pltpu

D = 1024
BATCH, SEQ = 32, 256
DEC_BATCH, DEC_SEQ = 8, 1024
PAST = 256
DEPTH = 2
GRID_W = 64
THETA = 10000.0
EPS = 1e-6
F_FLOOR = 1e-30
H_A, KV_A, HD_A = 8, 2, 64
H_B, Q_RANK, KV_RANK, NOPE_B, ROPE_B, V_B = 8, 384, 256, 64, 32, 64
H_C, DK_C, DV_C = 8, 64, 64
BRANCH_W = 512
N_GROUPS, E_PER_GROUP, N_EXPERTS, D_EXPERT = 4, 4, 16, 512

T_CTX = BATCH * SEQ
T_LAT = DEC_BATCH * DEC_SEQ
T_ALL = T_CTX + T_LAT
TM = 256
N_TILES = T_ALL // TM
CTX_TILES = T_CTX // TM
LAT_TILES_PER_SEQ = DEC_SEQ // TM
MOD_ROWS = 16
CTX_MOD_ROW = DEC_BATCH
LANE = 128
VMEM_LIMIT = 56 * 1024 * 1024

C_QA, C_KA, C_VA, C_QRA, C_KVA, C_KR = 0, 512, 640, 768, 1152, 1408
C_FF, C_FB, C_QC, C_IC, C_GC, C_END = 1536, 2048, 2560, 3072, 3584, 4096
R_QA, R_KA, R_VA, R_QRA, R_KVA, R_KR = 0, 512, 640, 768, 1152, 1408
R_FF, R_FB, R_QC, R_IC, R_GC, R_GATE, R_END = 1440, 1952, 2464, 2976, 3488, 4000, 7072

F32 = jnp.float32
BF16 = jnp.bfloat16


def _cparams(sem):
    return pltpu.CompilerParams(dimension_semantics=sem, vmem_limit_bytes=VMEM_LIMIT)


def _mod_row(i):
    return jnp.where(i < CTX_TILES, CTX_MOD_ROW, (i - CTX_TILES) // LAT_TILES_PER_SEQ)


def _pos_block(i):
    return jnp.where(i < CTX_TILES, LAT_TILES_PER_SEQ, (i - CTX_TILES) % LAT_TILES_PER_SEQ)


def _split_hi_lo(x):
    hi = x.astype(BF16)
    lo = (x - hi.astype(F32)).astype(BF16)
    return hi, lo


def _group_mean(x2, ones_blk, width):
    n = ones_blk.shape[0]
    outs = []
    for j in range(x2.shape[-1] // n):
        blk = x2[:, j * n:(j + 1) * n]
        hi, lo = _split_hi_lo(blk)
        s = jnp.dot(hi, ones_blk, preferred_element_type=F32) + jnp.dot(lo, ones_blk, preferred_element_type=F32)
        outs.append(s)
    s = outs[0] if len(outs) == 1 else jnp.concatenate(outs, axis=-1)
    return s * (1.0 / width)


def _rope(x, tab_ref, shift, period):
    c, s1, s2 = tab_ref[0], tab_ref[1], tab_ref[2]
    outs = []
    for j in range(x.shape[-1] // period):
        blk = x[:, j * period:(j + 1) * period]
        outs.append(blk * c + pltpu.roll(blk, shift, 1) * s1 + pltpu.roll(blk, period - shift, 1) * s2)
    return outs[0] if len(outs) == 1 else jnp.concatenate(outs, axis=-1)


def _mod_kernel(c_ref, w_ref, b_ref, o_ref):
    c = c_ref[...]
    a = c * jax.nn.sigmoid(c)
    o_ref[...] = jnp.dot(a, w_ref[...], preferred_element_type=F32, precision=lax.Precision.HIGHEST) + b_ref[...]


def _mod_table(cvec, w_mod, b_mod):
    nt = 1024
    return pl.pallas_call(
        _mod_kernel,
        grid=(DEPTH, 6 * D // nt),
        in_specs=[
            pl.BlockSpec((MOD_ROWS, D), lambda l, j: (0, 0)),
            pl.BlockSpec((None, D, nt), lambda l, j: (l, 0, j)),
            pl.BlockSpec((None, 1, nt), lambda l, j: (l, 0, j)),
        ],
        out_specs=pl.BlockSpec((None, MOD_ROWS, nt), lambda l, j: (l, 0, j)),
        out_shape=jax.ShapeDtypeStruct((DEPTH, MOD_ROWS, 6 * D), F32),
        compiler_params=_cparams(("arbitrary", "arbitrary")),
        name="mod_table",
    )(cvec, w_mod, b_mod.reshape(DEPTH, 1, 6 * D))


def _x_pair(x):
    if isinstance(x, tuple):
        return x[0], x[1], 0
    return x, x, T_CTX


def _x_specs(x, tile):
    _, _, lat_off = _x_pair(x)
    ctx = T_CTX // tile
    return [pl.BlockSpec((tile, D), lambda i: (jnp.minimum(i, ctx - 1), 0)),
            pl.BlockSpec((tile, D), lambda i: (jnp.maximum(i - ctx, 0) + lat_off // tile, 0))]


def _x_tile(xc_ref, xl_ref, tile):
    return jnp.where(pl.program_id(0) < T_CTX // tile, xc_ref[...], xl_ref[...])


def _inproj_kernel(layer, xc_ref, xl_ref, mod_ref, n1_ref, w_ref, aq_ref, ak_ref, bq_ref, wq_ref, bkv_ref, wkv_ref,
                   lbl_ref, ones_ref, taba_ref, tabb_ref, tabk_ref,
                   qa_o, kan_o, ka_o, va_o, qb_o, ckv_o, kvb_o, kr_o, kre_o,
                   lff_o, lfb_o, kff_o, kfb_o, qc_o, vc_o, sg_o):
    x = _x_tile(xc_ref, xl_ref, TM)
    mod = mod_ref[...]
    xn = x * lax.rsqrt(jnp.mean(x * x, axis=-1, keepdims=True) + EPS) * n1_ref[...]
    h = (xn * (1.0 + mod[:, D:2 * D]) + mod[:, 0:D]).astype(BF16)
    y = jnp.dot(h, w_ref[...], preferred_element_type=F32)
    ones = ones_ref[...]

    qa = y[:, C_QA:C_KA]
    qa = qa * lax.rsqrt(_group_mean(qa * qa, ones, HD_A) + EPS) * aq_ref[...]
    qa_o[...] = (_rope(qa, taba_ref, 16, LANE) * (HD_A ** -0.5)).astype(BF16)
    ka = y[:, C_KA:C_VA]
    ka = ka * lax.rsqrt(_group_mean(ka * ka, ones[:LANE, :LANE], HD_A) + EPS) * ak_ref[...]
    kan_o[...] = ka
    ka_o[...] = _rope(ka, taba_ref, 16, LANE).astype(BF16)
    va_o[...] = y[:, C_VA:C_QRA]

    qr = y[:, C_QRA:C_KVA]
    qr = qr * lax.rsqrt(jnp.mean(qr * qr, axis=-1, keepdims=True) + EPS) * bq_ref[...]
    qb = jnp.dot(qr.astype(BF16), wq_ref[...], preferred_element_type=F32)
    qb_o[...] = (_rope(qb, tabb_ref, 8, 2 * LANE) * ((NOPE_B + ROPE_B) ** -0.5)).astype(BF16)
    kv = y[:, C_KVA:C_KR]
    ckv = kv * lax.rsqrt(jnp.mean(kv * kv, axis=-1, keepdims=True) + EPS) * bkv_ref[...]
    ckv_o[...] = ckv
    kvb_o[...] = jnp.dot(ckv.astype(BF16), wkv_ref[...], preferred_element_type=F32).astype(BF16)
    kr = y[:, C_KR:C_FF]
    kr_o[...] = kr
    kre_o[...] = _rope(kr, tabk_ref, 8, LANE).astype(BF16)

    lbl = lbl_ref[...]
    e = jnp.exp(lbl - jnp.max(lbl, axis=0, keepdims=True))
    p = e / jnp.sum(e, axis=0, keepdims=True)
    lb = p[0] * 0.0
    for i in range(1, layer + 1):
        lb = lb + p[i]
    for d, (c0, lf_o, kf_o) in enumerate(((C_FF, lff_o, kff_o), (C_FB, lfb_o, kfb_o))):
        pre = y[:, c0:c0 + 512]
        lbd = lb[d:d + 1, :]
        f = jnp.maximum(lbd + (1.0 - lbd) * jax.nn.sigmoid(pre), F_FLOOR)
        lf_o[...] = jnp.log(f)
        kf_o[...] = 1.0 - f
    qc_o[...] = y[:, C_QC:C_IC].astype(BF16)
    vc_o[...] = y[:, C_IC:C_GC].astype(BF16)
    gc = y[:, C_GC:C_END]
    sg_o[...] = (gc * jax.nn.sigmoid(gc)).astype(BF16)


def _const_spec(shape):
    nd = len(shape)
    return pl.BlockSpec(shape, lambda i: (0,) * nd)


def _tile_spec(width):
    return pl.BlockSpec((TM, width), lambda i: (i, 0))


def _layer_spec(tail, layer):
    return pl.BlockSpec((None,) + tuple(tail), lambda *_: (layer,) + (0,) * len(tail))


def _mod_spec(layer, row_of_tile):
    return pl.BlockSpec((None, 1, 6 * D), lambda i, *_: (layer * MOD_ROWS + row_of_tile(i), 0, 0))


def _inproj(layer, x, mod, n1, w_in_p, aq, ak, bq, wq_p, bkv, wkv_p, lbl, ones, taba, tabb, tabk):
    outs = [
        (512, BF16), (128, F32), (128, BF16), (128, F32), (1024, BF16), (256, F32), (1024, BF16),
        (128, F32), (128, BF16), (512, F32), (512, F32), (512, F32), (512, F32), (512, BF16), (512, BF16),
        (512, BF16),
    ]
    tab_spec = lambda w: pl.BlockSpec((3, TM, w), lambda i: (0, _pos_block(i), 0))
    return pl.pallas_call(
        functools.partial(_inproj_kernel, layer),
        grid=(N_TILES,),
        in_specs=_x_specs(x, TM) + [
            _mod_spec(layer, _mod_row),
            _layer_spec((1, D), layer),
            _layer_spec((D, C_END), layer),
            _layer_spec((1, 512), layer), _layer_spec((1, 128), layer), _layer_spec((1, Q_RANK), layer),
            _layer_spec((Q_RANK, 1024), layer), _layer_spec((1, KV_RANK), layer), _layer_spec((KV_RANK, 1024), layer),
            _const_spec((DEPTH, 2, 512)), _const_spec((256, 256)),
            tab_spec(LANE), tab_spec(2 * LANE), tab_spec(LANE),
        ],
        out_specs=[_tile_spec(w) for w, _ in outs],
        out_shape=[jax.ShapeDtypeStruct((T_ALL, w), dt) for w, dt in outs],
        compiler_params=_cparams(("parallel",)),
        name="inproj",
    )(*_x_pair(x)[:2], mod, n1, w_in_p, aq, ak, bq, wq_p, bkv, wkv_p, lbl, ones, taba, tabb, tabk)


def _pack_w_in(w):
    z = jnp.zeros((DEPTH, D, 32), w.dtype)
    kr = w[..., R_KR:R_FF]
    main = jnp.concatenate([w[..., :R_KR], kr, z, kr, z, w[..., R_FF:R_GATE]], axis=-1)
    return main.astype(BF16), w[..., R_GATE:].astype(BF16)


def _pack_wq(wq):
    w = wq.reshape(DEPTH, Q_RANK, H_B, NOPE_B + ROPE_B)
    nope, rope = w[..., :NOPE_B], w[..., NOPE_B:]
    z = jnp.zeros((DEPTH, Q_RANK, H_B, 32), wq.dtype)
    even = jnp.concatenate([rope, z, nope], axis=-1)
    odd = jnp.concatenate([nope, rope, z], axis=-1)
    is_even = (jnp.arange(H_B) % 2 == 0)[None, None, :, None]
    return jnp.where(is_even, even, odd).reshape(DEPTH, Q_RANK, H_B * LANE).astype(BF16)


def _pack_wkv(wkv):
    w = wkv.reshape(DEPTH, KV_RANK, H_B, NOPE_B + V_B)
    nope, v = w[..., :NOPE_B], w[..., NOPE_B:]
    is_even = (jnp.arange(H_B) % 2 == 0)[None, None, :, None]
    return jnp.where(is_even, jnp.concatenate([v, nope], -1), jnp.concatenate([nope, v], -1)).reshape(
        DEPTH, KV_RANK, H_B * LANE).astype(BF16)


def _rope_tables():
    pos = np.arange(DEC_SEQ)
    row, col = pos // GRID_W, pos % GRID_W

    def pattern(half):
        quarter = half // 2
        inv = THETA ** (-np.arange(0, half, 2, dtype=np.float64) / half)
        ang = np.concatenate([row[:, None] * inv, row[:, None] * inv, col[:, None] * inv, col[:, None] * inv], 1)
        is_x2 = np.tile(np.concatenate([np.zeros(quarter), np.ones(quarter)]), 2)[None, :]
        c = np.cos(ang)
        s1 = np.sin(ang) * is_x2
        s2 = -np.sin(ang) * (1 - is_x2)
        return c, s1, s2

    def assemble(width, spans, half):
        c, s1, s2 = pattern(half)
        tc = np.ones((DEC_SEQ + TM, width))
        t1 = np.zeros((DEC_SEQ + TM, width))
        t2 = np.zeros((DEC_SEQ + TM, width))
        for start in spans:
            tc[:DEC_SEQ, start:start + 2 * half] = c
            t1[:DEC_SEQ, start:start + 2 * half] = s1
            t2[:DEC_SEQ, start:start + 2 * half] = s2
        return jnp.asarray(np.stack([tc, t1, t2]), F32)

    taba = assemble(LANE, (0, 64), 32)
    tabb = assemble(2 * LANE, (0, 128 + 64), 16)
    tabk = assemble(LANE, (0, 64), 16)
    return taba, tabb, tabk


def _ones_block(n, width):
    g = np.arange(n) // width
    return jnp.asarray(g[:, None] == g[None, :], BF16)


def _ctxkv_kernel(c_ref, w_ref, o_ref):
    o_ref[...] = jnp.dot(c_ref[...].astype(BF16), w_ref[...], preferred_element_type=F32).astype(BF16)


def _ctx_kv(ckv_cache, wkv_p):
    rows = ckv_cache.shape[0]
    return pl.pallas_call(
        _ctxkv_kernel,
        grid=(rows // PAST,),
        in_specs=[pl.BlockSpec((PAST, KV_RANK), lambda i: (i, 0)),
                  pl.BlockSpec((None, KV_RANK, 1024), lambda i: (i % DEPTH, 0, 0))],
        out_specs=pl.BlockSpec((PAST, 1024), lambda i: (i, 0)),
        out_shape=jax.ShapeDtypeStruct((rows, 1024), BF16),
        compiler_params=_cparams(("parallel",)),
        name="ctx_kv",
    )(ckv_cache, wkv_p)


_NT = (((1,), (1,)), ((), ()))


def _softmax_pv(scores, values, parity):
    lane = lax.broadcasted_iota(jnp.int32, (1, LANE), 1)
    den_lane = 64 if parity == 0 else 0
    keep = (lane < 64) if parity == 0 else (lane >= 64)
    m = scores[0].max(axis=-1, keepdims=True)
    for s in scores[1:]:
        m = jnp.maximum(m, s.max(axis=-1, keepdims=True))
    if len(scores) == 1:
        p = jnp.exp(scores[0] - m)
        o = jnp.dot(p.astype(BF16), values[0], preferred_element_type=F32)
        return o / p.sum(axis=-1, keepdims=True)
    acc = None
    for s, v in zip(scores, values):
        p = jnp.exp((s - m).astype(BF16))
        o = jnp.dot(p, jnp.where(lane == den_lane, jnp.ones_like(v), v), preferred_element_type=F32)
        acc = o if acc is None else acc + o
    return jnp.where(keep, acc / acc[:, den_lane:den_lane + 1], 0.0)


def _attn_kernel(n_pieces, qa_ref, qb_ref, *refs):
    kv_refs = refs[:4 * n_pieces]
    oa_ref, ob_ref = refs[-2], refs[-1]
    lane = lax.broadcasted_iota(jnp.int32, (1, LANE), 1)
    lo = lane < 64

    ka = [kv_refs[4 * i][...].astype(F32) for i in range(n_pieces)]
    va = [kv_refs[4 * i + 1][...].astype(F32) for i in range(n_pieces)]

    def place(x, g, parity):
        if g != parity:
            x = pltpu.roll(x, 64, 1)
        keep = lo if parity == 0 else jnp.logical_not(lo)
        return jnp.where(keep, x, 0.0).astype(BF16)

    placed = {(g, parity): ([place(k, g, parity) for k in ka], [place(v, g, parity) for v in va])
              for g in range(KV_A) for parity in range(2)}
    for pair in range(H_A // 2):
        g = (2 * pair) // (H_A // KV_A)
        q = qa_ref[:, pair * LANE:(pair + 1) * LANE]
        acc = None
        for parity in range(2):
            ks, vs = placed[(g, parity)]
            scores = [lax.dot_general(q, k, _NT, preferred_element_type=F32) for k in ks]
            o = _softmax_pv(scores, vs, parity)
            acc = o if acc is None else acc + o
        oa_ref[:, pair * LANE:(pair + 1) * LANE] = acc.astype(BF16)

    for pair in range(H_B // 2):
        acc = None
        for parity in range(2):
            h = 2 * pair + parity
            q = qb_ref[:, h * LANE:(h + 1) * LANE]
            nope = jnp.logical_not(lo) if parity == 0 else lo
            scores, vs = [], []
            for i in range(n_pieces):
                kvb = kv_refs[4 * i + 2][:, h * LANE:(h + 1) * LANE]
                kre = kv_refs[4 * i + 3][...].astype(BF16)
                k = jnp.where(nope, kvb, kre)
                scores.append(lax.dot_general(q, k, _NT, preferred_element_type=F32))
                vs.append(jnp.where(nope, jnp.zeros_like(kvb), kvb))
            o = _softmax_pv(scores, vs, parity)
            acc = o if acc is None else acc + o
        ob_ref[:, pair * LANE:(pair + 1) * LANE] = acc.astype(BF16)


def _attention(qa, qb, ka, va, kvb, kre, prev=None, cache=None, layer=0):
    if cache is None:
        nb, nqt, nk, q_blk0, k_blk0 = BATCH, 1, SEQ, 0, 0
    else:
        nb, nqt, nk, q_blk0, k_blk0 = DEC_BATCH, DEC_SEQ // TM, DEC_SEQ, CTX_TILES, T_CTX // DEC_SEQ
    qspec = lambda w: pl.BlockSpec((TM, w), lambda b, j: (q_blk0 + b * nqt + j, 0))
    kspec = lambda w: pl.BlockSpec((nk, w), lambda b, j: (k_blk0 + b, 0))
    in_specs = [qspec(512), qspec(1024), kspec(128), kspec(128), kspec(1024), kspec(128)]
    args = [qa, qb, ka, va, kvb, kre]
    n_pieces = 1
    aliases = {}
    if cache is not None:
        cspec4 = pl.BlockSpec((None, None, PAST, 128), lambda b, j: (b, layer, 0, 0))
        cspec = lambda w: pl.BlockSpec((PAST, w), lambda b, j: (b * DEPTH + layer, 0))
        in_specs += [cspec4, cspec4, cspec(1024), cspec(128)]
        args += list(cache)
        n_pieces = 2
        in_specs += [pl.BlockSpec(memory_space=pl.ANY)] * 2
        args += list(prev)
        aliases = {len(args) - 2: 0, len(args) - 1: 1}

    def body(*refs):
        if cache is not None:
            refs = refs[:2 + 4 * n_pieces] + refs[-2:]
        _attn_kernel(n_pieces, *refs)

    return pl.pallas_call(
        body,
        grid=(nb, nqt),
        in_specs=in_specs,
        out_specs=[qspec(512), qspec(512)],
        out_shape=[jax.ShapeDtypeStruct((T_ALL, 512), BF16)] * 2,
        input_output_aliases=aliases,
        compiler_params=_cparams(("parallel", "arbitrary")),
        name="attention_ctx" if cache is None else "attention_lat",
    )(*args)


HL = 256
HS = 128
HG = 64
N_PAIRS = H_C // 2
FAST_DECAY_LIMIT = 80.0


def _hgrn_bottom_exact(q, k, c, lo, rev):
    row = lax.broadcasted_iota(jnp.int32, (HS, LANE), 0)
    srow = lax.broadcasted_iota(jnp.int32, (HS, HS), 0)
    scol = lax.broadcasted_iota(jnp.int32, (HS, HS), 1)
    out = []
    for parity in range(2):
        def dup(x):
            xs = pltpu.roll(x, 64, 1)
            return jnp.where(lo, x, xs) if parity == 0 else jnp.where(lo, xs, x)
        qd, kd, bd = dup(q), dup(k), dup(c)

        dg = row & 3
        if rev:
            dg = 3 - dg
        e = [None]
        for delta in range(1, 4):
            shifted = pltpu.roll(bd, delta if rev else HS - delta, 0)
            e.append(jnp.exp(jnp.minimum(shifted - bd, 0.0)))
        qp, kp = [], []
        for c1, c2 in ((0, 1), (2, 3)):
            cv = jnp.where(lo, c1, c2)
            dl = cv - dg
            fac = jnp.where(dl == 0, 1.0, jnp.where(dl == 1, e[1], jnp.where(dl == 2, e[2],
                            jnp.where(dl == 3, e[3], 0.0))))
            kp.append((kd * fac).astype(BF16))
            qp.append(jnp.where(dg == cv, qd, 0.0).astype(BF16))
        s = lax.dot_general(jnp.concatenate(qp, axis=1), jnp.concatenate(kp, axis=1), _NT,
                            preferred_element_type=F32)
        tot = jnp.where((srow >> 2) == (scol >> 2), s, 0.0)

        for lev in range(1, 3):
            g = 4 ** lev
            par = 4 * g
            shape3 = (HS // par, par, LANE)
            rid = lax.broadcasted_iota(jnp.int32, shape3, 1)
            dg3 = rid >> (2 * lev)
            if rev:
                dg3 = 3 - dg3
            b3, q3, k3 = bd.reshape(shape3), qd.reshape(shape3), kd.reshape(shape3)
            lo3 = lo.reshape(1, 1, LANE)
            qp, kp = [], []
            for c1, c2 in ((1, 2), (3, None)):
                idx = lambda cc: (4 - cc) * g if rev else cc * g - 1
                i1 = idx(c1)
                i2 = idx(c2) if c2 is not None else i1
                ridx = jnp.where(lo3, i1, i2)
                ref = jnp.sum(jnp.where(rid == ridx, b3, 0.0), axis=1, keepdims=True)
                cvk = jnp.where(lo3, c1, c2 if c2 is not None else 0)
                cvq = jnp.where(lo3, c1, c2 if c2 is not None else -1)
                kk = jnp.where(dg3 < cvk, k3 * jnp.exp(jnp.minimum(ref - b3, 0.0)), 0.0)
                qq = jnp.where(dg3 == cvq, q3 * jnp.exp(jnp.minimum(b3 - ref, 0.0)), 0.0)
                kp.append(kk.reshape(HS, LANE).astype(BF16))
                qp.append(qq.reshape(HS, LANE).astype(BF16))
            s = lax.dot_general(jnp.concatenate(qp, axis=1), jnp.concatenate(kp, axis=1), _NT,
                                preferred_element_type=F32)
            sh = 2 * lev + 2
            tot = tot + jnp.where((srow >> sh) == (scol >> sh), s, 0.0)
        out.append(tot)
    return out


def _hgrn_head(q, k, lf, v, st_ref, rev):
    row = lax.broadcasted_iota(jnp.int32, (HS, LANE), 0)
    lane = lax.broadcasted_iota(jnp.int32, (1, LANE), 1)
    lo = lane < 64
    hi = jnp.logical_not(lo)
    grow = row & (HG - 1)
    in_g1 = row >= HG

    c = lf
    d = 1
    while d < HG:
        if rev:
            c = c + jnp.where(grow < HG - d, pltpu.roll(c, HS - d, 0), 0.0)
        else:
            c = c + jnp.where(grow >= d, pltpu.roll(c, d, 0), 0.0)
        d *= 2
    if rev:
        t0, t1 = c[0:1, :], c[HG:HG + 1, :]
    else:
        t0, t1 = c[HG - 1:HG, :], c[HS - 1:HS, :]
    et0, et1 = jnp.exp(t0), jnp.exp(t1)
    qe = q * jnp.exp(c)
    e_out = jnp.exp(jnp.where(in_g1, t1, t0) - c)
    ke = k * e_out

    if rev:
        qb = qe * jnp.where(in_g1, 1.0, et1)
        kh = ke * jnp.where(in_g1, et0, 1.0)
    else:
        qb = qe * jnp.where(in_g1, et0, 1.0)
        kh = ke * jnp.where(in_g1, 1.0, et1)
    st = st_ref[...]
    o_int = lax.dot_general(qb.astype(BF16), st.astype(BF16), _NT, preferred_element_type=F32)
    upd = lax.dot_general(v, kh.astype(BF16), (((0,), (0,)), ((), ())), preferred_element_type=F32)
    r128 = lax.broadcasted_iota(jnp.int32, (LANE, LANE), 0)
    c128 = lax.broadcasted_iota(jnp.int32, (LANE, LANE), 1)
    st_ref[...] = st * (et0 * et1) + jnp.where((r128 >> 6) == (c128 >> 6), upd, 0.0)

    q_late = in_g1 if not rev else jnp.logical_not(in_g1)
    q_top = jnp.where(q_late, qe, 0.0)
    k_top = jnp.where(q_late, 0.0, ke).astype(BF16)
    top = [lax.dot_general(jnp.where(m, q_top, 0.0).astype(BF16), k_top, _NT, preferred_element_type=F32)
           for m in (lo, hi)]
    mid = HG // 2 if rev else HG // 2 - 1
    cm = c - jnp.where(in_g1, c[HG + mid:HG + mid + 1, :], c[mid:mid + 1, :])
    return c, top, o_int, cm, jnp.max(jnp.abs(cm))


def _hgrn_steps(jobs, bot_ref):
    lane = lax.broadcasted_iota(jnp.int32, (1, LANE), 1)
    lo = lane < 64
    hi = jnp.logical_not(lo)
    heads = [_hgrn_head(*job) for job in jobs]
    fast = functools.reduce(jnp.maximum, [h[4] for h in heads]) <= FAST_DECAY_LIMIT

    @pl.when(fast)
    def _():
        srow = lax.broadcasted_iota(jnp.int32, (HS, HS), 0)
        scol = lax.broadcasted_iota(jnp.int32, (HS, HS), 1)
        same = (srow >> 6) == (scol >> 6)
        for ji, (job, (_, _, _, cm, _)) in enumerate(zip(jobs, heads)):
            keep = same & ((scol >= srow) if job[5] else (scol <= srow))
            qf = job[0] * jnp.exp(cm)
            kf = (job[1] * jnp.exp(-cm)).astype(BF16)
            for parity, m in enumerate((lo, hi)):
                s = lax.dot_general(jnp.where(m, qf, 0.0).astype(BF16), kf, _NT, preferred_element_type=F32)
                bot_ref[ji, parity] = jnp.where(keep, s, 0.0)

    @pl.when(jnp.logical_not(fast))
    def _():
        for ji, (job, (c, _, _, _, _)) in enumerate(zip(jobs, heads)):
            for parity, s in enumerate(_hgrn_bottom_exact(job[0], job[1], c, lo, job[5])):
                bot_ref[ji, parity] = s

    outs = []
    for ji, (job, (_, top, o_int, _, _)) in enumerate(zip(jobs, heads)):
        v = job[3]
        probs = jnp.concatenate([(bot_ref[ji, 0] + top[0]).astype(BF16), (bot_ref[ji, 1] + top[1]).astype(BF16)],
                                axis=1)
        vv = jnp.concatenate([jnp.where(lo, v, jnp.zeros_like(v)), jnp.where(hi, v, jnp.zeros_like(v))], axis=0)
        outs.append(jnp.dot(probs, vv, preferred_element_type=F32) + o_int)
    return outs


PAIRS_PER_ITER = 2


def _hgrn_kernel(has_s0, nt, *refs):
    if has_s0:
        (qf_ref, vf_ref, kf_ref, lf_ref, qb_ref, vb_ref, kb_ref, lb_ref, s0_ref,
         of_ref, ob_ref, so_ref, st_scr, bot_scr) = refs
    else:
        (qf_ref, vf_ref, kf_ref, lf_ref, qb_ref, vb_ref, kb_ref, lb_ref,
         of_ref, ob_ref, so_ref, st_scr, bot_scr) = refs
    j = pl.program_id(1)

    @pl.when(j == 0)
    def _():
        if has_s0:
            st_scr[...] = s0_ref[...]
        else:
            st_scr[...] = jnp.zeros_like(st_scr)

    n_sub = HL // HS

    def pair_body(it, carry):
        for step in range(n_sub):
            jobs, dests = [], []
            for pp in range(PAIRS_PER_ITER):
                p = it * PAIRS_PER_ITER + pp
                cols = pl.ds(pl.multiple_of(p * LANE, LANE), LANE)
                rf = pl.ds(step * HS, HS)
                rb = pl.ds((n_sub - 1 - step) * HS, HS)
                jobs.append((qf_ref[rf, cols].astype(F32), kf_ref[rf, cols], lf_ref[rf, cols], vf_ref[rf, cols],
                             st_scr.at[p, 0], False))
                dests.append((of_ref, rf, cols))
                jobs.append((qb_ref[rb, cols].astype(F32), kb_ref[rb, cols], lb_ref[rb, cols], vb_ref[rb, cols],
                             st_scr.at[p, 1], True))
                dests.append((ob_ref, rb, cols))
            for (ref, rows, cols), o in zip(dests, _hgrn_steps(jobs, bot_scr)):
                ref[rows, cols] = o
        return carry

    lax.fori_loop(0, N_PAIRS // PAIRS_PER_ITER, pair_body, 0)

    @pl.when(j == nt - 1)
    def _():
        so_ref[...] = st_scr[...]


def _hgrn(qc, vc, kff, lff, kfb, lfb, s0=None, prev=None, layer=0):
    if s0 is None:
        nb, nt, blk0 = BATCH, SEQ // HL, 0
    else:
        nb, nt, blk0 = DEC_BATCH, DEC_SEQ // HL, T_CTX // HL
    fspec = pl.BlockSpec((HL, 512), lambda b, j: (blk0 + b * nt + j, 0))
    bspec = pl.BlockSpec((HL, 512), lambda b, j: (blk0 + b * nt + nt - 1 - j, 0))
    sspec = pl.BlockSpec((None, N_PAIRS, 2, LANE, LANE), lambda b, j: (b, 0, 0, 0, 0))
    in_specs = [fspec] * 4 + [bspec] * 4
    args = [qc, vc, kff, lff, qc, vc, kfb, lfb]
    aliases = {}
    if s0 is not None:
        in_specs += [pl.BlockSpec((None, None, N_PAIRS, 2, LANE, LANE), lambda b, j: (b, layer, 0, 0, 0, 0))]
        args += [s0]
        in_specs += [pl.BlockSpec(memory_space=pl.ANY)] * 2
        args += list(prev)
        aliases = {len(args) - 2: 0, len(args) - 1: 1}

    def body(*refs):
        if s0 is not None:
            refs = refs[:9] + refs[11:]
        _hgrn_kernel(s0 is not None, nt, *refs)

    return pl.pallas_call(
        body,
        grid=(nb, nt),
        in_specs=in_specs,
        out_specs=[fspec, bspec, sspec],
        out_shape=[jax.ShapeDtypeStruct((T_ALL, 512), F32), jax.ShapeDtypeStruct((T_ALL, 512), F32),
                   jax.ShapeDtypeStruct((nb, N_PAIRS, 2, LANE, LANE), F32)],
        scratch_shapes=[pltpu.VMEM((N_PAIRS, 2, LANE, LANE), F32),
                        pltpu.VMEM((2 * PAIRS_PER_ITER, 2, HS, HS), F32)],
        input_output_aliases=aliases,
        compiler_params=_cparams(("parallel", "arbitrary")),
        name="hgrn_ctx" if s0 is None else "hgrn_lat",
    )(*args)


def _merge_kernel(xc_ref, xl_ref, mod_ref, n1_ref, wg_ref, oa_ref, ob_ref, ocf_ref, ocb_ref, sg_ref, con_ref,
                  ones_ref, wbr_ref, wout_ref, n2_ref, wrh_ref, wrl_ref, br_ref, x1_o, h2_o, comb_o):
    x = _x_tile(xc_ref, xl_ref, TMG)
    mod = mod_ref[...]
    xn = x * lax.rsqrt(jnp.mean(x * x, axis=-1, keepdims=True) + EPS) * n1_ref[...]
    h = (xn * (1.0 + mod[:, D:2 * D]) + mod[:, 0:D]).astype(BF16)

    oc = ocf_ref[...] + ocb_ref[...]
    oc = oc * lax.rsqrt(_group_mean(oc * oc, ones_ref[...], DV_C) + EPS) * con_ref[...]
    oc = (oc * sg_ref[...].astype(F32)).astype(BF16)
    branches = (oa_ref[...], ob_ref[...], oc)
    mix = None
    for jb in range(3):
        gate = jax.nn.sigmoid(jnp.dot(h, wg_ref[:, jb * D:(jb + 1) * D], preferred_element_type=F32))
        t = gate * jnp.dot(branches[jb], wbr_ref[jb], preferred_element_type=F32)
        mix = t if mix is None else mix + t
    out = jnp.dot(mix.astype(BF16), wout_ref[...], preferred_element_type=F32)
    x1 = x + mod[:, 2 * D:3 * D] * out
    x1_o[...] = x1

    x1n = x1 * lax.rsqrt(jnp.mean(x1 * x1, axis=-1, keepdims=True) + EPS) * n2_ref[...]
    h2 = x1n * (1.0 + mod[:, 4 * D:5 * D]) + mod[:, 3 * D:4 * D]
    h2_o[...] = h2.astype(BF16)

    h2h, h2l = _split_hi_lo(h2)
    logits = (jnp.dot(h2h, wrh_ref[...], preferred_element_type=F32)
              + jnp.dot(h2l, wrh_ref[...], preferred_element_type=F32)
              + jnp.dot(h2h, wrl_ref[...], preferred_element_type=F32)) + br_ref[...]
    lane = lax.broadcasted_iota(jnp.int32, logits.shape, 1)
    neg = -jnp.inf
    is_g = lane < N_GROUPS
    gl = jnp.where(is_g, logits, neg)
    gmax = gl.max(axis=-1, keepdims=True)
    gidx = jnp.min(jnp.where(gl == gmax, lane, LANE), axis=-1, keepdims=True)
    gp = 1.0 / jnp.sum(jnp.where(is_g, jnp.exp(gl - gmax), 0.0), axis=-1, keepdims=True)
    eid = lane - N_GROUPS
    in_grp = (eid >= 0) & (eid < N_EXPERTS) & ((eid >> 2) == gidx)
    el = jnp.where(in_grp, logits, neg)
    v1 = el.max(axis=-1, keepdims=True)
    i1 = jnp.min(jnp.where(el == v1, lane, LANE), axis=-1, keepdims=True)
    el2 = jnp.where(lane == i1, neg, el)
    v2 = el2.max(axis=-1, keepdims=True)
    i2 = jnp.min(jnp.where(el2 == v2, lane, LANE), axis=-1, keepdims=True)
    e2 = jnp.exp(v2 - v1)
    w1 = gp / (1.0 + e2)
    w2 = gp * e2 / (1.0 + e2)
    comb = jnp.where(lane == i1, w1, 0.0) + jnp.where(lane == i2, w2, 0.0)
    comb_o[...] = pltpu.roll(comb, LANE - N_GROUPS, 1)


TMG = 512


def _merge(layer, x, mod, n1, wgate, oa, ob, ocf, ocb, sg, con, ones, wbr, wout, n2, wr_hi, wr_lo, br):
    ctx_tiles, per_seq = T_CTX // TMG, DEC_SEQ // TMG
    mrow = lambda i: jnp.where(i < ctx_tiles, CTX_MOD_ROW, (i - ctx_tiles) // per_seq)
    tspec = lambda w: pl.BlockSpec((TMG, w), lambda i: (i, 0))
    wspec = lambda tail: pl.BlockSpec((None,) + tail, lambda i: (layer,) + (0,) * len(tail),
                                      pipeline_mode=pl.Buffered(1))
    return pl.pallas_call(
        _merge_kernel,
        grid=(T_ALL // TMG,),
        in_specs=_x_specs(x, TMG) + [
            _mod_spec(layer, mrow),
            _layer_spec((1, D), layer), wspec((D, 3 * D)),
            tspec(512), tspec(512), tspec(512), tspec(512), tspec(512),
            _layer_spec((1, 512), layer), _const_spec((256, 256)),
            wspec((3, BRANCH_W, D)), wspec((D, D)), _layer_spec((1, D), layer),
            _layer_spec((D, LANE), layer), _layer_spec((D, LANE), layer), _layer_spec((1, LANE), layer),
        ],
        out_specs=[tspec(D), tspec(D), tspec(LANE)],
        out_shape=[jax.ShapeDtypeStruct((T_ALL, D), F32), jax.ShapeDtypeStruct((T_ALL, D), BF16),
                   jax.ShapeDtypeStruct((T_ALL, LANE), F32)],
        compiler_params=_cparams(("parallel",)),
        name="merge",
    )(*_x_pair(x)[:2], mod, n1, wgate, oa, ob, ocf, ocb, sg, con, ones, wbr, wout, n2, wr_hi, wr_lo, br)


NB = 2048
N_BLK = T_ALL // NB
SBK = 256
N_SB = NB // SBK
WIN_SHIFT, FT_SHIFT, SEG_SHIFT = 6, 7, 4
WIN = 1 << WIN_SHIFT
FT = 1 << FT_SHIFT
SEG_ALIGN = 1 << SEG_SHIFT
STG = 2 * NB + N_EXPERTS * SEG_ALIGN + 256
QUAD = 4
E_STEP = 2


def _route_kernel(comb_ref, tri_ref, upper_ref, rank_ref, carry_ref, tab_ref, carry_scr):
    s = pl.program_id(1)

    @pl.when(s == 0)
    def _():
        carry_scr[...] = jnp.zeros_like(carry_scr)

    routed = comb_ref[...] > 0.0
    ind = jnp.where(routed, 1.0, 0.0)
    carry = carry_scr[...]
    rank = jnp.dot(tri_ref[...], ind.astype(BF16), preferred_element_type=F32) + carry
    rank_ref[...] = jnp.where(routed, rank, -1.0)
    carry_ref[...] = carry
    count = carry + jnp.sum(ind, axis=0, keepdims=True)
    carry_scr[...] = count

    @pl.when(s == N_SB - 1)
    def _():
        seg = jnp.floor((count + (SEG_ALIGN - 1.0)) * (1.0 / SEG_ALIGN)) * SEG_ALIGN
        off = jnp.dot(jnp.broadcast_to(seg, (8, LANE)), upper_ref[...], preferred_element_type=F32,
                      precision=lax.Precision.HIGHEST)
        tab_ref[0:1, :] = count
        tab_ref[1:2, :] = off[0:1, :]


def _route(comb):
    tri = jnp.asarray(np.tril(np.ones((SBK, SBK)), -1), BF16)
    upper = jnp.asarray(np.triu(np.ones((LANE, LANE)), 1), F32)
    return pl.pallas_call(
        _route_kernel,
        grid=(N_BLK, N_SB),
        in_specs=[
            pl.BlockSpec((SBK, LANE), lambda b, s: (b * N_SB + s, 0)),
            pl.BlockSpec((SBK, SBK), lambda b, s: (0, 0)),
            pl.BlockSpec((LANE, LANE), lambda b, s: (0, 0)),
        ],
        out_specs=[
            pl.BlockSpec((SBK, LANE), lambda b, s: (b * N_SB + s, 0)),
            pl.BlockSpec((None, None, 1, LANE), lambda b, s: (b, s, 0, 0)),
            pl.BlockSpec((None, 2, LANE), lambda b, s: (b, 0, 0)),
        ],
        out_shape=[jax.ShapeDtypeStruct((T_ALL, LANE), F32),
                   jax.ShapeDtypeStruct((N_BLK, N_SB, 1, LANE), F32),
                   jax.ShapeDtypeStruct((N_BLK, 2, LANE), F32)],
        scratch_shapes=[pltpu.VMEM((1, LANE), F32)],
        compiler_params=_cparams(("parallel", "arbitrary")),
        name="moe_route",
    )(comb, tri, upper)


def _moe_kernel(cnt_s, off_s, car_s, h2_ref, rank_ref, comb_ref, offv_ref, eg_ref, eu_ref, ed_ref, o_ref,
                stg_ref, acc_ref):
    blk = pl.program_id(0)
    step = pl.program_id(1)
    lane = lax.broadcasted_iota(jnp.int32, (1, LANE), 1)
    lane_lo = (lane & (2 * WIN - 1)) < WIN
    slot = (lane & (WIN - 1)).astype(F32)
    srow = lax.broadcasted_iota(jnp.int32, (WIN, SBK), 0).astype(F32)

    def windows(s):
        out = []
        for ex in range(N_EXPERTS):
            start = off_s[blk, ex] + car_s[blk, s, ex]
            length = car_s[blk, s + 1, ex] - car_s[blk, s, ex]
            ws = (start >> SEG_SHIFT) << SEG_SHIFT
            out.append((ws, (start - ws + length + (WIN - 1)) >> WIN_SHIFT))
        return out, functools.reduce(jnp.maximum, [w[1] for w in out])

    def positions(s):
        rows = pl.ds(pl.multiple_of(s * SBK, SBK), SBK)
        rank = rank_ref[rows, :]
        return rows, jnp.where(rank >= 0.0, rank + offv_ref[1:2, :], -1.0e6)

    @pl.when(step == 0)
    def _():
        stg_ref[...] = jnp.zeros_like(stg_ref)

        def sub_body(s, carry):
            rows, pos = positions(s)
            pos_t = pos.T
            h2 = h2_ref[rows, :]
            wins, nmax = windows(s)

            def chunk_body(c, carry2):
                for quad in range(N_EXPERTS // QUAD):
                    blocks = []
                    for ex in range(quad * QUAD, (quad + 1) * QUAD):
                        base = (wins[ex][0] + c * WIN).astype(F32)
                        hit = (pos_t[ex:ex + 1, :] - base) == srow
                        blocks.append(jnp.where(hit, 1.0, 0.0).astype(BF16))
                    moved = jnp.dot(jnp.concatenate(blocks, axis=0), h2, preferred_element_type=F32).astype(BF16)
                    for i in range(QUAD):
                        ws, nch = wins[quad * QUAD + i]

                        @pl.when(c < nch)
                        def _():
                            dst = pl.ds(pl.multiple_of(ws + c * WIN, SEG_ALIGN), WIN)
                            stg_ref[dst, :] = stg_ref[dst, :] + moved[i * WIN:(i + 1) * WIN, :]
                return carry2

            lax.fori_loop(0, nmax, chunk_body, 0)
            return carry

        lax.fori_loop(0, N_SB, sub_body, 0)

    def ffn_tile(j, first, n_rows, n_valid):
        rows = pl.ds(pl.multiple_of(first, SEG_ALIGN), n_rows)
        xs = stg_ref[rows, :]
        hg = jnp.dot(xs, eg_ref[j], preferred_element_type=F32)
        hu = jnp.dot(xs, eu_ref[j], preferred_element_type=F32)
        act = (hg * jax.nn.sigmoid(hg) * hu).astype(BF16)
        y = jnp.dot(act, ed_ref[j], preferred_element_type=F32).astype(BF16)
        rid = lax.broadcasted_iota(jnp.int32, (n_rows, 1), 0)
        stg_ref[rows, :] = jnp.where(rid < n_valid, y, xs)

    for j in range(E_STEP):
        ex = step * E_STEP + j
        count = cnt_s[blk, ex]
        seg0 = off_s[blk, ex]
        n_big = (count + (FT - 1)) >> (FT_SHIFT + 1)

        def ffn_body(t, carry, j=j, count=count, seg0=seg0):
            ffn_tile(j, seg0 + t * (2 * FT), 2 * FT, count - t * (2 * FT))
            return carry

        lax.fori_loop(0, n_big, ffn_body, 0)

        @pl.when(count > n_big * (2 * FT))
        def _(j=j, count=count, seg0=seg0, n_big=n_big):
            ffn_tile(j, seg0 + n_big * (2 * FT), FT, count - n_big * (2 * FT))

    @pl.when(step == N_EXPERTS // E_STEP - 1)
    def _():
        def sub_body(s, carry):
            rows, pos = positions(s)
            wts = comb_ref[rows, :]
            wins, nmax = windows(s)
            acc_ref[...] = jnp.zeros_like(acc_ref)

            def chunk_body(c, carry2):
                sel, srcs = [], []
                for ex0 in range(0, N_EXPERTS, 2):
                    b0 = (wins[ex0][0] + c * WIN).astype(F32)
                    b1 = (wins[ex0 + 1][0] + c * WIN).astype(F32)
                    rel = jnp.where(lane_lo, pos[:, ex0:ex0 + 1] - b0, pos[:, ex0 + 1:ex0 + 2] - b1)
                    w = jnp.where(lane_lo, wts[:, ex0:ex0 + 1], wts[:, ex0 + 1:ex0 + 2])
                    sel.append(jnp.where(rel == slot, w, 0.0).astype(BF16))
                for ex in range(N_EXPERTS):
                    first = jnp.minimum(wins[ex][0] + c * WIN, STG - WIN)
                    srcs.append(stg_ref[pl.ds(pl.multiple_of(first, SEG_ALIGN), WIN), :])
                acc_ref[...] += jnp.dot(jnp.concatenate(sel, axis=1), jnp.concatenate(srcs, axis=0),
                                        preferred_element_type=F32)
                return carry2

            lax.fori_loop(0, nmax, chunk_body, 0)
            o_ref[rows, :] = acc_ref[...].astype(BF16)
            return carry

        lax.fori_loop(0, N_SB, sub_body, 0)


def _moe(layer, h2, comb, eg, eu, ed):
    rank, carry, tab = _route(comb)
    cnt_i = tab[:, 0, :N_EXPERTS].astype(jnp.int32)
    off_i = tab[:, 1, :N_EXPERTS].astype(jnp.int32)
    car_i = jnp.concatenate([carry[:, :, 0, :N_EXPERTS].astype(jnp.int32), cnt_i[:, None, :]], axis=1)
    bspec = lambda w: pl.BlockSpec((NB, w), lambda b, e, *_: (b, 0))
    grid_spec = pltpu.PrefetchScalarGridSpec(
        num_scalar_prefetch=3,
        grid=(N_BLK, N_EXPERTS // E_STEP),
        in_specs=[
            bspec(D), bspec(LANE), bspec(LANE),
            pl.BlockSpec((None, 2, LANE), lambda b, e, *_: (b, 0, 0)),
            pl.BlockSpec((None, E_STEP, D, D_EXPERT), lambda b, e, *_: (layer, e, 0, 0)),
            pl.BlockSpec((None, E_STEP, D, D_EXPERT), lambda b, e, *_: (layer, e, 0, 0)),
            pl.BlockSpec((None, E_STEP, D_EXPERT, D), lambda b, e, *_: (layer, e, 0, 0)),
        ],
        out_specs=bspec(D),
        scratch_shapes=[pltpu.VMEM((STG, D), BF16), pltpu.VMEM((SBK, D), F32)],
    )
    return pl.pallas_call(
        _moe_kernel,
        grid_spec=grid_spec,
        out_shape=jax.ShapeDtypeStruct((T_ALL, D), BF16),
        compiler_params=_cparams(("parallel", "arbitrary")),
        name="moe",
    )(cnt_i, off_i, car_i, h2, rank, comb, tab, eg, eu, ed)


def _residual_kernel(final, x1_ref, moe_ref, mod_ref, fg_ref, o_ref):
    x2 = x1_ref[...] + mod_ref[:, 5 * D:6 * D] * moe_ref[...].astype(F32)
    if final:
        x2 = x2 * lax.rsqrt(jnp.mean(x2 * x2, axis=-1, keepdims=True) + EPS) * fg_ref[...]
    o_ref[...] = x2


def _residual(layer, final, x1, moe, mod, fg, tile0=0, n_tiles=N_TILES):
    src = lambda: pl.BlockSpec((TM, D), lambda i: (tile0 + i, 0))
    return pl.pallas_call(
        functools.partial(_residual_kernel, final),
        grid=(n_tiles,),
        in_specs=[src(), src(), _mod_spec(layer, lambda i: _mod_row(tile0 + i)), _const_spec((1, D))],
        out_specs=_tile_spec(D),
        out_shape=jax.ShapeDtypeStruct((n_tiles * TM, D), F32),
        compiler_params=_cparams(("parallel",)),
        name="residual",
    )(x1, moe, mod, fg)


def _state_to_blockdiag(s):
    lead = s.shape[:-3]
    st = jnp.swapaxes(s, -1, -2).reshape(lead + (N_PAIRS, 2, DV_C, DK_C))
    z = jnp.zeros_like(st[..., 0, :, :])
    top = jnp.concatenate([st[..., 0, :, :], z], axis=-1)
    bot = jnp.concatenate([z, st[..., 1, :, :]], axis=-1)
    return jnp.concatenate([top, bot], axis=-2)


def _blockdiag_to_state(sb):
    lead = sb.shape[:-3]
    even = sb[..., :DV_C, :DK_C]
    odd = sb[..., DV_C:, DK_C:]
    st = jnp.stack([even, odd], axis=-3).reshape(lead + (H_C, DV_C, DK_C))
    return jnp.swapaxes(st, -1, -2)


def kernel(x_prompt, x_sample, cache_gqa_k, cache_gqa_v, cache_mla_ckv, cache_mla_krope, state_hgrn, c, c_ctx,
           w_mod, b_mod, norm1_g, norm2_g, w_in, a_qnorm, a_knorm, b_qnorm, b_wq, b_kvnorm, b_wkv, c_lb_logits,
           c_onorm, w_branch, w_out, r_group_w, r_group_b, r_expert_w, r_expert_b, e_gate, e_up, e_down, final_g):
    x = (x_prompt.reshape(T_CTX, D), x_sample.reshape(T_LAT, D))
    cvec = jnp.concatenate([c, c_ctx[None, :], jnp.zeros((MOD_ROWS - DEC_BATCH - 1, D), F32)], axis=0)
    mod = _mod_table(cvec, w_mod, b_mod).reshape(DEPTH * MOD_ROWS, 1, 6 * D)
    taba, tabb, tabk = _rope_tables()
    ones = _ones_block(256, 64)
    lbl = c_lb_logits.reshape(DEPTH, 2, H_C * DK_C)

    vec = lambda g, reps=1: jnp.tile(g, (1, reps))[:, None, :]
    n1, n2 = vec(norm1_g), vec(norm2_g)
    aq, ak, con = vec(a_qnorm, H_A), vec(a_knorm, KV_A), vec(c_onorm, H_C)
    bq, bkv = vec(b_qnorm), vec(b_kvnorm)
    w_in_p, w_gate = _pack_w_in(w_in)
    wq_p, wkv_p = _pack_wq(b_wq), _pack_wkv(b_wkv)
    wbr, wout = w_branch.astype(BF16), w_out.astype(BF16)
    n_pad = LANE - N_GROUPS - N_EXPERTS
    wr_hi, wr_lo = _split_hi_lo(jnp.concatenate([r_group_w, r_expert_w, jnp.zeros((DEPTH, D, n_pad), F32)], axis=-1))
    br = jnp.concatenate([r_group_b, r_expert_b, jnp.zeros((DEPTH, n_pad), F32)], axis=-1)[:, None, :]
    eg, eu, ed = e_gate.astype(BF16), e_up.astype(BF16), e_down.astype(BF16)

    ck = cache_gqa_k.reshape(DEC_BATCH, DEPTH, PAST, KV_A * HD_A)
    cv = cache_gqa_v.reshape(DEC_BATCH, DEPTH, PAST, KV_A * HD_A)
    ckvb = _ctx_kv(cache_mla_ckv.reshape(DEC_BATCH * DEPTH * PAST, KV_RANK), wkv_p)
    ckr = cache_mla_krope.reshape(DEC_BATCH * DEPTH * PAST, ROPE_B)
    zpad = jnp.zeros_like(ckr)
    ckre = jnp.concatenate([ckr, zpad, ckr, zpad], axis=1)
    s0 = jnp.swapaxes(_state_to_blockdiag(state_hgrn), 2, 3)

    new_k, new_v, new_ckv, new_kr, new_s = [], [], [], [], []
    for l in range(DEPTH):
        (qa, kan, ka, va, qb, ckv, kvb, kr, kre, lff, lfb, kff, kfb, qc, vc, sg) = _inproj(
            l, x, mod, n1, w_in_p, aq, ak, bq, wq_p, bkv, wkv_p, lbl, ones, taba, tabb, tabk)
        new_k.append(kan[:T_CTX])
        new_v.append(va[:T_CTX])
        new_ckv.append(ckv[:T_CTX])
        new_kr.append(kr[:T_CTX, :ROPE_B])

        oa, ob = _attention(qa, qb, ka, va, kvb, kre)
        oa, ob = _attention(qa, qb, ka, va, kvb, kre, prev=(oa, ob), cache=(ck, cv, ckvb, ckre), layer=l)

        ocf, ocb, s_ctx = _hgrn(qc, vc, kff, lff, kfb, lfb)
        ocf, ocb, _ = _hgrn(qc, vc, kff, lff, kfb, lfb, s0=s0, prev=(ocf, ocb), layer=l)
        new_s.append(s_ctx)

        x1, h2, comb = _merge(l, x, mod, n1, w_gate, oa, ob, ocf, ocb, sg, con, ones, wbr, wout, n2,
                              wr_hi, wr_lo, br)
        moe = _moe(l, h2, comb, eg, eu, ed)
        if l < DEPTH - 1:
            x = _residual(l, False, x1, moe, mod, final_g[None, :])

    last = DEPTH - 1
    y_prompt = _residual(last, True, x1, moe, mod, final_g[None, :], 0, CTX_TILES)
    y_sample = _residual(last, True, x1, moe, mod, final_g[None, :], CTX_TILES, N_TILES - CTX_TILES)
    stack = lambda parts, tail: jnp.stack([p.reshape(BATCH, SEQ, -1) for p in parts], axis=1).reshape(
        (BATCH, DEPTH, SEQ) + tail)
    states = _blockdiag_to_state(jnp.swapaxes(jnp.stack(new_s, axis=1), 2, 3))
    return (y_prompt.reshape(BATCH, SEQ, D), y_sample.reshape(DEC_BATCH, DEC_SEQ, D),
            stack(new_k, (KV_A, HD_A)), stack(new_v, (KV_A, HD_A)), stack(new_ckv, (KV_RANK,)),
            stack(new_kr, (ROPE_B,)), states)
```

```python
import functools

import numpy as np
import jax
import jax.numpy as jnp
from jax import lax
from jax.experimental import pallas as pl
from jax.experimental.pallas import tpu as pltpu

D = 1024
BATCH, SEQ = 32, 256
DEC_BATCH, DEC_SEQ = 8, 1024
PAST = 256
DEPTH = 2
GRID_W = 64
THETA = 10000.0
EPS = 1e-6
F_FLOOR = 1e-30
H_A, KV_A, HD_A = 8, 2, 64
H_B, Q_RANK, KV_RANK, NOPE_B, ROPE_B, V_B = 8, 384, 256, 64, 32, 64
H_C, DK_C, DV_C = 8, 64, 64
BRANCH_W = 512
N_GROUPS, E_PER_GROUP, N_EXPERTS, D_EXPERT = 4, 4, 16, 512

T_CTX = BATCH * SEQ
T_LAT = DEC_BATCH * DEC_SEQ
T_ALL = T_CTX + T_LAT
TM = 256
N_TILES = T_ALL // TM
CTX_TILES = T_CTX // TM
LAT_TILES_PER_SEQ = DEC_SEQ // TM
MOD_ROWS = 16
CTX_MOD_ROW = DEC_BATCH
LANE = 128
VMEM_LIMIT = 56 * 1024 * 1024

C_QA, C_KA, C_VA, C_QRA, C_KVA, C_KR = 0, 512, 640, 768, 1152, 1408
C_FF, C_FB, C_QC, C_IC, C_GC, C_END = 1536, 2048, 2560, 3072, 3584, 4096
R_QA, R_KA, R_VA, R_QRA, R_KVA, R_KR = 0, 512, 640, 768, 1152, 1408
R_FF, R_FB, R_QC, R_IC, R_GC, R_GATE, R_END = 1440, 1952, 2464, 2976, 3488, 4000, 7072

F32 = jnp.float32
BF16 = jnp.bfloat16


def _cparams(sem):
    return pltpu.CompilerParams(dimension_semantics=sem, vmem_limit_bytes=VMEM_LIMIT)


def _mod_row(i):
    return jnp.where(i < CTX_TILES, CTX_MOD_ROW, (i - CTX_TILES) // LAT_TILES_PER_SEQ)


def _pos_block(i):
    return jnp.where(i < CTX_TILES, LAT_TILES_PER_SEQ, (i - CTX_TILES) % LAT_TILES_PER_SEQ)


def _split_hi_lo(x):
    hi = x.astype(BF16)
    lo = (x - hi.astype(F32)).astype(BF16)
    return hi, lo


def _group_mean(x2, ones_blk, width):
    n = ones_blk.shape[0]
    outs = []
    for j in range(x2.shape[-1] // n):
        blk = x2[:, j * n:(j + 1) * n]
        hi, lo = _split_hi_lo(blk)
        s = jnp.dot(hi, ones_blk, preferred_element_type=F32) + jnp.dot(lo, ones_blk, preferred_element_type=F32)
        outs.append(s)
    s = outs[0] if len(outs) == 1 else jnp.concatenate(outs, axis=-1)
    return s * (1.0 / width)


def _rope(x, tab_ref, shift, period):
    c, s1, s2 = tab_ref[0], tab_ref[1], tab_ref[2]
    outs = []
    for j in range(x.shape[-1] // period):
        blk = x[:, j * period:(j + 1) * period]
        outs.append(blk * c + pltpu.roll(blk, shift, 1) * s1 + pltpu.roll(blk, period - shift, 1) * s2)
    return outs[0] if len(outs) == 1 else jnp.concatenate(outs, axis=-1)


def _mod_kernel(c_ref, w_ref, b_ref, o_ref):
    c = c_ref[...]
    a = c * jax.nn.sigmoid(c)
    o_ref[...] = jnp.dot(a, w_ref[...], preferred_element_type=F32, precision=lax.Precision.HIGHEST) + b_ref[...]


def _mod_table(cvec, w_mod, b_mod):
    nt = 1024
    return pl.pallas_call(
        _mod_kernel,
        grid=(DEPTH, 6 * D // nt),
        in_specs=[
            pl.BlockSpec((MOD_ROWS, D), lambda l, j: (0, 0)),
            pl.BlockSpec((None, D, nt), lambda l, j: (l, 0, j)),
            pl.BlockSpec((None, 1, nt), lambda l, j: (l, 0, j)),
        ],
        out_specs=pl.BlockSpec((None, MOD_ROWS, nt), lambda l, j: (l, 0, j)),
        out_shape=jax.ShapeDtypeStruct((DEPTH, MOD_ROWS, 6 * D), F32),
        compiler_params=_cparams(("arbitrary", "arbitrary")),
        name="mod_table",
    )(cvec, w_mod, b_mod.reshape(DEPTH, 1, 6 * D))


def _x_pair(x):
    if isinstance(x, tuple):
        return x[0], x[1], 0
    return x, x, T_CTX


def _x_specs(x, tile):
    _, _, lat_off = _x_pair(x)
    ctx = T_CTX // tile
    return [pl.BlockSpec((tile, D), lambda i: (jnp.minimum(i, ctx - 1), 0)),
            pl.BlockSpec((tile, D), lambda i: (jnp.maximum(i - ctx, 0) + lat_off // tile, 0))]


def _x_tile(xc_ref, xl_ref, tile):
    return jnp.where(pl.program_id(0) < T_CTX // tile, xc_ref[...], xl_ref[...])


def _inproj_kernel(layer, xc_ref, xl_ref, mod_ref, n1_ref, w_ref, aq_ref, ak_ref, bq_ref, wq_ref, bkv_ref, wkv_ref,
                   lbl_ref, ones_ref, taba_ref, tabb_ref, tabk_ref,
                   qa_o, kan_o, ka_o, va_o, qb_o, ckv_o, kvb_o, kr_o, kre_o,
                   lff_o, lfb_o, kff_o, kfb_o, qc_o, vc_o, sg_o):
    x = _x_tile(xc_ref, xl_ref, TM)
    mod = mod_ref[...]
    xn = x * lax.rsqrt(jnp.mean(x * x, axis=-1, keepdims=True) + EPS) * n1_ref[...]
    h = (xn * (1.0 + mod[:, D:2 * D]) + mod[:, 0:D]).astype(BF16)
    y = jnp.dot(h, w_ref[...], preferred_element_type=F32)
    ones = ones_ref[...]

    qa = y[:, C_QA:C_KA]
    qa = qa * lax.rsqrt(_group_mean(qa * qa, ones, HD_A) + EPS) * aq_ref[...]
    qa_o[...] = (_rope(qa, taba_ref, 16, LANE) * (HD_A ** -0.5)).astype(BF16)
    ka = y[:, C_KA:C_VA]
    ka = ka * lax.rsqrt(_group_mean(ka * ka, ones[:LANE, :LANE], HD_A) + EPS) * ak_ref[...]
    kan_o[...] = ka
    ka_o[...] = _rope(ka, taba_ref, 16, LANE).astype(BF16)
    va_o[...] = y[:, C_VA:C_QRA]

    qr = y[:, C_QRA:C_KVA]
    qr = qr * lax.rsqrt(jnp.mean(qr * qr, axis=-1, keepdims=True) + EPS) * bq_ref[...]
    qb = jnp.dot(qr.astype(BF16), wq_ref[...], preferred_element_type=F32)
    qb_o[...] = (_rope(qb, tabb_ref, 8, 2 * LANE) * ((NOPE_B + ROPE_B) ** -0.5)).astype(BF16)
    kv = y[:, C_KVA:C_KR]
    ckv = kv * lax.rsqrt(jnp.mean(kv * kv, axis=-1, keepdims=True) + EPS) * bkv_ref[...]
    ckv_o[...] = ckv
    kvb_o[...] = jnp.dot(ckv.astype(BF16), wkv_ref[...], preferred_element_type=F32).astype(BF16)
    kr = y[:, C_KR:C_FF]
    kr_o[...] = kr
    kre_o[...] = _rope(kr, tabk_ref, 8, LANE).astype(BF16)

    lbl = lbl_ref[...]
    e = jnp.exp(lbl - jnp.max(lbl, axis=0, keepdims=True))
    p = e / jnp.sum(e, axis=0, keepdims=True)
    lb = p[0] * 0.0
    for i in range(1, layer + 1):
        lb = lb + p[i]
    for d, (c0, lf_o, kf_o) in enumerate(((C_FF, lff_o, kff_o), (C_FB, lfb_o, kfb_o))):
        pre = y[:, c0:c0 + 512]
        lbd = lb[d:d + 1, :]
        f = jnp.maximum(lbd + (1.0 - lbd) * jax.nn.sigmoid(pre), F_FLOOR)
        lf_o[...] = jnp.log(f)
        kf_o[...] = 1.0 - f
    qc_o[...] = y[:, C_QC:C_IC].astype(BF16)
    vc_o[...] = y[:, C_IC:C_GC].astype(BF16)
    gc = y[:, C_GC:C_END]
    sg_o[...] = (gc * jax.nn.sigmoid(gc)).astype(BF16)


def _const_spec(shape):
    nd = len(shape)
    return pl.BlockSpec(shape, lambda i: (0,) * nd)


def _tile_spec(width):
    return pl.BlockSpec((TM, width), lambda i: (i, 0))


def _layer_spec(tail, layer):
    return pl.BlockSpec((None,) + tuple(tail), lambda *_: (layer,) + (0,) * len(tail))


def _mod_spec(layer, row_of_tile):
    return pl.BlockSpec((None, 1, 6 * D), lambda i, *_: (layer * MOD_ROWS + row_of_tile(i), 0, 0))


def _inproj(layer, x, mod, n1, w_in_p, aq, ak, bq, wq_p, bkv, wkv_p, lbl, ones, taba, tabb, tabk):
    outs = [
        (512, BF16), (128, F32), (128, BF16), (128, F32), (1024, BF16), (256, F32), (1024, BF16),
        (128, F32), (128, BF16), (512, F32), (512, F32), (512, F32), (512, F32), (512, BF16), (512, BF16),
        (512, BF16),
    ]
    tab_spec = lambda w: pl.BlockSpec((3, TM, w), lambda i: (0, _pos_block(i), 0))
    return pl.pallas_call(
        functools.partial(_inproj_kernel, layer),
        grid=(N_TILES,),
        in_specs=_x_specs(x, TM) + [
            _mod_spec(layer, _mod_row),
            _layer_spec((1, D), layer),
            _layer_spec((D, C_END), layer),
            _layer_spec((1, 512), layer), _layer_spec((1, 128), layer), _layer_spec((1, Q_RANK), layer),
            _layer_spec((Q_RANK, 1024), layer), _layer_spec((1, KV_RANK), layer), _layer_spec((KV_RANK, 1024), layer),
            _const_spec((DEPTH, 2, 512)), _const_spec((256, 256)),
            tab_spec(LANE), tab_spec(2 * LANE), tab_spec(LANE),
        ],
        out_specs=[_tile_spec(w) for w, _ in outs],
        out_shape=[jax.ShapeDtypeStruct((T_ALL, w), dt) for w, dt in outs],
        compiler_params=_cparams(("parallel",)),
        name="inproj",
    )(*_x_pair(x)[:2], mod, n1, w_in_p, aq, ak, bq, wq_p, bkv, wkv_p, lbl, ones, taba, tabb, tabk)


def _pack_w_in(w):
    z = jnp.zeros((DEPTH, D, 32), w.dtype)
    kr = w[..., R_KR:R_FF]
    main = jnp.concatenate([w[..., :R_KR], kr, z, kr, z, w[..., R_FF:R_GATE]], axis=-1)
    return main.astype(BF16), w[..., R_GATE:].astype(BF16)


def _pack_wq(wq):
    w = wq.reshape(DEPTH, Q_RANK, H_B, NOPE_B + ROPE_B)
    nope, rope = w[..., :NOPE_B], w[..., NOPE_B:]
    z = jnp.zeros((DEPTH, Q_RANK, H_B, 32), wq.dtype)
    even = jnp.concatenate([rope, z, nope], axis=-1)
    odd = jnp.concatenate([nope, rope, z], axis=-1)
    is_even = (jnp.arange(H_B) % 2 == 0)[None, None, :, None]
    return jnp.where(is_even, even, odd).reshape(DEPTH, Q_RANK, H_B * LANE).astype(BF16)


def _pack_wkv(wkv):
    w = wkv.reshape(DEPTH, KV_RANK, H_B, NOPE_B + V_B)
    nope, v = w[..., :NOPE_B], w[..., NOPE_B:]
    is_even = (jnp.arange(H_B) % 2 == 0)[None, None, :, None]
    return jnp.where(is_even, jnp.concatenate([v, nope], -1), jnp.concatenate([nope, v], -1)).reshape(
        DEPTH, KV_RANK, H_B * LANE).astype(BF16)


def _rope_tables():
    pos = np.arange(DEC_SEQ)
    row, col = pos // GRID_W, pos % GRID_W

    def pattern(half):
        quarter = half // 2
        inv = THETA ** (-np.arange(0, half, 2, dtype=np.float64) / half)
        ang = np.concatenate([row[:, None] * inv, row[:, None] * inv, col[:, None] * inv, col[:, None] * inv], 1)
        is_x2 = np.tile(np.concatenate([np.zeros(quarter), np.ones(quarter)]), 2)[None, :]
        c = np.cos(ang)
        s1 = np.sin(ang) * is_x2
        s2 = -np.sin(ang) * (1 - is_x2)
        return c, s1, s2

    def assemble(width, spans, half):
        c, s1, s2 = pattern(half)
        tc = np.ones((DEC_SEQ + TM, width))
        t1 = np.zeros((DEC_SEQ + TM, width))
        t2 = np.zeros((DEC_SEQ + TM, width))
        for start in spans:
            tc[:DEC_SEQ, start:start + 2 * half] = c
            t1[:DEC_SEQ, start:start + 2 * half] = s1
            t2[:DEC_SEQ, start:start + 2 * half] = s2
        return jnp.asarray(np.stack([tc, t1, t2]), F32)

    taba = assemble(LANE, (0, 64), 32)
    tabb = assemble(2 * LANE, (0, 128 + 64), 16)
    tabk = assemble(LANE, (0, 64), 16)
    return taba, tabb, tabk


def _ones_block(n, width):
    g = np.arange(n) // width
    return jnp.asarray(g[:, None] == g[None, :], BF16)


def _ctxkv_kernel(c_ref, w_ref, o_ref):
    o_ref[...] = jnp.dot(c_ref[...].astype(BF16), w_ref[...], preferred_element_type=F32).astype(BF16)


def _ctx_kv(ckv_cache, wkv_p):
    rows = ckv_cache.shape[0]
    return pl.pallas_call(
        _ctxkv_kernel,
        grid=(rows // PAST,),
        in_specs=[pl.BlockSpec((PAST, KV_RANK), lambda i: (i, 0)),
                  pl.BlockSpec((None, KV_RANK, 1024), lambda i: (i % DEPTH, 0, 0))],
        out_specs=pl.BlockSpec((PAST, 1024), lambda i: (i, 0)),
        out_shape=jax.ShapeDtypeStruct((rows, 1024), BF16),
        compiler_params=_cparams(("parallel",)),
        name="ctx_kv",
    )(ckv_cache, wkv_p)


_NT = (((1,), (1,)), ((), ()))


def _softmax_pv(scores, values, parity):
    lane = lax.broadcasted_iota(jnp.int32, (1, LANE), 1)
    den_lane = 64 if parity == 0 else 0
    keep = (lane < 64) if parity == 0 else (lane >= 64)
    m = scores[0].max(axis=-1, keepdims=True)
    for s in scores[1:]:
        m = jnp.maximum(m, s.max(axis=-1, keepdims=True))
    if len(scores) == 1:
        p = jnp.exp(scores[0] - m)
        o = jnp.dot(p.astype(BF16), values[0], preferred_element_type=F32)
        return o / p.sum(axis=-1, keepdims=True)
    acc = None
    for s, v in zip(scores, values):
        p = jnp.exp((s - m).astype(BF16))
        o = jnp.dot(p, jnp.where(lane == den_lane, jnp.ones_like(v), v), preferred_element_type=F32)
        acc = o if acc is None else acc + o
    return jnp.where(keep, acc / acc[:, den_lane:den_lane + 1], 0.0)


def _attn_kernel(n_pieces, qa_ref, qb_ref, *refs):
    kv_refs = refs[:4 * n_pieces]
    oa_ref, ob_ref = refs[-2], refs[-1]
    lane = lax.broadcasted_iota(jnp.int32, (1, LANE), 1)
    lo = lane < 64

    ka = [kv_refs[4 * i][...].astype(F32) for i in range(n_pieces)]
    va = [kv_refs[4 * i + 1][...].astype(F32) for i in range(n_pieces)]

    def place(x, g, parity):
        if g != parity:
            x = pltpu.roll(x, 64, 1)
        keep = lo if parity == 0 else jnp.logical_not(lo)
        return jnp.where(keep, x, 0.0).astype(BF16)

    placed = {(g, parity): ([place(k, g, parity) for k in ka], [place(v, g, parity) for v in va])
              for g in range(KV_A) for parity in range(2)}
    for pair in range(H_A // 2):
        g = (2 * pair) // (H_A // KV_A)
        q = qa_ref[:, pair * LANE:(pair + 1) * LANE]
        acc = None
        for parity in range(2):
            ks, vs = placed[(g, parity)]
            scores = [lax.dot_general(q, k, _NT, preferred_element_type=F32) for k in ks]
            o = _softmax_pv(scores, vs, parity)
            acc = o if acc is None else acc + o
        oa_ref[:, pair * LANE:(pair + 1) * LANE] = acc.astype(BF16)

    for pair in range(H_B // 2):
        acc = None
        for parity in range(2):
            h = 2 * pair + parity
            q = qb_ref[:, h * LANE:(h + 1) * LANE]
            nope = jnp.logical_not(lo) if parity == 0 else lo
            scores, vs = [], []
            for i in range(n_pieces):
                kvb = kv_refs[4 * i + 2][:, h * LANE:(h + 1) * LANE]
                kre = kv_refs[4 * i + 3][...].astype(BF16)
                k = jnp.where(nope, kvb, kre)
                scores.append(lax.dot_general(q, k, _NT, preferred_element_type=F32))
                vs.append(jnp.where(nope, jnp.zeros_like(kvb), kvb))
            o = _softmax_pv(scores, vs, parity)
            acc = o if acc is None else acc + o
        ob_ref[:, pair * LANE:(pair + 1) * LANE] = acc.astype(BF16)


def _attention(qa, qb, ka, va, kvb, kre, prev=None, cache=None, layer=0):
    if cache is None:
        nb, nqt, nk, q_blk0, k_blk0 = BATCH, 1, SEQ, 0, 0
    else:
        nb, nqt, nk, q_blk0, k_blk0 = DEC_BATCH, DEC_SEQ // TM, DEC_SEQ, CTX_TILES, T_CTX // DEC_SEQ
    qspec = lambda w: pl.BlockSpec((TM, w), lambda b, j: (q_blk0 + b * nqt + j, 0))
    kspec = lambda w: pl.BlockSpec((nk, w), lambda b, j: (k_blk0 + b, 0))
    in_specs = [qspec(512), qspec(1024), kspec(128), kspec(128), kspec(1024), kspec(128)]
    args = [qa, qb, ka, va, kvb, kre]
    n_pieces = 1
    aliases = {}
    if cache is not None:
        cspec4 = pl.BlockSpec((None, None, PAST, 128), lambda b, j: (b, layer, 0, 0))
        cspec = lambda w: pl.BlockSpec((PAST, w), lambda b, j: (b * DEPTH + layer, 0))
        in_specs += [cspec4, cspec4, cspec(1024), cspec(128)]
        args += list(cache)
        n_pieces = 2
        in_specs += [pl.BlockSpec(memory_space=pl.ANY)] * 2
        args += list(prev)
        aliases = {len(args) - 2: 0, len(args) - 1: 1}

    def body(*refs):
        if cache is not None:
            refs = refs[:2 + 4 * n_pieces] + refs[-2:]
        _attn_kernel(n_pieces, *refs)

    return pl.pallas_call(
        body,
        grid=(nb, nqt),
        in_specs=in_specs,
        out_specs=[qspec(512), qspec(512)],
        out_shape=[jax.ShapeDtypeStruct((T_ALL, 512), BF16)] * 2,
        input_output_aliases=aliases,
        compiler_params=_cparams(("parallel", "arbitrary")),
        name="attention_ctx" if cache is None else "attention_lat",
    )(*args)


HL = 256
HS = 128
HG = 64
N_PAIRS = H_C // 2
FAST_DECAY_LIMIT = 80.0


def _hgrn_bottom_exact(q, k, c, lo, rev):
    row = lax.broadcasted_iota(jnp.int32, (HS, LANE), 0)
    srow = lax.broadcasted_iota(jnp.int32, (HS, HS), 0)
    scol = lax.broadcasted_iota(jnp.int32, (HS, HS), 1)
    out = []
    for parity in range(2):
        def dup(x):
            xs = pltpu.roll(x, 64, 1)
            return jnp.where(lo, x, xs) if parity == 0 else jnp.where(lo, xs, x)
        qd, kd, bd = dup(q), dup(k), dup(c)

        dg = row & 3
        if rev:
            dg = 3 - dg
        e = [None]
        for delta in range(1, 4):
            shifted = pltpu.roll(bd, delta if rev else HS - delta, 0)
            e.append(jnp.exp(jnp.minimum(shifted - bd, 0.0)))
        qp, kp = [], []
        for c1, c2 in ((0, 1), (2, 3)):
            cv = jnp.where(lo, c1, c2)
            dl = cv - dg
            fac = jnp.where(dl == 0, 1.0, jnp.where(dl == 1, e[1], jnp.where(dl == 2, e[2],
                            jnp.where(dl == 3, e[3], 0.0))))
            kp.append((kd * fac).astype(BF16))
            qp.append(jnp.where(dg == cv, qd, 0.0).astype(BF16))
        s = lax.dot_general(jnp.concatenate(qp, axis=1), jnp.concatenate(kp, axis=1), _NT,
                            preferred_element_type=F32)
        tot = jnp.where((srow >> 2) == (scol >> 2), s, 0.0)

        for lev in range(1, 3):
            g = 4 ** lev
            par = 4 * g
            shape3 = (HS // par, par, LANE)
            rid = lax.broadcasted_iota(jnp.int32, shape3, 1)
            dg3 = rid >> (2 * lev)
            if rev:
                dg3 = 3 - dg3
            b3, q3, k3 = bd.reshape(shape3), qd.reshape(shape3), kd.reshape(shape3)
            lo3 = lo.reshape(1, 1, LANE)
            qp, kp = [], []
            for c1, c2 in ((1, 2), (3, None)):
                idx = lambda cc: (4 - cc) * g if rev else cc * g - 1
                i1 = idx(c1)
                i2 = idx(c2) if c2 is not None else i1
                ridx = jnp.where(lo3, i1, i2)
                ref = jnp.sum(jnp.where(rid == ridx, b3, 0.0), axis=1, keepdims=True)
                cvk = jnp.where(lo3, c1, c2 if c2 is not None else 0)
                cvq = jnp.where(lo3, c1, c2 if c2 is not None else -1)
                kk = jnp.where(dg3 < cvk, k3 * jnp.exp(jnp.minimum(ref - b3, 0.0)), 0.0)
                qq = jnp.where(dg3 == cvq, q3 * jnp.exp(jnp.minimum(b3 - ref, 0.0)), 0.0)
                kp.append(kk.reshape(HS, LANE).astype(BF16))
                qp.append(qq.reshape(HS, LANE).astype(BF16))
            s = lax.dot_general(jnp.concatenate(qp, axis=1), jnp.concatenate(kp, axis=1), _NT,
                                preferred_element_type=F32)
            sh = 2 * lev + 2
            tot = tot + jnp.where((srow >> sh) == (scol >> sh), s, 0.0)
        out.append(tot)
    return out


def _hgrn_head(q, k, lf, v, st_ref, rev):
    row = lax.broadcasted_iota(jnp.int32, (HS, LANE), 0)
    lane = lax.broadcasted_iota(jnp.int32, (1, LANE), 1)
    lo = lane < 64
    hi = jnp.logical_not(lo)
    grow = row & (HG - 1)
    in_g1 = row >= HG

    c = lf
    d = 1
    while d < HG:
        if rev:
            c = c + jnp.where(grow < HG - d, pltpu.roll(c, HS - d, 0), 0.0)
        else:
            c = c + jnp.where(grow >= d, pltpu.roll(c, d, 0), 0.0)
        d *= 2
    if rev:
        t0, t1 = c[0:1, :], c[HG:HG + 1, :]
    else:
        t0, t1 = c[HG - 1:HG, :], c[HS - 1:HS, :]
    et0, et1 = jnp.exp(t0), jnp.exp(t1)
    qe = q * jnp.exp(c)
    e_out = jnp.exp(jnp.where(in_g1, t1, t0) - c)
    ke = k * e_out

    if rev:
        qb = qe * jnp.where(in_g1, 1.0, et1)
        kh = ke * jnp.where(in_g1, et0, 1.0)
    else:
        qb = qe * jnp.where(in_g1, et0, 1.0)
        kh = ke * jnp.where(in_g1, 1.0, et1)
    st = st_ref[...]
    o_int = lax.dot_general(qb.astype(BF16), st.astype(BF16), _NT, preferred_element_type=F32)
    upd = lax.dot_general(v, kh.astype(BF16), (((0,), (0,)), ((), ())), preferred_element_type=F32)
    r128 = lax.broadcasted_iota(jnp.int32, (LANE, LANE), 0)
    c128 = lax.broadcasted_iota(jnp.int32, (LANE, LANE), 1)
    st_ref[...] = st * (et0 * et1) + jnp.where((r128 >> 6) == (c128 >> 6), upd, 0.0)

    q_late = in_g1 if not rev else jnp.logical_not(in_g1)
    q_top = jnp.where(q_late, qe, 0.0)
    k_top = jnp.where(q_late, 0.0, ke).astype(BF16)
    top = [lax.dot_general(jnp.where(m, q_top, 0.0).astype(BF16), k_top, _NT, preferred_element_type=F32)
           for m in (lo, hi)]
    mid = HG // 2 if rev else HG // 2 - 1
    cm = c - jnp.where(in_g1, c[HG + mid:HG + mid + 1, :], c[mid:mid + 1, :])
    return c, top, o_int, cm, jnp.max(jnp.abs(cm))


def _hgrn_steps(jobs, bot_ref):
    lane = lax.broadcasted_iota(jnp.int32, (1, LANE), 1)
    lo = lane < 64
    hi = jnp.logical_not(lo)
    heads = [_hgrn_head(*job) for job in jobs]
    fast = functools.reduce(jnp.maximum, [h[4] for h in heads]) <= FAST_DECAY_LIMIT

    @pl.when(fast)
    def _():
        srow = lax.broadcasted_iota(jnp.int32, (HS, HS), 0)
        scol = lax.broadcasted_iota(jnp.int32, (HS, HS), 1)
        same = (srow >> 6) == (scol >> 6)
        for ji, (job, (_, _, _, cm, _)) in enumerate(zip(jobs, heads)):
            keep = same & ((scol >= srow) if job[5] else (scol <= srow))
            qf = job[0] * jnp.exp(cm)
            kf = (job[1] * jnp.exp(-cm)).astype(BF16)
            for parity, m in enumerate((lo, hi)):
                s = lax.dot_general(jnp.where(m, qf, 0.0).astype(BF16), kf, _NT, preferred_element_type=F32)
                bot_ref[ji, parity] = jnp.where(keep, s, 0.0)

    @pl.when(jnp.logical_not(fast))
    def _():
        for ji, (job, (c, _, _, _, _)) in enumerate(zip(jobs, heads)):
            for parity, s in enumerate(_hgrn_bottom_exact(job[0], job[1], c, lo, job[5])):
                bot_ref[ji, parity] = s

    outs = []
    for ji, (job, (_, top, o_int, _, _)) in enumerate(zip(jobs, heads)):
        v = job[3]
        probs = jnp.concatenate([(bot_ref[ji, 0] + top[0]).astype(BF16), (bot_ref[ji, 1] + top[1]).astype(BF16)],
                                axis=1)
        vv = jnp.concatenate([jnp.where(lo, v, jnp.zeros_like(v)), jnp.where(hi, v, jnp.zeros_like(v))], axis=0)
        outs.append(jnp.dot(probs, vv, preferred_element_type=F32) + o_int)
    return outs


PAIRS_PER_ITER = 2


def _hgrn_kernel(has_s0, nt, *refs):
    if has_s0:
        (qf_ref, vf_ref, kf_ref, lf_ref, qb_ref, vb_ref, kb_ref, lb_ref, s0_ref,
         of_ref, ob_ref, so_ref, st_scr, bot_scr) = refs
    else:
        (qf_ref, vf_ref, kf_ref, lf_ref, qb_ref, vb_ref, kb_ref, lb_ref,
         of_ref, ob_ref, so_ref, st_scr, bot_scr) = refs
    j = pl.program_id(1)

    @pl.when(j == 0)
    def _():
        if has_s0:
            st_scr[...] = s0_ref[...]
        else:
            st_scr[...] = jnp.zeros_like(st_scr)

    n_sub = HL // HS

    def pair_body(it, carry):
        for step in range(n_sub):
            jobs, dests = [], []
            for pp in range(PAIRS_PER_ITER):
                p = it * PAIRS_PER_ITER + pp
                cols = pl.ds(pl.multiple_of(p * LANE, LANE), LANE)
                rf = pl.ds(step * HS, HS)
                rb = pl.ds((n_sub - 1 - step) * HS, HS)
                jobs.append((qf_ref[rf, cols].astype(F32), kf_ref[rf, cols], lf_ref[rf, cols], vf_ref[rf, cols],
                             st_scr.at[p, 0], False))
                dests.append((of_ref, rf, cols))
                jobs.append((qb_ref[rb, cols].astype(F32), kb_ref[rb, cols], lb_ref[rb, cols], vb_ref[rb, cols],
                             st_scr.at[p, 1], True))
                dests.append((ob_ref, rb, cols))
            for (ref, rows, cols), o in zip(dests, _hgrn_steps(jobs, bot_scr)):
                ref[rows, cols] = o
        return carry

    lax.fori_loop(0, N_PAIRS // PAIRS_PER_ITER, pair_body, 0)

    @pl.when(j == nt - 1)
    def _():
        so_ref[...] = st_scr[...]


def _hgrn(qc, vc, kff, lff, kfb, lfb, s0=None, prev=None, layer=0):
    if s0 is None:
        nb, nt, blk0 = BATCH, SEQ // HL, 0
    else:
        nb, nt, blk0 = DEC_BATCH, DEC_SEQ // HL, T_CTX // HL
    fspec = pl.BlockSpec((HL, 512), lambda b, j: (blk0 + b * nt + j, 0))
    bspec = pl.BlockSpec((HL, 512), lambda b, j: (blk0 + b * nt + nt - 1 - j, 0))
    sspec = pl.BlockSpec((None, N_PAIRS, 2, LANE, LANE), lambda b, j: (b, 0, 0, 0, 0))
    in_specs = [fspec] * 4 + [bspec] * 4
    args = [qc, vc, kff, lff, qc, vc, kfb, lfb]
    aliases = {}
    if s0 is not None:
        in_specs += [pl.BlockSpec((None, None, N_PAIRS, 2, LANE, LANE), lambda b, j: (b, layer, 0, 0, 0, 0))]
        args += [s0]
        in_specs += [pl.BlockSpec(memory_space=pl.ANY)] * 2
        args += list(prev)
        aliases = {len(args) - 2: 0, len(args) - 1: 1}

    def body(*refs):
        if s0 is not None:
            refs = refs[:9] + refs[11:]
        _hgrn_kernel(s0 is not None, nt, *refs)

    return pl.pallas_call(
        body,
        grid=(nb, nt),
        in_specs=in_specs,
        out_specs=[fspec, bspec, sspec],
        out_shape=[jax.ShapeDtypeStruct((T_ALL, 512), F32), jax.ShapeDtypeStruct((T_ALL, 512), F32),
                   jax.ShapeDtypeStruct((nb, N_PAIRS, 2, LANE, LANE), F32)],
        scratch_shapes=[pltpu.VMEM((N_PAIRS, 2, LANE, LANE), F32),
                        pltpu.VMEM((2 * PAIRS_PER_ITER, 2, HS, HS), F32)],
        input_output_aliases=aliases,
        compiler_params=_cparams(("parallel", "arbitrary")),
        name="hgrn_ctx" if s0 is None else "hgrn_lat",
    )(*args)


def _merge_kernel(xc_ref, xl_ref, mod_ref, n1_ref, wg_ref, oa_ref, ob_ref, ocf_ref, ocb_ref, sg_ref, con_ref,
                  ones_ref, wbr_ref, wout_ref, n2_ref, wrh_ref, wrl_ref, br_ref, x1_o, h2_o, comb_o):
    x = _x_tile(xc_ref, xl_ref, TMG)
    mod = mod_ref[...]
    xn = x * lax.rsqrt(jnp.mean(x * x, axis=-1, keepdims=True) + EPS) * n1_ref[...]
    h = (xn * (1.0 + mod[:, D:2 * D]) + mod[:, 0:D]).astype(BF16)

    oc = ocf_ref[...] + ocb_ref[...]
    oc = oc * lax.rsqrt(_group_mean(oc * oc, ones_ref[...], DV_C) + EPS) * con_ref[...]
    oc = (oc * sg_ref[...].astype(F32)).astype(BF16)
    branches = (oa_ref[...], ob_ref[...], oc)
    mix = None
    for jb in range(3):
        gate = jax.nn.sigmoid(jnp.dot(h, wg_ref[:, jb * D:(jb + 1) * D], preferred_element_type=F32))
        t = gate * jnp.dot(branches[jb], wbr_ref[jb], preferred_element_type=F32)
        mix = t if mix is None else mix + t
    out = jnp.dot(mix.astype(BF16), wout_ref[...], preferred_element_type=F32)
    x1 = x + mod[:, 2 * D:3 * D] * out
    x1_o[...] = x1

    x1n = x1 * lax.rsqrt(jnp.mean(x1 * x1, axis=-1, keepdims=True) + EPS) * n2_ref[...]
    h2 = x1n * (1.0 + mod[:, 4 * D:5 * D]) + mod[:, 3 * D:4 * D]
    h2_o[...] = h2.astype(BF16)

    h2h, h2l = _split_hi_lo(h2)
    logits = (jnp.dot(h2h, wrh_ref[...], preferred_element_type=F32)
              + jnp.dot(h2l, wrh_ref[...], preferred_element_type=F32)
              + jnp.dot(h2h, wrl_ref[...], preferred_element_type=F32)) + br_ref[...]
    lane = lax.broadcasted_iota(jnp.int32, logits.shape, 1)
    neg = -jnp.inf
    is_g = lane < N_GROUPS
    gl = jnp.where(is_g, logits, neg)
    gmax = gl.max(axis=-1, keepdims=True)
    gidx = jnp.min(jnp.where(gl == gmax, lane, LANE), axis=-1, keepdims=True)
    gp = 1.0 / jnp.sum(jnp.where(is_g, jnp.exp(gl - gmax), 0.0), axis=-1, keepdims=True)
    eid = lane - N_GROUPS
    in_grp = (eid >= 0) & (eid < N_EXPERTS) & ((eid >> 2) == gidx)
    el = jnp.where(in_grp, logits, neg)
    v1 = el.max(axis=-1, keepdims=True)
    i1 = jnp.min(jnp.where(el == v1, lane, LANE), axis=-1, keepdims=True)
    el2 = jnp.where(lane == i1, neg, el)
    v2 = el2.max(axis=-1, keepdims=True)
    i2 = jnp.min(jnp.where(el2 == v2, lane, LANE), axis=-1, keepdims=True)
    e2 = jnp.exp(v2 - v1)
    w1 = gp / (1.0 + e2)
    w2 = gp * e2 / (1.0 + e2)
    comb = jnp.where(lane == i1, w1, 0.0) + jnp.where(lane == i2, w2, 0.0)
    comb_o[...] = pltpu.roll(comb, LANE - N_GROUPS, 1)


TMG = 512


def _merge(layer, x, mod, n1, wgate, oa, ob, ocf, ocb, sg, con, ones, wbr, wout, n2, wr_hi, wr_lo, br):
    ctx_tiles, per_seq = T_CTX // TMG, DEC_SEQ // TMG
    mrow = lambda i: jnp.where(i < ctx_tiles, CTX_MOD_ROW, (i - ctx_tiles) // per_seq)
    tspec = lambda w: pl.BlockSpec((TMG, w), lambda i: (i, 0))
    wspec = lambda tail: pl.BlockSpec((None,) + tail, lambda i: (layer,) + (0,) * len(tail),
                                      pipeline_mode=pl.Buffered(1))
    return pl.pallas_call(
        _merge_kernel,
        grid=(T_ALL // TMG,),
        in_specs=_x_specs(x, TMG) + [
            _mod_spec(layer, mrow),
            _layer_spec((1, D), layer), wspec((D, 3 * D)),
            tspec(512), tspec(512), tspec(512), tspec(512), tspec(512),
            _layer_spec((1, 512), layer), _const_spec((256, 256)),
            wspec((3, BRANCH_W, D)), wspec((D, D)), _layer_spec((1, D), layer),
            _layer_spec((D, LANE), layer), _layer_spec((D, LANE), layer), _layer_spec((1, LANE), layer),
        ],
        out_specs=[tspec(D), tspec(D), tspec(LANE)],
        out_shape=[jax.ShapeDtypeStruct((T_ALL, D), F32), jax.ShapeDtypeStruct((T_ALL, D), BF16),
                   jax.ShapeDtypeStruct((T_ALL, LANE), F32)],
        compiler_params=_cparams(("parallel",)),
        name="merge",
    )(*_x_pair(x)[:2], mod, n1, wgate, oa, ob, ocf, ocb, sg, con, ones, wbr, wout, n2, wr_hi, wr_lo, br)


NB = 2048
N_BLK = T_ALL // NB
SBK = 256
N_SB = NB // SBK
WIN_SHIFT, FT_SHIFT, SEG_SHIFT = 6, 7, 4
WIN = 1 << WIN_SHIFT
FT = 1 << FT_SHIFT
SEG_ALIGN = 1 << SEG_SHIFT
STG = 2 * NB + N_EXPERTS * SEG_ALIGN + 256
QUAD = 4
E_STEP = 2


def _route_kernel(comb_ref, tri_ref, upper_ref, rank_ref, carry_ref, tab_ref, carry_scr):
    s = pl.program_id(1)

    @pl.when(s == 0)
    def _():
        carry_scr[...] = jnp.zeros_like(carry_scr)

    routed = comb_ref[...] > 0.0
    ind = jnp.where(routed, 1.0, 0.0)
    carry = carry_scr[...]
    rank = jnp.dot(tri_ref[...], ind.astype(BF16), preferred_element_type=F32) + carry
    rank_ref[...] = jnp.where(routed, rank, -1.0)
    carry_ref[...] = carry
    count = carry + jnp.sum(ind, axis=0, keepdims=True)
    carry_scr[...] = count

    @pl.when(s == N_SB - 1)
    def _():
        seg = jnp.floor((count + (SEG_ALIGN - 1.0)) * (1.0 / SEG_ALIGN)) * SEG_ALIGN
        off = jnp.dot(jnp.broadcast_to(seg, (8, LANE)), upper_ref[...], preferred_element_type=F32,
                      precision=lax.Precision.HIGHEST)
        tab_ref[0:1, :] = count
        tab_ref[1:2, :] = off[0:1, :]


def _route(comb):
    tri = jnp.asarray(np.tril(np.ones((SBK, SBK)), -1), BF16)
    upper = jnp.asarray(np.triu(np.ones((LANE, LANE)), 1), F32)
    return pl.pallas_call(
        _route_kernel,
        grid=(N_BLK, N_SB),
        in_specs=[
            pl.BlockSpec((SBK, LANE), lambda b, s: (b * N_SB + s, 0)),
            pl.BlockSpec((SBK, SBK), lambda b, s: (0, 0)),
            pl.BlockSpec((LANE, LANE), lambda b, s: (0, 0)),
        ],
        out_specs=[
            pl.BlockSpec((SBK, LANE), lambda b, s: (b * N_SB + s, 0)),
            pl.BlockSpec((None, None, 1, LANE), lambda b, s: (b, s, 0, 0)),
            pl.BlockSpec((None, 2, LANE), lambda b, s: (b, 0, 0)),
        ],
        out_shape=[jax.ShapeDtypeStruct((T_ALL, LANE), F32),
                   jax.ShapeDtypeStruct((N_BLK, N_SB, 1, LANE), F32),
                   jax.ShapeDtypeStruct((N_BLK, 2, LANE), F32)],
        scratch_shapes=[pltpu.VMEM((1, LANE), F32)],
        compiler_params=_cparams(("parallel", "arbitrary")),
        name="moe_route",
    )(comb, tri, upper)


def _moe_kernel(cnt_s, off_s, car_s, h2_ref, rank_ref, comb_ref, offv_ref, eg_ref, eu_ref, ed_ref, o_ref,
                stg_ref, acc_ref):
    blk = pl.program_id(0)
    step = pl.program_id(1)
    srow = lax.broadcasted_iota(jnp.int32, (WIN, SBK), 0).astype(F32)

    def windows(s):
        out = []
        for ex in range(N_EXPERTS):
            start = off_s[blk, ex] + car_s[blk, s, ex]
            length = car_s[blk, s + 1, ex] - car_s[blk, s, ex]
            ws = (start >> SEG_SHIFT) << SEG_SHIFT
            out.append((ws, (start - ws + length + (WIN - 1)) >> WIN_SHIFT))
        return out, functools.reduce(jnp.maximum, [w[1] for w in out])

    def positions(s):
        rows = pl.ds(pl.multiple_of(s * SBK, SBK), SBK)
        rank = rank_ref[rows, :]
        return rows, jnp.where(rank >= 0.0, rank + offv_ref[1:2, :], -1.0e6)

    @pl.when(step == 0)
    def _():
        stg_ref[...] = jnp.zeros_like(stg_ref)

        def sub_body(s, carry):
            rows, pos = positions(s)
            pos_t = pos.T
            h2 = h2_ref[rows, :]
            wins, nmax = windows(s)

            def chunk_body(c, carry2):
                for quad in range(N_EXPERTS // QUAD):
                    blocks = []
                    for ex in range(quad * QUAD, (quad + 1) * QUAD):
                        base = (wins[ex][0] + c * WIN).astype(F32)
                        hit = (pos_t[ex:ex + 1, :] - base) == srow
                        blocks.append(jnp.where(hit, 1.0, 0.0).astype(BF16))
                    moved = jnp.dot(jnp.concatenate(blocks, axis=0), h2, preferred_element_type=F32).astype(BF16)
                    for i in range(QUAD):
                        first = jnp.minimum(wins[quad * QUAD + i][0] + c * WIN, STG - WIN)
                        dst = pl.ds(pl.multiple_of(first, SEG_ALIGN), WIN)
                        stg_ref[dst, :] = stg_ref[dst, :] + moved[i * WIN:(i + 1) * WIN, :]
                return carry2

            lax.fori_loop(0, nmax, chunk_body, 0)
            return carry

        lax.fori_loop(0, N_SB, sub_body, 0)

    def ffn_tile(j, first, n_rows, n_valid):
        rows = pl.ds(pl.multiple_of(first, SEG_ALIGN), n_rows)
        xs = stg_ref[rows, :]
        hg = jnp.dot(xs, eg_ref[j], preferred_element_type=F32)
        hu = jnp.dot(xs, eu_ref[j], preferred_element_type=F32)
        act = (hg * jax.nn.sigmoid(hg) * hu).astype(BF16)
        y = jnp.dot(act, ed_ref[j], preferred_element_type=F32).astype(BF16)
        rid = lax.broadcasted_iota(jnp.int32, (n_rows, 1), 0)
        stg_ref[rows, :] = jnp.where(rid < n_valid, y, xs)

    for j in range(E_STEP):
        ex = step * E_STEP + j
        count = cnt_s[blk, ex]
        seg0 = off_s[blk, ex]
        n_big = (count + (FT - 1)) >> (FT_SHIFT + 1)

        def ffn_body(t, carry, j=j, count=count, seg0=seg0):
            ffn_tile(j, seg0 + t * (2 * FT), 2 * FT, count - t * (2 * FT))
            return carry

        lax.fori_loop(0, n_big, ffn_body, 0)

        @pl.when(count > n_big * (2 * FT))
        def _(j=j, count=count, seg0=seg0, n_big=n_big):
            ffn_tile(j, seg0 + n_big * (2 * FT), FT, count - n_big * (2 * FT))

    @pl.when(step == N_EXPERTS // E_STEP - 1)
    def _():
        def sub_body(s, carry):
            rows, pos = positions(s)
            pos_t = pos.T
            wts_t = comb_ref[rows, :].T
            wins, nmax = windows(s)
            acc_ref[...] = jnp.zeros_like(acc_ref)

            def chunk_body(c, carry2):
                sel_t, srcs = [], []
                for ex in range(N_EXPERTS):
                    base = (wins[ex][0] + c * WIN).astype(F32)
                    hit = (pos_t[ex:ex + 1, :] - base) == srow
                    sel_t.append(jnp.where(hit, wts_t[ex:ex + 1, :], 0.0).astype(BF16))
                    first = jnp.minimum(wins[ex][0] + c * WIN, STG - WIN)
                    srcs.append(stg_ref[pl.ds(pl.multiple_of(first, SEG_ALIGN), WIN), :])
                acc_ref[...] += lax.dot_general(jnp.concatenate(sel_t, axis=0), jnp.concatenate(srcs, axis=0),
                                                (((0,), (0,)), ((), ())), preferred_element_type=F32)
                return carry2

            lax.fori_loop(0, nmax, chunk_body, 0)
            o_ref[rows, :] = acc_ref[...].astype(BF16)
            return carry

        lax.fori_loop(0, N_SB, sub_body, 0)


def _moe(layer, h2, comb, eg, eu, ed):
    rank, carry, tab = _route(comb)
    cnt_i = tab[:, 0, :N_EXPERTS].astype(jnp.int32)
    off_i = tab[:, 1, :N_EXPERTS].astype(jnp.int32)
    car_i = jnp.concatenate([carry[:, :, 0, :N_EXPERTS].astype(jnp.int32), cnt_i[:, None, :]], axis=1)
    bspec = lambda w: pl.BlockSpec((NB, w), lambda b, e, *_: (b, 0))
    grid_spec = pltpu.PrefetchScalarGridSpec(
        num_scalar_prefetch=3,
        grid=(N_BLK, N_EXPERTS // E_STEP),
        in_specs=[
            bspec(D), bspec(LANE), bspec(LANE),
            pl.BlockSpec((None, 2, LANE), lambda b, e, *_: (b, 0, 0)),
            pl.BlockSpec((None, E_STEP, D, D_EXPERT), lambda b, e, *_: (layer, e, 0, 0)),
            pl.BlockSpec((None, E_STEP, D, D_EXPERT), lambda b, e, *_: (layer, e, 0, 0)),
            pl.BlockSpec((None, E_STEP, D_EXPERT, D), lambda b, e, *_: (layer, e, 0, 0)),
        ],
        out_specs=bspec(D),
        scratch_shapes=[pltpu.VMEM((STG, D), BF16), pltpu.VMEM((SBK, D), F32)],
    )
    return pl.pallas_call(
        _moe_kernel,
        grid_spec=grid_spec,
        out_shape=jax.ShapeDtypeStruct((T_ALL, D), BF16),
        compiler_params=_cparams(("parallel", "arbitrary")),
        name="moe",
    )(cnt_i, off_i, car_i, h2, rank, comb, tab, eg, eu, ed)


def _residual_kernel(final, x1_ref, moe_ref, mod_ref, fg_ref, o_ref):
    x2 = x1_ref[...] + mod_ref[:, 5 * D:6 * D] * moe_ref[...].astype(F32)
    if final:
        x2 = x2 * lax.rsqrt(jnp.mean(x2 * x2, axis=-1, keepdims=True) + EPS) * fg_ref[...]
    o_ref[...] = x2


def _residual(layer, final, x1, moe, mod, fg, tile0=0, n_tiles=N_TILES):
    src = lambda: pl.BlockSpec((TM, D), lambda i: (tile0 + i, 0))
    return pl.pallas_call(
        functools.partial(_residual_kernel, final),
        grid=(n_tiles,),
        in_specs=[src(), src(), _mod_spec(layer, lambda i: _mod_row(tile0 + i)), _const_spec((1, D))],
        out_specs=_tile_spec(D),
        out_shape=jax.ShapeDtypeStruct((n_tiles * TM, D), F32),
        compiler_params=_cparams(("parallel",)),
        name="residual",
    )(x1, moe, mod, fg)


def _state_to_blockdiag(s):
    lead = s.shape[:-3]
    st = jnp.swapaxes(s, -1, -2).reshape(lead + (N_PAIRS, 2, DV_C, DK_C))
    z = jnp.zeros_like(st[..., 0, :, :])
    top = jnp.concatenate([st[..., 0, :, :], z], axis=-1)
    bot = jnp.concatenate([z, st[..., 1, :, :]], axis=-1)
    return jnp.concatenate([top, bot], axis=-2)


def _blockdiag_to_state(sb):
    lead = sb.shape[:-3]
    even = sb[..., :DV_C, :DK_C]
    odd = sb[..., DV_C:, DK_C:]
    st = jnp.stack([even, odd], axis=-3).reshape(lead + (H_C, DV_C, DK_C))
    return jnp.swapaxes(st, -1, -2)


def kernel(x_prompt, x_sample, cache_gqa_k, cache_gqa_v, cache_mla_ckv, cache_mla_krope, state_hgrn, c, c_ctx,
           w_mod, b_mod, norm1_g, norm2_g, w_in, a_qnorm, a_knorm, b_qnorm, b_wq, b_kvnorm, b_wkv, c_lb_logits,
           c_onorm, w_branch, w_out, r_group_w, r_group_b, r_expert_w, r_expert_b, e_gate, e_up, e_down, final_g):
    x = (x_prompt.reshape(T_CTX, D), x_sample.reshape(T_LAT, D))
    cvec = jnp.concatenate([c, c_ctx[None, :], jnp.zeros((MOD_ROWS - DEC_BATCH - 1, D), F32)], axis=0)
    mod = _mod_table(cvec, w_mod, b_mod).reshape(DEPTH * MOD_ROWS, 1, 6 * D)
    taba, tabb, tabk = _rope_tables()
    ones = _ones_block(256, 64)
    lbl = c_lb_logits.reshape(DEPTH, 2, H_C * DK_C)

    vec = lambda g, reps=1: jnp.tile(g, (1, reps))[:, None, :]
    n1, n2 = vec(norm1_g), vec(norm2_g)
    aq, ak, con = vec(a_qnorm, H_A), vec(a_knorm, KV_A), vec(c_onorm, H_C)
    bq, bkv = vec(b_qnorm), vec(b_kvnorm)
    w_in_p, w_gate = _pack_w_in(w_in)
    wq_p, wkv_p = _pack_wq(b_wq), _pack_wkv(b_wkv)
    wbr, wout = w_branch.astype(BF16), w_out.astype(BF16)
    n_pad = LANE - N_GROUPS - N_EXPERTS
    wr_hi, wr_lo = _split_hi_lo(jnp.concatenate([r_group_w, r_expert_w, jnp.zeros((DEPTH, D, n_pad), F32)], axis=-1))
    br = jnp.concatenate([r_group_b, r_expert_b, jnp.zeros((DEPTH, n_pad), F32)], axis=-1)[:, None, :]
    eg, eu, ed = e_gate.astype(BF16), e_up.astype(BF16), e_down.astype(BF16)

    ck = cache_gqa_k.reshape(DEC_BATCH, DEPTH, PAST, KV_A * HD_A)
    cv = cache_gqa_v.reshape(DEC_BATCH, DEPTH, PAST, KV_A * HD_A)
    ckvb = _ctx_kv(cache_mla_ckv.reshape(DEC_BATCH * DEPTH * PAST, KV_RANK), wkv_p)
    ckr = cache_mla_krope.reshape(DEC_BATCH * DEPTH * PAST, ROPE_B)
    zpad = jnp.zeros_like(ckr)
    ckre = jnp.concatenate([ckr, zpad, ckr, zpad], axis=1)
    s0 = jnp.swapaxes(_state_to_blockdiag(state_hgrn), 2, 3)

    new_k, new_v, new_ckv, new_kr, new_s = [], [], [], [], []
    for l in range(DEPTH):
        (qa, kan, ka, va, qb, ckv, kvb, kr, kre, lff, lfb, kff, kfb, qc, vc, sg) = _inproj(
            l, x, mod, n1, w_in_p, aq, ak, bq, wq_p, bkv, wkv_p, lbl, ones, taba, tabb, tabk)
        new_k.append(kan[:T_CTX])
        new_v.append(va[:T_CTX])
        new_ckv.append(ckv[:T_CTX])
        new_kr.append(kr[:T_CTX, :ROPE_B])

        oa, ob = _attention(qa, qb, ka, va, kvb, kre)
        oa, ob = _attention(qa, qb, ka, va, kvb, kre, prev=(oa, ob), cache=(ck, cv, ckvb, ckre), layer=l)

        ocf, ocb, s_ctx = _hgrn(qc, vc, kff, lff, kfb, lfb)
        ocf, ocb, _ = _hgrn(qc, vc, kff, lff, kfb, lfb, s0=s0, prev=(ocf, ocb), layer=l)
        new_s.append(s_ctx)

        x1, h2, comb = _merge(l, x, mod, n1, w_gate, oa, ob, ocf, ocb, sg, con, ones, wbr, wout, n2,
                              wr_hi, wr_lo, br)
        moe = _moe(l, h2, comb, eg, eu, ed)
        if l < DEPTH - 1:
            x = _residual(l, False, x1, moe, mod, final_g[None, :])

    last = DEPTH - 1
    y_prompt = _residual(last, True, x1, moe, mod, final_g[None, :], 0, CTX_TILES)
    y_sample = _residual(last, True, x1, moe, mod, final_g[None, :], CTX_TILES, N_TILES - CTX_TILES)
    stack = lambda parts, tail: jnp.stack([p.reshape(BATCH, SEQ, -1) for p in parts], axis=1).reshape(
        (BATCH, DEPTH, SEQ) + tail)
    states = _blockdiag_to_state(jnp.swapaxes(jnp.stack(new_s, axis=1), 2, 3))
    return (y_prompt.reshape(BATCH, SEQ, D), y_sample.reshape(DEC_BATCH, DEC_SEQ, D),
            stack(new_k, (KV_A, HD_A)), stack(new_v, (KV_A, HD_A)), stack(new_ckv, (KV_RANK,)),
            stack(new_kr, (ROPE_B,)), states)
```

```python
import functools

import numpy as np
import jax
import jax.numpy as jnp
from jax import lax
from jax.experimental import pallas as pl
from jax.experimental.pallas import tpu as pltpu

D = 1024
BATCH, SEQ = 32, 256
DEC_BATCH, DEC_SEQ = 8, 1024
PAST = 256
DEPTH = 2
GRID_W = 64
THETA = 10000.0
EPS = 1e-6
F_FLOOR = 1e-30
H_A, KV_A, HD_A = 8, 2, 64
H_B, Q_RANK, KV_RANK, NOPE_B, ROPE_B, V_B = 8, 384, 256, 64, 32, 64
H_C, DK_C, DV_C = 8, 64, 64
BRANCH_W = 512
N_GROUPS, E_PER_GROUP, N_EXPERTS, D_EXPERT = 4, 4, 16, 512

T_CTX = BATCH * SEQ
T_LAT = DEC_BATCH * DEC_SEQ
T_ALL = T_CTX + T_LAT
TM = 256
N_TILES = T_ALL // TM
CTX_TILES = T_CTX // TM
LAT_TILES_PER_SEQ = DEC_SEQ // TM
MOD_ROWS = 16
CTX_MOD_ROW = DEC_BATCH
LANE = 128
VMEM_LIMIT = 56 * 1024 * 1024

C_QA, C_KA, C_VA, C_QRA, C_KVA, C_KR = 0, 512, 640, 768, 1152, 1408
C_FF, C_FB, C_QC, C_IC, C_GC, C_END = 1536, 2048, 2560, 3072, 3584, 4096
R_QA, R_KA, R_VA, R_QRA, R_KVA, R_KR = 0, 512, 640, 768, 1152, 1408
R_FF, R_FB, R_QC, R_IC, R_GC, R_GATE, R_END = 1440, 1952, 2464, 2976, 3488, 4000, 7072

F32 = jnp.float32
BF16 = jnp.bfloat16


def _cparams(sem):
    return pltpu.CompilerParams(dimension_semantics=sem, vmem_limit_bytes=VMEM_LIMIT)


def _mod_row(i):
    return jnp.where(i < CTX_TILES, CTX_MOD_ROW, (i - CTX_TILES) // LAT_TILES_PER_SEQ)


def _pos_block(i):
    return jnp.where(i < CTX_TILES, LAT_TILES_PER_SEQ, (i - CTX_TILES) % LAT_TILES_PER_SEQ)


def _split_hi_lo(x):
    hi = x.astype(BF16)
    lo = (x - hi.astype(F32)).astype(BF16)
    return hi, lo


def _group_mean(x2, ones_blk, width):
    n = ones_blk.shape[0]
    outs = []
    for j in range(x2.shape[-1] // n):
        blk = x2[:, j * n:(j + 1) * n]
        hi, lo = _split_hi_lo(blk)
        s = jnp.dot(hi, ones_blk, preferred_element_type=F32) + jnp.dot(lo, ones_blk, preferred_element_type=F32)
        outs.append(s)
    s = outs[0] if len(outs) == 1 else jnp.concatenate(outs, axis=-1)
    return s * (1.0 / width)


def _rope(x, tab_ref, shift, period):
    c, s1, s2 = tab_ref[0], tab_ref[1], tab_ref[2]
    outs = []
    for j in range(x.shape[-1] // period):
        blk = x[:, j * period:(j + 1) * period]
        outs.append(blk * c + pltpu.roll(blk, shift, 1) * s1 + pltpu.roll(blk, period - shift, 1) * s2)
    return outs[0] if len(outs) == 1 else jnp.concatenate(outs, axis=-1)


def _mod_kernel(c_ref, w_ref, b_ref, o_ref):
    c = c_ref[...]
    a = c * jax.nn.sigmoid(c)
    o_ref[...] = jnp.dot(a, w_ref[...], preferred_element_type=F32, precision=lax.Precision.HIGHEST) + b_ref[...]


def _mod_table(cvec, w_mod, b_mod):
    nt = 1024
    return pl.pallas_call(
        _mod_kernel,
        grid=(DEPTH, 6 * D // nt),
        in_specs=[
            pl.BlockSpec((MOD_ROWS, D), lambda l, j: (0, 0)),
            pl.BlockSpec((None, D, nt), lambda l, j: (l, 0, j)),
            pl.BlockSpec((None, 1, nt), lambda l, j: (l, 0, j)),
        ],
        out_specs=pl.BlockSpec((None, MOD_ROWS, nt), lambda l, j: (l, 0, j)),
        out_shape=jax.ShapeDtypeStruct((DEPTH, MOD_ROWS, 6 * D), F32),
        compiler_params=_cparams(("arbitrary", "arbitrary")),
        name="mod_table",
    )(cvec, w_mod, b_mod.reshape(DEPTH, 1, 6 * D))


def _x_pair(x):
    if isinstance(x, tuple):
        return x[0], x[1], 0
    return x, x, T_CTX


def _x_specs(x, tile):
    _, _, lat_off = _x_pair(x)
    ctx = T_CTX // tile
    return [pl.BlockSpec((tile, D), lambda i: (jnp.minimum(i, ctx - 1), 0)),
            pl.BlockSpec((tile, D), lambda i: (jnp.maximum(i - ctx, 0) + lat_off // tile, 0))]


def _x_tile(xc_ref, xl_ref, tile):
    return jnp.where(pl.program_id(0) < T_CTX // tile, xc_ref[...], xl_ref[...])


def _inproj_kernel(layer, xc_ref, xl_ref, mod_ref, n1_ref, w_ref, aq_ref, ak_ref, bq_ref, wq_ref, bkv_ref, wkv_ref,
                   lbl_ref, ones_ref, taba_ref, tabb_ref, tabk_ref,
                   qa_o, kan_o, ka_o, va_o, qb_o, ckv_o, kvb_o, kr_o, kre_o,
                   lff_o, lfb_o, kff_o, kfb_o, qc_o, vc_o, sg_o):
    x = _x_tile(xc_ref, xl_ref, TM)
    mod = mod_ref[...]
    xn = x * lax.rsqrt(jnp.mean(x * x, axis=-1, keepdims=True) + EPS) * n1_ref[...]
    h = (xn * (1.0 + mod[:, D:2 * D]) + mod[:, 0:D]).astype(BF16)
    y = jnp.dot(h, w_ref[...], preferred_element_type=F32)
    ones = ones_ref[...]

    qa = y[:, C_QA:C_KA]
    qa = qa * lax.rsqrt(_group_mean(qa * qa, ones, HD_A) + EPS) * aq_ref[...]
    qa_o[...] = (_rope(qa, taba_ref, 16, LANE) * (HD_A ** -0.5)).astype(BF16)
    ka = y[:, C_KA:C_VA]
    ka = ka * lax.rsqrt(_group_mean(ka * ka, ones[:LANE, :LANE], HD_A) + EPS) * ak_ref[...]
    kan_o[...] = ka
    ka_o[...] = _rope(ka, taba_ref, 16, LANE).astype(BF16)
    va_o[...] = y[:, C_VA:C_QRA]

    qr = y[:, C_QRA:C_KVA]
    qr = qr * lax.rsqrt(jnp.mean(qr * qr, axis=-1, keepdims=True) + EPS) * bq_ref[...]
    qb = jnp.dot(qr.astype(BF16), wq_ref[...], preferred_element_type=F32)
    qb_o[...] = (_rope(qb, tabb_ref, 8, 2 * LANE) * ((NOPE_B + ROPE_B) ** -0.5)).astype(BF16)
    kv = y[:, C_KVA:C_KR]
    ckv = kv * lax.rsqrt(jnp.mean(kv * kv, axis=-1, keepdims=True) + EPS) * bkv_ref[...]
    ckv_o[...] = ckv
    kvb_o[...] = jnp.dot(ckv.astype(BF16), wkv_ref[...], preferred_element_type=F32).astype(BF16)
    kr = y[:, C_KR:C_FF]
    kr_o[...] = kr
    kre_o[...] = _rope(kr, tabk_ref, 8, LANE).astype(BF16)

    lbl = lbl_ref[...]
    e = jnp.exp(lbl - jnp.max(lbl, axis=0, keepdims=True))
    p = e / jnp.sum(e, axis=0, keepdims=True)
    lb = p[0] * 0.0
    for i in range(1, layer + 1):
        lb = lb + p[i]
    for d, (c0, lf_o, kf_o) in enumerate(((C_FF, lff_o, kff_o), (C_FB, lfb_o, kfb_o))):
        pre = y[:, c0:c0 + 512]
        lbd = lb[d:d + 1, :]
        f = jnp.maximum(lbd + (1.0 - lbd) * jax.nn.sigmoid(pre), F_FLOOR)
        lf_o[...] = jnp.log(f)
        kf_o[...] = 1.0 - f
    qc_o[...] = y[:, C_QC:C_IC].astype(BF16)
    vc_o[...] = y[:, C_IC:C_GC].astype(BF16)
    gc = y[:, C_GC:C_END]
    sg_o[...] = (gc * jax.nn.sigmoid(gc)).astype(BF16)


def _const_spec(shape):
    nd = len(shape)
    return pl.BlockSpec(shape, lambda i: (0,) * nd)


def _tile_spec(width):
    return pl.BlockSpec((TM, width), lambda i: (i, 0))


def _layer_spec(tail, layer):
    return pl.BlockSpec((None,) + tuple(tail), lambda *_: (layer,) + (0,) * len(tail))


def _mod_spec(layer, row_of_tile):
    return pl.BlockSpec((None, 1, 6 * D), lambda i, *_: (layer * MOD_ROWS + row_of_tile(i), 0, 0))


def _inproj(layer, x, mod, n1, w_in_p, aq, ak, bq, wq_p, bkv, wkv_p, lbl, ones, taba, tabb, tabk):
    outs = [
        (512, BF16), (128, F32), (128, BF16), (128, F32), (1024, BF16), (256, F32), (1024, BF16),
        (128, F32), (128, BF16), (512, F32), (512, F32), (512, F32), (512, F32), (512, BF16), (512, BF16),
        (512, BF16),
    ]
    tab_spec = lambda w: pl.BlockSpec((3, TM, w), lambda i: (0, _pos_block(i), 0))
    return pl.pallas_call(
        functools.partial(_inproj_kernel, layer),
        grid=(N_TILES,),
        in_specs=_x_specs(x, TM) + [
            _mod_spec(layer, _mod_row),
            _layer_spec((1, D), layer),
            _layer_spec((D, C_END), layer),
            _layer_spec((1, 512), layer), _layer_spec((1, 128), layer), _layer_spec((1, Q_RANK), layer),
            _layer_spec((Q_RANK, 1024), layer), _layer_spec((1, KV_RANK), layer), _layer_spec((KV_RANK, 1024), layer),
            _const_spec((DEPTH, 2, 512)), _const_spec((256, 256)),
            tab_spec(LANE), tab_spec(2 * LANE), tab_spec(LANE),
        ],
        out_specs=[_tile_spec(w) for w, _ in outs],
        out_shape=[jax.ShapeDtypeStruct((T_ALL, w), dt) for w, dt in outs],
        compiler_params=_cparams(("parallel",)),
        name="inproj",
    )(*_x_pair(x)[:2], mod, n1, w_in_p, aq, ak, bq, wq_p, bkv, wkv_p, lbl, ones, taba, tabb, tabk)


def _pack_w_in(w):
    z = jnp.zeros((DEPTH, D, 32), w.dtype)
    kr = w[..., R_KR:R_FF]
    main = jnp.concatenate([w[..., :R_KR], kr, z, kr, z, w[..., R_FF:R_GATE]], axis=-1)
    return main.astype(BF16), w[..., R_GATE:].astype(BF16)


def _pack_wq(wq):
    w = wq.reshape(DEPTH, Q_RANK, H_B, NOPE_B + ROPE_B)
    nope, rope = w[..., :NOPE_B], w[..., NOPE_B:]
    z = jnp.zeros((DEPTH, Q_RANK, H_B, 32), wq.dtype)
    even = jnp.concatenate([rope, z, nope], axis=-1)
    odd = jnp.concatenate([nope, rope, z], axis=-1)
    is_even = (jnp.arange(H_B) % 2 == 0)[None, None, :, None]
    return jnp.where(is_even, even, odd).reshape(DEPTH, Q_RANK, H_B * LANE).astype(BF16)


def _pack_wkv(wkv):
    w = wkv.reshape(DEPTH, KV_RANK, H_B, NOPE_B + V_B)
    nope, v = w[..., :NOPE_B], w[..., NOPE_B:]
    is_even = (jnp.arange(H_B) % 2 == 0)[None, None, :, None]
    return jnp.where(is_even, jnp.concatenate([v, nope], -1), jnp.concatenate([nope, v], -1)).reshape(
        DEPTH, KV_RANK, H_B * LANE).astype(BF16)


def _rope_tables():
    pos = np.arange(DEC_SEQ)
    row, col = pos // GRID_W, pos % GRID_W

    def pattern(half):
        quarter = half // 2
        inv = THETA ** (-np.arange(0, half, 2, dtype=np.float64) / half)
        ang = np.concatenate([row[:, None] * inv, row[:, None] * inv, col[:, None] * inv, col[:, None] * inv], 1)
        is_x2 = np.tile(np.concatenate([np.zeros(quarter), np.ones(quarter)]), 2)[None, :]
        c = np.cos(ang)
        s1 = np.sin(ang) * is_x2
        s2 = -np.sin(ang) * (1 - is_x2)
        return c, s1, s2

    def assemble(width, spans, half):
        c, s1, s2 = pattern(half)
        tc = np.ones((DEC_SEQ + TM, width))
        t1 = np.zeros((DEC_SEQ + TM, width))
        t2 = np.zeros((DEC_SEQ + TM, width))
        for start in spans:
            tc[:DEC_SEQ, start:start + 2 * half] = c
            t1[:DEC_SEQ, start:start + 2 * half] = s1
            t2[:DEC_SEQ, start:start + 2 * half] = s2
        return jnp.asarray(np.stack([tc, t1, t2]), F32)

    taba = assemble(LANE, (0, 64), 32)
    tabb = assemble(2 * LANE, (0, 128 + 64), 16)
    tabk = assemble(LANE, (0, 64), 16)
    return taba, tabb, tabk


def _ones_block(n, width):
    g = np.arange(n) // width
    return jnp.asarray(g[:, None] == g[None, :], BF16)


def _ctxkv_kernel(c_ref, w_ref, o_ref):
    o_ref[...] = jnp.dot(c_ref[...].astype(BF16), w_ref[...], preferred_element_type=F32).astype(BF16)


def _ctx_kv(ckv_cache, wkv_p):
    rows = ckv_cache.shape[0]
    return pl.pallas_call(
        _ctxkv_kernel,
        grid=(rows // PAST,),
        in_specs=[pl.BlockSpec((PAST, KV_RANK), lambda i: (i, 0)),
                  pl.BlockSpec((None, KV_RANK, 1024), lambda i: (i % DEPTH, 0, 0))],
        out_specs=pl.BlockSpec((PAST, 1024), lambda i: (i, 0)),
        out_shape=jax.ShapeDtypeStruct((rows, 1024), BF16),
        compiler_params=_cparams(("parallel",)),
        name="ctx_kv",
    )(ckv_cache, wkv_p)


_NT = (((1,), (1,)), ((), ()))


def _den_lane(parity):
    return 64 if parity == 0 else 0


def _softmax_pv(s, v, parity, mxu_den):
    m = s.max(axis=-1, keepdims=True)
    if not mxu_den:
        p = jnp.exp(s - m)
        return jnp.dot(p.astype(BF16), v, preferred_element_type=F32) / p.sum(axis=-1, keepdims=True)
    lane = lax.broadcasted_iota(jnp.int32, (1, LANE), 1)
    keep = (lane < 64) if parity == 0 else (lane >= 64)
    o = jnp.dot(jnp.exp((s - m).astype(BF16)), v, preferred_element_type=F32)
    return jnp.where(keep, o / o[:, _den_lane(parity):_den_lane(parity) + 1], 0.0)


def _attn_kernel(n_pieces, qa_ref, qb_ref, *refs):
    kv_refs = refs[:4 * n_pieces]
    oa_ref, ob_ref = refs[4 * n_pieces:4 * n_pieces + 2]
    hoist = n_pieces == 2
    lane = lax.broadcasted_iota(jnp.int32, (1, LANE), 1)
    lo = lane < 64
    hi = jnp.logical_not(lo)

    def rows(parts):
        return parts[0] if len(parts) == 1 else jnp.concatenate(parts, axis=0)

    def with_den(v, parity):
        return jnp.where(lane == _den_lane(parity), jnp.ones_like(v), v) if hoist else v

    def slabs_a():
        ka = [kv_refs[4 * i][...].astype(F32) for i in range(n_pieces)]
        va = [kv_refs[4 * i + 1][...].astype(F32) for i in range(n_pieces)]

        def place(x, g, parity):
            if g != parity:
                x = pltpu.roll(x, 64, 1)
            return jnp.where(lo if parity == 0 else hi, x, 0.0).astype(BF16)

        return {2 * g + parity: (rows([place(k, g, parity) for k in ka]),
                                 with_den(rows([place(v, g, parity) for v in va]), parity))
                for g in range(KV_A) for parity in range(2)}

    def slabs_b():
        out = {}
        kre = rows([kv_refs[4 * i + 3][...].astype(BF16) for i in range(n_pieces)])
        for h in range(H_B):
            parity = h % 2
            nope = hi if parity == 0 else lo
            kvb = rows([kv_refs[4 * i + 2][:, h * LANE:(h + 1) * LANE] for i in range(n_pieces)])
            out[h] = (jnp.where(nope, kvb, kre), with_den(jnp.where(nope, jnp.zeros_like(kvb), kvb), parity))
        return out

    if hoist:
        ka_scr, va_scr, kb_scr, vb_scr = refs[4 * n_pieces + 2:]

        @pl.when(pl.program_id(1) == 0)
        def _():
            for idx, (k, v) in slabs_a().items():
                ka_scr[idx] = k
                va_scr[idx] = v
            for idx, (k, v) in slabs_b().items():
                kb_scr[idx] = k
                vb_scr[idx] = v

        get_a = lambda idx: (ka_scr[idx], va_scr[idx])
        get_b = lambda idx: (kb_scr[idx], vb_scr[idx])
    else:
        sa, sb = slabs_a(), slabs_b()
        get_a, get_b = sa.__getitem__, sb.__getitem__

    for pair in range(H_A // 2):
        g = (2 * pair) // (H_A // KV_A)
        q = qa_ref[:, pair * LANE:(pair + 1) * LANE]
        acc = None
        for parity in range(2):
            k, v = get_a(2 * g + parity)
            o = _softmax_pv(lax.dot_general(q, k, _NT, preferred_element_type=F32), v, parity, hoist)
            acc = o if acc is None else acc + o
        oa_ref[:, pair * LANE:(pair + 1) * LANE] = acc.astype(BF16)

    for pair in range(H_B // 2):
        acc = None
        for parity in range(2):
            h = 2 * pair + parity
            k, v = get_b(h)
            q = qb_ref[:, h * LANE:(h + 1) * LANE]
            o = _softmax_pv(lax.dot_general(q, k, _NT, preferred_element_type=F32), v, parity, hoist)
            acc = o if acc is None else acc + o
        ob_ref[:, pair * LANE:(pair + 1) * LANE] = acc.astype(BF16)


def _attention(qa, qb, ka, va, kvb, kre, prev=None, cache=None, layer=0):
    if cache is None:
        nb, nqt, nk, q_blk0, k_blk0 = BATCH, 1, SEQ, 0, 0
    else:
        nb, nqt, nk, q_blk0, k_blk0 = DEC_BATCH, DEC_SEQ // TM, DEC_SEQ, CTX_TILES, T_CTX // DEC_SEQ
    qspec = lambda w: pl.BlockSpec((TM, w), lambda b, j: (q_blk0 + b * nqt + j, 0))
    kspec = lambda w: pl.BlockSpec((nk, w), lambda b, j: (k_blk0 + b, 0))
    in_specs = [qspec(512), qspec(1024), kspec(128), kspec(128), kspec(1024), kspec(128)]
    args = [qa, qb, ka, va, kvb, kre]
    n_pieces = 1
    aliases = {}
    if cache is not None:
        cspec4 = pl.BlockSpec((None, None, PAST, 128), lambda b, j: (b, layer, 0, 0))
        cspec = lambda w: pl.BlockSpec((PAST, w), lambda b, j: (b * DEPTH + layer, 0))
        in_specs += [cspec4, cspec4, cspec(1024), cspec(128)]
        args += list(cache)
        n_pieces = 2
        in_specs += [pl.BlockSpec(memory_space=pl.ANY)] * 2
        args += list(prev)
        aliases = {len(args) - 2: 0, len(args) - 1: 1}

    scratch = []
    if cache is not None:
        nkt = DEC_SEQ + PAST
        scratch = [pltpu.VMEM((2 * KV_A, nkt, LANE), BF16)] * 2 + [pltpu.VMEM((H_B, nkt, LANE), BF16)] * 2

    def body(*refs):
        if cache is not None:
            n_in = 2 + 4 * n_pieces
            refs = refs[:n_in] + refs[n_in + 2:]
        _attn_kernel(n_pieces, *refs)

    return pl.pallas_call(
        body,
        grid=(nb, nqt),
        in_specs=in_specs,
        out_specs=[qspec(512), qspec(512)],
        out_shape=[jax.ShapeDtypeStruct((T_ALL, 512), BF16)] * 2,
        scratch_shapes=scratch,
        input_output_aliases=aliases,
        compiler_params=_cparams(("parallel", "arbitrary")),
        name="attention_ctx" if cache is None else "attention_lat",
    )(*args)


HL = 256
HS = 128
HG = 64
N_PAIRS = H_C // 2
FAST_DECAY_LIMIT = 80.0


def _hgrn_bottom_exact(q, k, c, lo, rev):
    row = lax.broadcasted_iota(jnp.int32, (HS, LANE), 0)
    srow = lax.broadcasted_iota(jnp.int32, (HS, HS), 0)
    scol = lax.broadcasted_iota(jnp.int32, (HS, HS), 1)
    out = []
    for parity in range(2):
        def dup(x):
            xs = pltpu.roll(x, 64, 1)
            return jnp.where(lo, x, xs) if parity == 0 else jnp.where(lo, xs, x)
        qd, kd, bd = dup(q), dup(k), dup(c)

        dg = row & 3
        if rev:
            dg = 3 - dg
        e = [None]
        for delta in range(1, 4):
            shifted = pltpu.roll(bd, delta if rev else HS - delta, 0)
            e.append(jnp.exp(jnp.minimum(shifted - bd, 0.0)))
        qp, kp = [], []
        for c1, c2 in ((0, 1), (2, 3)):
            cv = jnp.where(lo, c1, c2)
            dl = cv - dg
            fac = jnp.where(dl == 0, 1.0, jnp.where(dl == 1, e[1], jnp.where(dl == 2, e[2],
                            jnp.where(dl == 3, e[3], 0.0))))
            kp.append((kd * fac).astype(BF16))
            qp.append(jnp.where(dg == cv, qd, 0.0).astype(BF16))
        s = lax.dot_general(jnp.concatenate(qp, axis=1), jnp.concatenate(kp, axis=1), _NT,
                            preferred_element_type=F32)
        tot = jnp.where((srow >> 2) == (scol >> 2), s, 0.0)

        for lev in range(1, 3):
            g = 4 ** lev
            par = 4 * g
            shape3 = (HS // par, par, LANE)
            rid = lax.broadcasted_iota(jnp.int32, shape3, 1)
            dg3 = rid >> (2 * lev)
            if rev:
                dg3 = 3 - dg3
            b3, q3, k3 = bd.reshape(shape3), qd.reshape(shape3), kd.reshape(shape3)
            lo3 = lo.reshape(1, 1, LANE)
            qp, kp = [], []
            for c1, c2 in ((1, 2), (3, None)):
                idx = lambda cc: (4 - cc) * g if rev else cc * g - 1
                i1 = idx(c1)
                i2 = idx(c2) if c2 is not None else i1
                ridx = jnp.where(lo3, i1, i2)
                ref = jnp.sum(jnp.where(rid == ridx, b3, 0.0), axis=1, keepdims=True)
                cvk = jnp.where(lo3, c1, c2 if c2 is not None else 0)
                cvq = jnp.where(lo3, c1, c2 if c2 is not None else -1)
                kk = jnp.where(dg3 < cvk, k3 * jnp.exp(jnp.minimum(ref - b3, 0.0)), 0.0)
                qq = jnp.where(dg3 == cvq, q3 * jnp.exp(jnp.minimum(b3 - ref, 0.0)), 0.0)
                kp.append(kk.reshape(HS, LANE).astype(BF16))
                qp.append(qq.reshape(HS, LANE).astype(BF16))
            s = lax.dot_general(jnp.concatenate(qp, axis=1), jnp.concatenate(kp, axis=1), _NT,
                                preferred_element_type=F32)
            sh = 2 * lev + 2
            tot = tot + jnp.where((srow >> sh) == (scol >> sh), s, 0.0)
        out.append(tot)
    return out


def _hgrn_head(q, k, lf, v, st_ref, rev):
    row = lax.broadcasted_iota(jnp.int32, (HS, LANE), 0)
    lane = lax.broadcasted_iota(jnp.int32, (1, LANE), 1)
    lo = lane < 64
    hi = jnp.logical_not(lo)
    in_g1 = row >= HG

    grow = row & (HG - 1)
    c = lf
    d = 1
    while d < HG:
        if rev:
            c = c + jnp.where(grow < HG - d, pltpu.roll(c, HS - d, 0), 0.0)
        else:
            c = c + jnp.where(grow >= d, pltpu.roll(c, d, 0), 0.0)
        d *= 2
    if rev:
        t0, t1 = c[0:1, :], c[HG:HG + 1, :]
    else:
        t0, t1 = c[HG - 1:HG, :], c[HS - 1:HS, :]
    et0, et1 = jnp.exp(t0), jnp.exp(t1)
    qe = q * jnp.exp(c)
    e_out = jnp.exp(jnp.where(in_g1, t1, t0) - c)
    ke = k * e_out

    if rev:
        qb = qe * jnp.where(in_g1, 1.0, et1)
        kh = ke * jnp.where(in_g1, et0, 1.0)
    else:
        qb = qe * jnp.where(in_g1, et0, 1.0)
        kh = ke * jnp.where(in_g1, 1.0, et1)
    st = st_ref[...]
    o_int = lax.dot_general(qb.astype(BF16), st.astype(BF16), _NT, preferred_element_type=F32)
    upd = lax.dot_general(v, kh.astype(BF16), (((0,), (0,)), ((), ())), preferred_element_type=F32)
    r128 = lax.broadcasted_iota(jnp.int32, (LANE, LANE), 0)
    c128 = lax.broadcasted_iota(jnp.int32, (LANE, LANE), 1)
    st_ref[...] = st * (et0 * et1) + jnp.where((r128 >> 6) == (c128 >> 6), upd, 0.0)

    q_late = in_g1 if not rev else jnp.logical_not(in_g1)
    q_top = jnp.where(q_late, qe, 0.0)
    k_top = jnp.where(q_late, 0.0, ke).astype(BF16)
    top = [lax.dot_general(jnp.where(m, q_top, 0.0).astype(BF16), k_top, _NT, preferred_element_type=F32)
           for m in (lo, hi)]
    mid = HG // 2 if rev else HG // 2 - 1
    cm = c - jnp.where(in_g1, c[HG + mid:HG + mid + 1, :], c[mid:mid + 1, :])
    return c, top, o_int, cm, jnp.max(jnp.abs(cm))


def _hgrn_steps(jobs, bot_ref):
    lane = lax.broadcasted_iota(jnp.int32, (1, LANE), 1)
    lo = lane < 64
    hi = jnp.logical_not(lo)
    heads = [_hgrn_head(*job) for job in jobs]
    fast = functools.reduce(jnp.maximum, [h[4] for h in heads]) <= FAST_DECAY_LIMIT

    @pl.when(fast)
    def _():
        srow = lax.broadcasted_iota(jnp.int32, (HS, HS), 0)
        scol = lax.broadcasted_iota(jnp.int32, (HS, HS), 1)
        same = (srow >> 6) == (scol >> 6)
        for ji, (job, (_, _, _, cm, _)) in enumerate(zip(jobs, heads)):
            keep = same & ((scol >= srow) if job[5] else (scol <= srow))
            qf = job[0] * jnp.exp(cm)
            kf = (job[1] * jnp.exp(-cm)).astype(BF16)
            for parity, m in enumerate((lo, hi)):
                s = lax.dot_general(jnp.where(m, qf, 0.0).astype(BF16), kf, _NT, preferred_element_type=F32)
                bot_ref[ji, parity] = jnp.where(keep, s, 0.0)

    @pl.when(jnp.logical_not(fast))
    def _():
        for ji, (job, (c, _, _, _, _)) in enumerate(zip(jobs, heads)):
            for parity, s in enumerate(_hgrn_bottom_exact(job[0], job[1], c, lo, job[5])):
                bot_ref[ji, parity] = s

    outs = []
    for ji, (job, (_, top, o_int, _, _)) in enumerate(zip(jobs, heads)):
        v = job[3]
        probs = jnp.concatenate([(bot_ref[ji, 0] + top[0]).astype(BF16), (bot_ref[ji, 1] + top[1]).astype(BF16)],
                                axis=1)
        vv = jnp.concatenate([jnp.where(lo, v, jnp.zeros_like(v)), jnp.where(hi, v, jnp.zeros_like(v))], axis=0)
        outs.append(jnp.dot(probs, vv, preferred_element_type=F32) + o_int)
    return outs


PAIRS_PER_ITER = 2


def _hgrn_kernel(has_s0, nt, *refs):
    if has_s0:
        (qf_ref, vf_ref, kf_ref, lf_ref, qb_ref, vb_ref, kb_ref, lb_ref, s0_ref,
         of_ref, ob_ref, so_ref, st_scr, bot_scr) = refs
    else:
        (qf_ref, vf_ref, kf_ref, lf_ref, qb_ref, vb_ref, kb_ref, lb_ref,
         of_ref, ob_ref, so_ref, st_scr, bot_scr) = refs
    j = pl.program_id(1)

    @pl.when(j == 0)
    def _():
        if has_s0:
            st_scr[...] = s0_ref[...]
        else:
            st_scr[...] = jnp.zeros_like(st_scr)

    n_sub = HL // HS

    def pair_body(it, carry):
        for step in range(n_sub):
            jobs, dests = [], []
            for pp in range(PAIRS_PER_ITER):
                p = it * PAIRS_PER_ITER + pp
                cols = pl.ds(pl.multiple_of(p * LANE, LANE), LANE)
                rf = pl.ds(step * HS, HS)
                rb = pl.ds((n_sub - 1 - step) * HS, HS)
                jobs.append((qf_ref[rf, cols].astype(F32), kf_ref[rf, cols], lf_ref[rf, cols], vf_ref[rf, cols],
                             st_scr.at[p, 0], False))
                dests.append((of_ref, rf, cols))
                jobs.append((qb_ref[rb, cols].astype(F32), kb_ref[rb, cols], lb_ref[rb, cols], vb_ref[rb, cols],
                             st_scr.at[p, 1], True))
                dests.append((ob_ref, rb, cols))
            for (ref, rows, cols), o in zip(dests, _hgrn_steps(jobs, bot_scr)):
                ref[rows, cols] = o
        return carry

    lax.fori_loop(0, N_PAIRS // PAIRS_PER_ITER, pair_body, 0)

    @pl.when(j == nt - 1)
    def _():
        so_ref[...] = st_scr[...]


def _hgrn(qc, vc, kff, lff, kfb, lfb, s0=None, prev=None, layer=0):
    if s0 is None:
        nb, nt, blk0 = BATCH, SEQ // HL, 0
    else:
        nb, nt, blk0 = DEC_BATCH, DEC_SEQ // HL, T_CTX // HL
    fspec = pl.BlockSpec((HL, 512), lambda b, j: (blk0 + b * nt + j, 0))
    bspec = pl.BlockSpec((HL, 512), lambda b, j: (blk0 + b * nt + nt - 1 - j, 0))
    sspec = pl.BlockSpec((None, N_PAIRS, 2, LANE, LANE), lambda b, j: (b, 0, 0, 0, 0))
    in_specs = [fspec] * 4 + [bspec] * 4
    args = [qc, vc, kff, lff, qc, vc, kfb, lfb]
    aliases = {}
    if s0 is not None:
        in_specs += [pl.BlockSpec((None, None, N_PAIRS, 2, LANE, LANE), lambda b, j: (b, layer, 0, 0, 0, 0))]
        args += [s0]
        in_specs += [pl.BlockSpec(memory_space=pl.ANY)] * 2
        args += list(prev)
        aliases = {len(args) - 2: 0, len(args) - 1: 1}

    def body(*refs):
        if s0 is not None:
            refs = refs[:9] + refs[11:]
        _hgrn_kernel(s0 is not None, nt, *refs)

    return pl.pallas_call(
        body,
        grid=(nb, nt),
        in_specs=in_specs,
        out_specs=[fspec, bspec, sspec],
        out_shape=[jax.ShapeDtypeStruct((T_ALL, 512), F32), jax.ShapeDtypeStruct((T_ALL, 512), F32),
                   jax.ShapeDtypeStruct((nb, N_PAIRS, 2, LANE, LANE), F32)],
        scratch_shapes=[pltpu.VMEM((N_PAIRS, 2, LANE, LANE), F32),
                        pltpu.VMEM((2 * PAIRS_PER_ITER, 2, HS, HS), F32)],
        input_output_aliases=aliases,
        compiler_params=_cparams(("parallel", "arbitrary")),
        name="hgrn_ctx" if s0 is None else "hgrn_lat",
    )(*args)


def _merge_kernel(xc_ref, xl_ref, mod_ref, n1_ref, wg_ref, oa_ref, ob_ref, ocf_ref, ocb_ref, sg_ref, con_ref,
                  ones_ref, wbr_ref, wout_ref, n2_ref, wrh_ref, wrl_ref, br_ref, tri_ref, upper_ref,
                  x1_o, h2_o, comb_o, rank_o, carry_o, tab_o, carry_scr):
    x = _x_tile(xc_ref, xl_ref, TMG)
    mod = mod_ref[...]
    xn = x * lax.rsqrt(jnp.mean(x * x, axis=-1, keepdims=True) + EPS) * n1_ref[...]
    h = (xn * (1.0 + mod[:, D:2 * D]) + mod[:, 0:D]).astype(BF16)

    oc = ocf_ref[...] + ocb_ref[...]
    oc = oc * lax.rsqrt(_group_mean(oc * oc, ones_ref[...], DV_C) + EPS) * con_ref[...]
    oc = (oc * sg_ref[...].astype(F32)).astype(BF16)
    branches = (oa_ref[...], ob_ref[...], oc)
    mix = None
    for jb in range(3):
        gate = jax.nn.sigmoid(jnp.dot(h, wg_ref[:, jb * D:(jb + 1) * D], preferred_element_type=F32))
        t = gate * jnp.dot(branches[jb], wbr_ref[jb], preferred_element_type=F32)
        mix = t if mix is None else mix + t
    out = jnp.dot(mix.astype(BF16), wout_ref[...], preferred_element_type=F32)
    x1 = x + mod[:, 2 * D:3 * D] * out
    x1_o[...] = x1

    x1n = x1 * lax.rsqrt(jnp.mean(x1 * x1, axis=-1, keepdims=True) + EPS) * n2_ref[...]
    h2 = x1n * (1.0 + mod[:, 4 * D:5 * D]) + mod[:, 3 * D:4 * D]
    h2_o[...] = h2.astype(BF16)

    h2h, h2l = _split_hi_lo(h2)
    logits = (jnp.dot(h2h, wrh_ref[...], preferred_element_type=F32)
              + jnp.dot(h2l, wrh_ref[...], preferred_element_type=F32)
              + jnp.dot(h2h, wrl_ref[...], preferred_element_type=F32)) + br_ref[...]
    lane = lax.broadcasted_iota(jnp.int32, logits.shape, 1)
    neg = -jnp.inf
    is_g = lane < N_GROUPS
    gl = jnp.where(is_g, logits, neg)
    gmax = gl.max(axis=-1, keepdims=True)
    gidx = jnp.min(jnp.where(gl == gmax, lane, LANE), axis=-1, keepdims=True)
    gp = 1.0 / jnp.sum(jnp.where(is_g, jnp.exp(gl - gmax), 0.0), axis=-1, keepdims=True)
    eid = lane - N_GROUPS
    in_grp = (eid >= 0) & (eid < N_EXPERTS) & ((eid >> 2) == gidx)
    el = jnp.where(in_grp, logits, neg)
    v1 = el.max(axis=-1, keepdims=True)
    i1 = jnp.min(jnp.where(el == v1, lane, LANE), axis=-1, keepdims=True)
    el2 = jnp.where(lane == i1, neg, el)
    v2 = el2.max(axis=-1, keepdims=True)
    i2 = jnp.min(jnp.where(el2 == v2, lane, LANE), axis=-1, keepdims=True)
    e2 = jnp.exp(v2 - v1)
    w1 = gp / (1.0 + e2)
    w2 = gp * e2 / (1.0 + e2)
    comb = jnp.where(lane == i1, w1, 0.0) + jnp.where(lane == i2, w2, 0.0)
    comb = pltpu.roll(comb, LANE - N_GROUPS, 1)
    comb_o[...] = comb

    step = pl.program_id(0) % MOE_STEPS

    @pl.when(step == 0)
    def _():
        carry_scr[...] = jnp.zeros_like(carry_scr)

    for sb in range(TMG // SBK):
        sub = comb[sb * SBK:(sb + 1) * SBK, :]
        routed = sub > 0.0
        ind = jnp.where(routed, 1.0, 0.0)
        carry = carry_scr[...]
        rank = jnp.dot(tri_ref[...], ind.astype(BF16), preferred_element_type=F32) + carry
        rank_o[sb * SBK:(sb + 1) * SBK, :] = jnp.where(routed, rank, -1.0)
        carry_o[sb] = carry
        carry_scr[...] = carry + jnp.sum(ind, axis=0, keepdims=True)

    @pl.when(step == MOE_STEPS - 1)
    def _():
        count = carry_scr[...]
        seg = jnp.floor((count + (SEG_ALIGN - 1.0)) * (1.0 / SEG_ALIGN)) * SEG_ALIGN
        off = jnp.dot(jnp.broadcast_to(seg, (8, LANE)), upper_ref[...], preferred_element_type=F32,
                      precision=lax.Precision.HIGHEST)
        tab_o[0:1, :] = count
        tab_o[1:2, :] = off[0:1, :]


TMG = 512


def _merge(layer, x, mod, n1, wgate, oa, ob, ocf, ocb, sg, con, ones, wbr, wout, n2, wr_hi, wr_lo, br):
    ctx_tiles, per_seq = T_CTX // TMG, DEC_SEQ // TMG
    mrow = lambda i: jnp.where(i < ctx_tiles, CTX_MOD_ROW, (i - ctx_tiles) // per_seq)
    tspec = lambda w: pl.BlockSpec((TMG, w), lambda i: (i, 0))
    wspec = lambda tail: pl.BlockSpec((None,) + tail, lambda i: (layer,) + (0,) * len(tail),
                                      pipeline_mode=pl.Buffered(1))
    return pl.pallas_call(
        _merge_kernel,
        grid=(T_ALL // TMG,),
        in_specs=_x_specs(x, TMG) + [
            _mod_spec(layer, mrow),
            _layer_spec((1, D), layer), wspec((D, 3 * D)),
            tspec(512), tspec(512), tspec(512), tspec(512), tspec(512),
            _layer_spec((1, 512), layer), _const_spec((256, 256)),
            wspec((3, BRANCH_W, D)), wspec((D, D)), _layer_spec((1, D), layer),
            _layer_spec((D, LANE), layer), _layer_spec((D, LANE), layer), _layer_spec((1, LANE), layer),
            _const_spec((SBK, SBK)), _const_spec((LANE, LANE)),
        ],
        out_specs=[
            tspec(D), tspec(D), tspec(LANE), tspec(LANE),
            pl.BlockSpec((None, TMG // SBK, 1, LANE), lambda i: (i // MOE_STEPS, i % MOE_STEPS, 0, 0)),
            pl.BlockSpec((None, 2, LANE), lambda i: (i // MOE_STEPS, 0, 0)),
        ],
        out_shape=[jax.ShapeDtypeStruct((T_ALL, D), F32), jax.ShapeDtypeStruct((T_ALL, D), BF16),
                   jax.ShapeDtypeStruct((T_ALL, LANE), F32), jax.ShapeDtypeStruct((T_ALL, LANE), F32),
                   jax.ShapeDtypeStruct((N_BLK, N_SB, 1, LANE), F32), jax.ShapeDtypeStruct((N_BLK, 2, LANE), F32)],
        scratch_shapes=[pltpu.VMEM((1, LANE), F32)],
        compiler_params=_cparams(("arbitrary",)),
        name="merge",
    )(*_x_pair(x)[:2], mod, n1, wgate, oa, ob, ocf, ocb, sg, con, ones, wbr, wout, n2, wr_hi, wr_lo, br,
      jnp.asarray(np.tril(np.ones((SBK, SBK)), -1), BF16), jnp.asarray(np.triu(np.ones((LANE, LANE)), 1), F32))


NB = 2048
N_BLK = T_ALL // NB
SBK = 256
N_SB = NB // SBK
WIN_SHIFT, FT_SHIFT, SEG_SHIFT = 6, 7, 4
WIN = 1 << WIN_SHIFT
FT = 1 << FT_SHIFT
SEG_ALIGN = 1 << SEG_SHIFT
STG = 2 * NB + N_EXPERTS * SEG_ALIGN + 256
QUAD = 4
E_STEP = 2
MOE_STEPS = NB // TMG


def _moe_kernel(cnt_s, off_s, car_s, h2_ref, rank_ref, comb_ref, offv_ref, eg_ref, eu_ref, ed_ref, o_ref,
                stg_ref, acc_ref):
    blk = pl.program_id(0)
    step = pl.program_id(1)
    srow = lax.broadcasted_iota(jnp.int32, (WIN, SBK), 0).astype(F32)

    def windows(s):
        out = []
        for ex in range(N_EXPERTS):
            start = off_s[blk, ex] + car_s[blk, s, ex]
            length = car_s[blk, s + 1, ex] - car_s[blk, s, ex]
            ws = (start >> SEG_SHIFT) << SEG_SHIFT
            out.append((ws, (start - ws + length + (WIN - 1)) >> WIN_SHIFT))
        return out, functools.reduce(jnp.maximum, [w[1] for w in out])

    def positions(s):
        rows = pl.ds(pl.multiple_of(s * SBK, SBK), SBK)
        rank = rank_ref[rows, :]
        return rows, jnp.where(rank >= 0.0, rank + offv_ref[1:2, :], -1.0e6)

    @pl.when(step == 0)
    def _():
        stg_ref[...] = jnp.zeros_like(stg_ref)

        def sub_body(s, carry):
            rows, pos = positions(s)
            pos_t = pos.T
            h2 = h2_ref[rows, :]
            wins, nmax = windows(s)

            def chunk_body(c, carry2):
                for quad in range(N_EXPERTS // QUAD):
                    blocks = []
                    for ex in range(quad * QUAD, (quad + 1) * QUAD):
                        base = (wins[ex][0] + c * WIN).astype(F32)
                        hit = (pos_t[ex:ex + 1, :] - base) == srow
                        blocks.append(jnp.where(hit, 1.0, 0.0).astype(BF16))
                    moved = jnp.dot(jnp.concatenate(blocks, axis=0), h2, preferred_element_type=F32).astype(BF16)
                    for i in range(QUAD):
                        first = jnp.minimum(wins[quad * QUAD + i][0] + c * WIN, STG - WIN)
                        dst = pl.ds(pl.multiple_of(first, SEG_ALIGN), WIN)
                        stg_ref[dst, :] = stg_ref[dst, :] + moved[i * WIN:(i + 1) * WIN, :]
                return carry2

            lax.fori_loop(0, nmax, chunk_body, 0)
            return carry

        lax.fori_loop(0, N_SB, sub_body, 0)

    def ffn_tile(j, first, n_rows, n_valid):
        rows = pl.ds(pl.multiple_of(first, SEG_ALIGN), n_rows)
        xs = stg_ref[rows, :]
        hg = jnp.dot(xs, eg_ref[j], preferred_element_type=F32)
        hu = jnp.dot(xs, eu_ref[j], preferred_element_type=F32)
        act = (hg * jax.nn.sigmoid(hg) * hu).astype(BF16)
        y = jnp.dot(act, ed_ref[j], preferred_element_type=F32).astype(BF16)
        rid = lax.broadcasted_iota(jnp.int32, (n_rows, 1), 0)
        stg_ref[rows, :] = jnp.where(rid < n_valid, y, xs)

    for j in range(E_STEP):
        ex = step * E_STEP + j
        count = cnt_s[blk, ex]
        seg0 = off_s[blk, ex]
        n_big = (count + (FT - 1)) >> (FT_SHIFT + 1)

        def ffn_body(t, carry, j=j, count=count, seg0=seg0):
            ffn_tile(j, seg0 + t * (2 * FT), 2 * FT, count - t * (2 * FT))
            return carry

        lax.fori_loop(0, n_big, ffn_body, 0)

        @pl.when(count > n_big * (2 * FT))
        def _(j=j, count=count, seg0=seg0, n_big=n_big):
            ffn_tile(j, seg0 + n_big * (2 * FT), FT, count - n_big * (2 * FT))

    @pl.when(step == N_EXPERTS // E_STEP - 1)
    def _():
        def sub_body(s, carry):
            rows, pos = positions(s)
            pos_t = pos.T
            wts_t = comb_ref[rows, :].T
            wins, nmax = windows(s)
            acc_ref[...] = jnp.zeros_like(acc_ref)

            def chunk_body(c, carry2):
                sel_t, srcs = [], []
                for ex in range(N_EXPERTS):
                    base = (wins[ex][0] + c * WIN).astype(F32)
                    hit = (pos_t[ex:ex + 1, :] - base) == srow
                    sel_t.append(jnp.where(hit, wts_t[ex:ex + 1, :], 0.0).astype(BF16))
                    first = jnp.minimum(wins[ex][0] + c * WIN, STG - WIN)
                    srcs.append(stg_ref[pl.ds(pl.multiple_of(first, SEG_ALIGN), WIN), :])
                acc_ref[...] += lax.dot_general(jnp.concatenate(sel_t, axis=0), jnp.concatenate(srcs, axis=0),
                                                (((0,), (0,)), ((), ())), preferred_element_type=F32)
                return carry2

            lax.fori_loop(0, nmax, chunk_body, 0)
            o_ref[rows, :] = acc_ref[...].astype(BF16)
            return carry

        lax.fori_loop(0, N_SB, sub_body, 0)


def _moe(layer, h2, comb, rank, carry, tab, eg, eu, ed):
    cnt_i = tab[:, 0, :N_EXPERTS].astype(jnp.int32)
    off_i = tab[:, 1, :N_EXPERTS].astype(jnp.int32)
    car_i = jnp.concatenate([carry[:, :, 0, :N_EXPERTS].astype(jnp.int32), cnt_i[:, None, :]], axis=1)
    bspec = lambda w: pl.BlockSpec((NB, w), lambda b, e, *_: (b, 0))
    grid_spec = pltpu.PrefetchScalarGridSpec(
        num_scalar_prefetch=3,
        grid=(N_BLK, N_EXPERTS // E_STEP),
        in_specs=[
            bspec(D), bspec(LANE), bspec(LANE),
            pl.BlockSpec((None, 2, LANE), lambda b, e, *_: (b, 0, 0)),
            pl.BlockSpec((None, E_STEP, D, D_EXPERT), lambda b, e, *_: (layer, e, 0, 0)),
            pl.BlockSpec((None, E_STEP, D, D_EXPERT), lambda b, e, *_: (layer, e, 0, 0)),
            pl.BlockSpec((None, E_STEP, D_EXPERT, D), lambda b, e, *_: (layer, e, 0, 0)),
        ],
        out_specs=bspec(D),
        scratch_shapes=[pltpu.VMEM((STG, D), BF16), pltpu.VMEM((SBK, D), F32)],
    )
    return pl.pallas_call(
        _moe_kernel,
        grid_spec=grid_spec,
        out_shape=jax.ShapeDtypeStruct((T_ALL, D), BF16),
        compiler_params=_cparams(("parallel", "arbitrary")),
        name="moe",
    )(cnt_i, off_i, car_i, h2, rank, comb, tab, eg, eu, ed)


def _residual_kernel(final, x1_ref, moe_ref, mod_ref, fg_ref, o_ref):
    x2 = x1_ref[...] + mod_ref[:, 5 * D:6 * D] * moe_ref[...].astype(F32)
    if final:
        x2 = x2 * lax.rsqrt(jnp.mean(x2 * x2, axis=-1, keepdims=True) + EPS) * fg_ref[...]
    o_ref[...] = x2


TR = DEC_SEQ


def _residual(layer, final, x1, moe, mod, fg, tile0=0, n_tiles=T_ALL // TR):
    ctx_tiles = T_CTX // TR
    mrow = lambda i: jnp.where(tile0 + i < ctx_tiles, CTX_MOD_ROW, tile0 + i - ctx_tiles)
    src = lambda: pl.BlockSpec((TR, D), lambda i: (tile0 + i, 0))
    return pl.pallas_call(
        functools.partial(_residual_kernel, final),
        grid=(n_tiles,),
        in_specs=[src(), src(), _mod_spec(layer, mrow), _const_spec((1, D))],
        out_specs=pl.BlockSpec((TR, D), lambda i: (i, 0)),
        out_shape=jax.ShapeDtypeStruct((n_tiles * TR, D), F32),
        compiler_params=_cparams(("parallel",)),
        name="residual",
    )(x1, moe, mod, fg)


def _state_to_blockdiag(s):
    lead = s.shape[:-3]
    st = jnp.swapaxes(s, -1, -2).reshape(lead + (N_PAIRS, 2, DV_C, DK_C))
    z = jnp.zeros_like(st[..., 0, :, :])
    top = jnp.concatenate([st[..., 0, :, :], z], axis=-1)
    bot = jnp.concatenate([z, st[..., 1, :, :]], axis=-1)
    return jnp.concatenate([top, bot], axis=-2)


def _blockdiag_to_state(sb):
    lead = sb.shape[:-3]
    even = sb[..., :DV_C, :DK_C]
    odd = sb[..., DV_C:, DK_C:]
    st = jnp.stack([even, odd], axis=-3).reshape(lead + (H_C, DV_C, DK_C))
    return jnp.swapaxes(st, -1, -2)


def kernel(x_prompt, x_sample, cache_gqa_k, cache_gqa_v, cache_mla_ckv, cache_mla_krope, state_hgrn, c, c_ctx,
           w_mod, b_mod, norm1_g, norm2_g, w_in, a_qnorm, a_knorm, b_qnorm, b_wq, b_kvnorm, b_wkv, c_lb_logits,
           c_onorm, w_branch, w_out, r_group_w, r_group_b, r_expert_w, r_expert_b, e_gate, e_up, e_down, final_g):
    x = (x_prompt.reshape(T_CTX, D), x_sample.reshape(T_LAT, D))
    cvec = jnp.concatenate([c, c_ctx[None, :], jnp.zeros((MOD_ROWS - DEC_BATCH - 1, D), F32)], axis=0)
    mod = _mod_table(cvec, w_mod, b_mod).reshape(DEPTH * MOD_ROWS, 1, 6 * D)
    taba, tabb, tabk = _rope_tables()
    ones = _ones_block(256, 64)
    lbl = c_lb_logits.reshape(DEPTH, 2, H_C * DK_C)

    vec = lambda g, reps=1: jnp.tile(g, (1, reps))[:, None, :]
    n1, n2 = vec(norm1_g), vec(norm2_g)
    aq, ak, con = vec(a_qnorm, H_A), vec(a_knorm, KV_A), vec(c_onorm, H_C)
    bq, bkv = vec(b_qnorm), vec(b_kvnorm)
    w_in_p, w_gate = _pack_w_in(w_in)
    wq_p, wkv_p = _pack_wq(b_wq), _pack_wkv(b_wkv)
    wbr, wout = w_branch.astype(BF16), w_out.astype(BF16)
    n_pad = LANE - N_GROUPS - N_EXPERTS
    wr_hi, wr_lo = _split_hi_lo(jnp.concatenate([r_group_w, r_expert_w, jnp.zeros((DEPTH, D, n_pad), F32)], axis=-1))
    br = jnp.concatenate([r_group_b, r_expert_b, jnp.zeros((DEPTH, n_pad), F32)], axis=-1)[:, None, :]
    eg, eu, ed = e_gate.astype(BF16), e_up.astype(BF16), e_down.astype(BF16)

    ck = cache_gqa_k.reshape(DEC_BATCH, DEPTH, PAST, KV_A * HD_A)
    cv = cache_gqa_v.reshape(DEC_BATCH, DEPTH, PAST, KV_A * HD_A)
    ckvb = _ctx_kv(cache_mla_ckv.reshape(DEC_BATCH * DEPTH * PAST, KV_RANK), wkv_p)
    ckr = cache_mla_krope.reshape(DEC_BATCH * DEPTH * PAST, ROPE_B)
    zpad = jnp.zeros_like(ckr)
    ckre = jnp.concatenate([ckr, zpad, ckr, zpad], axis=1)
    s0 = jnp.swapaxes(_state_to_blockdiag(state_hgrn), 2, 3)

    new_k, new_v, new_ckv, new_kr, new_s = [], [], [], [], []
    for l in range(DEPTH):
        (qa, kan, ka, va, qb, ckv, kvb, kr, kre, lff, lfb, kff, kfb, qc, vc, sg) = _inproj(
            l, x, mod, n1, w_in_p, aq, ak, bq, wq_p, bkv, wkv_p, lbl, ones, taba, tabb, tabk)
        new_k.append(kan[:T_CTX])
        new_v.append(va[:T_CTX])
        new_ckv.append(ckv[:T_CTX])
        new_kr.append(kr[:T_CTX, :ROPE_B])

        oa, ob = _attention(qa, qb, ka, va, kvb, kre)
        oa, ob = _attention(qa, qb, ka, va, kvb, kre, prev=(oa, ob), cache=(ck, cv, ckvb, ckre), layer=l)

        ocf, ocb, s_ctx = _hgrn(qc, vc, kff, lff, kfb, lfb)
        ocf, ocb, _ = _hgrn(qc, vc, kff, lff, kfb, lfb, s0=s0, prev=(ocf, ocb), layer=l)
        new_s.append(s_ctx)

        x1, h2, comb, rank, carry, tab = _merge(l, x, mod, n1, w_gate, oa, ob, ocf, ocb, sg, con, ones, wbr, wout,
                                                n2, wr_hi, wr_lo, br)
        moe = _moe(l, h2, comb, rank, carry, tab, eg, eu, ed)
        if l < DEPTH - 1:
            x = _residual(l, False, x1, moe, mod, final_g[None, :])

    last = DEPTH - 1
    y_prompt = _residual(last, True, x1, moe, mod, final_g[None, :], 0, T_CTX // TR)
    y_sample = _residual(last, True, x1, moe, mod, final_g[None, :], T_CTX // TR, T_LAT // TR)
    stack = lambda parts, tail: jnp.stack([p.reshape(BATCH, SEQ, -1) for p in parts], axis=1).reshape(
        (BATCH, DEPTH, SEQ) + tail)
    states = _blockdiag_to_state(jnp.swapaxes(jnp.stack(new_s, axis=1), 2, 3))
    return (y_prompt.reshape(BATCH, SEQ, D), y_sample.reshape(DEC_BATCH, DEC_SEQ, D),
            stack(new_k, (KV_A, HD_A)), stack(new_v, (KV_A, HD_A)), stack(new_ckv, (KV_RANK,)),
            stack(new_kr, (ROPE_B,)), states)
```

```python
import functools

import numpy as np
import jax
import jax.numpy as jnp
from jax import lax
from jax.experimental import pallas as pl
from jax.experimental.pallas import tpu as pltpu

D = 1024
BATCH, SEQ = 32, 256
DEC_BATCH, DEC_SEQ = 8, 1024
PAST = 256
DEPTH = 2
GRID_W = 64
THETA = 10000.0
EPS = 1e-6
F_FLOOR = 1e-30
H_A, KV_A, HD_A = 8, 2, 64
H_B, Q_RANK, KV_RANK, NOPE_B, ROPE_B, V_B = 8, 384, 256, 64, 32, 64
H_C, DK_C, DV_C = 8, 64, 64
BRANCH_W = 512
N_GROUPS, E_PER_GROUP, N_EXPERTS, D_EXPERT = 4, 4, 16, 512

T_CTX = BATCH * SEQ
T_LAT = DEC_BATCH * DEC_SEQ
T_ALL = T_CTX + T_LAT
TM = 256
N_TILES = T_ALL // TM
CTX_TILES = T_CTX // TM
LAT_TILES_PER_SEQ = DEC_SEQ // TM
MOD_ROWS = 16
CTX_MOD_ROW = DEC_BATCH
LANE = 128
VMEM_LIMIT = 56 * 1024 * 1024

C_QA, C_KA, C_VA, C_QRA, C_KVA, C_KR = 0, 512, 640, 768, 1152, 1408
C_FF, C_FB, C_QC, C_IC, C_GC, C_END = 1536, 2048, 2560, 3072, 3584, 4096
R_QA, R_KA, R_VA, R_QRA, R_KVA, R_KR = 0, 512, 640, 768, 1152, 1408
R_FF, R_FB, R_QC, R_IC, R_GC, R_GATE, R_END = 1440, 1952, 2464, 2976, 3488, 4000, 7072

F32 = jnp.float32
BF16 = jnp.bfloat16


def _cparams(sem):
    return pltpu.CompilerParams(dimension_semantics=sem, vmem_limit_bytes=VMEM_LIMIT)


def _mod_row(i):
    return jnp.where(i < CTX_TILES, CTX_MOD_ROW, (i - CTX_TILES) // LAT_TILES_PER_SEQ)


def _pos_block(i):
    return jnp.where(i < CTX_TILES, LAT_TILES_PER_SEQ, (i - CTX_TILES) % LAT_TILES_PER_SEQ)


def _split_hi_lo(x):
    hi = x.astype(BF16)
    lo = (x - hi.astype(F32)).astype(BF16)
    return hi, lo


def _group_mean(x2, ones_blk, width):
    n = ones_blk.shape[0]
    outs = []
    for j in range(x2.shape[-1] // n):
        blk = x2[:, j * n:(j + 1) * n]
        hi, lo = _split_hi_lo(blk)
        s = jnp.dot(hi, ones_blk, preferred_element_type=F32) + jnp.dot(lo, ones_blk, preferred_element_type=F32)
        outs.append(s)
    s = outs[0] if len(outs) == 1 else jnp.concatenate(outs, axis=-1)
    return s * (1.0 / width)


def _rope(x, tab_ref, shift, period):
    c, s1, s2 = tab_ref[0], tab_ref[1], tab_ref[2]
    outs = []
    for j in range(x.shape[-1] // period):
        blk = x[:, j * period:(j + 1) * period]
        outs.append(blk * c + pltpu.roll(blk, shift, 1) * s1 + pltpu.roll(blk, period - shift, 1) * s2)
    return outs[0] if len(outs) == 1 else jnp.concatenate(outs, axis=-1)


def _mod_kernel(c_ref, w_ref, b_ref, o_ref):
    c = c_ref[...]
    a = c * jax.nn.sigmoid(c)
    o_ref[...] = jnp.dot(a, w_ref[...], preferred_element_type=F32, precision=lax.Precision.HIGHEST) + b_ref[...]


def _mod_table(cvec, w_mod, b_mod):
    nt = 1024
    return pl.pallas_call(
        _mod_kernel,
        grid=(DEPTH, 6 * D // nt),
        in_specs=[
            pl.BlockSpec((MOD_ROWS, D), lambda l, j: (0, 0)),
            pl.BlockSpec((None, D, nt), lambda l, j: (l, 0, j)),
            pl.BlockSpec((None, 1, nt), lambda l, j: (l, 0, j)),
        ],
        out_specs=pl.BlockSpec((None, MOD_ROWS, nt), lambda l, j: (l, 0, j)),
        out_shape=jax.ShapeDtypeStruct((DEPTH, MOD_ROWS, 6 * D), F32),
        compiler_params=_cparams(("arbitrary", "arbitrary")),
        name="mod_table",
    )(cvec, w_mod, b_mod.reshape(DEPTH, 1, 6 * D))


def _x_pair(x):
    if isinstance(x, tuple):
        return x[0], x[1], 0
    return x, x, T_CTX


def _x_specs(x, tile):
    _, _, lat_off = _x_pair(x)
    ctx = T_CTX // tile
    return [pl.BlockSpec((tile, D), lambda i: (jnp.minimum(i, ctx - 1), 0)),
            pl.BlockSpec((tile, D), lambda i: (jnp.maximum(i - ctx, 0) + lat_off // tile, 0))]


def _x_tile(xc_ref, xl_ref, tile):
    return jnp.where(pl.program_id(0) < T_CTX // tile, xc_ref[...], xl_ref[...])


def _inproj_kernel(layer, xc_ref, xl_ref, mod_ref, n1_ref, w_ref, aq_ref, ak_ref, bq_ref, wq_ref, bkv_ref, wkv_ref,
                   lbl_ref, ones_ref, taba_ref, tabb_ref, tabk_ref,
                   qa_o, kan_o, ka_o, va_o, qb_o, ckv_o, kvb_o, kr_o, kre_o,
                   lff_o, lfb_o, kff_o, kfb_o, qc_o, vc_o, sg_o):
    x = _x_tile(xc_ref, xl_ref, TM)
    mod = mod_ref[...]
    xn = x * lax.rsqrt(jnp.mean(x * x, axis=-1, keepdims=True) + EPS) * n1_ref[...]
    h = (xn * (1.0 + mod[:, D:2 * D]) + mod[:, 0:D]).astype(BF16)
    y_all = jnp.dot(h, w_ref[...], preferred_element_type=F32)

    def y(c0, c1):
        return y_all[:, c0:c1]

    ones = ones_ref[...]

    qa = y(C_QA, C_KA)
    qa = qa * lax.rsqrt(_group_mean(qa * qa, ones, HD_A) + EPS) * aq_ref[...]
    qa_o[...] = (_rope(qa, taba_ref, 16, LANE) * (HD_A ** -0.5)).astype(BF16)
    ka = y(C_KA, C_VA)
    ka = ka * lax.rsqrt(_group_mean(ka * ka, ones[:LANE, :LANE], HD_A) + EPS) * ak_ref[...]
    kan_o[...] = ka
    ka_o[...] = _rope(ka, taba_ref, 16, LANE).astype(BF16)
    va_o[...] = y(C_VA, C_QRA)

    qr = y(C_QRA, C_KVA)
    qr = qr * lax.rsqrt(jnp.mean(qr * qr, axis=-1, keepdims=True) + EPS) * bq_ref[...]
    qb = jnp.dot(qr.astype(BF16), wq_ref[...], preferred_element_type=F32)
    qb_o[...] = (_rope(qb, tabb_ref, 8, 2 * LANE) * ((NOPE_B + ROPE_B) ** -0.5)).astype(BF16)
    kv = y(C_KVA, C_KR)
    ckv = kv * lax.rsqrt(jnp.mean(kv * kv, axis=-1, keepdims=True) + EPS) * bkv_ref[...]
    ckv_o[...] = ckv
    kvb_o[...] = jnp.dot(ckv.astype(BF16), wkv_ref[...], preferred_element_type=F32).astype(BF16)
    kr = y(C_KR, C_FF)
    kr_o[...] = kr
    kre_o[...] = _rope(kr, tabk_ref, 8, LANE).astype(BF16)

    lbl = lbl_ref[...]
    e = jnp.exp(lbl - jnp.max(lbl, axis=0, keepdims=True))
    p = e / jnp.sum(e, axis=0, keepdims=True)
    lb = p[0] * 0.0
    for i in range(1, layer + 1):
        lb = lb + p[i]
    for d, (c0, lf_o, kf_o) in enumerate(((C_FF, lff_o, kff_o), (C_FB, lfb_o, kfb_o))):
        pre = y(c0, c0 + 512)
        lbd = lb[d:d + 1, :]
        f = jnp.maximum(lbd + (1.0 - lbd) * jax.nn.sigmoid(pre), F_FLOOR)
        lf_o[...] = jnp.log(f)
        kf_o[...] = 1.0 - f
    qc_o[...] = y(C_QC, C_IC).astype(BF16)
    vc_o[...] = y(C_IC, C_GC).astype(BF16)
    gc = y(C_GC, C_END)
    sg_o[...] = (gc * jax.nn.sigmoid(gc)).astype(BF16)


def _const_spec(shape):
    nd = len(shape)
    return pl.BlockSpec(shape, lambda i: (0,) * nd)


def _tile_spec(width):
    return pl.BlockSpec((TM, width), lambda i: (i, 0))


def _layer_spec(tail, layer):
    return pl.BlockSpec((None,) + tuple(tail), lambda *_: (layer,) + (0,) * len(tail))


def _mod_spec(layer, row_of_tile):
    return pl.BlockSpec((None, 1, 6 * D), lambda i, *_: (layer * MOD_ROWS + row_of_tile(i), 0, 0))


def _inproj(layer, x, mod, n1, w_in_p, aq, ak, bq, wq_p, bkv, wkv_p, lbl, ones, taba, tabb, tabk):
    outs = [
        (512, BF16), (128, F32), (128, BF16), (128, F32), (1024, BF16), (256, F32), (1024, BF16),
        (128, F32), (128, BF16), (512, F32), (512, F32), (512, F32), (512, F32), (512, BF16), (512, BF16),
        (512, BF16),
    ]
    tab_spec = lambda w: pl.BlockSpec((3, TM, w), lambda i: (0, _pos_block(i), 0))
    return pl.pallas_call(
        functools.partial(_inproj_kernel, layer),
        grid=(N_TILES,),
        in_specs=_x_specs(x, TM) + [
            _mod_spec(layer, _mod_row),
            _layer_spec((1, D), layer),
            _layer_spec((D, C_END), layer),
            _layer_spec((1, 512), layer), _layer_spec((1, 128), layer), _layer_spec((1, Q_RANK), layer),
            _layer_spec((Q_RANK, 1024), layer), _layer_spec((1, KV_RANK), layer), _layer_spec((KV_RANK, 1024), layer),
            _const_spec((DEPTH, 2, 512)), _const_spec((256, 256)),
            tab_spec(LANE), tab_spec(2 * LANE), tab_spec(LANE),
        ],
        out_specs=[_tile_spec(w) for w, _ in outs],
        out_shape=[jax.ShapeDtypeStruct((T_ALL, w), dt) for w, dt in outs],
        compiler_params=_cparams(("parallel",)),
        name="inproj",
    )(*_x_pair(x)[:2], mod, n1, w_in_p, aq, ak, bq, wq_p, bkv, wkv_p, lbl, ones, taba, tabb, tabk)


def _pack_w_in(w):
    w = w.astype(BF16)
    z = jnp.zeros((DEPTH, D, 32), BF16)
    kr = w[..., R_KR:R_FF]
    main = jnp.concatenate([w[..., :R_KR], kr, z, kr, z, w[..., R_FF:R_GATE]], axis=-1)
    return main, w[..., R_GATE:]


def _pack_wq(wq):
    w = wq.reshape(DEPTH, Q_RANK, H_B, NOPE_B + ROPE_B)
    nope, rope = w[..., :NOPE_B], w[..., NOPE_B:]
    z = jnp.zeros((DEPTH, Q_RANK, H_B, 32), wq.dtype)
    even = jnp.concatenate([rope, z, nope], axis=-1)
    odd = jnp.concatenate([nope, rope, z], axis=-1)
    is_even = (jnp.arange(H_B) % 2 == 0)[None, None, :, None]
    return jnp.where(is_even, even, odd).reshape(DEPTH, Q_RANK, H_B * LANE).astype(BF16)


def _pack_wkv(wkv):
    w = wkv.reshape(DEPTH, KV_RANK, H_B, NOPE_B + V_B)
    nope, v = w[..., :NOPE_B], w[..., NOPE_B:]
    is_even = (jnp.arange(H_B) % 2 == 0)[None, None, :, None]
    return jnp.where(is_even, jnp.concatenate([v, nope], -1), jnp.concatenate([nope, v], -1)).reshape(
        DEPTH, KV_RANK, H_B * LANE).astype(BF16)


def _rope_tables():
    pos = np.arange(DEC_SEQ)
    row, col = pos // GRID_W, pos % GRID_W

    def pattern(half):
        quarter = half // 2
        inv = THETA ** (-np.arange(0, half, 2, dtype=np.float64) / half)
        ang = np.concatenate([row[:, None] * inv, row[:, None] * inv, col[:, None] * inv, col[:, None] * inv], 1)
        is_x2 = np.tile(np.concatenate([np.zeros(quarter), np.ones(quarter)]), 2)[None, :]
        c = np.cos(ang)
        s1 = np.sin(ang) * is_x2
        s2 = -np.sin(ang) * (1 - is_x2)
        return c, s1, s2

    def assemble(width, spans, half):
        c, s1, s2 = pattern(half)
        tc = np.ones((DEC_SEQ + TM, width))
        t1 = np.zeros((DEC_SEQ + TM, width))
        t2 = np.zeros((DEC_SEQ + TM, width))
        for start in spans:
            tc[:DEC_SEQ, start:start + 2 * half] = c
            t1[:DEC_SEQ, start:start + 2 * half] = s1
            t2[:DEC_SEQ, start:start + 2 * half] = s2
        return jnp.asarray(np.stack([tc, t1, t2]), F32)

    taba = assemble(LANE, (0, 64), 32)
    tabb = assemble(2 * LANE, (0, 128 + 64), 16)
    tabk = assemble(LANE, (0, 64), 16)
    return taba, tabb, tabk


def _ones_block(n, width):
    g = np.arange(n) // width
    return jnp.asarray(g[:, None] == g[None, :], BF16)


def _ctxkv_kernel(c_ref, w_ref, o_ref):
    o_ref[...] = jnp.dot(c_ref[...].astype(BF16), w_ref[...], preferred_element_type=F32).astype(BF16)


def _ctx_kv(ckv_cache, wkv_p):
    rows = ckv_cache.shape[0]
    return pl.pallas_call(
        _ctxkv_kernel,
        grid=(rows // PAST,),
        in_specs=[pl.BlockSpec((PAST, KV_RANK), lambda i: (i, 0)),
                  pl.BlockSpec((None, KV_RANK, 1024), lambda i: (i % DEPTH, 0, 0))],
        out_specs=pl.BlockSpec((PAST, 1024), lambda i: (i, 0)),
        out_shape=jax.ShapeDtypeStruct((rows, 1024), BF16),
        compiler_params=_cparams(("parallel",)),
        name="ctx_kv",
    )(ckv_cache, wkv_p)


_NT = (((1,), (1,)), ((), ()))


def _den_lane(parity):
    return 64 if parity == 0 else 0


def _softmax_pv(s, v, parity, mxu_den):
    m = s.max(axis=-1, keepdims=True)
    if not mxu_den:
        p = jnp.exp(s - m)
        return jnp.dot(p.astype(BF16), v, preferred_element_type=F32) / p.sum(axis=-1, keepdims=True)
    lane = lax.broadcasted_iota(jnp.int32, (1, LANE), 1)
    keep = (lane < 64) if parity == 0 else (lane >= 64)
    o = jnp.dot(jnp.exp((s - m).astype(BF16)), v, preferred_element_type=F32)
    return jnp.where(keep, o / o[:, _den_lane(parity):_den_lane(parity) + 1], 0.0)


def _attn_kernel(n_pieces, qa_ref, qb_ref, *refs):
    kv_refs = refs[:4 * n_pieces]
    oa_ref, ob_ref = refs[4 * n_pieces:4 * n_pieces + 2]
    mxu_den = n_pieces == 2
    lane = lax.broadcasted_iota(jnp.int32, (1, LANE), 1)
    lo = lane < 64
    hi = jnp.logical_not(lo)

    def rows(parts):
        return parts[0] if len(parts) == 1 else jnp.concatenate(parts, axis=0)

    def with_den(v, parity):
        return jnp.where(lane == _den_lane(parity), jnp.ones_like(v), v) if mxu_den else v

    def slabs_a():
        ka = [kv_refs[4 * i][...].astype(F32) for i in range(n_pieces)]
        va = [kv_refs[4 * i + 1][...].astype(F32) for i in range(n_pieces)]

        def place(x, g, parity):
            if g != parity:
                x = pltpu.roll(x, 64, 1)
            return jnp.where(lo if parity == 0 else hi, x, 0.0).astype(BF16)

        return {2 * g + parity: (rows([place(k, g, parity) for k in ka]),
                                 with_den(rows([place(v, g, parity) for v in va]), parity))
                for g in range(KV_A) for parity in range(2)}

    def slabs_b():
        out = {}
        kre = rows([kv_refs[4 * i + 3][...].astype(BF16) for i in range(n_pieces)])
        for h in range(H_B):
            parity = h % 2
            nope = hi if parity == 0 else lo
            kvb = rows([kv_refs[4 * i + 2][:, h * LANE:(h + 1) * LANE] for i in range(n_pieces)])
            out[h] = (jnp.where(nope, kvb, kre), with_den(jnp.where(nope, jnp.zeros_like(kvb), kvb), parity))
        return out

    sa, sb = slabs_a(), slabs_b()
    get_a, get_b = sa.__getitem__, sb.__getitem__

    for pair in range(H_A // 2):
        g = (2 * pair) // (H_A // KV_A)
        q = qa_ref[:, pair * LANE:(pair + 1) * LANE]
        acc = None
        for parity in range(2):
            k, v = get_a(2 * g + parity)
            o = _softmax_pv(lax.dot_general(q, k, _NT, preferred_element_type=F32), v, parity, mxu_den)
            acc = o if acc is None else acc + o
        oa_ref[:, pair * LANE:(pair + 1) * LANE] = acc.astype(BF16)

    for pair in range(H_B // 2):
        acc = None
        for parity in range(2):
            h = 2 * pair + parity
            k, v = get_b(h)
            q = qb_ref[:, h * LANE:(h + 1) * LANE]
            o = _softmax_pv(lax.dot_general(q, k, _NT, preferred_element_type=F32), v, parity, mxu_den)
            acc = o if acc is None else acc + o
        ob_ref[:, pair * LANE:(pair + 1) * LANE] = acc.astype(BF16)


def _attention(qa, qb, ka, va, kvb, kre, prev=None, cache=None, layer=0):
    if cache is None:
        nb, nqt, nk, q_blk0, k_blk0 = BATCH, 1, SEQ, 0, 0
    else:
        nb, nqt, nk, q_blk0, k_blk0 = DEC_BATCH, DEC_SEQ // TM, DEC_SEQ, CTX_TILES, T_CTX // DEC_SEQ
    qspec = lambda w: pl.BlockSpec((TM, w), lambda b, j: (q_blk0 + b * nqt + j, 0))
    kspec = lambda w: pl.BlockSpec((nk, w), lambda b, j: (k_blk0 + b, 0))
    in_specs = [qspec(512), qspec(1024), kspec(128), kspec(128), kspec(1024), kspec(128)]
    args = [qa, qb, ka, va, kvb, kre]
    n_pieces = 1
    aliases = {}
    if cache is not None:
        cspec4 = pl.BlockSpec((None, None, PAST, 128), lambda b, j: (b, layer, 0, 0))
        cspec = lambda w: pl.BlockSpec((PAST, w), lambda b, j: (b * DEPTH + layer, 0))
        in_specs += [cspec4, cspec4, cspec(1024), cspec(128)]
        args += list(cache)
        n_pieces = 2
        in_specs += [pl.BlockSpec(memory_space=pl.ANY)] * 2
        args += list(prev)
        aliases = {len(args) - 2: 0, len(args) - 1: 1}

    def body(*refs):
        if cache is not None:
            n_in = 2 + 4 * n_pieces
            refs = refs[:n_in] + refs[n_in + 2:]
        _attn_kernel(n_pieces, *refs)

    return pl.pallas_call(
        body,
        grid=(nb, nqt),
        in_specs=in_specs,
        out_specs=[qspec(512), qspec(512)],
        out_shape=[jax.ShapeDtypeStruct((T_ALL, 512), BF16)] * 2,
        input_output_aliases=aliases,
        compiler_params=_cparams(("parallel", "arbitrary")),
        name="attention_ctx" if cache is None else "attention_lat",
    )(*args)


HL = 256
HS = 128
HG = 64
N_PAIRS = H_C // 2
FAST_DECAY_LIMIT = 80.0


def _hgrn_bottom_exact(q, k, c, lo, rev):
    row = lax.broadcasted_iota(jnp.int32, (HS, LANE), 0)
    srow = lax.broadcasted_iota(jnp.int32, (HS, HS), 0)
    scol = lax.broadcasted_iota(jnp.int32, (HS, HS), 1)
    out = []
    for parity in range(2):
        def dup(x):
            xs = pltpu.roll(x, 64, 1)
            return jnp.where(lo, x, xs) if parity == 0 else jnp.where(lo, xs, x)
        qd, kd, bd = dup(q), dup(k), dup(c)

        dg = row & 3
        if rev:
            dg = 3 - dg
        e = [None]
        for delta in range(1, 4):
            shifted = pltpu.roll(bd, delta if rev else HS - delta, 0)
            e.append(jnp.exp(jnp.minimum(shifted - bd, 0.0)))
        qp, kp = [], []
        for c1, c2 in ((0, 1), (2, 3)):
            cv = jnp.where(lo, c1, c2)
            dl = cv - dg
            fac = jnp.where(dl == 0, 1.0, jnp.where(dl == 1, e[1], jnp.where(dl == 2, e[2],
                            jnp.where(dl == 3, e[3], 0.0))))
            kp.append((kd * fac).astype(BF16))
            qp.append(jnp.where(dg == cv, qd, 0.0).astype(BF16))
        s = lax.dot_general(jnp.concatenate(qp, axis=1), jnp.concatenate(kp, axis=1), _NT,
                            preferred_element_type=F32)
        tot = jnp.where((srow >> 2) == (scol >> 2), s, 0.0)

        for lev in range(1, 3):
            g = 4 ** lev
            par = 4 * g
            shape3 = (HS // par, par, LANE)
            rid = lax.broadcasted_iota(jnp.int32, shape3, 1)
            dg3 = rid >> (2 * lev)
            if rev:
                dg3 = 3 - dg3
            b3, q3, k3 = bd.reshape(shape3), qd.reshape(shape3), kd.reshape(shape3)
            lo3 = lo.reshape(1, 1, LANE)
            qp, kp = [], []
            for c1, c2 in ((1, 2), (3, None)):
                idx = lambda cc: (4 - cc) * g if rev else cc * g - 1
                i1 = idx(c1)
                i2 = idx(c2) if c2 is not None else i1
                ridx = jnp.where(lo3, i1, i2)
                ref = jnp.sum(jnp.where(rid == ridx, b3, 0.0), axis=1, keepdims=True)
                cvk = jnp.where(lo3, c1, c2 if c2 is not None else 0)
                cvq = jnp.where(lo3, c1, c2 if c2 is not None else -1)
                kk = jnp.where(dg3 < cvk, k3 * jnp.exp(jnp.minimum(ref - b3, 0.0)), 0.0)
                qq = jnp.where(dg3 == cvq, q3 * jnp.exp(jnp.minimum(b3 - ref, 0.0)), 0.0)
                kp.append(kk.reshape(HS, LANE).astype(BF16))
                qp.append(qq.reshape(HS, LANE).astype(BF16))
            s = lax.dot_general(jnp.concatenate(qp, axis=1), jnp.concatenate(kp, axis=1), _NT,
                                preferred_element_type=F32)
            sh = 2 * lev + 2
            tot = tot + jnp.where((srow >> sh) == (scol >> sh), s, 0.0)
        out.append(tot)
    return out


def _hgrn_head(q, k, lf, v, st_ref, rev):
    row = lax.broadcasted_iota(jnp.int32, (HS, LANE), 0)
    lane = lax.broadcasted_iota(jnp.int32, (1, LANE), 1)
    lo = lane < 64
    hi = jnp.logical_not(lo)
    in_g1 = row >= HG

    grow = row & (HG - 1)
    c = lf
    d = 1
    while d < HG:
        if rev:
            c = c + jnp.where(grow < HG - d, pltpu.roll(c, HS - d, 0), 0.0)
        else:
            c = c + jnp.where(grow >= d, pltpu.roll(c, d, 0), 0.0)
        d *= 2
    if rev:
        t0, t1 = c[0:1, :], c[HG:HG + 1, :]
    else:
        t0, t1 = c[HG - 1:HG, :], c[HS - 1:HS, :]
    et0, et1 = jnp.exp(t0), jnp.exp(t1)
    qe = q * jnp.exp(c)
    e_out = jnp.exp(jnp.where(in_g1, t1, t0) - c)
    ke = k * e_out

    if rev:
        qb = qe * jnp.where(in_g1, 1.0, et1)
        kh = ke * jnp.where(in_g1, et0, 1.0)
    else:
        qb = qe * jnp.where(in_g1, et0, 1.0)
        kh = ke * jnp.where(in_g1, 1.0, et1)
    st = st_ref[...]
    o_int = lax.dot_general(qb.astype(BF16), st.astype(BF16), _NT, preferred_element_type=F32)
    upd = lax.dot_general(v, kh.astype(BF16), (((0,), (0,)), ((), ())), preferred_element_type=F32)
    r128 = lax.broadcasted_iota(jnp.int32, (LANE, LANE), 0)
    c128 = lax.broadcasted_iota(jnp.int32, (LANE, LANE), 1)
    st_ref[...] = st * (et0 * et1) + jnp.where((r128 >> 6) == (c128 >> 6), upd, 0.0)

    q_late = in_g1 if not rev else jnp.logical_not(in_g1)
    q_top = jnp.where(q_late, qe, 0.0)
    k_top = jnp.where(q_late, 0.0, ke).astype(BF16)
    top = [lax.dot_general(jnp.where(m, q_top, 0.0).astype(BF16), k_top, _NT, preferred_element_type=F32)
           for m in (lo, hi)]
    mid = HG // 2 if rev else HG // 2 - 1
    cm = c - jnp.where(in_g1, c[HG + mid:HG + mid + 1, :], c[mid:mid + 1, :])
    return c, top, o_int, cm, jnp.max(jnp.abs(cm))


def _hgrn_steps(jobs, bot_ref):
    lane = lax.broadcasted_iota(jnp.int32, (1, LANE), 1)
    lo = lane < 64
    hi = jnp.logical_not(lo)
    heads = [_hgrn_head(*job) for job in jobs]
    fast = functools.reduce(jnp.maximum, [h[4] for h in heads]) <= FAST_DECAY_LIMIT

    @pl.when(fast)
    def _():
        srow = lax.broadcasted_iota(jnp.int32, (HS, HS), 0)
        scol = lax.broadcasted_iota(jnp.int32, (HS, HS), 1)
        same = (srow >> 6) == (scol >> 6)
        for ji, (job, (_, _, _, cm, _)) in enumerate(zip(jobs, heads)):
            keep = same & ((scol >= srow) if job[5] else (scol <= srow))
            qf = job[0] * jnp.exp(cm)
            kf = (job[1] * jnp.exp(-cm)).astype(BF16)
            for parity, m in enumerate((lo, hi)):
                s = lax.dot_general(jnp.where(m, qf, 0.0).astype(BF16), kf, _NT, preferred_element_type=F32)
                bot_ref[ji, parity] = jnp.where(keep, s, 0.0)

    @pl.when(jnp.logical_not(fast))
    def _():
        for ji, (job, (c, _, _, _, _)) in enumerate(zip(jobs, heads)):
            for parity, s in enumerate(_hgrn_bottom_exact(job[0], job[1], c, lo, job[5])):
                bot_ref[ji, parity] = s

    outs = []
    for ji, (job, (_, top, o_int, _, _)) in enumerate(zip(jobs, heads)):
        v = job[3]
        probs = jnp.concatenate([(bot_ref[ji, 0] + top[0]).astype(BF16), (bot_ref[ji, 1] + top[1]).astype(BF16)],
                                axis=1)
        vv = jnp.concatenate([jnp.where(lo, v, jnp.zeros_like(v)), jnp.where(hi, v, jnp.zeros_like(v))], axis=0)
        outs.append(jnp.dot(probs, vv, preferred_element_type=F32) + o_int)
    return outs


PAIRS_PER_ITER = 2


def _hgrn_kernel(has_s0, nt, *refs):
    if has_s0:
        (qf_ref, vf_ref, kf_ref, lf_ref, qb_ref, vb_ref, kb_ref, lb_ref, s0_ref,
         of_ref, ob_ref, so_ref, st_scr, bot_scr) = refs
    else:
        (qf_ref, vf_ref, kf_ref, lf_ref, qb_ref, vb_ref, kb_ref, lb_ref,
         of_ref, ob_ref, so_ref, st_scr, bot_scr) = refs
    j = pl.program_id(1)

    @pl.when(j == 0)
    def _():
        if has_s0:
            st_scr[...] = s0_ref[...]
        else:
            st_scr[...] = jnp.zeros_like(st_scr)

    n_sub = HL // HS

    def pair_body(it, carry):
        for step in range(n_sub):
            jobs, dests = [], []
            for pp in range(PAIRS_PER_ITER):
                p = it * PAIRS_PER_ITER + pp
                cols = pl.ds(pl.multiple_of(p * LANE, LANE), LANE)
                rf = pl.ds(step * HS, HS)
                rb = pl.ds((n_sub - 1 - step) * HS, HS)
                jobs.append((qf_ref[rf, cols].astype(F32), kf_ref[rf, cols], lf_ref[rf, cols], vf_ref[rf, cols],
                             st_scr.at[p, 0], False))
                dests.append((of_ref, rf, cols))
                jobs.append((qb_ref[rb, cols].astype(F32), kb_ref[rb, cols], lb_ref[rb, cols], vb_ref[rb, cols],
                             st_scr.at[p, 1], True))
                dests.append((ob_ref, rb, cols))
            for (ref, rows, cols), o in zip(dests, _hgrn_steps(jobs, bot_scr)):
                ref[rows, cols] = o
        return carry

    lax.fori_loop(0, N_PAIRS // PAIRS_PER_ITER, pair_body, 0)

    @pl.when(j == nt - 1)
    def _():
        so_ref[...] = st_scr[...]


def _hgrn(qc, vc, kff, lff, kfb, lfb, s0=None, prev=None, layer=0):
    if s0 is None:
        nb, nt, blk0 = BATCH, SEQ // HL, 0
    else:
        nb, nt, blk0 = DEC_BATCH, DEC_SEQ // HL, T_CTX // HL
    fspec = pl.BlockSpec((HL, 512), lambda b, j: (blk0 + b * nt + j, 0))
    bspec = pl.BlockSpec((HL, 512), lambda b, j: (blk0 + b * nt + nt - 1 - j, 0))
    sspec = pl.BlockSpec((None, N_PAIRS, 2, LANE, LANE), lambda b, j: (b, 0, 0, 0, 0))
    in_specs = [fspec] * 4 + [bspec] * 4
    args = [qc, vc, kff, lff, qc, vc, kfb, lfb]
    aliases = {}
    if s0 is not None:
        in_specs += [pl.BlockSpec((None, None, N_PAIRS, 2, LANE, LANE), lambda b, j: (b, layer, 0, 0, 0, 0))]
        args += [s0]
        in_specs += [pl.BlockSpec(memory_space=pl.ANY)] * 2
        args += list(prev)
        aliases = {len(args) - 2: 0, len(args) - 1: 1}

    def body(*refs):
        if s0 is not None:
            refs = refs[:9] + refs[11:]
        _hgrn_kernel(s0 is not None, nt, *refs)

    return pl.pallas_call(
        body,
        grid=(nb, nt),
        in_specs=in_specs,
        out_specs=[fspec, bspec, sspec],
        out_shape=[jax.ShapeDtypeStruct((T_ALL, 512), F32), jax.ShapeDtypeStruct((T_ALL, 512), F32),
                   jax.ShapeDtypeStruct((nb, N_PAIRS, 2, LANE, LANE), F32)],
        scratch_shapes=[pltpu.VMEM((N_PAIRS, 2, LANE, LANE), F32),
                        pltpu.VMEM((2 * PAIRS_PER_ITER, 2, HS, HS), F32)],
        input_output_aliases=aliases,
        compiler_params=_cparams(("parallel", "arbitrary")),
        name="hgrn_ctx" if s0 is None else "hgrn_lat",
    )(*args)


def _merge_kernel(xc_ref, xl_ref, mod_ref, n1_ref, wg_ref, oa_ref, ob_ref, ocf_ref, ocb_ref, sg_ref, con_ref,
                  ones_ref, wbr_ref, wout_ref, n2_ref, wrh_ref, wrl_ref, br_ref, tri_ref, upper_ref,
                  x1_o, h2_o, comb_o, rank_o, carry_o, tab_o, carry_scr):
    x = _x_tile(xc_ref, xl_ref, TMG)
    mod = mod_ref[...]
    xn = x * lax.rsqrt(jnp.mean(x * x, axis=-1, keepdims=True) + EPS) * n1_ref[...]
    h = (xn * (1.0 + mod[:, D:2 * D]) + mod[:, 0:D]).astype(BF16)

    oc = ocf_ref[...] + ocb_ref[...]
    oc = oc * lax.rsqrt(_group_mean(oc * oc, ones_ref[...], DV_C) + EPS) * con_ref[...]
    oc = (oc * sg_ref[...].astype(F32)).astype(BF16)
    branches = (oa_ref[...], ob_ref[...], oc)
    mix = None
    for jb in range(3):
        gate = jax.nn.sigmoid(jnp.dot(h, wg_ref[:, jb * D:(jb + 1) * D], preferred_element_type=F32))
        t = gate * jnp.dot(branches[jb], wbr_ref[jb], preferred_element_type=F32)
        mix = t if mix is None else mix + t
    out = jnp.dot(mix.astype(BF16), wout_ref[...], preferred_element_type=F32)
    x1 = x + mod[:, 2 * D:3 * D] * out
    x1_o[...] = x1

    x1n = x1 * lax.rsqrt(jnp.mean(x1 * x1, axis=-1, keepdims=True) + EPS) * n2_ref[...]
    h2 = x1n * (1.0 + mod[:, 4 * D:5 * D]) + mod[:, 3 * D:4 * D]
    h2_o[...] = h2.astype(BF16)

    h2h, h2l = _split_hi_lo(h2)
    logits = (jnp.dot(h2h, wrh_ref[...], preferred_element_type=F32)
              + jnp.dot(h2l, wrh_ref[...], preferred_element_type=F32)
              + jnp.dot(h2h, wrl_ref[...], preferred_element_type=F32)) + br_ref[...]
    lane = lax.broadcasted_iota(jnp.int32, logits.shape, 1)
    neg = -jnp.inf
    is_g = lane < N_GROUPS
    gl = jnp.where(is_g, logits, neg)
    gmax = gl.max(axis=-1, keepdims=True)
    gidx = jnp.min(jnp.where(gl == gmax, lane, LANE), axis=-1, keepdims=True)
    gp = 1.0 / jnp.sum(jnp.where(is_g, jnp.exp(gl - gmax), 0.0), axis=-1, keepdims=True)
    eid = lane - N_GROUPS
    in_grp = (eid >= 0) & (eid < N_EXPERTS) & ((eid >> 2) == gidx)
    el = jnp.where(in_grp, logits, neg)
    v1 = el.max(axis=-1, keepdims=True)
    i1 = jnp.min(jnp.where(el == v1, lane, LANE), axis=-1, keepdims=True)
    el2 = jnp.where(lane == i1, neg, el)
    v2 = el2.max(axis=-1, keepdims=True)
    i2 = jnp.min(jnp.where(el2 == v2, lane, LANE), axis=-1, keepdims=True)
    e2 = jnp.exp(v2 - v1)
    w1 = gp / (1.0 + e2)
    w2 = gp * e2 / (1.0 + e2)
    comb = jnp.where(lane == i1, w1, 0.0) + jnp.where(lane == i2, w2, 0.0)
    comb = pltpu.roll(comb, LANE - N_GROUPS, 1)
    comb_o[...] = comb

    step = pl.program_id(0) % MOE_STEPS

    @pl.when(step == 0)
    def _():
        carry_scr[...] = jnp.zeros_like(carry_scr)

    for sb in range(TMG // SBK):
        sub = comb[sb * SBK:(sb + 1) * SBK, :]
        routed = sub > 0.0
        ind = jnp.where(routed, 1.0, 0.0)
        carry = carry_scr[...]
        rank = jnp.dot(tri_ref[...], ind.astype(BF16), preferred_element_type=F32) + carry
        rank_o[sb * SBK:(sb + 1) * SBK, :] = jnp.where(routed, rank, -1.0)
        carry_o[sb] = carry
        carry_scr[...] = carry + jnp.sum(ind, axis=0, keepdims=True)

    @pl.when(step == MOE_STEPS - 1)
    def _():
        count = carry_scr[...]
        seg = jnp.floor((count + (SEG_ALIGN - 1.0)) * (1.0 / SEG_ALIGN)) * SEG_ALIGN
        off = jnp.dot(jnp.broadcast_to(seg, (8, LANE)), upper_ref[...], preferred_element_type=F32,
                      precision=lax.Precision.HIGHEST)
        tab_o[0:1, :] = count
        tab_o[1:2, :] = off[0:1, :]


TMG = 512


def _merge(layer, x, mod, n1, wgate, oa, ob, ocf, ocb, sg, con, ones, wbr, wout, n2, wr_hi, wr_lo, br):
    ctx_tiles, per_seq = T_CTX // TMG, DEC_SEQ // TMG
    mrow = lambda i: jnp.where(i < ctx_tiles, CTX_MOD_ROW, (i - ctx_tiles) // per_seq)
    tspec = lambda w: pl.BlockSpec((TMG, w), lambda i: (i, 0))
    wspec = lambda tail: pl.BlockSpec((None,) + tail, lambda i: (layer,) + (0,) * len(tail),
                                      pipeline_mode=pl.Buffered(1))
    return pl.pallas_call(
        _merge_kernel,
        grid=(T_ALL // TMG,),
        in_specs=_x_specs(x, TMG) + [
            _mod_spec(layer, mrow),
            _layer_spec((1, D), layer), wspec((D, 3 * D)),
            tspec(512), tspec(512), tspec(512), tspec(512), tspec(512),
            _layer_spec((1, 512), layer), _const_spec((256, 256)),
            wspec((3, BRANCH_W, D)), wspec((D, D)), _layer_spec((1, D), layer),
            _layer_spec((D, LANE), layer), _layer_spec((D, LANE), layer), _layer_spec((1, LANE), layer),
            _const_spec((SBK, SBK)), _const_spec((LANE, LANE)),
        ],
        out_specs=[
            tspec(D), tspec(D), tspec(LANE), tspec(LANE),
            pl.BlockSpec((None, TMG // SBK, 1, LANE), lambda i: (i // MOE_STEPS, i % MOE_STEPS, 0, 0)),
            pl.BlockSpec((None, 2, LANE), lambda i: (i // MOE_STEPS, 0, 0)),
        ],
        out_shape=[jax.ShapeDtypeStruct((T_ALL, D), F32), jax.ShapeDtypeStruct((T_ALL, D), BF16),
                   jax.ShapeDtypeStruct((T_ALL, LANE), F32), jax.ShapeDtypeStruct((T_ALL, LANE), F32),
                   jax.ShapeDtypeStruct((N_BLK, N_SB, 1, LANE), F32), jax.ShapeDtypeStruct((N_BLK, 2, LANE), F32)],
        scratch_shapes=[pltpu.VMEM((1, LANE), F32)],
        compiler_params=_cparams(("arbitrary",)),
        name="merge",
    )(*_x_pair(x)[:2], mod, n1, wgate, oa, ob, ocf, ocb, sg, con, ones, wbr, wout, n2, wr_hi, wr_lo, br,
      jnp.asarray(np.tril(np.ones((SBK, SBK)), -1), BF16), jnp.asarray(np.triu(np.ones((LANE, LANE)), 1), F32))


NB = 2048
N_BLK = T_ALL // NB
SBK = 256
N_SB = NB // SBK
WIN_SHIFT, FT_SHIFT, SEG_SHIFT = 6, 7, 4
WIN = 1 << WIN_SHIFT
FT = 1 << FT_SHIFT
SEG_ALIGN = 1 << SEG_SHIFT
STG = 2 * NB + N_EXPERTS * SEG_ALIGN + 256
QUAD = 4
E_STEP = 2
MOE_STEPS = NB // TMG


def _moe_kernel(cnt_s, off_s, car_s, h2_ref, rank_ref, comb_ref, offv_ref, egu_ref, ed_ref, o_ref,
                stg_ref, acc_ref):
    blk = pl.program_id(0)
    step = pl.program_id(1)
    srow = lax.broadcasted_iota(jnp.int32, (WIN, SBK), 0).astype(F32)

    def windows(s):
        out = []
        for ex in range(N_EXPERTS):
            start = off_s[blk, ex] + car_s[blk, s, ex]
            length = car_s[blk, s + 1, ex] - car_s[blk, s, ex]
            ws = (start >> SEG_SHIFT) << SEG_SHIFT
            out.append((ws, (start - ws + length + (WIN - 1)) >> WIN_SHIFT))
        return out, functools.reduce(jnp.maximum, [w[1] for w in out])

    def positions(s):
        rows = pl.ds(pl.multiple_of(s * SBK, SBK), SBK)
        rank = rank_ref[rows, :]
        return rows, jnp.where(rank >= 0.0, rank + offv_ref[1:2, :], -1.0e6)

    @pl.when(step == 0)
    def _():
        stg_ref[...] = jnp.zeros_like(stg_ref)

        def sub_body(s, carry):
            rows, pos = positions(s)
            pos_t = pos.T
            h2 = h2_ref[rows, :]
            wins, nmax = windows(s)

            def chunk_body(c, carry2):
                for quad in range(N_EXPERTS // QUAD):
                    blocks = []
                    for ex in range(quad * QUAD, (quad + 1) * QUAD):
                        base = (wins[ex][0] + c * WIN).astype(F32)
                        hit = (pos_t[ex:ex + 1, :] - base) == srow
                        blocks.append(jnp.where(hit, 1.0, 0.0).astype(BF16))
                    moved = jnp.dot(jnp.concatenate(blocks, axis=0), h2, preferred_element_type=F32).astype(BF16)
                    for i in range(QUAD):
                        first = jnp.minimum(wins[quad * QUAD + i][0] + c * WIN, STG - WIN)
                        dst = pl.ds(pl.multiple_of(first, SEG_ALIGN), WIN)
                        stg_ref[dst, :] = stg_ref[dst, :] + moved[i * WIN:(i + 1) * WIN, :]
                return carry2

            lax.fori_loop(0, nmax, chunk_body, 0)
            return carry

        lax.fori_loop(0, N_SB, sub_body, 0)

    def ffn_tile(j, first, n_rows, n_valid):
        rows = pl.ds(pl.multiple_of(first, SEG_ALIGN), n_rows)
        xs = stg_ref[rows, :]
        hgu = jnp.dot(xs, egu_ref[j], preferred_element_type=F32)
        hg, hu = hgu[:, :D_EXPERT], hgu[:, D_EXPERT:]
        act = (hg * jax.nn.sigmoid(hg) * hu).astype(BF16)
        y = jnp.dot(act, ed_ref[j], preferred_element_type=F32).astype(BF16)
        rid = lax.broadcasted_iota(jnp.int32, (n_rows, 1), 0)
        stg_ref[rows, :] = jnp.where(rid < n_valid, y, xs)

    for j in range(E_STEP):
        ex = step * E_STEP + j
        count = cnt_s[blk, ex]
        seg0 = off_s[blk, ex]
        n_big = (count + (FT - 1)) >> (FT_SHIFT + 1)

        def ffn_body(t, carry, j=j, count=count, seg0=seg0):
            ffn_tile(j, seg0 + t * (2 * FT), 2 * FT, count - t * (2 * FT))
            return carry

        lax.fori_loop(0, n_big, ffn_body, 0)

        @pl.when(count > n_big * (2 * FT))
        def _(j=j, count=count, seg0=seg0, n_big=n_big):
            ffn_tile(j, seg0 + n_big * (2 * FT), FT, count - n_big * (2 * FT))

    @pl.when(step == N_EXPERTS // E_STEP - 1)
    def _():
        def sub_body(s, carry):
            rows, pos = positions(s)
            pos_t = pos.T
            wts_t = comb_ref[rows, :].T
            wins, nmax = windows(s)
            acc_ref[...] = jnp.zeros_like(acc_ref)

            def chunk_body(c, carry2):
                sel_t, srcs = [], []
                for ex in range(N_EXPERTS):
                    base = (wins[ex][0] + c * WIN).astype(F32)
                    hit = (pos_t[ex:ex + 1, :] - base) == srow
                    sel_t.append(jnp.where(hit, wts_t[ex:ex + 1, :], 0.0).astype(BF16))
                    first = jnp.minimum(wins[ex][0] + c * WIN, STG - WIN)
                    srcs.append(stg_ref[pl.ds(pl.multiple_of(first, SEG_ALIGN), WIN), :])
                acc_ref[...] += lax.dot_general(jnp.concatenate(sel_t, axis=0), jnp.concatenate(srcs, axis=0),
                                                (((0,), (0,)), ((), ())), preferred_element_type=F32)
                return carry2

            lax.fori_loop(0, nmax, chunk_body, 0)
            o_ref[rows, :] = acc_ref[...].astype(BF16)
            return carry

        lax.fori_loop(0, N_SB, sub_body, 0)


def _moe(layer, h2, comb, rank, carry, tab, egu, ed):
    cnt_i = tab[:, 0, :N_EXPERTS].astype(jnp.int32)
    off_i = tab[:, 1, :N_EXPERTS].astype(jnp.int32)
    car_i = jnp.concatenate([carry[:, :, 0, :N_EXPERTS].astype(jnp.int32), cnt_i[:, None, :]], axis=1)
    bspec = lambda w: pl.BlockSpec((NB, w), lambda b, e, *_: (b, 0))
    grid_spec = pltpu.PrefetchScalarGridSpec(
        num_scalar_prefetch=3,
        grid=(N_BLK, N_EXPERTS // E_STEP),
        in_specs=[
            bspec(D), bspec(LANE), bspec(LANE),
            pl.BlockSpec((None, 2, LANE), lambda b, e, *_: (b, 0, 0)),
            pl.BlockSpec((None, E_STEP, D, 2 * D_EXPERT), lambda b, e, *_: (layer, e, 0, 0)),
            pl.BlockSpec((None, E_STEP, D_EXPERT, D), lambda b, e, *_: (layer, e, 0, 0)),
        ],
        out_specs=bspec(D),
        scratch_shapes=[pltpu.VMEM((STG, D), BF16), pltpu.VMEM((SBK, D), F32)],
    )
    return pl.pallas_call(
        _moe_kernel,
        grid_spec=grid_spec,
        out_shape=jax.ShapeDtypeStruct((T_ALL, D), BF16),
        compiler_params=_cparams(("parallel", "arbitrary")),
        name="moe",
    )(cnt_i, off_i, car_i, h2, rank, comb, tab, egu, ed)


def _residual_kernel(final, x1_ref, moe_ref, mod_ref, fg_ref, o_ref):
    x2 = x1_ref[...] + mod_ref[:, 5 * D:6 * D] * moe_ref[...].astype(F32)
    if final:
        x2 = x2 * lax.rsqrt(jnp.mean(x2 * x2, axis=-1, keepdims=True) + EPS) * fg_ref[...]
    o_ref[...] = x2


TR = DEC_SEQ


def _residual(layer, final, x1, moe, mod, fg, tile0=0, n_tiles=T_ALL // TR):
    ctx_tiles = T_CTX // TR
    mrow = lambda i: jnp.where(tile0 + i < ctx_tiles, CTX_MOD_ROW, tile0 + i - ctx_tiles)
    src = lambda: pl.BlockSpec((TR, D), lambda i: (tile0 + i, 0))
    return pl.pallas_call(
        functools.partial(_residual_kernel, final),
        grid=(n_tiles,),
        in_specs=[src(), src(), _mod_spec(layer, mrow), _const_spec((1, D))],
        out_specs=pl.BlockSpec((TR, D), lambda i: (i, 0)),
        out_shape=jax.ShapeDtypeStruct((n_tiles * TR, D), F32),
        compiler_params=_cparams(("parallel",)),
        name="residual",
    )(x1, moe, mod, fg)


def _state_to_blockdiag(s):
    lead = s.shape[:-3]
    st = jnp.swapaxes(s, -1, -2).reshape(lead + (N_PAIRS, 2, DV_C, DK_C))
    z = jnp.zeros_like(st[..., 0, :, :])
    top = jnp.concatenate([st[..., 0, :, :], z], axis=-1)
    bot = jnp.concatenate([z, st[..., 1, :, :]], axis=-1)
    return jnp.concatenate([top, bot], axis=-2)


def _blockdiag_to_state(sb):
    lead = sb.shape[:-3]
    even = sb[..., :DV_C, :DK_C]
    odd = sb[..., DV_C:, DK_C:]
    st = jnp.stack([even, odd], axis=-3).reshape(lead + (H_C, DV_C, DK_C))
    return jnp.swapaxes(st, -1, -2)


def kernel(x_prompt, x_sample, cache_gqa_k, cache_gqa_v, cache_mla_ckv, cache_mla_krope, state_hgrn, c, c_ctx,
           w_mod, b_mod, norm1_g, norm2_g, w_in, a_qnorm, a_knorm, b_qnorm, b_wq, b_kvnorm, b_wkv, c_lb_logits,
           c_onorm, w_branch, w_out, r_group_w, r_group_b, r_expert_w, r_expert_b, e_gate, e_up, e_down, final_g):
    x = (x_prompt.reshape(T_CTX, D), x_sample.reshape(T_LAT, D))
    cvec = jnp.concatenate([c, c_ctx[None, :], jnp.zeros((MOD_ROWS - DEC_BATCH - 1, D), F32)], axis=0)
    mod = _mod_table(cvec, w_mod, b_mod).reshape(DEPTH * MOD_ROWS, 1, 6 * D)
    taba, tabb, tabk = _rope_tables()
    ones = _ones_block(256, 64)
    lbl = c_lb_logits.reshape(DEPTH, 2, H_C * DK_C)

    vec = lambda g, reps=1: jnp.tile(g, (1, reps))[:, None, :]
    n1, n2 = vec(norm1_g), vec(norm2_g)
    aq, ak, con = vec(a_qnorm, H_A), vec(a_knorm, KV_A), vec(c_onorm, H_C)
    bq, bkv = vec(b_qnorm), vec(b_kvnorm)
    w_in_p, w_gate = _pack_w_in(w_in)
    wq_p, wkv_p = _pack_wq(b_wq), _pack_wkv(b_wkv)
    wbr, wout = w_branch.astype(BF16), w_out.astype(BF16)
    n_pad = LANE - N_GROUPS - N_EXPERTS
    wr_hi, wr_lo = _split_hi_lo(jnp.concatenate([r_group_w, r_expert_w, jnp.zeros((DEPTH, D, n_pad), F32)], axis=-1))
    br = jnp.concatenate([r_group_b, r_expert_b, jnp.zeros((DEPTH, n_pad), F32)], axis=-1)[:, None, :]
    egu = jnp.concatenate([e_gate.astype(BF16), e_up.astype(BF16)], axis=-1)
    ed = e_down.astype(BF16)

    ck = cache_gqa_k.reshape(DEC_BATCH, DEPTH, PAST, KV_A * HD_A)
    cv = cache_gqa_v.reshape(DEC_BATCH, DEPTH, PAST, KV_A * HD_A)
    ckvb = _ctx_kv(cache_mla_ckv.reshape(DEC_BATCH * DEPTH * PAST, KV_RANK), wkv_p)
    ckr = cache_mla_krope.reshape(DEC_BATCH * DEPTH * PAST, ROPE_B)
    zpad = jnp.zeros_like(ckr)
    ckre = jnp.concatenate([ckr, zpad, ckr, zpad], axis=1)
    s0 = jnp.swapaxes(_state_to_blockdiag(state_hgrn), 2, 3)

    new_k, new_v, new_ckv, new_kr, new_s = [], [], [], [], []
    for l in range(DEPTH):
        (qa, kan, ka, va, qb, ckv, kvb, kr, kre, lff, lfb, kff, kfb, qc, vc, sg) = _inproj(
            l, x, mod, n1, w_in_p, aq, ak, bq, wq_p, bkv, wkv_p, lbl, ones, taba, tabb, tabk)
        new_k.append(kan[:T_CTX])
        new_v.append(va[:T_CTX])
        new_ckv.append(ckv[:T_CTX])
        new_kr.append(kr[:T_CTX, :ROPE_B])

        oa, ob = _attention(qa, qb, ka, va, kvb, kre)
        oa, ob = _attention(qa, qb, ka, va, kvb, kre, prev=(oa, ob), cache=(ck, cv, ckvb, ckre), layer=l)

        ocf, ocb, s_ctx = _hgrn(qc, vc, kff, lff, kfb, lfb)
        ocf, ocb, _ = _hgrn(qc, vc, kff, lff, kfb, lfb, s0=s0, prev=(ocf, ocb), layer=l)
        new_s.append(s_ctx)

        x1, h2, comb, rank, carry, tab = _merge(l, x, mod, n1, w_gate, oa, ob, ocf, ocb, sg, con, ones, wbr, wout,
                                                n2, wr_hi, wr_lo, br)
        moe = _moe(l, h2, comb, rank, carry, tab, egu, ed)
        if l < DEPTH - 1:
            x = _residual(l, False, x1, moe, mod, final_g[None, :])

    last = DEPTH - 1
    y_prompt = _residual(last, True, x1, moe, mod, final_g[None, :], 0, T_CTX // TR)
    y_sample = _residual(last, True, x1, moe, mod, final_g[None, :], T_CTX // TR, T_LAT // TR)
    stack = lambda parts, tail: jnp.stack([p.reshape(BATCH, SEQ, -1) for p in parts], axis=1).reshape(
        (BATCH, DEPTH, SEQ) + tail)
    states = _blockdiag_to_state(jnp.swapaxes(jnp.stack(new_s, axis=1), 2, 3))
    return (y_prompt.reshape(BATCH, SEQ, D), y_sample.reshape(DEC_BATCH, DEC_SEQ, D),
            stack(new_k, (KV_A, HD_A)), stack(new_v, (KV_A, HD_A)), stack(new_ckv, (KV_RANK,)),
            stack(new_kr, (ROPE_B,)), states)
```

```python
import functools

import numpy as np
import jax
import jax.numpy as jnp
from jax import lax
from jax.experimental import pallas as pl
from jax.experimental.pallas import tpu as pltpu

D = 1024
BATCH, SEQ = 32, 256
DEC_BATCH, DEC_SEQ = 8, 1024
PAST = 256
DEPTH = 2
GRID_W = 64
THETA = 10000.0
EPS = 1e-6
F_FLOOR = 1e-30
H_A, KV_A, HD_A = 8, 2, 64
H_B, Q_RANK, KV_RANK, NOPE_B, ROPE_B, V_B = 8, 384, 256, 64, 32, 64
H_C, DK_C, DV_C = 8, 64, 64
BRANCH_W = 512
N_GROUPS, E_PER_GROUP, N_EXPERTS, D_EXPERT = 4, 4, 16, 512

T_CTX = BATCH * SEQ
T_LAT = DEC_BATCH * DEC_SEQ
T_ALL = T_CTX + T_LAT
TM = 256
N_TILES = T_ALL // TM
CTX_TILES = T_CTX // TM
LAT_TILES_PER_SEQ = DEC_SEQ // TM
MOD_ROWS = 16
CTX_MOD_ROW = DEC_BATCH
LANE = 128
VMEM_LIMIT = 56 * 1024 * 1024

C_QA, C_KA, C_VA, C_QRA, C_KVA, C_KR = 0, 512, 640, 768, 1152, 1408
C_FF, C_FB, C_QC, C_IC, C_GC, C_END = 1536, 2048, 2560, 3072, 3584, 4096
R_QA, R_KA, R_VA, R_QRA, R_KVA, R_KR = 0, 512, 640, 768, 1152, 1408
R_FF, R_FB, R_QC, R_IC, R_GC, R_GATE, R_END = 1440, 1952, 2464, 2976, 3488, 4000, 7072

F32 = jnp.float32
BF16 = jnp.bfloat16


def _cparams(sem):
    return pltpu.CompilerParams(dimension_semantics=sem, vmem_limit_bytes=VMEM_LIMIT)


def _mod_row(i):
    return jnp.where(i < CTX_TILES, CTX_MOD_ROW, (i - CTX_TILES) // LAT_TILES_PER_SEQ)


def _pos_block(i):
    return jnp.where(i < CTX_TILES, LAT_TILES_PER_SEQ, (i - CTX_TILES) % LAT_TILES_PER_SEQ)


def _split_hi_lo(x):
    hi = x.astype(BF16)
    lo = (x - hi.astype(F32)).astype(BF16)
    return hi, lo


def _group_mean(x2, ones_blk, width):
    n = ones_blk.shape[0]
    outs = []
    for j in range(x2.shape[-1] // n):
        blk = x2[:, j * n:(j + 1) * n]
        hi, lo = _split_hi_lo(blk)
        s = jnp.dot(hi, ones_blk, preferred_element_type=F32) + jnp.dot(lo, ones_blk, preferred_element_type=F32)
        outs.append(s)
    s = outs[0] if len(outs) == 1 else jnp.concatenate(outs, axis=-1)
    return s * (1.0 / width)


def _rope(x, tab_ref, shift, period):
    c, s1, s2 = tab_ref[0], tab_ref[1], tab_ref[2]
    outs = []
    for j in range(x.shape[-1] // period):
        blk = x[:, j * period:(j + 1) * period]
        outs.append(blk * c + pltpu.roll(blk, shift, 1) * s1 + pltpu.roll(blk, period - shift, 1) * s2)
    return outs[0] if len(outs) == 1 else jnp.concatenate(outs, axis=-1)


def _mod_kernel(c_ref, w_ref, b_ref, o_ref):
    c = c_ref[...]
    a = c * jax.nn.sigmoid(c)
    o_ref[...] = jnp.dot(a, w_ref[...], preferred_element_type=F32, precision=lax.Precision.HIGHEST) + b_ref[...]


def _mod_table(cvec, w_mod, b_mod):
    nt = 1024
    return pl.pallas_call(
        _mod_kernel,
        grid=(DEPTH, 6 * D // nt),
        in_specs=[
            pl.BlockSpec((MOD_ROWS, D), lambda l, j: (0, 0)),
            pl.BlockSpec((None, D, nt), lambda l, j: (l, 0, j)),
            pl.BlockSpec((None, 1, nt), lambda l, j: (l, 0, j)),
        ],
        out_specs=pl.BlockSpec((None, MOD_ROWS, nt), lambda l, j: (l, 0, j)),
        out_shape=jax.ShapeDtypeStruct((DEPTH, MOD_ROWS, 6 * D), F32),
        compiler_params=_cparams(("arbitrary", "arbitrary")),
        name="mod_table",
    )(cvec, w_mod, b_mod.reshape(DEPTH, 1, 6 * D))


def _x_pair(x):
    if isinstance(x, tuple):
        return x[0], x[1], 0
    return x, x, T_CTX


def _x_specs(x, tile):
    _, _, lat_off = _x_pair(x)
    ctx = T_CTX // tile
    return [pl.BlockSpec((tile, D), lambda i: (jnp.minimum(i, ctx - 1), 0)),
            pl.BlockSpec((tile, D), lambda i: (jnp.maximum(i - ctx, 0) + lat_off // tile, 0))]


def _x_tile(xc_ref, xl_ref, tile):
    return jnp.where(pl.program_id(0) < T_CTX // tile, xc_ref[...], xl_ref[...])


def _inproj_kernel(layer, xc_ref, xl_ref, mod_ref, n1_ref, w_ref, aq_ref, ak_ref, bq_ref, wq_ref, bkv_ref, wkv_ref,
                   lbl_ref, ones_ref, taba_ref, tabb_ref, tabk_ref,
                   qa_o, kan_o, ka_o, va_o, qb_o, ckv_o, kvb_o, kr_o, kre_o,
                   lff_o, lfb_o, kff_o, kfb_o, qc_o, vc_o, sg_o):
    x = _x_tile(xc_ref, xl_ref, TM)
    mod = mod_ref[...]
    xn = x * lax.rsqrt(jnp.mean(x * x, axis=-1, keepdims=True) + EPS) * n1_ref[...]
    h = (xn * (1.0 + mod[:, D:2 * D]) + mod[:, 0:D]).astype(BF16)
    y_all = jnp.dot(h, w_ref[...], preferred_element_type=F32)

    def y(c0, c1):
        return y_all[:, c0:c1]

    ones = ones_ref[...]

    qa = y(C_QA, C_KA)
    qa = qa * lax.rsqrt(_group_mean(qa * qa, ones, HD_A) + EPS) * aq_ref[...]
    qa_o[...] = (_rope(qa, taba_ref, 16, LANE) * (HD_A ** -0.5)).astype(BF16)
    ka = y(C_KA, C_VA)
    ka = ka * lax.rsqrt(_group_mean(ka * ka, ones[:LANE, :LANE], HD_A) + EPS) * ak_ref[...]
    kan_o[...] = ka
    ka_o[...] = _rope(ka, taba_ref, 16, LANE).astype(BF16)
    va_o[...] = y(C_VA, C_QRA)

    qr = y(C_QRA, C_KVA)
    qr = qr * lax.rsqrt(jnp.mean(qr * qr, axis=-1, keepdims=True) + EPS) * bq_ref[...]
    qb = jnp.dot(qr.astype(BF16), wq_ref[...], preferred_element_type=F32)
    qb_o[...] = (_rope(qb, tabb_ref, 8, 2 * LANE) * ((NOPE_B + ROPE_B) ** -0.5)).astype(BF16)
    kv = y(C_KVA, C_KR)
    ckv = kv * lax.rsqrt(jnp.mean(kv * kv, axis=-1, keepdims=True) + EPS) * bkv_ref[...]
    ckv_o[...] = ckv
    kvb_o[...] = jnp.dot(ckv.astype(BF16), wkv_ref[...], preferred_element_type=F32).astype(BF16)
    kr = y(C_KR, C_FF)
    kr_o[...] = kr
    kre_o[...] = _rope(kr, tabk_ref, 8, LANE).astype(BF16)

    lbl = lbl_ref[...]
    e = jnp.exp(lbl - jnp.max(lbl, axis=0, keepdims=True))
    p = e / jnp.sum(e, axis=0, keepdims=True)
    lb = p[0] * 0.0
    for i in range(1, layer + 1):
        lb = lb + p[i]
    for d, (c0, lf_o, kf_o) in enumerate(((C_FF, lff_o, kff_o), (C_FB, lfb_o, kfb_o))):
        pre = y(c0, c0 + 512)
        lbd = lb[d:d + 1, :]
        f = jnp.maximum(lbd + (1.0 - lbd) * jax.nn.sigmoid(pre), F_FLOOR)
        lf_o[...] = jnp.log(f)
        kf_o[...] = 1.0 - f
    qc_o[...] = y(C_QC, C_IC).astype(BF16)
    vc_o[...] = y(C_IC, C_GC).astype(BF16)
    gc = y(C_GC, C_END)
    sg_o[...] = (gc * jax.nn.sigmoid(gc)).astype(BF16)


def _const_spec(shape):
    nd = len(shape)
    return pl.BlockSpec(shape, lambda i: (0,) * nd)


def _tile_spec(width):
    return pl.BlockSpec((TM, width), lambda i: (i, 0))


def _layer_spec(tail, layer):
    return pl.BlockSpec((None,) + tuple(tail), lambda *_: (layer,) + (0,) * len(tail))


def _mod_spec(layer, row_of_tile):
    return pl.BlockSpec((None, 1, 6 * D), lambda i, *_: (layer * MOD_ROWS + row_of_tile(i), 0, 0))


def _inproj(layer, x, mod, n1, w_in_p, aq, ak, bq, wq_p, bkv, wkv_p, lbl, ones, taba, tabb, tabk):
    outs = [
        (512, BF16), (128, F32), (128, BF16), (128, F32), (1024, BF16), (256, F32), (1024, BF16),
        (128, F32), (128, BF16), (512, F32), (512, F32), (512, F32), (512, F32), (512, BF16), (512, BF16),
        (512, BF16),
    ]
    tab_spec = lambda w: pl.BlockSpec((3, TM, w), lambda i: (0, _pos_block(i), 0))
    return pl.pallas_call(
        functools.partial(_inproj_kernel, layer),
        grid=(N_TILES,),
        in_specs=_x_specs(x, TM) + [
            _mod_spec(layer, _mod_row),
            _layer_spec((1, D), layer),
            _layer_spec((D, C_END), layer),
            _layer_spec((1, 512), layer), _layer_spec((1, 128), layer), _layer_spec((1, Q_RANK), layer),
            _layer_spec((Q_RANK, 1024), layer), _layer_spec((1, KV_RANK), layer), _layer_spec((KV_RANK, 1024), layer),
            _const_spec((DEPTH, 2, 512)), _const_spec((256, 256)),
            tab_spec(LANE), tab_spec(2 * LANE), tab_spec(LANE),
        ],
        out_specs=[_tile_spec(w) for w, _ in outs],
        out_shape=[jax.ShapeDtypeStruct((T_ALL, w), dt) for w, dt in outs],
        compiler_params=_cparams(("parallel",)),
        name="inproj",
    )(*_x_pair(x)[:2], mod, n1, w_in_p, aq, ak, bq, wq_p, bkv, wkv_p, lbl, ones, taba, tabb, tabk)


def _pack_w_in(w):
    w = w.astype(BF16)
    z = jnp.zeros((DEPTH, D, 32), BF16)
    kr = w[..., R_KR:R_FF]
    main = jnp.concatenate([w[..., :R_KR], kr, z, kr, z, w[..., R_FF:R_GATE]], axis=-1)
    return main, w[..., R_GATE:]


def _pack_wq(wq):
    w = wq.reshape(DEPTH, Q_RANK, H_B, NOPE_B + ROPE_B)
    nope, rope = w[..., :NOPE_B], w[..., NOPE_B:]
    z = jnp.zeros((DEPTH, Q_RANK, H_B, 32), wq.dtype)
    even = jnp.concatenate([rope, z, nope], axis=-1)
    odd = jnp.concatenate([nope, rope, z], axis=-1)
    is_even = (jnp.arange(H_B) % 2 == 0)[None, None, :, None]
    return jnp.where(is_even, even, odd).reshape(DEPTH, Q_RANK, H_B * LANE).astype(BF16)


def _pack_wkv(wkv):
    w = wkv.reshape(DEPTH, KV_RANK, H_B, NOPE_B + V_B)
    nope, v = w[..., :NOPE_B], w[..., NOPE_B:]
    is_even = (jnp.arange(H_B) % 2 == 0)[None, None, :, None]
    return jnp.where(is_even, jnp.concatenate([v, nope], -1), jnp.concatenate([nope, v], -1)).reshape(
        DEPTH, KV_RANK, H_B * LANE).astype(BF16)


def _rope_tables():
    pos = np.arange(DEC_SEQ)
    row, col = pos // GRID_W, pos % GRID_W

    def pattern(half):
        quarter = half // 2
        inv = THETA ** (-np.arange(0, half, 2, dtype=np.float64) / half)
        ang = np.concatenate([row[:, None] * inv, row[:, None] * inv, col[:, None] * inv, col[:, None] * inv], 1)
        is_x2 = np.tile(np.concatenate([np.zeros(quarter), np.ones(quarter)]), 2)[None, :]
        c = np.cos(ang)
        s1 = np.sin(ang) * is_x2
        s2 = -np.sin(ang) * (1 - is_x2)
        return c, s1, s2

    def assemble(width, spans, half):
        c, s1, s2 = pattern(half)
        tc = np.ones((DEC_SEQ + TM, width))
        t1 = np.zeros((DEC_SEQ + TM, width))
        t2 = np.zeros((DEC_SEQ + TM, width))
        for start in spans:
            tc[:DEC_SEQ, start:start + 2 * half] = c
            t1[:DEC_SEQ, start:start + 2 * half] = s1
            t2[:DEC_SEQ, start:start + 2 * half] = s2
        return jnp.asarray(np.stack([tc, t1, t2]), F32)

    taba = assemble(LANE, (0, 64), 32)
    tabb = assemble(2 * LANE, (0, 128 + 64), 16)
    tabk = assemble(LANE, (0, 64), 16)
    return taba, tabb, tabk


def _ones_block(n, width):
    g = np.arange(n) // width
    return jnp.asarray(g[:, None] == g[None, :], BF16)


def _ctxkv_kernel(c_ref, w_ref, o_ref):
    o_ref[...] = jnp.dot(c_ref[...].astype(BF16), w_ref[...], preferred_element_type=F32).astype(BF16)


def _ctx_kv(ckv_cache, wkv_p):
    rows = ckv_cache.shape[0]
    return pl.pallas_call(
        _ctxkv_kernel,
        grid=(rows // PAST,),
        in_specs=[pl.BlockSpec((PAST, KV_RANK), lambda i: (i, 0)),
                  pl.BlockSpec((None, KV_RANK, 1024), lambda i: (i % DEPTH, 0, 0))],
        out_specs=pl.BlockSpec((PAST, 1024), lambda i: (i, 0)),
        out_shape=jax.ShapeDtypeStruct((rows, 1024), BF16),
        compiler_params=_cparams(("parallel",)),
        name="ctx_kv",
    )(ckv_cache, wkv_p)


_NT = (((1,), (1,)), ((), ()))


def _den_lane(parity):
    return 64 if parity == 0 else 0


def _softmax_pv(s, v, parity, mxu_den):
    m = s.max(axis=-1, keepdims=True)
    if not mxu_den:
        p = jnp.exp(s - m)
        return jnp.dot(p.astype(BF16), v, preferred_element_type=F32) / p.sum(axis=-1, keepdims=True)
    lane = lax.broadcasted_iota(jnp.int32, (1, LANE), 1)
    keep = (lane < 64) if parity == 0 else (lane >= 64)
    o = jnp.dot(jnp.exp((s - m).astype(BF16)), v, preferred_element_type=F32)
    return jnp.where(keep, o / o[:, _den_lane(parity):_den_lane(parity) + 1], 0.0)


def _attn_kernel(n_pieces, qa_ref, qb_ref, *refs):
    kv_refs = refs[:4 * n_pieces]
    oa_ref, ob_ref = refs[4 * n_pieces:4 * n_pieces + 2]
    mxu_den = n_pieces == 2
    lane = lax.broadcasted_iota(jnp.int32, (1, LANE), 1)
    lo = lane < 64
    hi = jnp.logical_not(lo)

    def rows(parts):
        return parts[0] if len(parts) == 1 else jnp.concatenate(parts, axis=0)

    def with_den(v, parity):
        return jnp.where(lane == _den_lane(parity), jnp.ones_like(v), v) if mxu_den else v

    def slabs_a():
        ka = [kv_refs[4 * i][...].astype(F32) for i in range(n_pieces)]
        va = [kv_refs[4 * i + 1][...].astype(F32) for i in range(n_pieces)]

        def place(x, g, parity):
            if g != parity:
                x = pltpu.roll(x, 64, 1)
            return jnp.where(lo if parity == 0 else hi, x, 0.0).astype(BF16)

        return {2 * g + parity: (rows([place(k, g, parity) for k in ka]),
                                 with_den(rows([place(v, g, parity) for v in va]), parity))
                for g in range(KV_A) for parity in range(2)}

    def slabs_b():
        out = {}
        kre = rows([kv_refs[4 * i + 3][...].astype(BF16) for i in range(n_pieces)])
        for h in range(H_B):
            parity = h % 2
            nope = hi if parity == 0 else lo
            kvb = rows([kv_refs[4 * i + 2][:, h * LANE:(h + 1) * LANE] for i in range(n_pieces)])
            out[h] = (jnp.where(nope, kvb, kre), with_den(jnp.where(nope, jnp.zeros_like(kvb), kvb), parity))
        return out

    sa, sb = slabs_a(), slabs_b()
    get_a, get_b = sa.__getitem__, sb.__getitem__

    for pair in range(H_A // 2):
        g = (2 * pair) // (H_A // KV_A)
        q = qa_ref[:, pair * LANE:(pair + 1) * LANE]
        acc = None
        for parity in range(2):
            k, v = get_a(2 * g + parity)
            o = _softmax_pv(lax.dot_general(q, k, _NT, preferred_element_type=F32), v, parity, mxu_den)
            acc = o if acc is None else acc + o
        oa_ref[:, pair * LANE:(pair + 1) * LANE] = acc.astype(BF16)

    for pair in range(H_B // 2):
        acc = None
        for parity in range(2):
            h = 2 * pair + parity
            k, v = get_b(h)
            q = qb_ref[:, h * LANE:(h + 1) * LANE]
            o = _softmax_pv(lax.dot_general(q, k, _NT, preferred_element_type=F32), v, parity, mxu_den)
            acc = o if acc is None else acc + o
        ob_ref[:, pair * LANE:(pair + 1) * LANE] = acc.astype(BF16)


def _attention(qa, qb, ka, va, kvb, kre, prev=None, cache=None, layer=0):
    if cache is None:
        nb, nqt, nk, q_blk0, k_blk0 = BATCH, 1, SEQ, 0, 0
    else:
        nb, nqt, nk, q_blk0, k_blk0 = DEC_BATCH, DEC_SEQ // TM, DEC_SEQ, CTX_TILES, T_CTX // DEC_SEQ
    qspec = lambda w: pl.BlockSpec((TM, w), lambda b, j: (q_blk0 + b * nqt + j, 0))
    kspec = lambda w: pl.BlockSpec((nk, w), lambda b, j: (k_blk0 + b, 0))
    in_specs = [qspec(512), qspec(1024), kspec(128), kspec(128), kspec(1024), kspec(128)]
    args = [qa, qb, ka, va, kvb, kre]
    n_pieces = 1
    aliases = {}
    if cache is not None:
        cspec4 = pl.BlockSpec((None, None, PAST, 128), lambda b, j: (b, layer, 0, 0))
        cspec = lambda w: pl.BlockSpec((PAST, w), lambda b, j: (b * DEPTH + layer, 0))
        in_specs += [cspec4, cspec4, cspec(1024), cspec(128)]
        args += list(cache)
        n_pieces = 2
        in_specs += [pl.BlockSpec(memory_space=pl.ANY)] * 2
        args += list(prev)
        aliases = {len(args) - 2: 0, len(args) - 1: 1}

    def body(*refs):
        if cache is not None:
            n_in = 2 + 4 * n_pieces
            refs = refs[:n_in] + refs[n_in + 2:]
        _attn_kernel(n_pieces, *refs)

    return pl.pallas_call(
        body,
        grid=(nb, nqt),
        in_specs=in_specs,
        out_specs=[qspec(512), qspec(512)],
        out_shape=[jax.ShapeDtypeStruct((T_ALL, 512), BF16)] * 2,
        input_output_aliases=aliases,
        compiler_params=_cparams(("parallel", "arbitrary")),
        name="attention_ctx" if cache is None else "attention_lat",
    )(*args)


HL = 256
HS = 128
HG = 64
N_PAIRS = H_C // 2
FAST_DECAY_LIMIT = 80.0


def _hgrn_bottom_exact(q, k, c, lo, rev):
    row = lax.broadcasted_iota(jnp.int32, (HS, LANE), 0)
    srow = lax.broadcasted_iota(jnp.int32, (HS, HS), 0)
    scol = lax.broadcasted_iota(jnp.int32, (HS, HS), 1)
    out = []
    for parity in range(2):
        def dup(x):
            xs = pltpu.roll(x, 64, 1)
            return jnp.where(lo, x, xs) if parity == 0 else jnp.where(lo, xs, x)
        qd, kd, bd = dup(q), dup(k), dup(c)

        dg = row & 3
        if rev:
            dg = 3 - dg
        e = [None]
        for delta in range(1, 4):
            shifted = pltpu.roll(bd, delta if rev else HS - delta, 0)
            e.append(jnp.exp(jnp.minimum(shifted - bd, 0.0)))
        qp, kp = [], []
        for c1, c2 in ((0, 1), (2, 3)):
            cv = jnp.where(lo, c1, c2)
            dl = cv - dg
            fac = jnp.where(dl == 0, 1.0, jnp.where(dl == 1, e[1], jnp.where(dl == 2, e[2],
                            jnp.where(dl == 3, e[3], 0.0))))
            kp.append((kd * fac).astype(BF16))
            qp.append(jnp.where(dg == cv, qd, 0.0).astype(BF16))
        s = lax.dot_general(jnp.concatenate(qp, axis=1), jnp.concatenate(kp, axis=1), _NT,
                            preferred_element_type=F32)
        tot = jnp.where((srow >> 2) == (scol >> 2), s, 0.0)

        for lev in range(1, 3):
            g = 4 ** lev
            par = 4 * g
            shape3 = (HS // par, par, LANE)
            rid = lax.broadcasted_iota(jnp.int32, shape3, 1)
            dg3 = rid >> (2 * lev)
            if rev:
                dg3 = 3 - dg3
            b3, q3, k3 = bd.reshape(shape3), qd.reshape(shape3), kd.reshape(shape3)
            lo3 = lo.reshape(1, 1, LANE)
            qp, kp = [], []
            for c1, c2 in ((1, 2), (3, None)):
                idx = lambda cc: (4 - cc) * g if rev else cc * g - 1
                i1 = idx(c1)
                i2 = idx(c2) if c2 is not None else i1
                ridx = jnp.where(lo3, i1, i2)
                ref = jnp.sum(jnp.where(rid == ridx, b3, 0.0), axis=1, keepdims=True)
                cvk = jnp.where(lo3, c1, c2 if c2 is not None else 0)
                cvq = jnp.where(lo3, c1, c2 if c2 is not None else -1)
                kk = jnp.where(dg3 < cvk, k3 * jnp.exp(jnp.minimum(ref - b3, 0.0)), 0.0)
                qq = jnp.where(dg3 == cvq, q3 * jnp.exp(jnp.minimum(b3 - ref, 0.0)), 0.0)
                kp.append(kk.reshape(HS, LANE).astype(BF16))
                qp.append(qq.reshape(HS, LANE).astype(BF16))
            s = lax.dot_general(jnp.concatenate(qp, axis=1), jnp.concatenate(kp, axis=1), _NT,
                                preferred_element_type=F32)
            sh = 2 * lev + 2
            tot = tot + jnp.where((srow >> sh) == (scol >> sh), s, 0.0)
        out.append(tot)
    return out


def _hgrn_head(q, k, lf, v, st_ref, rev):
    row = lax.broadcasted_iota(jnp.int32, (HS, LANE), 0)
    lane = lax.broadcasted_iota(jnp.int32, (1, LANE), 1)
    lo = lane < 64
    hi = jnp.logical_not(lo)
    in_g1 = row >= HG

    grow = row & (HG - 1)
    c = lf
    d = 1
    while d < HG:
        if rev:
            c = c + jnp.where(grow < HG - d, pltpu.roll(c, HS - d, 0), 0.0)
        else:
            c = c + jnp.where(grow >= d, pltpu.roll(c, d, 0), 0.0)
        d *= 2
    if rev:
        t0, t1 = c[0:1, :], c[HG:HG + 1, :]
    else:
        t0, t1 = c[HG - 1:HG, :], c[HS - 1:HS, :]
    et0, et1 = jnp.exp(t0), jnp.exp(t1)
    qe = q * jnp.exp(c)
    e_out = jnp.exp(jnp.where(in_g1, t1, t0) - c)
    ke = k * e_out

    if rev:
        qb = qe * jnp.where(in_g1, 1.0, et1)
        kh = ke * jnp.where(in_g1, et0, 1.0)
    else:
        qb = qe * jnp.where(in_g1, et0, 1.0)
        kh = ke * jnp.where(in_g1, 1.0, et1)
    st = st_ref[...]
    o_int = lax.dot_general(qb.astype(BF16), st.astype(BF16), _NT, preferred_element_type=F32)
    upd = lax.dot_general(v, kh.astype(BF16), (((0,), (0,)), ((), ())), preferred_element_type=F32)
    r128 = lax.broadcasted_iota(jnp.int32, (LANE, LANE), 0)
    c128 = lax.broadcasted_iota(jnp.int32, (LANE, LANE), 1)
    st_ref[...] = st * (et0 * et1) + jnp.where((r128 >> 6) == (c128 >> 6), upd, 0.0)

    q_late = in_g1 if not rev else jnp.logical_not(in_g1)
    q_top = jnp.where(q_late, qe, 0.0)
    k_top = jnp.where(q_late, 0.0, ke).astype(BF16)
    top = [lax.dot_general(jnp.where(m, q_top, 0.0).astype(BF16), k_top, _NT, preferred_element_type=F32)
           for m in (lo, hi)]
    mid = HG // 2 if rev else HG // 2 - 1
    cm = c - jnp.where(in_g1, c[HG + mid:HG + mid + 1, :], c[mid:mid + 1, :])
    return c, top, o_int, cm, jnp.max(jnp.abs(cm))


def _hgrn_steps(jobs, bot_ref):
    lane = lax.broadcasted_iota(jnp.int32, (1, LANE), 1)
    lo = lane < 64
    hi = jnp.logical_not(lo)
    heads = [_hgrn_head(*job) for job in jobs]
    fast = functools.reduce(jnp.maximum, [h[4] for h in heads]) <= FAST_DECAY_LIMIT

    @pl.when(fast)
    def _():
        srow = lax.broadcasted_iota(jnp.int32, (HS, HS), 0)
        scol = lax.broadcasted_iota(jnp.int32, (HS, HS), 1)
        same = (srow >> 6) == (scol >> 6)
        for ji, (job, (_, _, _, cm, _)) in enumerate(zip(jobs, heads)):
            keep = same & ((scol >= srow) if job[5] else (scol <= srow))
            qf = job[0] * jnp.exp(cm)
            kf = (job[1] * jnp.exp(-cm)).astype(BF16)
            for parity, m in enumerate((lo, hi)):
                s = lax.dot_general(jnp.where(m, qf, 0.0).astype(BF16), kf, _NT, preferred_element_type=F32)
                bot_ref[ji, parity] = jnp.where(keep, s, 0.0)

    @pl.when(jnp.logical_not(fast))
    def _():
        for ji, (job, (c, _, _, _, _)) in enumerate(zip(jobs, heads)):
            for parity, s in enumerate(_hgrn_bottom_exact(job[0], job[1], c, lo, job[5])):
                bot_ref[ji, parity] = s

    outs = []
    for ji, (job, (_, top, o_int, _, _)) in enumerate(zip(jobs, heads)):
        v = job[3]
        probs = jnp.concatenate([(bot_ref[ji, 0] + top[0]).astype(BF16), (bot_ref[ji, 1] + top[1]).astype(BF16)],
                                axis=1)
        vv = jnp.concatenate([jnp.where(lo, v, jnp.zeros_like(v)), jnp.where(hi, v, jnp.zeros_like(v))], axis=0)
        outs.append(jnp.dot(probs, vv, preferred_element_type=F32) + o_int)
    return outs


PAIRS_PER_ITER = 2


def _hgrn_kernel(has_s0, nt, *refs):
    if has_s0:
        (qf_ref, vf_ref, kf_ref, lf_ref, qb_ref, vb_ref, kb_ref, lb_ref, s0_ref,
         of_ref, ob_ref, so_ref, st_scr, bot_scr) = refs
    else:
        (qf_ref, vf_ref, kf_ref, lf_ref, qb_ref, vb_ref, kb_ref, lb_ref,
         of_ref, ob_ref, so_ref, st_scr, bot_scr) = refs
    j = pl.program_id(1)

    @pl.when(j == 0)
    def _():
        if has_s0:
            st_scr[...] = s0_ref[...]
        else:
            st_scr[...] = jnp.zeros_like(st_scr)

    n_sub = HL // HS

    def pair_body(it, carry):
        for step in range(n_sub):
            jobs, dests = [], []
            for pp in range(PAIRS_PER_ITER):
                p = it * PAIRS_PER_ITER + pp
                cols = pl.ds(pl.multiple_of(p * LANE, LANE), LANE)
                rf = pl.ds(step * HS, HS)
                rb = pl.ds((n_sub - 1 - step) * HS, HS)
                jobs.append((qf_ref[rf, cols].astype(F32), kf_ref[rf, cols], lf_ref[rf, cols], vf_ref[rf, cols],
                             st_scr.at[p, 0], False))
                dests.append((of_ref, rf, cols))
                jobs.append((qb_ref[rb, cols].astype(F32), kb_ref[rb, cols], lb_ref[rb, cols], vb_ref[rb, cols],
                             st_scr.at[p, 1], True))
                dests.append((ob_ref, rb, cols))
            for (ref, rows, cols), o in zip(dests, _hgrn_steps(jobs, bot_scr)):
                ref[rows, cols] = o
        return carry

    lax.fori_loop(0, N_PAIRS // PAIRS_PER_ITER, pair_body, 0)

    @pl.when(j == nt - 1)
    def _():
        so_ref[...] = st_scr[...]


def _hgrn(qc, vc, kff, lff, kfb, lfb, s0=None, prev=None, layer=0):
    if s0 is None:
        nb, nt, blk0 = BATCH, SEQ // HL, 0
    else:
        nb, nt, blk0 = DEC_BATCH, DEC_SEQ // HL, T_CTX // HL
    fspec = pl.BlockSpec((HL, 512), lambda b, j: (blk0 + b * nt + j, 0))
    bspec = pl.BlockSpec((HL, 512), lambda b, j: (blk0 + b * nt + nt - 1 - j, 0))
    sspec = pl.BlockSpec((None, N_PAIRS, 2, LANE, LANE), lambda b, j: (b, 0, 0, 0, 0))
    in_specs = [fspec] * 4 + [bspec] * 4
    args = [qc, vc, kff, lff, qc, vc, kfb, lfb]
    aliases = {}
    if s0 is not None:
        in_specs += [pl.BlockSpec((None, None, N_PAIRS, 2, LANE, LANE), lambda b, j: (b, layer, 0, 0, 0, 0))]
        args += [s0]
        in_specs += [pl.BlockSpec(memory_space=pl.ANY)] * 2
        args += list(prev)
        aliases = {len(args) - 2: 0, len(args) - 1: 1}

    def body(*refs):
        if s0 is not None:
            refs = refs[:9] + refs[11:]
        _hgrn_kernel(s0 is not None, nt, *refs)

    return pl.pallas_call(
        body,
        grid=(nb, nt),
        in_specs=in_specs,
        out_specs=[fspec, bspec, sspec],
        out_shape=[jax.ShapeDtypeStruct((T_ALL, 512), F32), jax.ShapeDtypeStruct((T_ALL, 512), F32),
                   jax.ShapeDtypeStruct((nb, N_PAIRS, 2, LANE, LANE), F32)],
        scratch_shapes=[pltpu.VMEM((N_PAIRS, 2, LANE, LANE), F32),
                        pltpu.VMEM((2 * PAIRS_PER_ITER, 2, HS, HS), F32)],
        input_output_aliases=aliases,
        compiler_params=_cparams(("parallel", "arbitrary")),
        name="hgrn_ctx" if s0 is None else "hgrn_lat",
    )(*args)


def _merge_kernel(xc_ref, xl_ref, mod_ref, n1_ref, wg_ref, oa_ref, ob_ref, ocf_ref, ocb_ref, sg_ref, con_ref,
                  ones_ref, wbr_ref, wout_ref, n2_ref, wrh_ref, wrl_ref, br_ref, tri_ref, upper_ref,
                  x1_o, h2_o, comb_o, rank_o, carry_o, tab_o, carry_scr):
    x = _x_tile(xc_ref, xl_ref, TMG)
    mod = mod_ref[...]
    xn = x * lax.rsqrt(jnp.mean(x * x, axis=-1, keepdims=True) + EPS) * n1_ref[...]
    h = (xn * (1.0 + mod[:, D:2 * D]) + mod[:, 0:D]).astype(BF16)

    oc = ocf_ref[...] + ocb_ref[...]
    oc = oc * lax.rsqrt(_group_mean(oc * oc, ones_ref[...], DV_C) + EPS) * con_ref[...]
    oc = (oc * sg_ref[...].astype(F32)).astype(BF16)
    branches = (oa_ref[...], ob_ref[...], oc)
    mix = None
    for jb in range(3):
        gate = jax.nn.sigmoid(jnp.dot(h, wg_ref[:, jb * D:(jb + 1) * D], preferred_element_type=F32))
        t = gate * jnp.dot(branches[jb], wbr_ref[jb], preferred_element_type=F32)
        mix = t if mix is None else mix + t
    out = jnp.dot(mix.astype(BF16), wout_ref[...], preferred_element_type=F32)
    x1 = x + mod[:, 2 * D:3 * D] * out
    x1_o[...] = x1

    x1n = x1 * lax.rsqrt(jnp.mean(x1 * x1, axis=-1, keepdims=True) + EPS) * n2_ref[...]
    h2 = x1n * (1.0 + mod[:, 4 * D:5 * D]) + mod[:, 3 * D:4 * D]
    h2_o[...] = h2.astype(BF16)

    h2h, h2l = _split_hi_lo(h2)
    logits = (jnp.dot(h2h, wrh_ref[...], preferred_element_type=F32)
              + jnp.dot(h2l, wrh_ref[...], preferred_element_type=F32)
              + jnp.dot(h2h, wrl_ref[...], preferred_element_type=F32)) + br_ref[...]
    lane = lax.broadcasted_iota(jnp.int32, logits.shape, 1)
    neg = -jnp.inf
    is_g = lane < N_GROUPS
    gl = jnp.where(is_g, logits, neg)
    gmax = gl.max(axis=-1, keepdims=True)
    gidx = jnp.min(jnp.where(gl == gmax, lane, LANE), axis=-1, keepdims=True)
    gp = 1.0 / jnp.sum(jnp.where(is_g, jnp.exp(gl - gmax), 0.0), axis=-1, keepdims=True)
    eid = lane - N_GROUPS
    in_grp = (eid >= 0) & (eid < N_EXPERTS) & ((eid >> 2) == gidx)
    el = jnp.where(in_grp, logits, neg)
    v1 = el.max(axis=-1, keepdims=True)
    i1 = jnp.min(jnp.where(el == v1, lane, LANE), axis=-1, keepdims=True)
    el2 = jnp.where(lane == i1, neg, el)
    v2 = el2.max(axis=-1, keepdims=True)
    i2 = jnp.min(jnp.where(el2 == v2, lane, LANE), axis=-1, keepdims=True)
    e2 = jnp.exp(v2 - v1)
    w1 = gp / (1.0 + e2)
    w2 = gp * e2 / (1.0 + e2)
    comb = jnp.where(lane == i1, w1, 0.0) + jnp.where(lane == i2, w2, 0.0)
    comb = pltpu.roll(comb, LANE - N_GROUPS, 1)
    comb_o[...] = comb

    step = pl.program_id(0) % MOE_STEPS

    @pl.when(step == 0)
    def _():
        carry_scr[...] = jnp.zeros_like(carry_scr)

    for sb in range(TMG // SBK):
        sub = comb[sb * SBK:(sb + 1) * SBK, :]
        routed = sub > 0.0
        ind = jnp.where(routed, 1.0, 0.0)
        carry = carry_scr[...]
        rank = jnp.dot(tri_ref[...], ind.astype(BF16), preferred_element_type=F32) + carry
        rank_o[sb * SBK:(sb + 1) * SBK, :] = jnp.where(routed, rank, -1.0)
        carry_o[sb] = carry
        carry_scr[...] = carry + jnp.sum(ind, axis=0, keepdims=True)

    @pl.when(step == MOE_STEPS - 1)
    def _():
        count = carry_scr[...]
        seg = jnp.floor((count + (SEG_ALIGN - 1.0)) * (1.0 / SEG_ALIGN)) * SEG_ALIGN
        off = jnp.dot(jnp.broadcast_to(seg, (8, LANE)), upper_ref[...], preferred_element_type=F32,
                      precision=lax.Precision.HIGHEST)
        tab_o[0:1, :] = count
        tab_o[1:2, :] = off[0:1, :]


TMG = 512


def _merge(layer, x, mod, n1, wgate, oa, ob, ocf, ocb, sg, con, ones, wbr, wout, n2, wr_hi, wr_lo, br):
    ctx_tiles, per_seq = T_CTX // TMG, DEC_SEQ // TMG
    mrow = lambda i: jnp.where(i < ctx_tiles, CTX_MOD_ROW, (i - ctx_tiles) // per_seq)
    tspec = lambda w: pl.BlockSpec((TMG, w), lambda i: (i, 0))
    wspec = lambda tail: pl.BlockSpec((None,) + tail, lambda i: (layer,) + (0,) * len(tail),
                                      pipeline_mode=pl.Buffered(1))
    return pl.pallas_call(
        _merge_kernel,
        grid=(T_ALL // TMG,),
        in_specs=_x_specs(x, TMG) + [
            _mod_spec(layer, mrow),
            _layer_spec((1, D), layer), wspec((D, 3 * D)),
            tspec(512), tspec(512), tspec(512), tspec(512), tspec(512),
            _layer_spec((1, 512), layer), _const_spec((256, 256)),
            wspec((3, BRANCH_W, D)), wspec((D, D)), _layer_spec((1, D), layer),
            _layer_spec((D, LANE), layer), _layer_spec((D, LANE), layer), _layer_spec((1, LANE), layer),
            _const_spec((SBK, SBK)), _const_spec((LANE, LANE)),
        ],
        out_specs=[
            tspec(D), tspec(D), tspec(LANE), tspec(LANE),
            pl.BlockSpec((None, TMG // SBK, 1, LANE), lambda i: (i // MOE_STEPS, i % MOE_STEPS, 0, 0)),
            pl.BlockSpec((None, 2, LANE), lambda i: (i // MOE_STEPS, 0, 0)),
        ],
        out_shape=[jax.ShapeDtypeStruct((T_ALL, D), F32), jax.ShapeDtypeStruct((T_ALL, D), BF16),
                   jax.ShapeDtypeStruct((T_ALL, LANE), F32), jax.ShapeDtypeStruct((T_ALL, LANE), F32),
                   jax.ShapeDtypeStruct((N_BLK, N_SB, 1, LANE), F32), jax.ShapeDtypeStruct((N_BLK, 2, LANE), F32)],
        scratch_shapes=[pltpu.VMEM((1, LANE), F32)],
        compiler_params=_cparams(("arbitrary",)),
        name="merge",
    )(*_x_pair(x)[:2], mod, n1, wgate, oa, ob, ocf, ocb, sg, con, ones, wbr, wout, n2, wr_hi, wr_lo, br,
      jnp.asarray(np.tril(np.ones((SBK, SBK)), -1), BF16), jnp.asarray(np.triu(np.ones((LANE, LANE)), 1), F32))


NB = 2048
N_BLK = T_ALL // NB
SBK = 256
N_SB = NB // SBK
WIN_SHIFT, FT_SHIFT, SEG_SHIFT = 6, 7, 4
WIN = 1 << WIN_SHIFT
FT = 1 << FT_SHIFT
SEG_ALIGN = 1 << SEG_SHIFT
STG = 2 * NB + N_EXPERTS * SEG_ALIGN + 256
QUAD = 4
E_STEP = 4
MOE_STEPS = NB // TMG


def _moe_kernel(cnt_s, off_s, car_s, h2_ref, rank_ref, comb_ref, offv_ref, eg_ref, eu_ref, ed_ref, o_ref,
                stg_ref, acc_ref):
    blk = pl.program_id(0)
    step = pl.program_id(1)
    srow = lax.broadcasted_iota(jnp.int32, (WIN, SBK), 0).astype(F32)

    def windows(s):
        out = []
        for ex in range(N_EXPERTS):
            start = off_s[blk, ex] + car_s[blk, s, ex]
            length = car_s[blk, s + 1, ex] - car_s[blk, s, ex]
            ws = (start >> SEG_SHIFT) << SEG_SHIFT
            out.append((ws, (start - ws + length + (WIN - 1)) >> WIN_SHIFT))
        return out, functools.reduce(jnp.maximum, [w[1] for w in out])

    def positions(s):
        rows = pl.ds(pl.multiple_of(s * SBK, SBK), SBK)
        rank = rank_ref[rows, :]
        return rows, jnp.where(rank >= 0.0, rank + offv_ref[1:2, :], -1.0e6)

    @pl.when(step == 0)
    def _():
        stg_ref[...] = jnp.zeros_like(stg_ref)

        def sub_body(s, carry):
            rows, pos = positions(s)
            pos_t = pos.T
            h2 = h2_ref[rows, :]
            wins, nmax = windows(s)

            def chunk_body(c, carry2):
                for quad in range(N_EXPERTS // QUAD):
                    blocks = []
                    for ex in range(quad * QUAD, (quad + 1) * QUAD):
                        base = (wins[ex][0] + c * WIN).astype(F32)
                        hit = (pos_t[ex:ex + 1, :] - base) == srow
                        blocks.append(jnp.where(hit, 1.0, 0.0).astype(BF16))
                    moved = jnp.dot(jnp.concatenate(blocks, axis=0), h2, preferred_element_type=F32).astype(BF16)
                    for i in range(QUAD):
                        first = jnp.minimum(wins[quad * QUAD + i][0] + c * WIN, STG - WIN)
                        dst = pl.ds(pl.multiple_of(first, SEG_ALIGN), WIN)
                        stg_ref[dst, :] = stg_ref[dst, :] + moved[i * WIN:(i + 1) * WIN, :]
                return carry2

            lax.fori_loop(0, nmax, chunk_body, 0)
            return carry

        lax.fori_loop(0, N_SB, sub_body, 0)

    def ffn_tile(j, first, n_rows, n_valid):
        rows = pl.ds(pl.multiple_of(first, SEG_ALIGN), n_rows)
        xs = stg_ref[rows, :]
        hg = jnp.dot(xs, eg_ref[j], preferred_element_type=F32)
        hu = jnp.dot(xs, eu_ref[j], preferred_element_type=F32)
        act = (hg * jax.nn.sigmoid(hg) * hu).astype(BF16)
        y = jnp.dot(act, ed_ref[j], preferred_element_type=F32).astype(BF16)
        rid = lax.broadcasted_iota(jnp.int32, (n_rows, 1), 0)
        stg_ref[rows, :] = jnp.where(rid < n_valid, y, xs)

    for j in range(E_STEP):
        ex = step * E_STEP + j
        count = cnt_s[blk, ex]
        seg0 = off_s[blk, ex]
        n_big = (count + (FT - 1)) >> (FT_SHIFT + 1)

        def ffn_body(t, carry, j=j, count=count, seg0=seg0):
            ffn_tile(j, seg0 + t * (2 * FT), 2 * FT, count - t * (2 * FT))
            return carry

        lax.fori_loop(0, n_big, ffn_body, 0)

        @pl.when(count > n_big * (2 * FT))
        def _(j=j, count=count, seg0=seg0, n_big=n_big):
            ffn_tile(j, seg0 + n_big * (2 * FT), FT, count - n_big * (2 * FT))

    @pl.when(step == N_EXPERTS // E_STEP - 1)
    def _():
        def sub_body(s, carry):
            rows, pos = positions(s)
            pos_t = pos.T
            wts_t = comb_ref[rows, :].T
            wins, nmax = windows(s)
            acc_ref[...] = jnp.zeros_like(acc_ref)

            def chunk_body(c, carry2):
                sel_t, srcs = [], []
                for ex in range(N_EXPERTS):
                    base = (wins[ex][0] + c * WIN).astype(F32)
                    hit = (pos_t[ex:ex + 1, :] - base) == srow
                    sel_t.append(jnp.where(hit, wts_t[ex:ex + 1, :], 0.0).astype(BF16))
                    first = jnp.minimum(wins[ex][0] + c * WIN, STG - WIN)
                    srcs.append(stg_ref[pl.ds(pl.multiple_of(first, SEG_ALIGN), WIN), :])
                acc_ref[...] += lax.dot_general(jnp.concatenate(sel_t, axis=0), jnp.concatenate(srcs, axis=0),
                                                (((0,), (0,)), ((), ())), preferred_element_type=F32)
                return carry2

            lax.fori_loop(0, nmax, chunk_body, 0)
            o_ref[rows, :] = acc_ref[...].astype(BF16)
            return carry

        lax.fori_loop(0, N_SB, sub_body, 0)


def _moe(layer, h2, comb, rank, carry, tab, eg, eu, ed):
    cnt_i = tab[:, 0, :N_EXPERTS].astype(jnp.int32)
    off_i = tab[:, 1, :N_EXPERTS].astype(jnp.int32)
    car_i = jnp.concatenate([carry[:, :, 0, :N_EXPERTS].astype(jnp.int32), cnt_i[:, None, :]], axis=1)
    bspec = lambda w: pl.BlockSpec((NB, w), lambda b, e, *_: (b, 0))
    bspec1 = lambda w: pl.BlockSpec((NB, w), lambda b, e, *_: (b, 0), pipeline_mode=pl.Buffered(1))
    grid_spec = pltpu.PrefetchScalarGridSpec(
        num_scalar_prefetch=3,
        grid=(N_BLK, N_EXPERTS // E_STEP),
        in_specs=[
            bspec1(D), bspec1(LANE), bspec1(LANE),
            pl.BlockSpec((None, 2, LANE), lambda b, e, *_: (b, 0, 0)),
            pl.BlockSpec((None, E_STEP, D, D_EXPERT), lambda b, e, *_: (layer, e, 0, 0)),
            pl.BlockSpec((None, E_STEP, D, D_EXPERT), lambda b, e, *_: (layer, e, 0, 0)),
            pl.BlockSpec((None, E_STEP, D_EXPERT, D), lambda b, e, *_: (layer, e, 0, 0)),
        ],
        out_specs=bspec(D),
        scratch_shapes=[pltpu.VMEM((STG, D), BF16), pltpu.VMEM((SBK, D), F32)],
    )
    return pl.pallas_call(
        _moe_kernel,
        grid_spec=grid_spec,
        out_shape=jax.ShapeDtypeStruct((T_ALL, D), BF16),
        compiler_params=_cparams(("parallel", "arbitrary")),
        name="moe",
    )(cnt_i, off_i, car_i, h2, rank, comb, tab, eg, eu, ed)


def _residual_kernel(final, x1_ref, moe_ref, mod_ref, fg_ref, o_ref):
    x2 = x1_ref[...] + mod_ref[:, 5 * D:6 * D] * moe_ref[...].astype(F32)
    if final:
        x2 = x2 * lax.rsqrt(jnp.mean(x2 * x2, axis=-1, keepdims=True) + EPS) * fg_ref[...]
    o_ref[...] = x2


TR = DEC_SEQ


def _residual(layer, final, x1, moe, mod, fg, tile0=0, n_tiles=T_ALL // TR):
    ctx_tiles = T_CTX // TR
    mrow = lambda i: jnp.where(tile0 + i < ctx_tiles, CTX_MOD_ROW, tile0 + i - ctx_tiles)
    src = lambda: pl.BlockSpec((TR, D), lambda i: (tile0 + i, 0))
    return pl.pallas_call(
        functools.partial(_residual_kernel, final),
        grid=(n_tiles,),
        in_specs=[src(), src(), _mod_spec(layer, mrow), _const_spec((1, D))],
        out_specs=pl.BlockSpec((TR, D), lambda i: (i, 0)),
        out_shape=jax.ShapeDtypeStruct((n_tiles * TR, D), F32),
        compiler_params=_cparams(("parallel",)),
        name="residual",
    )(x1, moe, mod, fg)


def _state_to_blockdiag(s):
    lead = s.shape[:-3]
    st = jnp.swapaxes(s, -1, -2).reshape(lead + (N_PAIRS, 2, DV_C, DK_C))
    z = jnp.zeros_like(st[..., 0, :, :])
    top = jnp.concatenate([st[..., 0, :, :], z], axis=-1)
    bot = jnp.concatenate([z, st[..., 1, :, :]], axis=-1)
    return jnp.concatenate([top, bot], axis=-2)


def _blockdiag_to_state(sb):
    lead = sb.shape[:-3]
    even = sb[..., :DV_C, :DK_C]
    odd = sb[..., DV_C:, DK_C:]
    st = jnp.stack([even, odd], axis=-3).reshape(lead + (H_C, DV_C, DK_C))
    return jnp.swapaxes(st, -1, -2)


def kernel(x_prompt, x_sample, cache_gqa_k, cache_gqa_v, cache_mla_ckv, cache_mla_krope, state_hgrn, c, c_ctx,
           w_mod, b_mod, norm1_g, norm2_g, w_in, a_qnorm, a_knorm, b_qnorm, b_wq, b_kvnorm, b_wkv, c_lb_logits,
           c_onorm, w_branch, w_out, r_group_w, r_group_b, r_expert_w, r_expert_b, e_gate, e_up, e_down, final_g):
    x = (x_prompt.reshape(T_CTX, D), x_sample.reshape(T_LAT, D))
    cvec = jnp.concatenate([c, c_ctx[None, :], jnp.zeros((MOD_ROWS - DEC_BATCH - 1, D), F32)], axis=0)
    mod = _mod_table(cvec, w_mod, b_mod).reshape(DEPTH * MOD_ROWS, 1, 6 * D)
    taba, tabb, tabk = _rope_tables()
    ones = _ones_block(256, 64)
    lbl = c_lb_logits.reshape(DEPTH, 2, H_C * DK_C)

    vec = lambda g, reps=1: jnp.tile(g, (1, reps))[:, None, :]
    n1, n2 = vec(norm1_g), vec(norm2_g)
    aq, ak, con = vec(a_qnorm, H_A), vec(a_knorm, KV_A), vec(c_onorm, H_C)
    bq, bkv = vec(b_qnorm), vec(b_kvnorm)
    w_in_p, w_gate = _pack_w_in(w_in)
    wq_p, wkv_p = _pack_wq(b_wq), _pack_wkv(b_wkv)
    wbr, wout = w_branch.astype(BF16), w_out.astype(BF16)
    n_pad = LANE - N_GROUPS - N_EXPERTS
    wr_hi, wr_lo = _split_hi_lo(jnp.concatenate([r_group_w, r_expert_w, jnp.zeros((DEPTH, D, n_pad), F32)], axis=-1))
    br = jnp.concatenate([r_group_b, r_expert_b, jnp.zeros((DEPTH, n_pad), F32)], axis=-1)[:, None, :]
    eg, eu, ed = e_gate.astype(BF16), e_up.astype(BF16), e_down.astype(BF16)

    ck = cache_gqa_k.reshape(DEC_BATCH, DEPTH, PAST, KV_A * HD_A)
    cv = cache_gqa_v.reshape(DEC_BATCH, DEPTH, PAST, KV_A * HD_A)
    ckvb = _ctx_kv(cache_mla_ckv.reshape(DEC_BATCH * DEPTH * PAST, KV_RANK), wkv_p)
    ckr = cache_mla_krope.reshape(DEC_BATCH * DEPTH * PAST, ROPE_B)
    zpad = jnp.zeros_like(ckr)
    ckre = jnp.concatenate([ckr, zpad, ckr, zpad], axis=1)
    s0 = jnp.swapaxes(_state_to_blockdiag(state_hgrn), 2, 3)

    new_k, new_v, new_ckv, new_kr, new_s = [], [], [], [], []
    for l in range(DEPTH):
        (qa, kan, ka, va, qb, ckv, kvb, kr, kre, lff, lfb, kff, kfb, qc, vc, sg) = _inproj(
            l, x, mod, n1, w_in_p, aq, ak, bq, wq_p, bkv, wkv_p, lbl, ones, taba, tabb, tabk)
        new_k.append(kan[:T_CTX])
        new_v.append(va[:T_CTX])
        new_ckv.append(ckv[:T_CTX])
        new_kr.append(kr[:T_CTX, :ROPE_B])

        oa, ob = _attention(qa, qb, ka, va, kvb, kre)
        oa, ob = _attention(qa, qb, ka, va, kvb, kre, prev=(oa, ob), cache=(ck, cv, ckvb, ckre), layer=l)

        ocf, ocb, s_ctx = _hgrn(qc, vc, kff, lff, kfb, lfb)
        ocf, ocb, _ = _hgrn(qc, vc, kff, lff, kfb, lfb, s0=s0, prev=(ocf, ocb), layer=l)
        new_s.append(s_ctx)

        x1, h2, comb, rank, carry, tab = _merge(l, x, mod, n1, w_gate, oa, ob, ocf, ocb, sg, con, ones, wbr, wout,
                                                n2, wr_hi, wr_lo, br)
        moe = _moe(l, h2, comb, rank, carry, tab, eg, eu, ed)
        if l < DEPTH - 1:
            x = _residual(l, False, x1, moe, mod, final_g[None, :])

    last = DEPTH - 1
    y_prompt = _residual(last, True, x1, moe, mod, final_g[None, :], 0, T_CTX // TR)
    y_sample = _residual(last, True, x1, moe, mod, final_g[None, :], T_CTX // TR, T_LAT // TR)
    stack = lambda parts, tail: jnp.stack([p.reshape(BATCH, SEQ, -1) for p in parts], axis=1).reshape(
        (BATCH, DEPTH, SEQ) + tail)
    states = _blockdiag_to_state(jnp.swapaxes(jnp.stack(new_s, axis=1), 2, 3))
    return (y_prompt.reshape(BATCH, SEQ, D), y_sample.reshape(DEC_BATCH, DEC_SEQ, D),
            stack(new_k, (KV_A, HD_A)), stack(new_v, (KV_A, HD_A)), stack(new_ckv, (KV_RANK,)),
            stack(new_kr, (ROPE_B,)), states)
```

```python
import functools

import numpy as np
import jax
import jax.numpy as jnp
from jax import lax
from jax.experimental import pallas as pl
from jax.experimental.pallas import tpu as pltpu

D = 1024
BATCH, SEQ = 32, 256
DEC_BATCH, DEC_SEQ = 8, 1024
PAST = 256
DEPTH = 2
GRID_W = 64
THETA = 10000.0
EPS = 1e-6
F_FLOOR = 1e-30
H_A, KV_A, HD_A = 8, 2, 64
H_B, Q_RANK, KV_RANK, NOPE_B, ROPE_B, V_B = 8, 384, 256, 64, 32, 64
H_C, DK_C, DV_C = 8, 64, 64
BRANCH_W = 512
N_GROUPS, E_PER_GROUP, N_EXPERTS, D_EXPERT = 4, 4, 16, 512

T_CTX = BATCH * SEQ
T_LAT = DEC_BATCH * DEC_SEQ
T_ALL = T_CTX + T_LAT
TM = 256
N_TILES = T_ALL // TM
CTX_TILES = T_CTX // TM
LAT_TILES_PER_SEQ = DEC_SEQ // TM
MOD_ROWS = 16
CTX_MOD_ROW = DEC_BATCH
LANE = 128
VMEM_LIMIT = 56 * 1024 * 1024

C_QA, C_KA, C_VA, C_QRA, C_KVA, C_KR = 0, 512, 640, 768, 1152, 1408
C_FF, C_FB, C_QC, C_IC, C_GC, C_END = 1536, 2048, 2560, 3072, 3584, 4096
R_QA, R_KA, R_VA, R_QRA, R_KVA, R_KR = 0, 512, 640, 768, 1152, 1408
R_FF, R_FB, R_QC, R_IC, R_GC, R_GATE, R_END = 1440, 1952, 2464, 2976, 3488, 4000, 7072

F32 = jnp.float32
BF16 = jnp.bfloat16


def _cparams(sem):
    return pltpu.CompilerParams(dimension_semantics=sem, vmem_limit_bytes=VMEM_LIMIT)


def _mod_row(i):
    return jnp.where(i < CTX_TILES, CTX_MOD_ROW, (i - CTX_TILES) // LAT_TILES_PER_SEQ)


def _pos_block(i):
    return jnp.where(i < CTX_TILES, LAT_TILES_PER_SEQ, (i - CTX_TILES) % LAT_TILES_PER_SEQ)


def _split_hi_lo(x):
    hi = x.astype(BF16)
    lo = (x - hi.astype(F32)).astype(BF16)
    return hi, lo


def _group_mean(x2, ones_blk, width):
    n = ones_blk.shape[0]
    outs = []
    for j in range(x2.shape[-1] // n):
        blk = x2[:, j * n:(j + 1) * n]
        hi, lo = _split_hi_lo(blk)
        s = jnp.dot(hi, ones_blk, preferred_element_type=F32) + jnp.dot(lo, ones_blk, preferred_element_type=F32)
        outs.append(s)
    s = outs[0] if len(outs) == 1 else jnp.concatenate(outs, axis=-1)
    return s * (1.0 / width)


def _rope(x, tab_ref, shift, period):
    c, s1, s2 = tab_ref[0], tab_ref[1], tab_ref[2]
    outs = []
    for j in range(x.shape[-1] // period):
        blk = x[:, j * period:(j + 1) * period]
        outs.append(blk * c + pltpu.roll(blk, shift, 1) * s1 + pltpu.roll(blk, period - shift, 1) * s2)
    return outs[0] if len(outs) == 1 else jnp.concatenate(outs, axis=-1)


def _mod_kernel(c_ref, w_ref, b_ref, o_ref):
    c = c_ref[...]
    a = c * jax.nn.sigmoid(c)
    o_ref[...] = jnp.dot(a, w_ref[...], preferred_element_type=F32, precision=lax.Precision.HIGHEST) + b_ref[...]


def _mod_table(cvec, w_mod, b_mod):
    nt = 1024
    return pl.pallas_call(
        _mod_kernel,
        grid=(DEPTH, 6 * D // nt),
        in_specs=[
            pl.BlockSpec((MOD_ROWS, D), lambda l, j: (0, 0)),
            pl.BlockSpec((None, D, nt), lambda l, j: (l, 0, j)),
            pl.BlockSpec((None, 1, nt), lambda l, j: (l, 0, j)),
        ],
        out_specs=pl.BlockSpec((None, MOD_ROWS, nt), lambda l, j: (l, 0, j)),
        out_shape=jax.ShapeDtypeStruct((DEPTH, MOD_ROWS, 6 * D), F32),
        compiler_params=_cparams(("arbitrary", "arbitrary")),
        name="mod_table",
    )(cvec, w_mod, b_mod.reshape(DEPTH, 1, 6 * D))


def _x_pair(x):
    if isinstance(x, tuple):
        return x[0], x[1], 0
    return x, x, T_CTX


def _x_specs(x, tile):
    _, _, lat_off = _x_pair(x)
    ctx = T_CTX // tile
    return [pl.BlockSpec((tile, D), lambda i: (jnp.minimum(i, ctx - 1), 0)),
            pl.BlockSpec((tile, D), lambda i: (jnp.maximum(i - ctx, 0) + lat_off // tile, 0))]


def _x_tile(xc_ref, xl_ref, tile):
    return jnp.where(pl.program_id(0) < T_CTX // tile, xc_ref[...], xl_ref[...])


def _inproj_kernel(layer, xc_ref, xl_ref, mod_ref, n1_ref, w_ref, aq_ref, ak_ref, bq_ref, wq_ref, bkv_ref, wkv_ref,
                   lbl_ref, ones_ref, taba_ref, tabb_ref, tabk_ref,
                   qa_o, kan_o, ka_o, va_o, qb_o, ckv_o, kvb_o, kr_o, kre_o,
                   lff_o, lfb_o, kff_o, kfb_o, qc_o, vc_o, sg_o):
    x = _x_tile(xc_ref, xl_ref, TM)
    mod = mod_ref[...]
    xn = x * lax.rsqrt(jnp.mean(x * x, axis=-1, keepdims=True) + EPS) * n1_ref[...]
    h = (xn * (1.0 + mod[:, D:2 * D]) + mod[:, 0:D]).astype(BF16)
    y_all = jnp.dot(h, w_ref[...], preferred_element_type=F32)

    def y(c0, c1):
        return y_all[:, c0:c1]

    ones = ones_ref[...]

    qa = y(C_QA, C_KA)
    qa = qa * lax.rsqrt(_group_mean(qa * qa, ones, HD_A) + EPS) * aq_ref[...]
    qa_o[...] = (_rope(qa, taba_ref, 16, LANE) * (HD_A ** -0.5)).astype(BF16)
    ka = y(C_KA, C_VA)
    ka = ka * lax.rsqrt(_group_mean(ka * ka, ones[:LANE, :LANE], HD_A) + EPS) * ak_ref[...]
    kan_o[...] = ka
    ka_o[...] = _rope(ka, taba_ref, 16, LANE).astype(BF16)
    va_o[...] = y(C_VA, C_QRA)

    qr = y(C_QRA, C_KVA)
    qr = qr * lax.rsqrt(jnp.mean(qr * qr, axis=-1, keepdims=True) + EPS) * bq_ref[...]
    qb = jnp.dot(qr.astype(BF16), wq_ref[...], preferred_element_type=F32)
    qb_o[...] = (_rope(qb, tabb_ref, 8, 2 * LANE) * ((NOPE_B + ROPE_B) ** -0.5)).astype(BF16)
    kv = y(C_KVA, C_KR)
    ckv = kv * lax.rsqrt(jnp.mean(kv * kv, axis=-1, keepdims=True) + EPS) * bkv_ref[...]
    ckv_o[...] = ckv
    kvb_o[...] = jnp.dot(ckv.astype(BF16), wkv_ref[...], preferred_element_type=F32).astype(BF16)
    kr = y(C_KR, C_FF)
    kr_o[...] = kr
    kre_o[...] = _rope(kr, tabk_ref, 8, LANE).astype(BF16)

    lbl = lbl_ref[...]
    e = jnp.exp(lbl - jnp.max(lbl, axis=0, keepdims=True))
    p = e / jnp.sum(e, axis=0, keepdims=True)
    lb = p[0] * 0.0
    for i in range(1, layer + 1):
        lb = lb + p[i]
    for d, (c0, lf_o, kf_o) in enumerate(((C_FF, lff_o, kff_o), (C_FB, lfb_o, kfb_o))):
        pre = y(c0, c0 + 512)
        lbd = lb[d:d + 1, :]
        f = jnp.maximum(lbd + (1.0 - lbd) * jax.nn.sigmoid(pre), F_FLOOR)
        lf_o[...] = jnp.log(f)
        kf_o[...] = 1.0 - f
    qc_o[...] = y(C_QC, C_IC).astype(BF16)
    vc_o[...] = y(C_IC, C_GC).astype(BF16)
    gc = y(C_GC, C_END)
    sg_o[...] = (gc * jax.nn.sigmoid(gc)).astype(BF16)


def _const_spec(shape):
    nd = len(shape)
    return pl.BlockSpec(shape, lambda i: (0,) * nd)


def _tile_spec(width):
    return pl.BlockSpec((TM, width), lambda i: (i, 0))


def _layer_spec(tail, layer):
    return pl.BlockSpec((None,) + tuple(tail), lambda *_: (layer,) + (0,) * len(tail))


def _mod_spec(layer, row_of_tile):
    return pl.BlockSpec((None, 1, 6 * D), lambda i, *_: (layer * MOD_ROWS + row_of_tile(i), 0, 0))


def _inproj(layer, x, mod, n1, w_in_p, aq, ak, bq, wq_p, bkv, wkv_p, lbl, ones, taba, tabb, tabk):
    outs = [
        (512, BF16), (128, F32), (128, BF16), (128, F32), (1024, BF16), (256, F32), (1024, BF16),
        (128, F32), (128, BF16), (512, F32), (512, F32), (512, F32), (512, F32), (512, BF16), (512, BF16),
        (512, BF16),
    ]
    tab_spec = lambda w: pl.BlockSpec((3, TM, w), lambda i: (0, _pos_block(i), 0))
    return pl.pallas_call(
        functools.partial(_inproj_kernel, layer),
        grid=(N_TILES,),
        in_specs=_x_specs(x, TM) + [
            _mod_spec(layer, _mod_row),
            _layer_spec((1, D), layer),
            _layer_spec((D, C_END), layer),
            _layer_spec((1, 512), layer), _layer_spec((1, 128), layer), _layer_spec((1, Q_RANK), layer),
            _layer_spec((Q_RANK, 1024), layer), _layer_spec((1, KV_RANK), layer), _layer_spec((KV_RANK, 1024), layer),
            _const_spec((DEPTH, 2, 512)), _const_spec((256, 256)),
            tab_spec(LANE), tab_spec(2 * LANE), tab_spec(LANE),
        ],
        out_specs=[_tile_spec(w) for w, _ in outs],
        out_shape=[jax.ShapeDtypeStruct((T_ALL, w), dt) for w, dt in outs],
        compiler_params=_cparams(("parallel",)),
        name="inproj",
    )(*_x_pair(x)[:2], mod, n1, w_in_p, aq, ak, bq, wq_p, bkv, wkv_p, lbl, ones, taba, tabb, tabk)


def _pack_w_in(w):
    w = w.astype(BF16)
    z = jnp.zeros((DEPTH, D, 32), BF16)
    kr = w[..., R_KR:R_FF]
    main = jnp.concatenate([w[..., :R_KR], kr, z, kr, z, w[..., R_FF:R_GATE]], axis=-1)
    return main, w[..., R_GATE:]


def _pack_wq(wq):
    w = wq.reshape(DEPTH, Q_RANK, H_B, NOPE_B + ROPE_B)
    nope, rope = w[..., :NOPE_B], w[..., NOPE_B:]
    z = jnp.zeros((DEPTH, Q_RANK, H_B, 32), wq.dtype)
    even = jnp.concatenate([rope, z, nope], axis=-1)
    odd = jnp.concatenate([nope, rope, z], axis=-1)
    is_even = (jnp.arange(H_B) % 2 == 0)[None, None, :, None]
    return jnp.where(is_even, even, odd).reshape(DEPTH, Q_RANK, H_B * LANE).astype(BF16)


def _pack_wkv(wkv):
    w = wkv.reshape(DEPTH, KV_RANK, H_B, NOPE_B + V_B)
    nope, v = w[..., :NOPE_B], w[..., NOPE_B:]
    is_even = (jnp.arange(H_B) % 2 == 0)[None, None, :, None]
    return jnp.where(is_even, jnp.concatenate([v, nope], -1), jnp.concatenate([nope, v], -1)).reshape(
        DEPTH, KV_RANK, H_B * LANE).astype(BF16)


def _rope_tables():
    pos = np.arange(DEC_SEQ)
    row, col = pos // GRID_W, pos % GRID_W

    def pattern(half):
        quarter = half // 2
        inv = THETA ** (-np.arange(0, half, 2, dtype=np.float64) / half)
        ang = np.concatenate([row[:, None] * inv, row[:, None] * inv, col[:, None] * inv, col[:, None] * inv], 1)
        is_x2 = np.tile(np.concatenate([np.zeros(quarter), np.ones(quarter)]), 2)[None, :]
        c = np.cos(ang)
        s1 = np.sin(ang) * is_x2
        s2 = -np.sin(ang) * (1 - is_x2)
        return c, s1, s2

    def assemble(width, spans, half):
        c, s1, s2 = pattern(half)
        tc = np.ones((DEC_SEQ + TM, width))
        t1 = np.zeros((DEC_SEQ + TM, width))
        t2 = np.zeros((DEC_SEQ + TM, width))
        for start in spans:
            tc[:DEC_SEQ, start:start + 2 * half] = c
            t1[:DEC_SEQ, start:start + 2 * half] = s1
            t2[:DEC_SEQ, start:start + 2 * half] = s2
        return jnp.asarray(np.stack([tc, t1, t2]), F32)

    taba = assemble(LANE, (0, 64), 32)
    tabb = assemble(2 * LANE, (0, 128 + 64), 16)
    tabk = assemble(LANE, (0, 64), 16)
    return taba, tabb, tabk


def _ones_block(n, width):
    g = np.arange(n) // width
    return jnp.asarray(g[:, None] == g[None, :], BF16)


def _ctxkv_kernel(c_ref, w_ref, o_ref):
    o_ref[...] = jnp.dot(c_ref[...].astype(BF16), w_ref[...], preferred_element_type=F32).astype(BF16)


def _ctx_kv(ckv_cache, wkv_p):
    rows = ckv_cache.shape[0]
    return pl.pallas_call(
        _ctxkv_kernel,
        grid=(rows // PAST,),
        in_specs=[pl.BlockSpec((PAST, KV_RANK), lambda i: (i, 0)),
                  pl.BlockSpec((None, KV_RANK, 1024), lambda i: (i % DEPTH, 0, 0))],
        out_specs=pl.BlockSpec((PAST, 1024), lambda i: (i, 0)),
        out_shape=jax.ShapeDtypeStruct((rows, 1024), BF16),
        compiler_params=_cparams(("parallel",)),
        name="ctx_kv",
    )(ckv_cache, wkv_p)


_NT = (((1,), (1,)), ((), ()))


def _den_lane(parity):
    return 64 if parity == 0 else 0


def _softmax_pv(s, v, parity, mxu_den):
    m = s.max(axis=-1, keepdims=True)
    if not mxu_den:
        p = jnp.exp(s - m)
        return jnp.dot(p.astype(BF16), v, preferred_element_type=F32) / p.sum(axis=-1, keepdims=True)
    lane = lax.broadcasted_iota(jnp.int32, (1, LANE), 1)
    keep = (lane < 64) if parity == 0 else (lane >= 64)
    o = jnp.dot(jnp.exp((s - m).astype(BF16)), v, preferred_element_type=F32)
    return jnp.where(keep, o / o[:, _den_lane(parity):_den_lane(parity) + 1], 0.0)


def _attn_kernel(n_pieces, qa_ref, qb_ref, *refs):
    kv_refs = refs[:4 * n_pieces]
    oa_ref, ob_ref = refs[4 * n_pieces:4 * n_pieces + 2]
    mxu_den = n_pieces == 2
    lane = lax.broadcasted_iota(jnp.int32, (1, LANE), 1)
    lo = lane < 64
    hi = jnp.logical_not(lo)

    def rows(parts):
        return parts[0] if len(parts) == 1 else jnp.concatenate(parts, axis=0)

    def with_den(v, parity):
        return jnp.where(lane == _den_lane(parity), jnp.ones_like(v), v) if mxu_den else v

    def slabs_a():
        ka = [kv_refs[4 * i][...].astype(F32) for i in range(n_pieces)]
        va = [kv_refs[4 * i + 1][...].astype(F32) for i in range(n_pieces)]

        def place(x, g, parity):
            if g != parity:
                x = pltpu.roll(x, 64, 1)
            return jnp.where(lo if parity == 0 else hi, x, 0.0).astype(BF16)

        return {2 * g + parity: (rows([place(k, g, parity) for k in ka]),
                                 with_den(rows([place(v, g, parity) for v in va]), parity))
                for g in range(KV_A) for parity in range(2)}

    def slabs_b():
        out = {}
        kre = rows([kv_refs[4 * i + 3][...].astype(BF16) for i in range(n_pieces)])
        for h in range(H_B):
            parity = h % 2
            nope = hi if parity == 0 else lo
            kvb = rows([kv_refs[4 * i + 2][:, h * LANE:(h + 1) * LANE] for i in range(n_pieces)])
            out[h] = (jnp.where(nope, kvb, kre), with_den(jnp.where(nope, jnp.zeros_like(kvb), kvb), parity))
        return out

    sa, sb = slabs_a(), slabs_b()
    get_a, get_b = sa.__getitem__, sb.__getitem__

    for pair in range(H_A // 2):
        g = (2 * pair) // (H_A // KV_A)
        q = qa_ref[:, pair * LANE:(pair + 1) * LANE]
        acc = None
        for parity in range(2):
            k, v = get_a(2 * g + parity)
            o = _softmax_pv(lax.dot_general(q, k, _NT, preferred_element_type=F32), v, parity, mxu_den)
            acc = o if acc is None else acc + o
        oa_ref[:, pair * LANE:(pair + 1) * LANE] = acc.astype(BF16)

    for pair in range(H_B // 2):
        acc = None
        for parity in range(2):
            h = 2 * pair + parity
            k, v = get_b(h)
            q = qb_ref[:, h * LANE:(h + 1) * LANE]
            o = _softmax_pv(lax.dot_general(q, k, _NT, preferred_element_type=F32), v, parity, mxu_den)
            acc = o if acc is None else acc + o
        ob_ref[:, pair * LANE:(pair + 1) * LANE] = acc.astype(BF16)


def _attention(qa, qb, ka, va, kvb, kre, prev=None, cache=None, layer=0):
    if cache is None:
        nb, nqt, nk, q_blk0, k_blk0 = BATCH, 1, SEQ, 0, 0
    else:
        nb, nqt, nk, q_blk0, k_blk0 = DEC_BATCH, DEC_SEQ // TM, DEC_SEQ, CTX_TILES, T_CTX // DEC_SEQ
    qspec = lambda w: pl.BlockSpec((TM, w), lambda b, j: (q_blk0 + b * nqt + j, 0))
    kspec = lambda w: pl.BlockSpec((nk, w), lambda b, j: (k_blk0 + b, 0))
    in_specs = [qspec(512), qspec(1024), kspec(128), kspec(128), kspec(1024), kspec(128)]
    args = [qa, qb, ka, va, kvb, kre]
    n_pieces = 1
    aliases = {}
    if cache is not None:
        cspec4 = pl.BlockSpec((None, None, PAST, 128), lambda b, j: (b, layer, 0, 0))
        cspec = lambda w: pl.BlockSpec((PAST, w), lambda b, j: (b * DEPTH + layer, 0))
        in_specs += [cspec4, cspec4, cspec(1024), cspec(128)]
        args += list(cache)
        n_pieces = 2
        in_specs += [pl.BlockSpec(memory_space=pl.ANY)] * 2
        args += list(prev)
        aliases = {len(args) - 2: 0, len(args) - 1: 1}

    def body(*refs):
        if cache is not None:
            n_in = 2 + 4 * n_pieces
            refs = refs[:n_in] + refs[n_in + 2:]
        _attn_kernel(n_pieces, *refs)

    return pl.pallas_call(
        body,
        grid=(nb, nqt),
        in_specs=in_specs,
        out_specs=[qspec(512), qspec(512)],
        out_shape=[jax.ShapeDtypeStruct((T_ALL, 512), BF16)] * 2,
        input_output_aliases=aliases,
        compiler_params=_cparams(("parallel", "arbitrary")),
        name="attention_ctx" if cache is None else "attention_lat",
    )(*args)


HL = 256
HS = 128
HG = 64
N_PAIRS = H_C // 2
FAST_DECAY_LIMIT = 80.0


def _hgrn_bottom_exact(q, k, c, lo, rev):
    row = lax.broadcasted_iota(jnp.int32, (HS, LANE), 0)
    srow = lax.broadcasted_iota(jnp.int32, (HS, HS), 0)
    scol = lax.broadcasted_iota(jnp.int32, (HS, HS), 1)
    out = []
    for parity in range(2):
        def dup(x):
            xs = pltpu.roll(x, 64, 1)
            return jnp.where(lo, x, xs) if parity == 0 else jnp.where(lo, xs, x)
        qd, kd, bd = dup(q), dup(k), dup(c)

        dg = row & 3
        if rev:
            dg = 3 - dg
        e = [None]
        for delta in range(1, 4):
            shifted = pltpu.roll(bd, delta if rev else HS - delta, 0)
            e.append(jnp.exp(jnp.minimum(shifted - bd, 0.0)))
        qp, kp = [], []
        for c1, c2 in ((0, 1), (2, 3)):
            cv = jnp.where(lo, c1, c2)
            dl = cv - dg
            fac = jnp.where(dl == 0, 1.0, jnp.where(dl == 1, e[1], jnp.where(dl == 2, e[2],
                            jnp.where(dl == 3, e[3], 0.0))))
            kp.append((kd * fac).astype(BF16))
            qp.append(jnp.where(dg == cv, qd, 0.0).astype(BF16))
        s = lax.dot_general(jnp.concatenate(qp, axis=1), jnp.concatenate(kp, axis=1), _NT,
                            preferred_element_type=F32)
        tot = jnp.where((srow >> 2) == (scol >> 2), s, 0.0)

        for lev in range(1, 3):
            g = 4 ** lev
            par = 4 * g
            shape3 = (HS // par, par, LANE)
            rid = lax.broadcasted_iota(jnp.int32, shape3, 1)
            dg3 = rid >> (2 * lev)
            if rev:
                dg3 = 3 - dg3
            b3, q3, k3 = bd.reshape(shape3), qd.reshape(shape3), kd.reshape(shape3)
            lo3 = lo.reshape(1, 1, LANE)
            qp, kp = [], []
            for c1, c2 in ((1, 2), (3, None)):
                idx = lambda cc: (4 - cc) * g if rev else cc * g - 1
                i1 = idx(c1)
                i2 = idx(c2) if c2 is not None else i1
                ridx = jnp.where(lo3, i1, i2)
                ref = jnp.sum(jnp.where(rid == ridx, b3, 0.0), axis=1, keepdims=True)
                cvk = jnp.where(lo3, c1, c2 if c2 is not None else 0)
                cvq = jnp.where(lo3, c1, c2 if c2 is not None else -1)
                kk = jnp.where(dg3 < cvk, k3 * jnp.exp(jnp.minimum(ref - b3, 0.0)), 0.0)
                qq = jnp.where(dg3 == cvq, q3 * jnp.exp(jnp.minimum(b3 - ref, 0.0)), 0.0)
                kp.append(kk.reshape(HS, LANE).astype(BF16))
                qp.append(qq.reshape(HS, LANE).astype(BF16))
            s = lax.dot_general(jnp.concatenate(qp, axis=1), jnp.concatenate(kp, axis=1), _NT,
                                preferred_element_type=F32)
            sh = 2 * lev + 2
            tot = tot + jnp.where((srow >> sh) == (scol >> sh), s, 0.0)
        out.append(tot)
    return out


def _hgrn_head(q, k, lf, v, st_ref, rev):
    row = lax.broadcasted_iota(jnp.int32, (HS, LANE), 0)
    lane = lax.broadcasted_iota(jnp.int32, (1, LANE), 1)
    lo = lane < 64
    hi = jnp.logical_not(lo)
    in_g1 = row >= HG

    grow = row & (HG - 1)
    c = lf
    d = 1
    while d < HG:
        if rev:
            c = c + jnp.where(grow < HG - d, pltpu.roll(c, HS - d, 0), 0.0)
        else:
            c = c + jnp.where(grow >= d, pltpu.roll(c, d, 0), 0.0)
        d *= 2
    if rev:
        t0, t1 = c[0:1, :], c[HG:HG + 1, :]
    else:
        t0, t1 = c[HG - 1:HG, :], c[HS - 1:HS, :]
    et0, et1 = jnp.exp(t0), jnp.exp(t1)
    qe = q * jnp.exp(c)
    e_out = jnp.exp(jnp.where(in_g1, t1, t0) - c)
    ke = k * e_out

    if rev:
        qb = qe * jnp.where(in_g1, 1.0, et1)
        kh = ke * jnp.where(in_g1, et0, 1.0)
    else:
        qb = qe * jnp.where(in_g1, et0, 1.0)
        kh = ke * jnp.where(in_g1, 1.0, et1)
    st = st_ref[...]
    o_int = lax.dot_general(qb.astype(BF16), st.astype(BF16), _NT, preferred_element_type=F32)
    upd = lax.dot_general(v, kh.astype(BF16), (((0,), (0,)), ((), ())), preferred_element_type=F32)
    r128 = lax.broadcasted_iota(jnp.int32, (LANE, LANE), 0)
    c128 = lax.broadcasted_iota(jnp.int32, (LANE, LANE), 1)
    st_ref[...] = st * (et0 * et1) + jnp.where((r128 >> 6) == (c128 >> 6), upd, 0.0)

    q_late = in_g1 if not rev else jnp.logical_not(in_g1)
    q_top = jnp.where(q_late, qe, 0.0)
    k_top = jnp.where(q_late, 0.0, ke).astype(BF16)
    top = [lax.dot_general(jnp.where(m, q_top, 0.0).astype(BF16), k_top, _NT, preferred_element_type=F32)
           for m in (lo, hi)]
    mid = HG // 2 if rev else HG // 2 - 1
    cm = c - jnp.where(in_g1, c[HG + mid:HG + mid + 1, :], c[mid:mid + 1, :])
    return c, top, o_int, cm, jnp.max(jnp.abs(cm))


def _hgrn_steps(jobs, bot_ref):
    lane = lax.broadcasted_iota(jnp.int32, (1, LANE), 1)
    lo = lane < 64
    hi = jnp.logical_not(lo)
    heads = [_hgrn_head(*job) for job in jobs]
    fast = functools.reduce(jnp.maximum, [h[4] for h in heads]) <= FAST_DECAY_LIMIT

    @pl.when(fast)
    def _():
        srow = lax.broadcasted_iota(jnp.int32, (HS, HS), 0)
        scol = lax.broadcasted_iota(jnp.int32, (HS, HS), 1)
        same = (srow >> 6) == (scol >> 6)
        for ji, (job, (_, _, _, cm, _)) in enumerate(zip(jobs, heads)):
            keep = same & ((scol >= srow) if job[5] else (scol <= srow))
            qf = job[0] * jnp.exp(cm)
            kf = (job[1] * jnp.exp(-cm)).astype(BF16)
            for parity, m in enumerate((lo, hi)):
                s = lax.dot_general(jnp.where(m, qf, 0.0).astype(BF16), kf, _NT, preferred_element_type=F32)
                bot_ref[ji, parity] = jnp.where(keep, s, 0.0)

    @pl.when(jnp.logical_not(fast))
    def _():
        for ji, (job, (c, _, _, _, _)) in enumerate(zip(jobs, heads)):
            for parity, s in enumerate(_hgrn_bottom_exact(job[0], job[1], c, lo, job[5])):
                bot_ref[ji, parity] = s

    outs = []
    for ji, (job, (_, top, o_int, _, _)) in enumerate(zip(jobs, heads)):
        v = job[3]
        probs = jnp.concatenate([(bot_ref[ji, 0] + top[0]).astype(BF16), (bot_ref[ji, 1] + top[1]).astype(BF16)],
                                axis=1)
        vv = jnp.concatenate([jnp.where(lo, v, jnp.zeros_like(v)), jnp.where(hi, v, jnp.zeros_like(v))], axis=0)
        outs.append(jnp.dot(probs, vv, preferred_element_type=F32) + o_int)
    return outs


PAIRS_PER_ITER = 2


def _hgrn_kernel(has_s0, nt, *refs):
    if has_s0:
        (qf_ref, vf_ref, kf_ref, lf_ref, qb_ref, vb_ref, kb_ref, lb_ref, s0_ref,
         of_ref, ob_ref, so_ref, st_scr, bot_scr) = refs
    else:
        (qf_ref, vf_ref, kf_ref, lf_ref, qb_ref, vb_ref, kb_ref, lb_ref,
         of_ref, ob_ref, so_ref, st_scr, bot_scr) = refs
    j = pl.program_id(1)

    @pl.when(j == 0)
    def _():
        if has_s0:
            st_scr[...] = s0_ref[...]
        else:
            st_scr[...] = jnp.zeros_like(st_scr)

    n_sub = HL // HS

    def pair_body(it, carry):
        for step in range(n_sub):
            jobs, dests = [], []
            for pp in range(PAIRS_PER_ITER):
                p = it * PAIRS_PER_ITER + pp
                cols = pl.ds(pl.multiple_of(p * LANE, LANE), LANE)
                rf = pl.ds(step * HS, HS)
                rb = pl.ds((n_sub - 1 - step) * HS, HS)
                jobs.append((qf_ref[rf, cols].astype(F32), kf_ref[rf, cols], lf_ref[rf, cols], vf_ref[rf, cols],
                             st_scr.at[p, 0], False))
                dests.append((of_ref, rf, cols))
                jobs.append((qb_ref[rb, cols].astype(F32), kb_ref[rb, cols], lb_ref[rb, cols], vb_ref[rb, cols],
                             st_scr.at[p, 1], True))
                dests.append((ob_ref, rb, cols))
            for (ref, rows, cols), o in zip(dests, _hgrn_steps(jobs, bot_scr)):
                ref[rows, cols] = o
        return carry

    lax.fori_loop(0, N_PAIRS // PAIRS_PER_ITER, pair_body, 0)

    @pl.when(j == nt - 1)
    def _():
        so_ref[...] = st_scr[...]


def _hgrn(qc, vc, kff, lff, kfb, lfb, s0=None, prev=None, layer=0):
    if s0 is None:
        nb, nt, blk0 = BATCH, SEQ // HL, 0
    else:
        nb, nt, blk0 = DEC_BATCH, DEC_SEQ // HL, T_CTX // HL
    fspec = pl.BlockSpec((HL, 512), lambda b, j: (blk0 + b * nt + j, 0))
    bspec = pl.BlockSpec((HL, 512), lambda b, j: (blk0 + b * nt + nt - 1 - j, 0))
    sspec = pl.BlockSpec((None, N_PAIRS, 2, LANE, LANE), lambda b, j: (b, 0, 0, 0, 0))
    in_specs = [fspec] * 4 + [bspec] * 4
    args = [qc, vc, kff, lff, qc, vc, kfb, lfb]
    aliases = {}
    if s0 is not None:
        in_specs += [pl.BlockSpec((None, None, N_PAIRS, 2, LANE, LANE), lambda b, j: (b, layer, 0, 0, 0, 0))]
        args += [s0]
        in_specs += [pl.BlockSpec(memory_space=pl.ANY)] * 2
        args += list(prev)
        aliases = {len(args) - 2: 0, len(args) - 1: 1}

    def body(*refs):
        if s0 is not None:
            refs = refs[:9] + refs[11:]
        _hgrn_kernel(s0 is not None, nt, *refs)

    return pl.pallas_call(
        body,
        grid=(nb, nt),
        in_specs=in_specs,
        out_specs=[fspec, bspec, sspec],
        out_shape=[jax.ShapeDtypeStruct((T_ALL, 512), F32), jax.ShapeDtypeStruct((T_ALL, 512), F32),
                   jax.ShapeDtypeStruct((nb, N_PAIRS, 2, LANE, LANE), F32)],
        scratch_shapes=[pltpu.VMEM((N_PAIRS, 2, LANE, LANE), F32),
                        pltpu.VMEM((2 * PAIRS_PER_ITER, 2, HS, HS), F32)],
        input_output_aliases=aliases,
        compiler_params=_cparams(("parallel", "arbitrary")),
        name="hgrn_ctx" if s0 is None else "hgrn_lat",
    )(*args)


def _merge_kernel(xc_ref, xl_ref, mod_ref, n1_ref, wg_ref, oa_ref, ob_ref, ocf_ref, ocb_ref, sg_ref, con_ref,
                  ones_ref, wbr_ref, wout_ref, n2_ref, wrh_ref, wrl_ref, br_ref, tri_ref, upper_ref,
                  x1_o, h2_o, comb_o, rank_o, carry_o, tab_o, carry_scr):
    x = _x_tile(xc_ref, xl_ref, TMG)
    mod = mod_ref[...]
    xn = x * lax.rsqrt(jnp.mean(x * x, axis=-1, keepdims=True) + EPS) * n1_ref[...]
    h = (xn * (1.0 + mod[:, D:2 * D]) + mod[:, 0:D]).astype(BF16)

    oc = ocf_ref[...] + ocb_ref[...]
    oc = oc * lax.rsqrt(_group_mean(oc * oc, ones_ref[...], DV_C) + EPS) * con_ref[...]
    oc = (oc * sg_ref[...].astype(F32)).astype(BF16)
    branches = (oa_ref[...], ob_ref[...], oc)
    mix = None
    for jb in range(3):
        gate = jax.nn.sigmoid(jnp.dot(h, wg_ref[:, jb * D:(jb + 1) * D], preferred_element_type=F32))
        t = gate * jnp.dot(branches[jb], wbr_ref[jb], preferred_element_type=F32)
        mix = t if mix is None else mix + t
    out = jnp.dot(mix.astype(BF16), wout_ref[...], preferred_element_type=F32)
    x1 = x + mod[:, 2 * D:3 * D] * out
    x1_o[...] = x1

    x1n = x1 * lax.rsqrt(jnp.mean(x1 * x1, axis=-1, keepdims=True) + EPS) * n2_ref[...]
    h2 = x1n * (1.0 + mod[:, 4 * D:5 * D]) + mod[:, 3 * D:4 * D]
    h2_o[...] = h2.astype(BF16)

    h2h, h2l = _split_hi_lo(h2)
    logits = (jnp.dot(h2h, wrh_ref[...], preferred_element_type=F32)
              + jnp.dot(h2l, wrh_ref[...], preferred_element_type=F32)
              + jnp.dot(h2h, wrl_ref[...], preferred_element_type=F32)) + br_ref[...]
    lane = lax.broadcasted_iota(jnp.int32, logits.shape, 1)
    neg = -jnp.inf
    is_g = lane < N_GROUPS
    gl = jnp.where(is_g, logits, neg)
    gmax = gl.max(axis=-1, keepdims=True)
    gidx = jnp.min(jnp.where(gl == gmax, lane, LANE), axis=-1, keepdims=True)
    gp = 1.0 / jnp.sum(jnp.where(is_g, jnp.exp(gl - gmax), 0.0), axis=-1, keepdims=True)
    eid = lane - N_GROUPS
    in_grp = (eid >= 0) & (eid < N_EXPERTS) & ((eid >> 2) == gidx)
    el = jnp.where(in_grp, logits, neg)
    v1 = el.max(axis=-1, keepdims=True)
    i1 = jnp.min(jnp.where(el == v1, lane, LANE), axis=-1, keepdims=True)
    el2 = jnp.where(lane == i1, neg, el)
    v2 = el2.max(axis=-1, keepdims=True)
    i2 = jnp.min(jnp.where(el2 == v2, lane, LANE), axis=-1, keepdims=True)
    e2 = jnp.exp(v2 - v1)
    w1 = gp / (1.0 + e2)
    w2 = gp * e2 / (1.0 + e2)
    comb = jnp.where(lane == i1, w1, 0.0) + jnp.where(lane == i2, w2, 0.0)
    comb = pltpu.roll(comb, LANE - N_GROUPS, 1)
    comb_o[...] = comb

    step = pl.program_id(0) % MOE_STEPS

    @pl.when(step == 0)
    def _():
        carry_scr[...] = jnp.zeros_like(carry_scr)

    for sb in range(TMG // SBK):
        sub = comb[sb * SBK:(sb + 1) * SBK, :]
        routed = sub > 0.0
        ind = jnp.where(routed, 1.0, 0.0)
        carry = carry_scr[...]
        rank = jnp.dot(tri_ref[...], ind.astype(BF16), preferred_element_type=F32) + carry
        rank_o[sb * SBK:(sb + 1) * SBK, :] = jnp.where(routed, rank, -1.0)
        carry_o[sb] = carry
        carry_scr[...] = carry + jnp.sum(ind, axis=0, keepdims=True)

    @pl.when(step == MOE_STEPS - 1)
    def _():
        count = carry_scr[...]
        seg = jnp.floor((count + (SEG_ALIGN - 1.0)) * (1.0 / SEG_ALIGN)) * SEG_ALIGN
        off = jnp.dot(jnp.broadcast_to(seg, (8, LANE)), upper_ref[...], preferred_element_type=F32,
                      precision=lax.Precision.HIGHEST)
        tab_o[0:1, :] = count
        tab_o[1:2, :] = off[0:1, :]


TMG = 512


def _merge(layer, x, mod, n1, wgate, oa, ob, ocf, ocb, sg, con, ones, wbr, wout, n2, wr_hi, wr_lo, br):
    ctx_tiles, per_seq = T_CTX // TMG, DEC_SEQ // TMG
    mrow = lambda i: jnp.where(i < ctx_tiles, CTX_MOD_ROW, (i - ctx_tiles) // per_seq)
    tspec = lambda w: pl.BlockSpec((TMG, w), lambda i: (i, 0))
    wspec = lambda tail: pl.BlockSpec((None,) + tail, lambda i: (layer,) + (0,) * len(tail),
                                      pipeline_mode=pl.Buffered(1))
    return pl.pallas_call(
        _merge_kernel,
        grid=(T_ALL // TMG,),
        in_specs=_x_specs(x, TMG) + [
            _mod_spec(layer, mrow),
            _layer_spec((1, D), layer), wspec((D, 3 * D)),
            tspec(512), tspec(512), tspec(512), tspec(512), tspec(512),
            _layer_spec((1, 512), layer), _const_spec((256, 256)),
            wspec((3, BRANCH_W, D)), wspec((D, D)), _layer_spec((1, D), layer),
            _layer_spec((D, LANE), layer), _layer_spec((D, LANE), layer), _layer_spec((1, LANE), layer),
            _const_spec((SBK, SBK)), _const_spec((LANE, LANE)),
        ],
        out_specs=[
            tspec(D), tspec(D), tspec(LANE), tspec(LANE),
            pl.BlockSpec((None, TMG // SBK, 1, LANE), lambda i: (i // MOE_STEPS, i % MOE_STEPS, 0, 0)),
            pl.BlockSpec((None, 2, LANE), lambda i: (i // MOE_STEPS, 0, 0)),
        ],
        out_shape=[jax.ShapeDtypeStruct((T_ALL, D), F32), jax.ShapeDtypeStruct((T_ALL, D), BF16),
                   jax.ShapeDtypeStruct((T_ALL, LANE), F32), jax.ShapeDtypeStruct((T_ALL, LANE), F32),
                   jax.ShapeDtypeStruct((N_BLK, N_SB, 1, LANE), F32), jax.ShapeDtypeStruct((N_BLK, 2, LANE), F32)],
        scratch_shapes=[pltpu.VMEM((1, LANE), F32)],
        compiler_params=_cparams(("arbitrary",)),
        name="merge",
    )(*_x_pair(x)[:2], mod, n1, wgate, oa, ob, ocf, ocb, sg, con, ones, wbr, wout, n2, wr_hi, wr_lo, br,
      jnp.asarray(np.tril(np.ones((SBK, SBK)), -1), BF16), jnp.asarray(np.triu(np.ones((LANE, LANE)), 1), F32))


NB = 2048
N_BLK = T_ALL // NB
SBK = 256
N_SB = NB // SBK
WIN_SHIFT, FT_SHIFT, SEG_SHIFT = 6, 7, 4
WIN = 1 << WIN_SHIFT
FT = 1 << FT_SHIFT
SEG_ALIGN = 1 << SEG_SHIFT
STG = 2 * NB + N_EXPERTS * SEG_ALIGN + 256
QUAD = 4
E_STEP = 2
MOE_STEPS = NB // TMG


def _moe_kernel(cnt_s, off_s, car_s, h2_ref, rank_ref, comb_ref, offv_ref, eg_ref, eu_ref, ed_ref, o_ref,
                stg_ref, acc_ref):
    blk = pl.program_id(0)
    step = pl.program_id(1)
    srow = lax.broadcasted_iota(jnp.int32, (WIN, SBK), 0).astype(F32)

    def windows(s):
        out = []
        for ex in range(N_EXPERTS):
            start = off_s[blk, ex] + car_s[blk, s, ex]
            length = car_s[blk, s + 1, ex] - car_s[blk, s, ex]
            ws = (start >> SEG_SHIFT) << SEG_SHIFT
            out.append((ws, (start - ws + length + (WIN - 1)) >> WIN_SHIFT))
        return out, functools.reduce(jnp.maximum, [w[1] for w in out])

    def positions(s):
        rows = pl.ds(pl.multiple_of(s * SBK, SBK), SBK)
        rank = rank_ref[rows, :]
        return rows, jnp.where(rank >= 0.0, rank + offv_ref[1:2, :], -1.0e6)

    @pl.when(step == 0)
    def _():
        stg_ref[...] = jnp.zeros_like(stg_ref)

        def sub_body(s, carry):
            rows, pos = positions(s)
            pos_t = pos.T
            h2 = h2_ref[rows, :]
            wins, nmax = windows(s)

            def chunk_body(c, carry2):
                for quad in range(N_EXPERTS // QUAD):
                    blocks = []
                    for ex in range(quad * QUAD, (quad + 1) * QUAD):
                        base = (wins[ex][0] + c * WIN).astype(F32)
                        hit = (pos_t[ex:ex + 1, :] - base) == srow
                        blocks.append(jnp.where(hit, 1.0, 0.0).astype(BF16))
                    moved = jnp.dot(jnp.concatenate(blocks, axis=0), h2, preferred_element_type=F32).astype(BF16)
                    for i in range(QUAD):
                        first = jnp.minimum(wins[quad * QUAD + i][0] + c * WIN, STG - WIN)
                        dst = pl.ds(pl.multiple_of(first, SEG_ALIGN), WIN)
                        stg_ref[dst, :] = stg_ref[dst, :] + moved[i * WIN:(i + 1) * WIN, :]
                return carry2

            lax.fori_loop(0, nmax, chunk_body, 0)
            return carry

        lax.fori_loop(0, N_SB, sub_body, 0)

    def ffn_tile(j, first, n_rows, n_valid):
        rows = pl.ds(pl.multiple_of(first, SEG_ALIGN), n_rows)
        xs = stg_ref[rows, :]
        hg = jnp.dot(xs, eg_ref[j], preferred_element_type=F32)
        hu = jnp.dot(xs, eu_ref[j], preferred_element_type=F32)
        act = (hg * jax.nn.sigmoid(hg) * hu).astype(BF16)
        y = jnp.dot(act, ed_ref[j], preferred_element_type=F32).astype(BF16)
        rid = lax.broadcasted_iota(jnp.int32, (n_rows, 1), 0)
        stg_ref[rows, :] = jnp.where(rid < n_valid, y, xs)

    for j in range(E_STEP):
        ex = step * E_STEP + j
        count = cnt_s[blk, ex]
        seg0 = off_s[blk, ex]
        n_big = (count + (FT - 1)) >> (FT_SHIFT + 1)

        def ffn_body(t, carry, j=j, count=count, seg0=seg0):
            ffn_tile(j, seg0 + t * (2 * FT), 2 * FT, count - t * (2 * FT))
            return carry

        lax.fori_loop(0, n_big, ffn_body, 0)

        @pl.when(count > n_big * (2 * FT))
        def _(j=j, count=count, seg0=seg0, n_big=n_big):
            ffn_tile(j, seg0 + n_big * (2 * FT), FT, count - n_big * (2 * FT))

    @pl.when(step == N_EXPERTS // E_STEP - 1)
    def _():
        def sub_body(s, carry):
            rows, pos = positions(s)
            pos_t = pos.T
            wts_t = comb_ref[rows, :].T
            wins, nmax = windows(s)
            acc_ref[...] = jnp.zeros_like(acc_ref)

            def chunk_body(c, carry2):
                sel_t, srcs = [], []
                for ex in range(N_EXPERTS):
                    base = (wins[ex][0] + c * WIN).astype(F32)
                    hit = (pos_t[ex:ex + 1, :] - base) == srow
                    sel_t.append(jnp.where(hit, wts_t[ex:ex + 1, :], 0.0).astype(BF16))
                    first = jnp.minimum(wins[ex][0] + c * WIN, STG - WIN)
                    srcs.append(stg_ref[pl.ds(pl.multiple_of(first, SEG_ALIGN), WIN), :])
                acc_ref[...] += lax.dot_general(jnp.concatenate(sel_t, axis=0), jnp.concatenate(srcs, axis=0),
                                                (((0,), (0,)), ((), ())), preferred_element_type=F32)
                return carry2

            lax.fori_loop(0, nmax, chunk_body, 0)
            o_ref[rows, :] = acc_ref[...].astype(BF16)
            return carry

        lax.fori_loop(0, N_SB, sub_body, 0)


def _moe(layer, h2, comb, rank, carry, tab, eg, eu, ed):
    cnt_i = tab[:, 0, :N_EXPERTS].astype(jnp.int32)
    off_i = tab[:, 1, :N_EXPERTS].astype(jnp.int32)
    car_i = jnp.concatenate([carry[:, :, 0, :N_EXPERTS].astype(jnp.int32), cnt_i[:, None, :]], axis=1)
    bspec = lambda w: pl.BlockSpec((NB, w), lambda b, e, *_: (b, 0))
    grid_spec = pltpu.PrefetchScalarGridSpec(
        num_scalar_prefetch=3,
        grid=(N_BLK, N_EXPERTS // E_STEP),
        in_specs=[
            bspec(D), bspec(LANE), bspec(LANE),
            pl.BlockSpec((None, 2, LANE), lambda b, e, *_: (b, 0, 0)),
            pl.BlockSpec((None, E_STEP, D, D_EXPERT), lambda b, e, *_: (layer, e, 0, 0)),
            pl.BlockSpec((None, E_STEP, D, D_EXPERT), lambda b, e, *_: (layer, e, 0, 0)),
            pl.BlockSpec((None, E_STEP, D_EXPERT, D), lambda b, e, *_: (layer, e, 0, 0)),
        ],
        out_specs=bspec(D),
        scratch_shapes=[pltpu.VMEM((STG, D), BF16), pltpu.VMEM((SBK, D), F32)],
    )
    return pl.pallas_call(
        _moe_kernel,
        grid_spec=grid_spec,
        out_shape=jax.ShapeDtypeStruct((T_ALL, D), BF16),
        compiler_params=_cparams(("parallel", "arbitrary")),
        name="moe",
    )(cnt_i, off_i, car_i, h2, rank, comb, tab, eg, eu, ed)


def _residual_kernel(final, x1_ref, moe_ref, mod_ref, fg_ref, o_ref):
    x2 = x1_ref[...] + mod_ref[:, 5 * D:6 * D] * moe_ref[...].astype(F32)
    if final:
        x2 = x2 * lax.rsqrt(jnp.mean(x2 * x2, axis=-1, keepdims=True) + EPS) * fg_ref[...]
    o_ref[...] = x2


TR = DEC_SEQ


def _residual(layer, final, x1, moe, mod, fg, tile0=0, n_tiles=T_ALL // TR):
    ctx_tiles = T_CTX // TR
    mrow = lambda i: jnp.where(tile0 + i < ctx_tiles, CTX_MOD_ROW, tile0 + i - ctx_tiles)
    src = lambda: pl.BlockSpec((TR, D), lambda i: (tile0 + i, 0))
    return pl.pallas_call(
        functools.partial(_residual_kernel, final),
        grid=(n_tiles,),
        in_specs=[src(), src(), _mod_spec(layer, mrow), _const_spec((1, D))],
        out_specs=pl.BlockSpec((TR, D), lambda i: (i, 0)),
        out_shape=jax.ShapeDtypeStruct((n_tiles * TR, D), F32),
        compiler_params=_cparams(("parallel",)),
        name="residual",
    )(x1, moe, mod, fg)


def _state_to_blockdiag(s):
    lead = s.shape[:-3]
    st = jnp.swapaxes(s, -1, -2).reshape(lead + (N_PAIRS, 2, DV_C, DK_C))
    z = jnp.zeros_like(st[..., 0, :, :])
    top = jnp.concatenate([st[..., 0, :, :], z], axis=-1)
    bot = jnp.concatenate([z, st[..., 1, :, :]], axis=-1)
    return jnp.concatenate([top, bot], axis=-2)


def _blockdiag_to_state(sb):
    lead = sb.shape[:-3]
    even = sb[..., :DV_C, :DK_C]
    odd = sb[..., DV_C:, DK_C:]
    st = jnp.stack([even, odd], axis=-3).reshape(lead + (H_C, DV_C, DK_C))
    return jnp.swapaxes(st, -1, -2)


def kernel(x_prompt, x_sample, cache_gqa_k, cache_gqa_v, cache_mla_ckv, cache_mla_krope, state_hgrn, c, c_ctx,
           w_mod, b_mod, norm1_g, norm2_g, w_in, a_qnorm, a_knorm, b_qnorm, b_wq, b_kvnorm, b_wkv, c_lb_logits,
           c_onorm, w_branch, w_out, r_group_w, r_group_b, r_expert_w, r_expert_b, e_gate, e_up, e_down, final_g):
    x = (x_prompt.reshape(T_CTX, D), x_sample.reshape(T_LAT, D))
    cvec = jnp.concatenate([c, c_ctx[None, :], jnp.zeros((MOD_ROWS - DEC_BATCH - 1, D), F32)], axis=0)
    mod = _mod_table(cvec, w_mod, b_mod).reshape(DEPTH * MOD_ROWS, 1, 6 * D)
    taba, tabb, tabk = _rope_tables()
    ones = _ones_block(256, 64)
    lbl = c_lb_logits.reshape(DEPTH, 2, H_C * DK_C)

    vec = lambda g, reps=1: jnp.tile(g, (1, reps))[:, None, :]
    n1, n2 = vec(norm1_g), vec(norm2_g)
    aq, ak, con = vec(a_qnorm, H_A), vec(a_knorm, KV_A), vec(c_onorm, H_C)
    bq, bkv = vec(b_qnorm), vec(b_kvnorm)
    w_in_p, w_gate = _pack_w_in(w_in)
    wq_p, wkv_p = _pack_wq(b_wq), _pack_wkv(b_wkv)
    wbr, wout = w_branch.astype(BF16), w_out.astype(BF16)
    n_pad = LANE - N_GROUPS - N_EXPERTS
    wr_hi, wr_lo = _split_hi_lo(jnp.concatenate([r_group_w, r_expert_w, jnp.zeros((DEPTH, D, n_pad), F32)], axis=-1))
    br = jnp.concatenate([r_group_b, r_expert_b, jnp.zeros((DEPTH, n_pad), F32)], axis=-1)[:, None, :]
    eg, eu, ed = e_gate.astype(BF16), e_up.astype(BF16), e_down.astype(BF16)

    ck = cache_gqa_k.reshape(DEC_BATCH, DEPTH, PAST, KV_A * HD_A)
    cv = cache_gqa_v.reshape(DEC_BATCH, DEPTH, PAST, KV_A * HD_A)
    ckvb = _ctx_kv(cache_mla_ckv.reshape(DEC_BATCH * DEPTH * PAST, KV_RANK), wkv_p)
    ckr = cache_mla_krope.reshape(DEC_BATCH * DEPTH * PAST, ROPE_B)
    zpad = jnp.zeros_like(ckr)
    ckre = jnp.concatenate([ckr, zpad, ckr, zpad], axis=1)
    s0 = jnp.swapaxes(_state_to_blockdiag(state_hgrn), 2, 3)

    new_k, new_v, new_ckv, new_kr, new_s = [], [], [], [], []
    for l in range(DEPTH):
        (qa, kan, ka, va, qb, ckv, kvb, kr, kre, lff, lfb, kff, kfb, qc, vc, sg) = _inproj(
            l, x, mod, n1, w_in_p, aq, ak, bq, wq_p, bkv, wkv_p, lbl, ones, taba, tabb, tabk)
        new_k.append(kan[:T_CTX])
        new_v.append(va[:T_CTX])
        new_ckv.append(ckv[:T_CTX])
        new_kr.append(kr[:T_CTX, :ROPE_B])

        oa, ob = _attention(qa, qb, ka, va, kvb, kre)
        oa, ob = _attention(qa, qb, ka, va, kvb, kre, prev=(oa, ob), cache=(ck, cv, ckvb, ckre), layer=l)

        ocf, ocb, s_ctx = _hgrn(qc, vc, kff, lff, kfb, lfb)
        ocf, ocb, _ = _hgrn(qc, vc, kff, lff, kfb, lfb, s0=s0, prev=(ocf, ocb), layer=l)
        new_s.append(s_ctx)

        x1, h2, comb, rank, carry, tab = _merge(l, x, mod, n1, w_gate, oa, ob, ocf, ocb, sg, con, ones, wbr, wout,
                                                n2, wr_hi, wr_lo, br)
        moe = _moe(l, h2, comb, rank, carry, tab, eg, eu, ed)
        if l < DEPTH - 1:
            x = _residual(l, False, x1, moe, mod, final_g[None, :])

    last = DEPTH - 1
    y_prompt = _residual(last, True, x1, moe, mod, final_g[None, :], 0, T_CTX // TR)
    y_sample = _residual(last, True, x1, moe, mod, final_g[None, :], T_CTX // TR, T_LAT // TR)
    stack = lambda parts, tail: jnp.stack([p.reshape(BATCH, SEQ, -1) for p in parts], axis=1).reshape(
        (BATCH, DEPTH, SEQ) + tail)
    states = _blockdiag_to_state(jnp.swapaxes(jnp.stack(new_s, axis=1), 2, 3))
    return (y_prompt.reshape(BATCH, SEQ, D), y_sample.reshape(DEC_BATCH, DEC_SEQ, D),
            stack(new_k, (KV_A, HD_A)), stack(new_v, (KV_A, HD_A)), stack(new_ckv, (KV_RANK,)),
            stack(new_kr, (ROPE_B,)), states)
```

```python
import functools

import numpy as np
import jax
import jax.numpy as jnp
from jax import lax
from jax.experimental import pallas as pl
from jax.experimental.pallas import tpu as pltpu

D = 1024
BATCH, SEQ = 32, 256
DEC_BATCH, DEC_SEQ = 8, 1024
PAST = 256
DEPTH = 2
GRID_W = 64
THETA = 10000.0
EPS = 1e-6
F_FLOOR = 1e-30
H_A, KV_A, HD_A = 8, 2, 64
H_B, Q_RANK, KV_RANK, NOPE_B, ROPE_B, V_B = 8, 384, 256, 64, 32, 64
H_C, DK_C, DV_C = 8, 64, 64
BRANCH_W = 512
N_GROUPS, E_PER_GROUP, N_EXPERTS, D_EXPERT = 4, 4, 16, 512

T_CTX = BATCH * SEQ
T_LAT = DEC_BATCH * DEC_SEQ
T_ALL = T_CTX + T_LAT
TMI = 512
MOD_ROWS = 16
CTX_MOD_ROW = DEC_BATCH
LANE = 128
VMEM_LIMIT = 56 * 1024 * 1024

C_QA, C_KA, C_VA, C_QRA, C_KVA, C_KR = 0, 512, 640, 768, 1152, 1408
C_FF, C_FB, C_QC, C_IC, C_GC, C_END = 1536, 2048, 2560, 3072, 3584, 4096
R_QA, R_KA, R_VA, R_QRA, R_KVA, R_KR = 0, 512, 640, 768, 1152, 1408
R_FF, R_FB, R_QC, R_IC, R_GC, R_GATE, R_END = 1440, 1952, 2464, 2976, 3488, 4000, 7072

F32 = jnp.float32
BF16 = jnp.bfloat16


def _cparams(sem):
    return pltpu.CompilerParams(dimension_semantics=sem, vmem_limit_bytes=VMEM_LIMIT)


def _split_hi_lo(x):
    hi = x.astype(BF16)
    lo = (x - hi.astype(F32)).astype(BF16)
    return hi, lo


def _group_mean(x2, ones_blk, width):
    n = ones_blk.shape[0]
    outs = []
    for j in range(x2.shape[-1] // n):
        blk = x2[:, j * n:(j + 1) * n]
        hi, lo = _split_hi_lo(blk)
        s = jnp.dot(hi, ones_blk, preferred_element_type=F32) + jnp.dot(lo, ones_blk, preferred_element_type=F32)
        outs.append(s)
    s = outs[0] if len(outs) == 1 else jnp.concatenate(outs, axis=-1)
    return s * (1.0 / width)


def _rope(x, tab_ref, shift, period):
    c, s1, s2 = tab_ref[0], tab_ref[1], tab_ref[2]
    outs = []
    for j in range(x.shape[-1] // period):
        blk = x[:, j * period:(j + 1) * period]
        outs.append(blk * c + pltpu.roll(blk, shift, 1) * s1 + pltpu.roll(blk, period - shift, 1) * s2)
    return outs[0] if len(outs) == 1 else jnp.concatenate(outs, axis=-1)


def _mod_kernel(c_ref, w_ref, b_ref, o_ref):
    c = c_ref[...]
    a = c * jax.nn.sigmoid(c)
    o_ref[...] = jnp.dot(a, w_ref[...], preferred_element_type=F32, precision=lax.Precision.HIGHEST) + b_ref[...]


def _mod_table(cvec, w_mod, b_mod):
    nt = 1024
    return pl.pallas_call(
        _mod_kernel,
        grid=(DEPTH, 6 * D // nt),
        in_specs=[
            pl.BlockSpec((MOD_ROWS, D), lambda l, j: (0, 0)),
            pl.BlockSpec((None, D, nt), lambda l, j: (l, 0, j)),
            pl.BlockSpec((None, 1, nt), lambda l, j: (l, 0, j)),
        ],
        out_specs=pl.BlockSpec((None, MOD_ROWS, nt), lambda l, j: (l, 0, j)),
        out_shape=jax.ShapeDtypeStruct((DEPTH, MOD_ROWS, 6 * D), F32),
        compiler_params=_cparams(("arbitrary", "arbitrary")),
        name="mod_table",
    )(cvec, w_mod, b_mod.reshape(DEPTH, 1, 6 * D))


def _x_pair(x):
    if isinstance(x, tuple):
        return x[0], x[1], 0
    return x, x, T_CTX


def _x_specs(x, tile):
    _, _, lat_off = _x_pair(x)
    ctx = T_CTX // tile
    return [pl.BlockSpec((tile, D), lambda i: (jnp.minimum(i, ctx - 1), 0)),
            pl.BlockSpec((tile, D), lambda i: (jnp.maximum(i - ctx, 0) + lat_off // tile, 0))]


def _x_tile(xc_ref, xl_ref, tile):
    return jnp.where(pl.program_id(0) < T_CTX // tile, xc_ref[...], xl_ref[...])


def _inproj_kernel(layer, xc_ref, xl_ref, mod_ref, n1_ref, w_ref, aq_ref, ak_ref, bq_ref, wq_ref, bkv_ref, wkv_ref,
                   lbl_ref, ones_ref, taba_ref, tabb_ref, tabk_ref,
                   qa_o, kan_o, ka_o, va_o, qb_o, ckv_o, kvb_o, kr_o, kre_o,
                   lff_o, lfb_o, kff_o, kfb_o, qc_o, vc_o, sg_o):
    x = _x_tile(xc_ref, xl_ref, TMI)
    mod = mod_ref[...]
    xn = x * lax.rsqrt(jnp.mean(x * x, axis=-1, keepdims=True) + EPS) * n1_ref[...]
    h = (xn * (1.0 + mod[:, D:2 * D]) + mod[:, 0:D]).astype(BF16)
    y_all = jnp.dot(h, w_ref[...], preferred_element_type=F32)

    def y(c0, c1):
        return y_all[:, c0:c1]

    ones = ones_ref[...]

    qa = y(C_QA, C_KA)
    qa = qa * lax.rsqrt(_group_mean(qa * qa, ones, HD_A) + EPS) * aq_ref[...]
    qa_o[...] = (_rope(qa, taba_ref, 16, LANE) * (HD_A ** -0.5)).astype(BF16)
    ka = y(C_KA, C_VA)
    ka = ka * lax.rsqrt(_group_mean(ka * ka, ones[:LANE, :LANE], HD_A) + EPS) * ak_ref[...]
    kan_o[...] = ka
    ka_o[...] = _rope(ka, taba_ref, 16, LANE).astype(BF16)
    va_o[...] = y(C_VA, C_QRA)

    qr = y(C_QRA, C_KVA)
    qr = qr * lax.rsqrt(jnp.mean(qr * qr, axis=-1, keepdims=True) + EPS) * bq_ref[...]
    qb = jnp.dot(qr.astype(BF16), wq_ref[...], preferred_element_type=F32)
    qb_o[...] = (_rope(qb, tabb_ref, 8, 2 * LANE) * ((NOPE_B + ROPE_B) ** -0.5)).astype(BF16)
    kv = y(C_KVA, C_KR)
    ckv = kv * lax.rsqrt(jnp.mean(kv * kv, axis=-1, keepdims=True) + EPS) * bkv_ref[...]
    ckv_o[...] = ckv
    kvb_o[...] = jnp.dot(ckv.astype(BF16), wkv_ref[...], preferred_element_type=F32).astype(BF16)
    kr = y(C_KR, C_FF)
    kr_o[...] = kr
    kre_o[...] = _rope(kr, tabk_ref, 8, LANE).astype(BF16)

    lbl = lbl_ref[...]
    e = jnp.exp(lbl - jnp.max(lbl, axis=0, keepdims=True))
    p = e / jnp.sum(e, axis=0, keepdims=True)
    lb = p[0] * 0.0
    for i in range(1, layer + 1):
        lb = lb + p[i]
    for d, (c0, lf_o, kf_o) in enumerate(((C_FF, lff_o, kff_o), (C_FB, lfb_o, kfb_o))):
        pre = y(c0, c0 + 512)
        lbd = lb[d:d + 1, :]
        f = jnp.maximum(lbd + (1.0 - lbd) * jax.nn.sigmoid(pre), F_FLOOR)
        lf_o[...] = jnp.log(f)
        kf_o[...] = 1.0 - f
    qc_o[...] = y(C_QC, C_IC).astype(BF16)
    vc_o[...] = y(C_IC, C_GC).astype(BF16)
    gc = y(C_GC, C_END)
    sg_o[...] = (gc * jax.nn.sigmoid(gc)).astype(BF16)


def _const_spec(shape):
    nd = len(shape)
    return pl.BlockSpec(shape, lambda i: (0,) * nd)


def _layer_spec(tail, layer):
    return pl.BlockSpec((None,) + tuple(tail), lambda *_: (layer,) + (0,) * len(tail))


def _mod_spec(layer, row_of_tile):
    return pl.BlockSpec((None, 1, 6 * D), lambda i, *_: (layer * MOD_ROWS + row_of_tile(i), 0, 0))


def _inproj(layer, x, mod, n1, w_in_p, aq, ak, bq, wq_p, bkv, wkv_p, lbl, ones, taba, tabb, tabk):
    outs = [
        (512, BF16), (128, F32), (128, BF16), (128, F32), (1024, BF16), (256, F32), (1024, BF16),
        (128, F32), (128, BF16), (512, F32), (512, F32), (512, F32), (512, F32), (512, BF16), (512, BF16),
        (512, BF16),
    ]
    ctx_tiles, per_seq = T_CTX // TMI, DEC_SEQ // TMI
    mrow = lambda i: jnp.where(i < ctx_tiles, CTX_MOD_ROW, (i - ctx_tiles) // per_seq)
    pos_blk = lambda i: jnp.where(i < ctx_tiles, per_seq, (i - ctx_tiles) % per_seq)
    tab_spec = lambda w: pl.BlockSpec((3, TMI, w), lambda i: (0, pos_blk(i), 0))
    tile_spec = lambda w: pl.BlockSpec((TMI, w), lambda i: (i, 0))
    return pl.pallas_call(
        functools.partial(_inproj_kernel, layer),
        grid=(T_ALL // TMI,),
        in_specs=_x_specs(x, TMI) + [
            _mod_spec(layer, mrow),
            _layer_spec((1, D), layer),
            _layer_spec((D, C_END), layer),
            _layer_spec((1, 512), layer), _layer_spec((1, 128), layer), _layer_spec((1, Q_RANK), layer),
            _layer_spec((Q_RANK, 1024), layer), _layer_spec((1, KV_RANK), layer), _layer_spec((KV_RANK, 1024), layer),
            _const_spec((DEPTH, 2, 512)), _const_spec((256, 256)),
            tab_spec(LANE), tab_spec(2 * LANE), tab_spec(LANE),
        ],
        out_specs=[tile_spec(w) for w, _ in outs],
        out_shape=[jax.ShapeDtypeStruct((T_ALL, w), dt) for w, dt in outs],
        compiler_params=_cparams(("parallel",)),
        name="inproj",
    )(*_x_pair(x)[:2], mod, n1, w_in_p, aq, ak, bq, wq_p, bkv, wkv_p, lbl, ones, taba, tabb, tabk)


def _pack_w_in(w):
    w = w.astype(BF16)
    z = jnp.zeros((DEPTH, D, 32), BF16)
    kr = w[..., R_KR:R_FF]
    main = jnp.concatenate([w[..., :R_KR], kr, z, kr, z, w[..., R_FF:R_GATE]], axis=-1)
    return main, w[..., R_GATE:]


def _pack_wq(wq):
    w = wq.reshape(DEPTH, Q_RANK, H_B, NOPE_B + ROPE_B)
    nope, rope = w[..., :NOPE_B], w[..., NOPE_B:]
    z = jnp.zeros((DEPTH, Q_RANK, H_B, 32), wq.dtype)
    even = jnp.concatenate([rope, z, nope], axis=-1)
    odd = jnp.concatenate([nope, rope, z], axis=-1)
    is_even = (jnp.arange(H_B) % 2 == 0)[None, None, :, None]
    return jnp.where(is_even, even, odd).reshape(DEPTH, Q_RANK, H_B * LANE).astype(BF16)


def _pack_wkv(wkv):
    w = wkv.reshape(DEPTH, KV_RANK, H_B, NOPE_B + V_B)
    nope, v = w[..., :NOPE_B], w[..., NOPE_B:]
    is_even = (jnp.arange(H_B) % 2 == 0)[None, None, :, None]
    return jnp.where(is_even, jnp.concatenate([v, nope], -1), jnp.concatenate([nope, v], -1)).reshape(
        DEPTH, KV_RANK, H_B * LANE).astype(BF16)


def _rope_tables():
    pos = np.arange(DEC_SEQ)
    row, col = pos // GRID_W, pos % GRID_W

    def pattern(half):
        quarter = half // 2
        inv = THETA ** (-np.arange(0, half, 2, dtype=np.float64) / half)
        ang = np.concatenate([row[:, None] * inv, row[:, None] * inv, col[:, None] * inv, col[:, None] * inv], 1)
        is_x2 = np.tile(np.concatenate([np.zeros(quarter), np.ones(quarter)]), 2)[None, :]
        c = np.cos(ang)
        s1 = np.sin(ang) * is_x2
        s2 = -np.sin(ang) * (1 - is_x2)
        return c, s1, s2

    def assemble(width, spans, half):
        c, s1, s2 = pattern(half)
        tc = np.ones((DEC_SEQ + TMI, width))
        t1 = np.zeros((DEC_SEQ + TMI, width))
        t2 = np.zeros((DEC_SEQ + TMI, width))
        for start in spans:
            tc[:DEC_SEQ, start:start + 2 * half] = c
            t1[:DEC_SEQ, start:start + 2 * half] = s1
            t2[:DEC_SEQ, start:start + 2 * half] = s2
        return jnp.asarray(np.stack([tc, t1, t2]), F32)

    taba = assemble(LANE, (0, 64), 32)
    tabb = assemble(2 * LANE, (0, 128 + 64), 16)
    tabk = assemble(LANE, (0, 64), 16)
    return taba, tabb, tabk


def _ones_block(n, width):
    g = np.arange(n) // width
    return jnp.asarray(g[:, None] == g[None, :], BF16)


def _ctxkv_kernel(c_ref, w_ref, o_ref):
    o_ref[...] = jnp.dot(c_ref[...].astype(BF16), w_ref[...], preferred_element_type=F32).astype(BF16)


def _ctx_kv(ckv_cache, wkv_p):
    rows = ckv_cache.shape[0]
    return pl.pallas_call(
        _ctxkv_kernel,
        grid=(rows // PAST,),
        in_specs=[pl.BlockSpec((PAST, KV_RANK), lambda i: (i, 0)),
                  pl.BlockSpec((None, KV_RANK, 1024), lambda i: (i % DEPTH, 0, 0))],
        out_specs=pl.BlockSpec((PAST, 1024), lambda i: (i, 0)),
        out_shape=jax.ShapeDtypeStruct((rows, 1024), BF16),
        compiler_params=_cparams(("parallel",)),
        name="ctx_kv",
    )(ckv_cache, wkv_p)


_NT = (((1,), (1,)), ((), ()))


def _den_lane(parity):
    return 64 if parity == 0 else 0


def _softmax_pv(s, v, parity, mxu_den):
    m = s.max(axis=-1, keepdims=True)
    if not mxu_den:
        p = jnp.exp(s - m)
        return jnp.dot(p.astype(BF16), v, preferred_element_type=F32) / p.sum(axis=-1, keepdims=True)
    lane = lax.broadcasted_iota(jnp.int32, (1, LANE), 1)
    keep = (lane < 64) if parity == 0 else (lane >= 64)
    o = jnp.dot(jnp.exp((s - m).astype(BF16)), v, preferred_element_type=F32)
    return jnp.where(keep, o / o[:, _den_lane(parity):_den_lane(parity) + 1], 0.0)


def _attn_kernel(n_pieces, qa_ref, qb_ref, *refs):
    kv_refs = refs[:4 * n_pieces]
    oa_ref, ob_ref = refs[4 * n_pieces:4 * n_pieces + 2]
    mxu_den = n_pieces == 2
    lane = lax.broadcasted_iota(jnp.int32, (1, LANE), 1)
    lo = lane < 64
    hi = jnp.logical_not(lo)

    def rows(parts):
        return parts[0] if len(parts) == 1 else jnp.concatenate(parts, axis=0)

    def with_den(v, parity):
        return jnp.where(lane == _den_lane(parity), jnp.ones_like(v), v) if mxu_den else v

    def slabs_a():
        ka = [kv_refs[4 * i][...].astype(F32) for i in range(n_pieces)]
        va = [kv_refs[4 * i + 1][...].astype(F32) for i in range(n_pieces)]

        def place(x, g, parity):
            if g != parity:
                x = pltpu.roll(x, 64, 1)
            return jnp.where(lo if parity == 0 else hi, x, 0.0).astype(BF16)

        return {2 * g + parity: (rows([place(k, g, parity) for k in ka]),
                                 with_den(rows([place(v, g, parity) for v in va]), parity))
                for g in range(KV_A) for parity in range(2)}

    def slabs_b():
        out = {}
        kre = rows([kv_refs[4 * i + 3][...].astype(BF16) for i in range(n_pieces)])
        for h in range(H_B):
            parity = h % 2
            nope = hi if parity == 0 else lo
            kvb = rows([kv_refs[4 * i + 2][:, h * LANE:(h + 1) * LANE] for i in range(n_pieces)])
            out[h] = (jnp.where(nope, kvb, kre), with_den(jnp.where(nope, jnp.zeros_like(kvb), kvb), parity))
        return out

    sa, sb = slabs_a(), slabs_b()
    get_a, get_b = sa.__getitem__, sb.__getitem__

    for pair in range(H_A // 2):
        g = (2 * pair) // (H_A // KV_A)
        q = qa_ref[:, pair * LANE:(pair + 1) * LANE]
        acc = None
        for parity in range(2):
            k, v = get_a(2 * g + parity)
            o = _softmax_pv(lax.dot_general(q, k, _NT, preferred_element_type=F32), v, parity, mxu_den)
            acc = o if acc is None else acc + o
        oa_ref[:, pair * LANE:(pair + 1) * LANE] = acc.astype(BF16)

    for pair in range(H_B // 2):
        acc = None
        for parity in range(2):
            h = 2 * pair + parity
            k, v = get_b(h)
            q = qb_ref[:, h * LANE:(h + 1) * LANE]
            o = _softmax_pv(lax.dot_general(q, k, _NT, preferred_element_type=F32), v, parity, mxu_den)
            acc = o if acc is None else acc + o
        ob_ref[:, pair * LANE:(pair + 1) * LANE] = acc.astype(BF16)


TQ_LAT = 512


def _attention(qa, qb, ka, va, kvb, kre, prev=None, cache=None, layer=0):
    if cache is None:
        tq, nb, nqt, nk, q_blk0, k_blk0 = SEQ, BATCH, 1, SEQ, 0, 0
    else:
        tq = TQ_LAT
        nb, nqt, nk, q_blk0, k_blk0 = DEC_BATCH, DEC_SEQ // tq, DEC_SEQ, T_CTX // tq, T_CTX // DEC_SEQ
    qspec = lambda w: pl.BlockSpec((tq, w), lambda b, j: (q_blk0 + b * nqt + j, 0))
    kspec = lambda w: pl.BlockSpec((nk, w), lambda b, j: (k_blk0 + b, 0))
    in_specs = [qspec(512), qspec(1024), kspec(128), kspec(128), kspec(1024), kspec(128)]
    args = [qa, qb, ka, va, kvb, kre]
    n_pieces = 1
    aliases = {}
    if cache is not None:
        cspec4 = pl.BlockSpec((None, None, PAST, 128), lambda b, j: (b, layer, 0, 0))
        cspec = lambda w: pl.BlockSpec((PAST, w), lambda b, j: (b * DEPTH + layer, 0))
        in_specs += [cspec4, cspec4, cspec(1024), cspec(128)]
        args += list(cache)
        n_pieces = 2
        in_specs += [pl.BlockSpec(memory_space=pl.ANY)] * 2
        args += list(prev)
        aliases = {len(args) - 2: 0, len(args) - 1: 1}

    def body(*refs):
        if cache is not None:
            n_in = 2 + 4 * n_pieces
            refs = refs[:n_in] + refs[n_in + 2:]
        _attn_kernel(n_pieces, *refs)

    return pl.pallas_call(
        body,
        grid=(nb, nqt),
        in_specs=in_specs,
        out_specs=[qspec(512), qspec(512)],
        out_shape=[jax.ShapeDtypeStruct((T_ALL, 512), BF16)] * 2,
        input_output_aliases=aliases,
        compiler_params=_cparams(("parallel", "arbitrary")),
        name="attention_ctx" if cache is None else "attention_lat",
    )(*args)


HL = 256
HS = 128
HG = 64
N_PAIRS = H_C // 2
FAST_DECAY_LIMIT = 80.0


def _hgrn_bottom_exact(q, k, c, lo, rev):
    row = lax.broadcasted_iota(jnp.int32, (HS, LANE), 0)
    srow = lax.broadcasted_iota(jnp.int32, (HS, HS), 0)
    scol = lax.broadcasted_iota(jnp.int32, (HS, HS), 1)
    out = []
    for parity in range(2):
        def dup(x):
            xs = pltpu.roll(x, 64, 1)
            return jnp.where(lo, x, xs) if parity == 0 else jnp.where(lo, xs, x)
        qd, kd, bd = dup(q), dup(k), dup(c)

        dg = row & 3
        if rev:
            dg = 3 - dg
        e = [None]
        for delta in range(1, 4):
            shifted = pltpu.roll(bd, delta if rev else HS - delta, 0)
            e.append(jnp.exp(jnp.minimum(shifted - bd, 0.0)))
        qp, kp = [], []
        for c1, c2 in ((0, 1), (2, 3)):
            cv = jnp.where(lo, c1, c2)
            dl = cv - dg
            fac = jnp.where(dl == 0, 1.0, jnp.where(dl == 1, e[1], jnp.where(dl == 2, e[2],
                            jnp.where(dl == 3, e[3], 0.0))))
            kp.append((kd * fac).astype(BF16))
            qp.append(jnp.where(dg == cv, qd, 0.0).astype(BF16))
        s = lax.dot_general(jnp.concatenate(qp, axis=1), jnp.concatenate(kp, axis=1), _NT,
                            preferred_element_type=F32)
        tot = jnp.where((srow >> 2) == (scol >> 2), s, 0.0)

        for lev in range(1, 3):
            g = 4 ** lev
            par = 4 * g
            shape3 = (HS // par, par, LANE)
            rid = lax.broadcasted_iota(jnp.int32, shape3, 1)
            dg3 = rid >> (2 * lev)
            if rev:
                dg3 = 3 - dg3
            b3, q3, k3 = bd.reshape(shape3), qd.reshape(shape3), kd.reshape(shape3)
            lo3 = lo.reshape(1, 1, LANE)
            qp, kp = [], []
            for c1, c2 in ((1, 2), (3, None)):
                idx = lambda cc: (4 - cc) * g if rev else cc * g - 1
                i1 = idx(c1)
                i2 = idx(c2) if c2 is not None else i1
                ridx = jnp.where(lo3, i1, i2)
                ref = jnp.sum(jnp.where(rid == ridx, b3, 0.0), axis=1, keepdims=True)
                cvk = jnp.where(lo3, c1, c2 if c2 is not None else 0)
                cvq = jnp.where(lo3, c1, c2 if c2 is not None else -1)
                kk = jnp.where(dg3 < cvk, k3 * jnp.exp(jnp.minimum(ref - b3, 0.0)), 0.0)
                qq = jnp.where(dg3 == cvq, q3 * jnp.exp(jnp.minimum(b3 - ref, 0.0)), 0.0)
                kp.append(kk.reshape(HS, LANE).astype(BF16))
                qp.append(qq.reshape(HS, LANE).astype(BF16))
            s = lax.dot_general(jnp.concatenate(qp, axis=1), jnp.concatenate(kp, axis=1), _NT,
                                preferred_element_type=F32)
            sh = 2 * lev + 2
            tot = tot + jnp.where((srow >> sh) == (scol >> sh), s, 0.0)
        out.append(tot)
    return out


def _hgrn_head(q, k, lf, v, st_ref, rev):
    row = lax.broadcasted_iota(jnp.int32, (HS, LANE), 0)
    lane = lax.broadcasted_iota(jnp.int32, (1, LANE), 1)
    lo = lane < 64
    hi = jnp.logical_not(lo)
    in_g1 = row >= HG

    grow = row & (HG - 1)
    c = lf
    d = 1
    while d < HG:
        if rev:
            c = c + jnp.where(grow < HG - d, pltpu.roll(c, HS - d, 0), 0.0)
        else:
            c = c + jnp.where(grow >= d, pltpu.roll(c, d, 0), 0.0)
        d *= 2
    if rev:
        t0, t1 = c[0:1, :], c[HG:HG + 1, :]
    else:
        t0, t1 = c[HG - 1:HG, :], c[HS - 1:HS, :]
    et0, et1 = jnp.exp(t0), jnp.exp(t1)
    qe = q * jnp.exp(c)
    e_out = jnp.exp(jnp.where(in_g1, t1, t0) - c)
    ke = k * e_out

    if rev:
        qb = qe * jnp.where(in_g1, 1.0, et1)
        kh = ke * jnp.where(in_g1, et0, 1.0)
    else:
        qb = qe * jnp.where(in_g1, et0, 1.0)
        kh = ke * jnp.where(in_g1, 1.0, et1)
    st = st_ref[...]
    o_int = lax.dot_general(qb.astype(BF16), st.astype(BF16), _NT, preferred_element_type=F32)
    upd = lax.dot_general(v, kh.astype(BF16), (((0,), (0,)), ((), ())), preferred_element_type=F32)
    r128 = lax.broadcasted_iota(jnp.int32, (LANE, LANE), 0)
    c128 = lax.broadcasted_iota(jnp.int32, (LANE, LANE), 1)
    st_ref[...] = st * (et0 * et1) + jnp.where((r128 >> 6) == (c128 >> 6), upd, 0.0)

    q_late = in_g1 if not rev else jnp.logical_not(in_g1)
    q_top = jnp.where(q_late, qe, 0.0)
    k_top = jnp.where(q_late, 0.0, ke).astype(BF16)
    top = [lax.dot_general(jnp.where(m, q_top, 0.0).astype(BF16), k_top, _NT, preferred_element_type=F32)
           for m in (lo, hi)]
    mid = HG // 2 if rev else HG // 2 - 1
    cm = c - jnp.where(in_g1, c[HG + mid:HG + mid + 1, :], c[mid:mid + 1, :])
    return c, top, o_int, cm, jnp.max(jnp.abs(cm))


def _hgrn_steps(jobs, bot_ref):
    lane = lax.broadcasted_iota(jnp.int32, (1, LANE), 1)
    lo = lane < 64
    hi = jnp.logical_not(lo)
    heads = [_hgrn_head(*job) for job in jobs]
    fast = functools.reduce(jnp.maximum, [h[4] for h in heads]) <= FAST_DECAY_LIMIT

    @pl.when(fast)
    def _():
        srow = lax.broadcasted_iota(jnp.int32, (HS, HS), 0)
        scol = lax.broadcasted_iota(jnp.int32, (HS, HS), 1)
        same = (srow >> 6) == (scol >> 6)
        for ji, (job, (_, _, _, cm, _)) in enumerate(zip(jobs, heads)):
            keep = same & ((scol >= srow) if job[5] else (scol <= srow))
            qf = job[0] * jnp.exp(cm)
            kf = (job[1] * jnp.exp(-cm)).astype(BF16)
            for parity, m in enumerate((lo, hi)):
                s = lax.dot_general(jnp.where(m, qf, 0.0).astype(BF16), kf, _NT, preferred_element_type=F32)
                bot_ref[ji, parity] = jnp.where(keep, s, 0.0)

    @pl.when(jnp.logical_not(fast))
    def _():
        for ji, (job, (c, _, _, _, _)) in enumerate(zip(jobs, heads)):
            for parity, s in enumerate(_hgrn_bottom_exact(job[0], job[1], c, lo, job[5])):
                bot_ref[ji, parity] = s

    outs = []
    for ji, (job, (_, top, o_int, _, _)) in enumerate(zip(jobs, heads)):
        v = job[3]
        probs = jnp.concatenate([(bot_ref[ji, 0] + top[0]).astype(BF16), (bot_ref[ji, 1] + top[1]).astype(BF16)],
                                axis=1)
        vv = jnp.concatenate([jnp.where(lo, v, jnp.zeros_like(v)), jnp.where(hi, v, jnp.zeros_like(v))], axis=0)
        outs.append(jnp.dot(probs, vv, preferred_element_type=F32) + o_int)
    return outs


PAIRS_PER_ITER = 2


def _hgrn_kernel(has_s0, nt, *refs):
    if has_s0:
        (qf_ref, vf_ref, kf_ref, lf_ref, qb_ref, vb_ref, kb_ref, lb_ref, s0_ref,
         of_ref, ob_ref, so_ref, st_scr, bot_scr) = refs
    else:
        (qf_ref, vf_ref, kf_ref, lf_ref, qb_ref, vb_ref, kb_ref, lb_ref,
         of_ref, ob_ref, so_ref, st_scr, bot_scr) = refs
    j = pl.program_id(1)

    @pl.when(j == 0)
    def _():
        if has_s0:
            st_scr[...] = s0_ref[...]
        else:
            st_scr[...] = jnp.zeros_like(st_scr)

    n_sub = HL // HS

    def pair_body(it, carry):
        for step in range(n_sub):
            jobs, dests = [], []
            for pp in range(PAIRS_PER_ITER):
                p = it * PAIRS_PER_ITER + pp
                cols = pl.ds(pl.multiple_of(p * LANE, LANE), LANE)
                rf = pl.ds(step * HS, HS)
                rb = pl.ds((n_sub - 1 - step) * HS, HS)
                jobs.append((qf_ref[rf, cols].astype(F32), kf_ref[rf, cols], lf_ref[rf, cols], vf_ref[rf, cols],
                             st_scr.at[p, 0], False))
                dests.append((of_ref, rf, cols))
                jobs.append((qb_ref[rb, cols].astype(F32), kb_ref[rb, cols], lb_ref[rb, cols], vb_ref[rb, cols],
                             st_scr.at[p, 1], True))
                dests.append((ob_ref, rb, cols))
            for (ref, rows, cols), o in zip(dests, _hgrn_steps(jobs, bot_scr)):
                ref[rows, cols] = o
        return carry

    lax.fori_loop(0, N_PAIRS // PAIRS_PER_ITER, pair_body, 0)

    @pl.when(j == nt - 1)
    def _():
        so_ref[...] = st_scr[...]


def _hgrn(qc, vc, kff, lff, kfb, lfb, s0=None, prev=None, layer=0):
    if s0 is None:
        nb, nt, blk0 = BATCH, SEQ // HL, 0
    else:
        nb, nt, blk0 = DEC_BATCH, DEC_SEQ // HL, T_CTX // HL
    fspec = pl.BlockSpec((HL, 512), lambda b, j: (blk0 + b * nt + j, 0))
    bspec = pl.BlockSpec((HL, 512), lambda b, j: (blk0 + b * nt + nt - 1 - j, 0))
    sspec = pl.BlockSpec((None, N_PAIRS, 2, LANE, LANE), lambda b, j: (b, 0, 0, 0, 0))
    in_specs = [fspec] * 4 + [bspec] * 4
    args = [qc, vc, kff, lff, qc, vc, kfb, lfb]
    aliases = {}
    if s0 is not None:
        in_specs += [pl.BlockSpec((None, None, N_PAIRS, 2, LANE, LANE), lambda b, j: (b, layer, 0, 0, 0, 0))]
        args += [s0]
        in_specs += [pl.BlockSpec(memory_space=pl.ANY)] * 2
        args += list(prev)
        aliases = {len(args) - 2: 0, len(args) - 1: 1}

    def body(*refs):
        if s0 is not None:
            refs = refs[:9] + refs[11:]
        _hgrn_kernel(s0 is not None, nt, *refs)

    return pl.pallas_call(
        body,
        grid=(nb, nt),
        in_specs=in_specs,
        out_specs=[fspec, bspec, sspec],
        out_shape=[jax.ShapeDtypeStruct((T_ALL, 512), F32), jax.ShapeDtypeStruct((T_ALL, 512), F32),
                   jax.ShapeDtypeStruct((nb, N_PAIRS, 2, LANE, LANE), F32)],
        scratch_shapes=[pltpu.VMEM((N_PAIRS, 2, LANE, LANE), F32),
                        pltpu.VMEM((2 * PAIRS_PER_ITER, 2, HS, HS), F32)],
        input_output_aliases=aliases,
        compiler_params=_cparams(("parallel", "arbitrary")),
        name="hgrn_ctx" if s0 is None else "hgrn_lat",
    )(*args)


def _merge_kernel(xc_ref, xl_ref, mod_ref, n1_ref, wg_ref, oa_ref, ob_ref, ocf_ref, ocb_ref, sg_ref, con_ref,
                  ones_ref, wbr_ref, wout_ref, n2_ref, wrh_ref, wrl_ref, br_ref, tri_ref, upper_ref,
                  x1_o, h2_o, comb_o, rank_o, carry_o, tab_o, carry_scr):
    x = _x_tile(xc_ref, xl_ref, TMG)
    mod = mod_ref[...]
    xn = x * lax.rsqrt(jnp.mean(x * x, axis=-1, keepdims=True) + EPS) * n1_ref[...]
    h = (xn * (1.0 + mod[:, D:2 * D]) + mod[:, 0:D]).astype(BF16)

    oc = ocf_ref[...] + ocb_ref[...]
    oc = oc * lax.rsqrt(_group_mean(oc * oc, ones_ref[...], DV_C) + EPS) * con_ref[...]
    oc = (oc * sg_ref[...].astype(F32)).astype(BF16)
    branches = (oa_ref[...], ob_ref[...], oc)
    mix = None
    for jb in range(3):
        gate = jax.nn.sigmoid(jnp.dot(h, wg_ref[:, jb * D:(jb + 1) * D], preferred_element_type=F32))
        t = gate * jnp.dot(branches[jb], wbr_ref[jb], preferred_element_type=F32)
        mix = t if mix is None else mix + t
    out = jnp.dot(mix.astype(BF16), wout_ref[...], preferred_element_type=F32)
    x1 = x + mod[:, 2 * D:3 * D] * out
    x1_o[...] = x1

    x1n = x1 * lax.rsqrt(jnp.mean(x1 * x1, axis=-1, keepdims=True) + EPS) * n2_ref[...]
    h2 = x1n * (1.0 + mod[:, 4 * D:5 * D]) + mod[:, 3 * D:4 * D]
    h2_o[...] = h2.astype(BF16)

    h2h, h2l = _split_hi_lo(h2)
    logits = (jnp.dot(h2h, wrh_ref[...], preferred_element_type=F32)
              + jnp.dot(h2l, wrh_ref[...], preferred_element_type=F32)
              + jnp.dot(h2h, wrl_ref[...], preferred_element_type=F32)) + br_ref[...]
    lane = lax.broadcasted_iota(jnp.int32, logits.shape, 1)
    neg = -jnp.inf
    is_g = lane < N_GROUPS
    gl = jnp.where(is_g, logits, neg)
    gmax = gl.max(axis=-1, keepdims=True)
    gidx = jnp.min(jnp.where(gl == gmax, lane, LANE), axis=-1, keepdims=True)
    gp = 1.0 / jnp.sum(jnp.where(is_g, jnp.exp(gl - gmax), 0.0), axis=-1, keepdims=True)
    eid = lane - N_GROUPS
    in_grp = (eid >= 0) & (eid < N_EXPERTS) & ((eid >> 2) == gidx)
    el = jnp.where(in_grp, logits, neg)
    v1 = el.max(axis=-1, keepdims=True)
    i1 = jnp.min(jnp.where(el == v1, lane, LANE), axis=-1, keepdims=True)
    el2 = jnp.where(lane == i1, neg, el)
    v2 = el2.max(axis=-1, keepdims=True)
    i2 = jnp.min(jnp.where(el2 == v2, lane, LANE), axis=-1, keepdims=True)
    e2 = jnp.exp(v2 - v1)
    w1 = gp / (1.0 + e2)
    w2 = gp * e2 / (1.0 + e2)
    comb = jnp.where(lane == i1, w1, 0.0) + jnp.where(lane == i2, w2, 0.0)
    comb = pltpu.roll(comb, LANE - N_GROUPS, 1)
    comb_o[...] = comb

    step = pl.program_id(0) % MOE_STEPS

    @pl.when(step == 0)
    def _():
        carry_scr[...] = jnp.zeros_like(carry_scr)

    for sb in range(TMG // SBK):
        sub = comb[sb * SBK:(sb + 1) * SBK, :]
        routed = sub > 0.0
        ind = jnp.where(routed, 1.0, 0.0)
        carry = carry_scr[...]
        rank = jnp.dot(tri_ref[...], ind.astype(BF16), preferred_element_type=F32) + carry
        rank_o[sb * SBK:(sb + 1) * SBK, :] = jnp.where(routed, rank, -1.0)
        carry_o[sb] = carry
        carry_scr[...] = carry + jnp.sum(ind, axis=0, keepdims=True)

    @pl.when(step == MOE_STEPS - 1)
    def _():
        count = carry_scr[...]
        seg = jnp.floor((count + (SEG_ALIGN - 1.0)) * (1.0 / SEG_ALIGN)) * SEG_ALIGN
        off = jnp.dot(jnp.broadcast_to(seg, (8, LANE)), upper_ref[...], preferred_element_type=F32,
                      precision=lax.Precision.HIGHEST)
        tab_o[0:1, :] = count
        tab_o[1:2, :] = off[0:1, :]


TMG = 512


def _merge(layer, x, mod, n1, wgate, oa, ob, ocf, ocb, sg, con, ones, wbr, wout, n2, wr_hi, wr_lo, br):
    ctx_tiles, per_seq = T_CTX // TMG, DEC_SEQ // TMG
    mrow = lambda i: jnp.where(i < ctx_tiles, CTX_MOD_ROW, (i - ctx_tiles) // per_seq)
    tspec = lambda w: pl.BlockSpec((TMG, w), lambda i: (i, 0))
    wspec = lambda tail: pl.BlockSpec((None,) + tail, lambda i: (layer,) + (0,) * len(tail),
                                      pipeline_mode=pl.Buffered(1))
    return pl.pallas_call(
        _merge_kernel,
        grid=(T_ALL // TMG,),
        in_specs=_x_specs(x, TMG) + [
            _mod_spec(layer, mrow),
            _layer_spec((1, D), layer), wspec((D, 3 * D)),
            tspec(512), tspec(512), tspec(512), tspec(512), tspec(512),
            _layer_spec((1, 512), layer), _const_spec((256, 256)),
            wspec((3, BRANCH_W, D)), wspec((D, D)), _layer_spec((1, D), layer),
            _layer_spec((D, LANE), layer), _layer_spec((D, LANE), layer), _layer_spec((1, LANE), layer),
            _const_spec((SBK, SBK)), _const_spec((LANE, LANE)),
        ],
        out_specs=[
            tspec(D), tspec(D), tspec(LANE), tspec(LANE),
            pl.BlockSpec((None, TMG // SBK, 1, LANE), lambda i: (i // MOE_STEPS, i % MOE_STEPS, 0, 0)),
            pl.BlockSpec((None, 2, LANE), lambda i: (i // MOE_STEPS, 0, 0)),
        ],
        out_shape=[jax.ShapeDtypeStruct((T_ALL, D), F32), jax.ShapeDtypeStruct((T_ALL, D), BF16),
                   jax.ShapeDtypeStruct((T_ALL, LANE), F32), jax.ShapeDtypeStruct((T_ALL, LANE), F32),
                   jax.ShapeDtypeStruct((N_BLK, N_SB, 1, LANE), F32), jax.ShapeDtypeStruct((N_BLK, 2, LANE), F32)],
        scratch_shapes=[pltpu.VMEM((1, LANE), F32)],
        compiler_params=_cparams(("arbitrary",)),
        name="merge",
    )(*_x_pair(x)[:2], mod, n1, wgate, oa, ob, ocf, ocb, sg, con, ones, wbr, wout, n2, wr_hi, wr_lo, br,
      jnp.asarray(np.tril(np.ones((SBK, SBK)), -1), BF16), jnp.asarray(np.triu(np.ones((LANE, LANE)), 1), F32))


NB = 2048
N_BLK = T_ALL // NB
SBK = 256
N_SB = NB // SBK
WIN_SHIFT, FT_SHIFT, SEG_SHIFT = 6, 7, 4
WIN = 1 << WIN_SHIFT
FT = 1 << FT_SHIFT
SEG_ALIGN = 1 << SEG_SHIFT
STG = 2 * NB + N_EXPERTS * SEG_ALIGN + 256
QUAD = 4
E_STEP = 2
MOE_STEPS = NB // TMG


def _moe_kernel(cnt_s, off_s, car_s, h2_ref, rank_ref, comb_ref, offv_ref, eg_ref, eu_ref, ed_ref, o_ref,
                stg_ref, acc_ref):
    blk = pl.program_id(0)
    step = pl.program_id(1)
    srow = lax.broadcasted_iota(jnp.int32, (WIN, SBK), 0).astype(F32)

    def windows(s):
        out = []
        for ex in range(N_EXPERTS):
            start = off_s[blk, ex] + car_s[blk, s, ex]
            length = car_s[blk, s + 1, ex] - car_s[blk, s, ex]
            ws = (start >> SEG_SHIFT) << SEG_SHIFT
            out.append((ws, (start - ws + length + (WIN - 1)) >> WIN_SHIFT))
        return out, functools.reduce(jnp.maximum, [w[1] for w in out])

    def positions(s):
        rows = pl.ds(pl.multiple_of(s * SBK, SBK), SBK)
        rank = rank_ref[rows, :]
        return rows, jnp.where(rank >= 0.0, rank + offv_ref[1:2, :], -1.0e6)

    @pl.when(step == 0)
    def _():
        stg_ref[...] = jnp.zeros_like(stg_ref)

        def sub_body(s, carry):
            rows, pos = positions(s)
            pos_t = pos.T
            h2 = h2_ref[rows, :]
            wins, nmax = windows(s)

            def chunk_body(c, carry2):
                for quad in range(N_EXPERTS // QUAD):
                    blocks = []
                    for ex in range(quad * QUAD, (quad + 1) * QUAD):
                        base = (wins[ex][0] + c * WIN).astype(F32)
                        hit = (pos_t[ex:ex + 1, :] - base) == srow
                        blocks.append(jnp.where(hit, 1.0, 0.0).astype(BF16))
                    moved = jnp.dot(jnp.concatenate(blocks, axis=0), h2, preferred_element_type=F32).astype(BF16)
                    for i in range(QUAD):
                        first = jnp.minimum(wins[quad * QUAD + i][0] + c * WIN, STG - WIN)
                        dst = pl.ds(pl.multiple_of(first, SEG_ALIGN), WIN)
                        stg_ref[dst, :] = stg_ref[dst, :] + moved[i * WIN:(i + 1) * WIN, :]
                return carry2

            lax.fori_loop(0, nmax, chunk_body, 0)
            return carry

        lax.fori_loop(0, N_SB, sub_body, 0)

    def ffn_tile(j, first, n_rows, n_valid):
        rows = pl.ds(pl.multiple_of(first, SEG_ALIGN), n_rows)
        xs = stg_ref[rows, :]
        hg = jnp.dot(xs, eg_ref[j], preferred_element_type=F32)
        hu = jnp.dot(xs, eu_ref[j], preferred_element_type=F32)
        act = (hg * jax.nn.sigmoid(hg) * hu).astype(BF16)
        y = jnp.dot(act, ed_ref[j], preferred_element_type=F32).astype(BF16)
        rid = lax.broadcasted_iota(jnp.int32, (n_rows, 1), 0)
        stg_ref[rows, :] = jnp.where(rid < n_valid, y, xs)

    for j in range(E_STEP):
        ex = step * E_STEP + j
        count = cnt_s[blk, ex]
        seg0 = off_s[blk, ex]
        n_big = (count + (FT - 1)) >> (FT_SHIFT + 1)

        def ffn_body(t, carry, j=j, count=count, seg0=seg0):
            ffn_tile(j, seg0 + t * (2 * FT), 2 * FT, count - t * (2 * FT))
            return carry

        lax.fori_loop(0, n_big, ffn_body, 0)

        @pl.when(count > n_big * (2 * FT))
        def _(j=j, count=count, seg0=seg0, n_big=n_big):
            ffn_tile(j, seg0 + n_big * (2 * FT), FT, count - n_big * (2 * FT))

    @pl.when(step == N_EXPERTS // E_STEP - 1)
    def _():
        def sub_body(s, carry):
            rows, pos = positions(s)
            pos_t = pos.T
            wts_t = comb_ref[rows, :].T
            wins, nmax = windows(s)
            acc_ref[...] = jnp.zeros_like(acc_ref)

            def chunk_body(c, carry2):
                sel_t, srcs = [], []
                for ex in range(N_EXPERTS):
                    base = (wins[ex][0] + c * WIN).astype(F32)
                    hit = (pos_t[ex:ex + 1, :] - base) == srow
                    sel_t.append(jnp.where(hit, wts_t[ex:ex + 1, :], 0.0).astype(BF16))
                    first = jnp.minimum(wins[ex][0] + c * WIN, STG - WIN)
                    srcs.append(stg_ref[pl.ds(pl.multiple_of(first, SEG_ALIGN), WIN), :])
                acc_ref[...] += lax.dot_general(jnp.concatenate(sel_t, axis=0), jnp.concatenate(srcs, axis=0),
                                                (((0,), (0,)), ((), ())), preferred_element_type=F32)
                return carry2

            lax.fori_loop(0, nmax, chunk_body, 0)
            o_ref[rows, :] = acc_ref[...].astype(BF16)
            return carry

        lax.fori_loop(0, N_SB, sub_body, 0)


def _moe(layer, h2, comb, rank, carry, tab, eg, eu, ed):
    cnt_i = tab[:, 0, :N_EXPERTS].astype(jnp.int32)
    off_i = tab[:, 1, :N_EXPERTS].astype(jnp.int32)
    car_i = jnp.concatenate([carry[:, :, 0, :N_EXPERTS].astype(jnp.int32), cnt_i[:, None, :]], axis=1)
    bspec = lambda w: pl.BlockSpec((NB, w), lambda b, e, *_: (b, 0))
    grid_spec = pltpu.PrefetchScalarGridSpec(
        num_scalar_prefetch=3,
        grid=(N_BLK, N_EXPERTS // E_STEP),
        in_specs=[
            bspec(D), bspec(LANE), bspec(LANE),
            pl.BlockSpec((None, 2, LANE), lambda b, e, *_: (b, 0, 0)),
            pl.BlockSpec((None, E_STEP, D, D_EXPERT), lambda b, e, *_: (layer, e, 0, 0)),
            pl.BlockSpec((None, E_STEP, D, D_EXPERT), lambda b, e, *_: (layer, e, 0, 0)),
            pl.BlockSpec((None, E_STEP, D_EXPERT, D), lambda b, e, *_: (layer, e, 0, 0)),
        ],
        out_specs=bspec(D),
        scratch_shapes=[pltpu.VMEM((STG, D), BF16), pltpu.VMEM((SBK, D), F32)],
    )
    return pl.pallas_call(
        _moe_kernel,
        grid_spec=grid_spec,
        out_shape=jax.ShapeDtypeStruct((T_ALL, D), BF16),
        compiler_params=_cparams(("parallel", "arbitrary")),
        name="moe",
    )(cnt_i, off_i, car_i, h2, rank, comb, tab, eg, eu, ed)


def _residual_kernel(final, x1_ref, moe_ref, mod_ref, fg_ref, o_ref):
    x2 = x1_ref[...] + mod_ref[:, 5 * D:6 * D] * moe_ref[...].astype(F32)
    if final:
        x2 = x2 * lax.rsqrt(jnp.mean(x2 * x2, axis=-1, keepdims=True) + EPS) * fg_ref[...]
    o_ref[...] = x2


TR = DEC_SEQ


def _residual(layer, final, x1, moe, mod, fg, tile0=0, n_tiles=T_ALL // TR):
    ctx_tiles = T_CTX // TR
    mrow = lambda i: jnp.where(tile0 + i < ctx_tiles, CTX_MOD_ROW, tile0 + i - ctx_tiles)
    src = lambda: pl.BlockSpec((TR, D), lambda i: (tile0 + i, 0))
    return pl.pallas_call(
        functools.partial(_residual_kernel, final),
        grid=(n_tiles,),
        in_specs=[src(), src(), _mod_spec(layer, mrow), _const_spec((1, D))],
        out_specs=pl.BlockSpec((TR, D), lambda i: (i, 0)),
        out_shape=jax.ShapeDtypeStruct((n_tiles * TR, D), F32),
        compiler_params=_cparams(("parallel",)),
        name="residual",
    )(x1, moe, mod, fg)


def _state_to_blockdiag(s):
    lead = s.shape[:-3]
    st = jnp.swapaxes(s, -1, -2).reshape(lead + (N_PAIRS, 2, DV_C, DK_C))
    z = jnp.zeros_like(st[..., 0, :, :])
    top = jnp.concatenate([st[..., 0, :, :], z], axis=-1)
    bot = jnp.concatenate([z, st[..., 1, :, :]], axis=-1)
    return jnp.concatenate([top, bot], axis=-2)


def _blockdiag_to_state(sb):
    lead = sb.shape[:-3]
    even = sb[..., :DV_C, :DK_C]
    odd = sb[..., DV_C:, DK_C:]
    st = jnp.stack([even, odd], axis=-3).reshape(lead + (H_C, DV_C, DK_C))
    return jnp.swapaxes(st, -1, -2)


def kernel(x_prompt, x_sample, cache_gqa_k, cache_gqa_v, cache_mla_ckv, cache_mla_krope, state_hgrn, c, c_ctx,
           w_mod, b_mod, norm1_g, norm2_g, w_in, a_qnorm, a_knorm, b_qnorm, b_wq, b_kvnorm, b_wkv, c_lb_logits,
           c_onorm, w_branch, w_out, r_group_w, r_group_b, r_expert_w, r_expert_b, e_gate, e_up, e_down, final_g):
    x = (x_prompt.reshape(T_CTX, D), x_sample.reshape(T_LAT, D))
    cvec = jnp.concatenate([c, c_ctx[None, :], jnp.zeros((MOD_ROWS - DEC_BATCH - 1, D), F32)], axis=0)
    mod = _mod_table(cvec, w_mod, b_mod).reshape(DEPTH * MOD_ROWS, 1, 6 * D)
    taba, tabb, tabk = _rope_tables()
    ones = _ones_block(256, 64)
    lbl = c_lb_logits.reshape(DEPTH, 2, H_C * DK_C)

    vec = lambda g, reps=1: jnp.tile(g, (1, reps))[:, None, :]
    n1, n2 = vec(norm1_g), vec(norm2_g)
    aq, ak, con = vec(a_qnorm, H_A), vec(a_knorm, KV_A), vec(c_onorm, H_C)
    bq, bkv = vec(b_qnorm), vec(b_kvnorm)
    w_in_p, w_gate = _pack_w_in(w_in)
    wq_p, wkv_p = _pack_wq(b_wq), _pack_wkv(b_wkv)
    wbr, wout = w_branch.astype(BF16), w_out.astype(BF16)
    n_pad = LANE - N_GROUPS - N_EXPERTS
    wr_hi, wr_lo = _split_hi_lo(jnp.concatenate([r_group_w, r_expert_w, jnp.zeros((DEPTH, D, n_pad), F32)], axis=-1))
    br = jnp.concatenate([r_group_b, r_expert_b, jnp.zeros((DEPTH, n_pad), F32)], axis=-1)[:, None, :]
    eg, eu, ed = e_gate.astype(BF16), e_up.astype(BF16), e_down.astype(BF16)

    ck = cache_gqa_k.reshape(DEC_BATCH, DEPTH, PAST, KV_A * HD_A)
    cv = cache_gqa_v.reshape(DEC_BATCH, DEPTH, PAST, KV_A * HD_A)
    ckvb = _ctx_kv(cache_mla_ckv.reshape(DEC_BATCH * DEPTH * PAST, KV_RANK), wkv_p)
    ckr = cache_mla_krope.reshape(DEC_BATCH * DEPTH * PAST, ROPE_B)
    zpad = jnp.zeros_like(ckr)
    ckre = jnp.concatenate([ckr, zpad, ckr, zpad], axis=1)
    s0 = jnp.swapaxes(_state_to_blockdiag(state_hgrn), 2, 3)

    new_k, new_v, new_ckv, new_kr, new_s = [], [], [], [], []
    for l in range(DEPTH):
        (qa, kan, ka, va, qb, ckv, kvb, kr, kre, lff, lfb, kff, kfb, qc, vc, sg) = _inproj(
            l, x, mod, n1, w_in_p, aq, ak, bq, wq_p, bkv, wkv_p, lbl, ones, taba, tabb, tabk)
        new_k.append(kan[:T_CTX])
        new_v.append(va[:T_CTX])
        new_ckv.append(ckv[:T_CTX])
        new_kr.append(kr[:T_CTX, :ROPE_B])

        oa, ob = _attention(qa, qb, ka, va, kvb, kre)
        oa, ob = _attention(qa, qb, ka, va, kvb, kre, prev=(oa, ob), cache=(ck, cv, ckvb, ckre), layer=l)

        ocf, ocb, s_ctx = _hgrn(qc, vc, kff, lff, kfb, lfb)
        ocf, ocb, _ = _hgrn(qc, vc, kff, lff, kfb, lfb, s0=s0, prev=(ocf, ocb), layer=l)
        new_s.append(s_ctx)

        x1, h2, comb, rank, carry, tab = _merge(l, x, mod, n1, w_gate, oa, ob, ocf, ocb, sg, con, ones, wbr, wout,
                                                n2, wr_hi, wr_lo, br)
        moe = _moe(l, h2, comb, rank, carry, tab, eg, eu, ed)
        if l < DEPTH - 1:
            x = _residual(l, False, x1, moe, mod, final_g[None, :])

    last = DEPTH - 1
    y_prompt = _residual(last, True, x1, moe, mod, final_g[None, :], 0, T_CTX // TR)
    y_sample = _residual(last, True, x1, moe, mod, final_g[None, :], T_CTX // TR, T_LAT // TR)
    stack = lambda parts, tail: jnp.stack([p.reshape(BATCH, SEQ, -1) for p in parts], axis=1).reshape(
        (BATCH, DEPTH, SEQ) + tail)
    states = _blockdiag_to_state(jnp.swapaxes(jnp.stack(new_s, axis=1), 2, 3))
    return (y_prompt.reshape(BATCH, SEQ, D), y_sample.reshape(DEC_BATCH, DEC_SEQ, D),
            stack(new_k, (KV_A, HD_A)), stack(new_v, (KV_A, HD_A)), stack(new_ckv, (KV_RANK,)),
            stack(new_kr, (ROPE_B,)), states)
```

```python
import functools

import numpy as np
import jax
import jax.numpy as jnp
from jax import lax
from jax.experimental import pallas as pl
from jax.experimental.pallas import tpu as pltpu

D = 1024
BATCH, SEQ = 32, 256
DEC_BATCH, DEC_SEQ = 8, 1024
PAST = 256
DEPTH = 2
GRID_W = 64
THETA = 10000.0
EPS = 1e-6
F_FLOOR = 1e-30
H_A, KV_A, HD_A = 8, 2, 64
H_B, Q_RANK, KV_RANK, NOPE_B, ROPE_B, V_B = 8, 384, 256, 64, 32, 64
H_C, DK_C, DV_C = 8, 64, 64
BRANCH_W = 512
N_GROUPS, E_PER_GROUP, N_EXPERTS, D_EXPERT = 4, 4, 16, 512

T_CTX = BATCH * SEQ
T_LAT = DEC_BATCH * DEC_SEQ
T_ALL = T_CTX + T_LAT
TMI = 512
MOD_ROWS = 16
CTX_MOD_ROW = DEC_BATCH
LANE = 128
VMEM_LIMIT = 56 * 1024 * 1024

C_QA, C_KA, C_VA, C_QRA, C_KVA, C_KR = 0, 512, 640, 768, 1152, 1408
C_FF, C_FB, C_QC, C_IC, C_GC, C_END = 1536, 2048, 2560, 3072, 3584, 4096
R_QA, R_KA, R_VA, R_QRA, R_KVA, R_KR = 0, 512, 640, 768, 1152, 1408
R_FF, R_FB, R_QC, R_IC, R_GC, R_GATE, R_END = 1440, 1952, 2464, 2976, 3488, 4000, 7072

F32 = jnp.float32
BF16 = jnp.bfloat16


def _cparams(sem):
    return pltpu.CompilerParams(dimension_semantics=sem, vmem_limit_bytes=VMEM_LIMIT)


def _split_hi_lo(x):
    hi = x.astype(BF16)
    lo = (x - hi.astype(F32)).astype(BF16)
    return hi, lo


def _group_mean(x2, ones_blk, width):
    n = ones_blk.shape[0]
    outs = []
    for j in range(x2.shape[-1] // n):
        blk = x2[:, j * n:(j + 1) * n]
        hi, lo = _split_hi_lo(blk)
        s = jnp.dot(hi, ones_blk, preferred_element_type=F32) + jnp.dot(lo, ones_blk, preferred_element_type=F32)
        outs.append(s)
    s = outs[0] if len(outs) == 1 else jnp.concatenate(outs, axis=-1)
    return s * (1.0 / width)


def _rope(x, tab_ref, shift, period):
    c, s1, s2 = tab_ref[0], tab_ref[1], tab_ref[2]
    outs = []
    for j in range(x.shape[-1] // period):
        blk = x[:, j * period:(j + 1) * period]
        outs.append(blk * c + pltpu.roll(blk, shift, 1) * s1 + pltpu.roll(blk, period - shift, 1) * s2)
    return outs[0] if len(outs) == 1 else jnp.concatenate(outs, axis=-1)


def _mod_kernel(c_ref, w_ref, b_ref, o_ref):
    c = c_ref[...]
    a = c * jax.nn.sigmoid(c)
    o_ref[...] = jnp.dot(a, w_ref[...], preferred_element_type=F32, precision=lax.Precision.HIGHEST) + b_ref[...]


def _mod_table(cvec, w_mod, b_mod):
    nt = 1024
    return pl.pallas_call(
        _mod_kernel,
        grid=(DEPTH, 6 * D // nt),
        in_specs=[
            pl.BlockSpec((MOD_ROWS, D), lambda l, j: (0, 0)),
            pl.BlockSpec((None, D, nt), lambda l, j: (l, 0, j)),
            pl.BlockSpec((None, 1, nt), lambda l, j: (l, 0, j)),
        ],
        out_specs=pl.BlockSpec((None, MOD_ROWS, nt), lambda l, j: (l, 0, j)),
        out_shape=jax.ShapeDtypeStruct((DEPTH, MOD_ROWS, 6 * D), F32),
        compiler_params=_cparams(("arbitrary", "arbitrary")),
        name="mod_table",
    )(cvec, w_mod, b_mod.reshape(DEPTH, 1, 6 * D))


def _x_pair(x):
    if isinstance(x, tuple):
        return x[0], x[1], 0
    return x, x, T_CTX


def _x_specs(x, tile):
    _, _, lat_off = _x_pair(x)
    ctx = T_CTX // tile
    return [pl.BlockSpec((tile, D), lambda i: (jnp.minimum(i, ctx - 1), 0)),
            pl.BlockSpec((tile, D), lambda i: (jnp.maximum(i - ctx, 0) + lat_off // tile, 0))]


def _x_tile(xc_ref, xl_ref, tile):
    return jnp.where(pl.program_id(0) < T_CTX // tile, xc_ref[...], xl_ref[...])


def _inproj_kernel(layer, xc_ref, xl_ref, mod_ref, n1_ref, w_ref, aq_ref, ak_ref, bq_ref, wq_ref, bkv_ref, wkv_ref,
                   lbl_ref, ones_ref, taba_ref, tabb_ref, tabk_ref,
                   qa_o, kan_o, ka_o, va_o, qb_o, ckv_o, kvb_o, kr_o, kre_o,
                   lff_o, lfb_o, kff_o, kfb_o, qc_o, vc_o, sg_o):
    x = _x_tile(xc_ref, xl_ref, TMI)
    mod = mod_ref[...]
    xn = x * lax.rsqrt(jnp.mean(x * x, axis=-1, keepdims=True) + EPS) * n1_ref[...]
    h = (xn * (1.0 + mod[:, D:2 * D]) + mod[:, 0:D]).astype(BF16)
    y_all = jnp.dot(h, w_ref[...], preferred_element_type=F32)

    def y(c0, c1):
        return y_all[:, c0:c1]

    ones = ones_ref[...]

    qa = y(C_QA, C_KA)
    qa = qa * lax.rsqrt(_group_mean(qa * qa, ones, HD_A) + EPS) * aq_ref[...]
    qa_o[...] = (_rope(qa, taba_ref, 16, LANE) * (HD_A ** -0.5)).astype(BF16)
    ka = y(C_KA, C_VA)
    ka = ka * lax.rsqrt(_group_mean(ka * ka, ones[:LANE, :LANE], HD_A) + EPS) * ak_ref[...]
    kan_o[...] = ka
    ka_o[...] = _rope(ka, taba_ref, 16, LANE).astype(BF16)
    va_o[...] = y(C_VA, C_QRA)

    qr = y(C_QRA, C_KVA)
    qr = qr * lax.rsqrt(jnp.mean(qr * qr, axis=-1, keepdims=True) + EPS) * bq_ref[...]
    qb = jnp.dot(qr.astype(BF16), wq_ref[...], preferred_element_type=F32)
    qb_o[...] = (_rope(qb, tabb_ref, 8, 2 * LANE) * ((NOPE_B + ROPE_B) ** -0.5)).astype(BF16)
    kv = y(C_KVA, C_KR)
    ckv = kv * lax.rsqrt(jnp.mean(kv * kv, axis=-1, keepdims=True) + EPS) * bkv_ref[...]
    ckv_o[...] = ckv
    kvb_o[...] = jnp.dot(ckv.astype(BF16), wkv_ref[...], preferred_element_type=F32).astype(BF16)
    kr = y(C_KR, C_FF)
    kr_o[...] = kr
    kre_o[...] = _rope(kr, tabk_ref, 8, LANE).astype(BF16)

    lbl = lbl_ref[...]
    e = jnp.exp(lbl - jnp.max(lbl, axis=0, keepdims=True))
    p = e / jnp.sum(e, axis=0, keepdims=True)
    lb = p[0] * 0.0
    for i in range(1, layer + 1):
        lb = lb + p[i]
    for d, (c0, lf_o, kf_o) in enumerate(((C_FF, lff_o, kff_o), (C_FB, lfb_o, kfb_o))):
        pre = y(c0, c0 + 512)
        lbd = lb[d:d + 1, :]
        f = jnp.maximum(lbd + (1.0 - lbd) * jax.nn.sigmoid(pre), F_FLOOR)
        lf_o[...] = jnp.log(f)
        kf_o[...] = 1.0 - f
    qc_o[...] = y(C_QC, C_IC).astype(BF16)
    vc_o[...] = y(C_IC, C_GC).astype(BF16)
    gc = y(C_GC, C_END)
    sg_o[...] = (gc * jax.nn.sigmoid(gc)).astype(BF16)


def _const_spec(shape):
    nd = len(shape)
    return pl.BlockSpec(shape, lambda i: (0,) * nd)


def _layer_spec(tail, layer):
    return pl.BlockSpec((None,) + tuple(tail), lambda *_: (layer,) + (0,) * len(tail))


def _mod_spec(layer, row_of_tile):
    return pl.BlockSpec((None, 1, 6 * D), lambda i, *_: (layer * MOD_ROWS + row_of_tile(i), 0, 0))


def _inproj(layer, x, mod, n1, w_in_p, aq, ak, bq, wq_p, bkv, wkv_p, lbl, ones, taba, tabb, tabk):
    outs = [
        (512, BF16), (128, F32), (128, BF16), (128, F32), (1024, BF16), (256, F32), (1024, BF16),
        (128, F32), (128, BF16), (512, F32), (512, F32), (512, F32), (512, F32), (512, BF16), (512, BF16),
        (512, BF16),
    ]
    ctx_tiles, per_seq = T_CTX // TMI, DEC_SEQ // TMI
    mrow = lambda i: jnp.where(i < ctx_tiles, CTX_MOD_ROW, (i - ctx_tiles) // per_seq)
    pos_blk = lambda i: jnp.where(i < ctx_tiles, per_seq, (i - ctx_tiles) % per_seq)
    tab_spec = lambda w: pl.BlockSpec((3, TMI, w), lambda i: (0, pos_blk(i), 0))
    tile_spec = lambda w: pl.BlockSpec((TMI, w), lambda i: (i, 0))
    return pl.pallas_call(
        functools.partial(_inproj_kernel, layer),
        grid=(T_ALL // TMI,),
        in_specs=_x_specs(x, TMI) + [
            _mod_spec(layer, mrow),
            _layer_spec((1, D), layer),
            _layer_spec((D, C_END), layer),
            _layer_spec((1, 512), layer), _layer_spec((1, 128), layer), _layer_spec((1, Q_RANK), layer),
            _layer_spec((Q_RANK, 1024), layer), _layer_spec((1, KV_RANK), layer), _layer_spec((KV_RANK, 1024), layer),
            _const_spec((DEPTH, 2, 512)), _const_spec((256, 256)),
            tab_spec(LANE), tab_spec(2 * LANE), tab_spec(LANE),
        ],
        out_specs=[tile_spec(w) for w, _ in outs],
        out_shape=[jax.ShapeDtypeStruct((T_ALL, w), dt) for w, dt in outs],
        compiler_params=_cparams(("parallel",)),
        name="inproj",
    )(*_x_pair(x)[:2], mod, n1, w_in_p, aq, ak, bq, wq_p, bkv, wkv_p, lbl, ones, taba, tabb, tabk)


def _pack_w_in(w):
    w = w.astype(BF16)
    z = jnp.zeros((DEPTH, D, 32), BF16)
    kr = w[..., R_KR:R_FF]
    main = jnp.concatenate([w[..., :R_KR], kr, z, kr, z, w[..., R_FF:R_GATE]], axis=-1)
    return main, w[..., R_GATE:]


def _pack_wq(wq):
    w = wq.reshape(DEPTH, Q_RANK, H_B, NOPE_B + ROPE_B)
    nope, rope = w[..., :NOPE_B], w[..., NOPE_B:]
    z = jnp.zeros((DEPTH, Q_RANK, H_B, 32), wq.dtype)
    even = jnp.concatenate([rope, z, nope], axis=-1)
    odd = jnp.concatenate([nope, rope, z], axis=-1)
    is_even = (jnp.arange(H_B) % 2 == 0)[None, None, :, None]
    return jnp.where(is_even, even, odd).reshape(DEPTH, Q_RANK, H_B * LANE).astype(BF16)


def _pack_wkv(wkv):
    w = wkv.reshape(DEPTH, KV_RANK, H_B, NOPE_B + V_B)
    nope, v = w[..., :NOPE_B], w[..., NOPE_B:]
    is_even = (jnp.arange(H_B) % 2 == 0)[None, None, :, None]
    return jnp.where(is_even, jnp.concatenate([v, nope], -1), jnp.concatenate([nope, v], -1)).reshape(
        DEPTH, KV_RANK, H_B * LANE).astype(BF16)


def _rope_tables():
    pos = np.arange(DEC_SEQ)
    row, col = pos // GRID_W, pos % GRID_W

    def pattern(half):
        quarter = half // 2
        inv = THETA ** (-np.arange(0, half, 2, dtype=np.float64) / half)
        ang = np.concatenate([row[:, None] * inv, row[:, None] * inv, col[:, None] * inv, col[:, None] * inv], 1)
        is_x2 = np.tile(np.concatenate([np.zeros(quarter), np.ones(quarter)]), 2)[None, :]
        c = np.cos(ang)
        s1 = np.sin(ang) * is_x2
        s2 = -np.sin(ang) * (1 - is_x2)
        return c, s1, s2

    def assemble(width, spans, half):
        c, s1, s2 = pattern(half)
        tc = np.ones((DEC_SEQ + TMI, width))
        t1 = np.zeros((DEC_SEQ + TMI, width))
        t2 = np.zeros((DEC_SEQ + TMI, width))
        for start in spans:
            tc[:DEC_SEQ, start:start + 2 * half] = c
            t1[:DEC_SEQ, start:start + 2 * half] = s1
            t2[:DEC_SEQ, start:start + 2 * half] = s2
        return jnp.asarray(np.stack([tc, t1, t2]), F32)

    taba = assemble(LANE, (0, 64), 32)
    tabb = assemble(2 * LANE, (0, 128 + 64), 16)
    tabk = assemble(LANE, (0, 64), 16)
    return taba, tabb, tabk


def _ones_block(n, width):
    g = np.arange(n) // width
    return jnp.asarray(g[:, None] == g[None, :], BF16)


def _ctxkv_kernel(c_ref, w_ref, o_ref):
    o_ref[...] = jnp.dot(c_ref[...].astype(BF16), w_ref[...], preferred_element_type=F32).astype(BF16)


def _ctx_kv(ckv_cache, wkv_p):
    rows = ckv_cache.shape[0]
    return pl.pallas_call(
        _ctxkv_kernel,
        grid=(rows // PAST,),
        in_specs=[pl.BlockSpec((PAST, KV_RANK), lambda i: (i, 0)),
                  pl.BlockSpec((None, KV_RANK, 1024), lambda i: (i % DEPTH, 0, 0))],
        out_specs=pl.BlockSpec((PAST, 1024), lambda i: (i, 0)),
        out_shape=jax.ShapeDtypeStruct((rows, 1024), BF16),
        compiler_params=_cparams(("parallel",)),
        name="ctx_kv",
    )(ckv_cache, wkv_p)


_NT = (((1,), (1,)), ((), ()))


def _den_lane(parity):
    return 64 if parity == 0 else 0


def _softmax_pv(s, v, parity, mxu_den):
    m = s.max(axis=-1, keepdims=True)
    if not mxu_den:
        p = jnp.exp(s - m)
        return jnp.dot(p.astype(BF16), v, preferred_element_type=F32) / p.sum(axis=-1, keepdims=True)
    lane = lax.broadcasted_iota(jnp.int32, (1, LANE), 1)
    keep = (lane < 64) if parity == 0 else (lane >= 64)
    o = jnp.dot(jnp.exp((s - m).astype(BF16)), v, preferred_element_type=F32)
    return jnp.where(keep, o / o[:, _den_lane(parity):_den_lane(parity) + 1], 0.0)


def _attn_kernel(n_pieces, qa_ref, qb_ref, *refs):
    kv_refs = refs[:4 * n_pieces]
    oa_ref, ob_ref = refs[4 * n_pieces:4 * n_pieces + 2]
    mxu_den = n_pieces == 2
    lane = lax.broadcasted_iota(jnp.int32, (1, LANE), 1)
    lo = lane < 64
    hi = jnp.logical_not(lo)

    def rows(parts):
        return parts[0] if len(parts) == 1 else jnp.concatenate(parts, axis=0)

    def with_den(v, parity):
        return jnp.where(lane == _den_lane(parity), jnp.ones_like(v), v) if mxu_den else v

    def slabs_a():
        ka = [kv_refs[4 * i][...].astype(F32) for i in range(n_pieces)]
        va = [kv_refs[4 * i + 1][...].astype(F32) for i in range(n_pieces)]

        def place(x, g, parity):
            if g != parity:
                x = pltpu.roll(x, 64, 1)
            return jnp.where(lo if parity == 0 else hi, x, 0.0).astype(BF16)

        return {2 * g + parity: (rows([place(k, g, parity) for k in ka]),
                                 with_den(rows([place(v, g, parity) for v in va]), parity))
                for g in range(KV_A) for parity in range(2)}

    def slabs_b():
        out = {}
        kre = rows([kv_refs[4 * i + 3][...].astype(BF16) for i in range(n_pieces)])
        for h in range(H_B):
            parity = h % 2
            nope = hi if parity == 0 else lo
            kvb = rows([kv_refs[4 * i + 2][:, h * LANE:(h + 1) * LANE] for i in range(n_pieces)])
            out[h] = (jnp.where(nope, kvb, kre), with_den(jnp.where(nope, jnp.zeros_like(kvb), kvb), parity))
        return out

    sa, sb = slabs_a(), slabs_b()
    get_a, get_b = sa.__getitem__, sb.__getitem__

    for pair in range(H_A // 2):
        g = (2 * pair) // (H_A // KV_A)
        q = qa_ref[:, pair * LANE:(pair + 1) * LANE]
        acc = None
        for parity in range(2):
            k, v = get_a(2 * g + parity)
            o = _softmax_pv(lax.dot_general(q, k, _NT, preferred_element_type=F32), v, parity, mxu_den)
            acc = o if acc is None else acc + o
        oa_ref[:, pair * LANE:(pair + 1) * LANE] = acc.astype(BF16)

    for pair in range(H_B // 2):
        acc = None
        for parity in range(2):
            h = 2 * pair + parity
            k, v = get_b(h)
            q = qb_ref[:, h * LANE:(h + 1) * LANE]
            o = _softmax_pv(lax.dot_general(q, k, _NT, preferred_element_type=F32), v, parity, mxu_den)
            acc = o if acc is None else acc + o
        ob_ref[:, pair * LANE:(pair + 1) * LANE] = acc.astype(BF16)


TQ_LAT = 512


def _attention(qa, qb, ka, va, kvb, kre, prev=None, cache=None, layer=0):
    if cache is None:
        tq, nb, nqt, nk, q_blk0, k_blk0 = SEQ, BATCH, 1, SEQ, 0, 0
    else:
        tq = TQ_LAT
        nb, nqt, nk, q_blk0, k_blk0 = DEC_BATCH, DEC_SEQ // tq, DEC_SEQ, T_CTX // tq, T_CTX // DEC_SEQ
    qspec = lambda w: pl.BlockSpec((tq, w), lambda b, j: (q_blk0 + b * nqt + j, 0))
    kspec = lambda w: pl.BlockSpec((nk, w), lambda b, j: (k_blk0 + b, 0))
    in_specs = [qspec(512), qspec(1024), kspec(128), kspec(128), kspec(1024), kspec(128)]
    args = [qa, qb, ka, va, kvb, kre]
    n_pieces = 1
    aliases = {}
    if cache is not None:
        cspec4 = pl.BlockSpec((None, None, PAST, 128), lambda b, j: (b, layer, 0, 0))
        cspec = lambda w: pl.BlockSpec((PAST, w), lambda b, j: (b * DEPTH + layer, 0))
        in_specs += [cspec4, cspec4, cspec(1024), cspec(128)]
        args += list(cache)
        n_pieces = 2
        in_specs += [pl.BlockSpec(memory_space=pl.ANY)] * 2
        args += list(prev)
        aliases = {len(args) - 2: 0, len(args) - 1: 1}

    def body(*refs):
        if cache is not None:
            n_in = 2 + 4 * n_pieces
            refs = refs[:n_in] + refs[n_in + 2:]
        _attn_kernel(n_pieces, *refs)

    return pl.pallas_call(
        body,
        grid=(nb, nqt),
        in_specs=in_specs,
        out_specs=[qspec(512), qspec(512)],
        out_shape=[jax.ShapeDtypeStruct((T_ALL, 512), BF16)] * 2,
        input_output_aliases=aliases,
        compiler_params=_cparams(("parallel", "arbitrary")),
        name="attention_ctx" if cache is None else "attention_lat",
    )(*args)


HL = 256
HS = 128
HG = 64
N_PAIRS = H_C // 2
FAST_DECAY_LIMIT = 80.0


def _hgrn_bottom_exact(q, k, c, lo, rev):
    row = lax.broadcasted_iota(jnp.int32, (HS, LANE), 0)
    srow = lax.broadcasted_iota(jnp.int32, (HS, HS), 0)
    scol = lax.broadcasted_iota(jnp.int32, (HS, HS), 1)
    out = []
    for parity in range(2):
        def dup(x):
            xs = pltpu.roll(x, 64, 1)
            return jnp.where(lo, x, xs) if parity == 0 else jnp.where(lo, xs, x)
        qd, kd, bd = dup(q), dup(k), dup(c)

        dg = row & 3
        if rev:
            dg = 3 - dg
        e = [None]
        for delta in range(1, 4):
            shifted = pltpu.roll(bd, delta if rev else HS - delta, 0)
            e.append(jnp.exp(jnp.minimum(shifted - bd, 0.0)))
        qp, kp = [], []
        for c1, c2 in ((0, 1), (2, 3)):
            cv = jnp.where(lo, c1, c2)
            dl = cv - dg
            fac = jnp.where(dl == 0, 1.0, jnp.where(dl == 1, e[1], jnp.where(dl == 2, e[2],
                            jnp.where(dl == 3, e[3], 0.0))))
            kp.append((kd * fac).astype(BF16))
            qp.append(jnp.where(dg == cv, qd, 0.0).astype(BF16))
        s = lax.dot_general(jnp.concatenate(qp, axis=1), jnp.concatenate(kp, axis=1), _NT,
                            preferred_element_type=F32)
        tot = jnp.where((srow >> 2) == (scol >> 2), s, 0.0)

        for lev in range(1, 3):
            g = 4 ** lev
            par = 4 * g
            shape3 = (HS // par, par, LANE)
            rid = lax.broadcasted_iota(jnp.int32, shape3, 1)
            dg3 = rid >> (2 * lev)
            if rev:
                dg3 = 3 - dg3
            b3, q3, k3 = bd.reshape(shape3), qd.reshape(shape3), kd.reshape(shape3)
            lo3 = lo.reshape(1, 1, LANE)
            qp, kp = [], []
            for c1, c2 in ((1, 2), (3, None)):
                idx = lambda cc: (4 - cc) * g if rev else cc * g - 1
                i1 = idx(c1)
                i2 = idx(c2) if c2 is not None else i1
                ridx = jnp.where(lo3, i1, i2)
                ref = jnp.sum(jnp.where(rid == ridx, b3, 0.0), axis=1, keepdims=True)
                cvk = jnp.where(lo3, c1, c2 if c2 is not None else 0)
                cvq = jnp.where(lo3, c1, c2 if c2 is not None else -1)
                kk = jnp.where(dg3 < cvk, k3 * jnp.exp(jnp.minimum(ref - b3, 0.0)), 0.0)
                qq = jnp.where(dg3 == cvq, q3 * jnp.exp(jnp.minimum(b3 - ref, 0.0)), 0.0)
                kp.append(kk.reshape(HS, LANE).astype(BF16))
                qp.append(qq.reshape(HS, LANE).astype(BF16))
            s = lax.dot_general(jnp.concatenate(qp, axis=1), jnp.concatenate(kp, axis=1), _NT,
                                preferred_element_type=F32)
            sh = 2 * lev + 2
            tot = tot + jnp.where((srow >> sh) == (scol >> sh), s, 0.0)
        out.append(tot)
    return out


SUBLANES = 8


def _group_cumsum(lf, rev):
    nv, per_group = HS // SUBLANES, HG // SUBLANES
    x = lf.reshape(nv, SUBLANES, LANE)
    sub = lax.broadcasted_iota(jnp.int32, (nv, SUBLANES, LANE), 1)
    d = 1
    while d < SUBLANES:
        if rev:
            x = x + jnp.where(sub < SUBLANES - d, pltpu.roll(x, SUBLANES - d, 1), 0.0)
        else:
            x = x + jnp.where(sub >= d, pltpu.roll(x, d, 1), 0.0)
        d *= 2
    edge = x[:, 0:1, :] if rev else x[:, SUBLANES - 1:SUBLANES, :]
    offs = [None] * nv
    for g in range(HS // HG):
        order = range(g * per_group, (g + 1) * per_group)
        acc = None
        for vi in (reversed(order) if rev else order):
            offs[vi] = jnp.zeros((1, 1, LANE), F32) if acc is None else acc
            acc = edge[vi:vi + 1] if acc is None else acc + edge[vi:vi + 1]
    return (x + jnp.concatenate(offs, axis=0)).reshape(HS, LANE)


def _hgrn_head(q, k, lf, v, st_ref, rev):
    row = lax.broadcasted_iota(jnp.int32, (HS, LANE), 0)
    lane = lax.broadcasted_iota(jnp.int32, (1, LANE), 1)
    lo = lane < 64
    hi = jnp.logical_not(lo)
    in_g1 = row >= HG

    c = _group_cumsum(lf, rev)
    if rev:
        t0, t1 = c[0:1, :], c[HG:HG + 1, :]
    else:
        t0, t1 = c[HG - 1:HG, :], c[HS - 1:HS, :]
    et0, et1 = jnp.exp(t0), jnp.exp(t1)
    qe = q * jnp.exp(c)
    e_out = jnp.exp(jnp.where(in_g1, t1, t0) - c)
    ke = k * e_out

    if rev:
        qb = qe * jnp.where(in_g1, 1.0, et1)
        kh = ke * jnp.where(in_g1, et0, 1.0)
    else:
        qb = qe * jnp.where(in_g1, et0, 1.0)
        kh = ke * jnp.where(in_g1, 1.0, et1)
    st = st_ref[...]
    o_int = lax.dot_general(qb.astype(BF16), st.astype(BF16), _NT, preferred_element_type=F32)
    upd = lax.dot_general(v, kh.astype(BF16), (((0,), (0,)), ((), ())), preferred_element_type=F32)
    r128 = lax.broadcasted_iota(jnp.int32, (LANE, LANE), 0)
    c128 = lax.broadcasted_iota(jnp.int32, (LANE, LANE), 1)
    st_ref[...] = st * (et0 * et1) + jnp.where((r128 >> 6) == (c128 >> 6), upd, 0.0)

    q_late = in_g1 if not rev else jnp.logical_not(in_g1)
    q_top = jnp.where(q_late, qe, 0.0)
    k_top = jnp.where(q_late, 0.0, ke).astype(BF16)
    top = [lax.dot_general(jnp.where(m, q_top, 0.0).astype(BF16), k_top, _NT, preferred_element_type=F32)
           for m in (lo, hi)]
    mid = HG // 2 if rev else HG // 2 - 1
    cm = c - jnp.where(in_g1, c[HG + mid:HG + mid + 1, :], c[mid:mid + 1, :])
    return c, top, o_int, cm, jnp.max(jnp.abs(cm))


def _hgrn_steps(jobs, bot_ref):
    lane = lax.broadcasted_iota(jnp.int32, (1, LANE), 1)
    lo = lane < 64
    hi = jnp.logical_not(lo)
    heads = [_hgrn_head(*job) for job in jobs]
    fast = functools.reduce(jnp.maximum, [h[4] for h in heads]) <= FAST_DECAY_LIMIT

    @pl.when(fast)
    def _():
        srow = lax.broadcasted_iota(jnp.int32, (HS, HS), 0)
        scol = lax.broadcasted_iota(jnp.int32, (HS, HS), 1)
        same = (srow >> 6) == (scol >> 6)
        for ji, (job, (_, _, _, cm, _)) in enumerate(zip(jobs, heads)):
            keep = same & ((scol >= srow) if job[5] else (scol <= srow))
            qf = job[0] * jnp.exp(cm)
            kf = (job[1] * jnp.exp(-cm)).astype(BF16)
            for parity, m in enumerate((lo, hi)):
                s = lax.dot_general(jnp.where(m, qf, 0.0).astype(BF16), kf, _NT, preferred_element_type=F32)
                bot_ref[ji, parity] = jnp.where(keep, s, 0.0)

    @pl.when(jnp.logical_not(fast))
    def _():
        for ji, (job, (c, _, _, _, _)) in enumerate(zip(jobs, heads)):
            for parity, s in enumerate(_hgrn_bottom_exact(job[0], job[1], c, lo, job[5])):
                bot_ref[ji, parity] = s

    outs = []
    for ji, (job, (_, top, o_int, _, _)) in enumerate(zip(jobs, heads)):
        v = job[3]
        probs = jnp.concatenate([(bot_ref[ji, 0] + top[0]).astype(BF16), (bot_ref[ji, 1] + top[1]).astype(BF16)],
                                axis=1)
        vv = jnp.concatenate([jnp.where(lo, v, jnp.zeros_like(v)), jnp.where(hi, v, jnp.zeros_like(v))], axis=0)
        outs.append(jnp.dot(probs, vv, preferred_element_type=F32) + o_int)
    return outs


PAIRS_PER_ITER = 4


def _hgrn_kernel(has_s0, nt, *refs):
    if has_s0:
        (qf_ref, vf_ref, kf_ref, lf_ref, qb_ref, vb_ref, kb_ref, lb_ref, s0_ref,
         of_ref, ob_ref, so_ref, st_scr, bot_scr) = refs
    else:
        (qf_ref, vf_ref, kf_ref, lf_ref, qb_ref, vb_ref, kb_ref, lb_ref,
         of_ref, ob_ref, so_ref, st_scr, bot_scr) = refs
    j = pl.program_id(1)

    @pl.when(j == 0)
    def _():
        if has_s0:
            st_scr[...] = s0_ref[...]
        else:
            st_scr[...] = jnp.zeros_like(st_scr)

    n_sub = HL // HS

    def pair_body(it, carry):
        for step in range(n_sub):
            jobs, dests = [], []
            for pp in range(PAIRS_PER_ITER):
                p = it * PAIRS_PER_ITER + pp
                cols = pl.ds(pl.multiple_of(p * LANE, LANE), LANE)
                rf = pl.ds(step * HS, HS)
                rb = pl.ds((n_sub - 1 - step) * HS, HS)
                jobs.append((qf_ref[rf, cols].astype(F32), kf_ref[rf, cols], lf_ref[rf, cols], vf_ref[rf, cols],
                             st_scr.at[p, 0], False))
                dests.append((of_ref, rf, cols))
                jobs.append((qb_ref[rb, cols].astype(F32), kb_ref[rb, cols], lb_ref[rb, cols], vb_ref[rb, cols],
                             st_scr.at[p, 1], True))
                dests.append((ob_ref, rb, cols))
            for (ref, rows, cols), o in zip(dests, _hgrn_steps(jobs, bot_scr)):
                ref[rows, cols] = o
        return carry

    lax.fori_loop(0, N_PAIRS // PAIRS_PER_ITER, pair_body, 0)

    @pl.when(j == nt - 1)
    def _():
        so_ref[...] = st_scr[...]


def _hgrn(qc, vc, kff, lff, kfb, lfb, s0=None, prev=None, layer=0):
    if s0 is None:
        nb, nt, blk0 = BATCH, SEQ // HL, 0
    else:
        nb, nt, blk0 = DEC_BATCH, DEC_SEQ // HL, T_CTX // HL
    fspec = pl.BlockSpec((HL, 512), lambda b, j: (blk0 + b * nt + j, 0))
    bspec = pl.BlockSpec((HL, 512), lambda b, j: (blk0 + b * nt + nt - 1 - j, 0))
    sspec = pl.BlockSpec((None, N_PAIRS, 2, LANE, LANE), lambda b, j: (b, 0, 0, 0, 0))
    in_specs = [fspec] * 4 + [bspec] * 4
    args = [qc, vc, kff, lff, qc, vc, kfb, lfb]
    aliases = {}
    if s0 is not None:
        in_specs += [pl.BlockSpec((None, None, N_PAIRS, 2, LANE, LANE), lambda b, j: (b, layer, 0, 0, 0, 0))]
        args += [s0]
        in_specs += [pl.BlockSpec(memory_space=pl.ANY)] * 2
        args += list(prev)
        aliases = {len(args) - 2: 0, len(args) - 1: 1}

    def body(*refs):
        if s0 is not None:
            refs = refs[:9] + refs[11:]
        _hgrn_kernel(s0 is not None, nt, *refs)

    return pl.pallas_call(
        body,
        grid=(nb, nt),
        in_specs=in_specs,
        out_specs=[fspec, bspec, sspec],
        out_shape=[jax.ShapeDtypeStruct((T_ALL, 512), F32), jax.ShapeDtypeStruct((T_ALL, 512), F32),
                   jax.ShapeDtypeStruct((nb, N_PAIRS, 2, LANE, LANE), F32)],
        scratch_shapes=[pltpu.VMEM((N_PAIRS, 2, LANE, LANE), F32),
                        pltpu.VMEM((2 * PAIRS_PER_ITER, 2, HS, HS), F32)],
        input_output_aliases=aliases,
        compiler_params=_cparams(("parallel", "arbitrary")),
        name="hgrn_ctx" if s0 is None else "hgrn_lat",
    )(*args)


def _merge_kernel(xc_ref, xl_ref, mod_ref, n1_ref, wg_ref, oa_ref, ob_ref, ocf_ref, ocb_ref, sg_ref, con_ref,
                  ones_ref, wbr_ref, wout_ref, n2_ref, wrh_ref, wrl_ref, br_ref, tri_ref, upper_ref,
                  x1_o, h2_o, comb_o, rank_o, carry_o, tab_o, carry_scr):
    x = _x_tile(xc_ref, xl_ref, TMG)
    mod = mod_ref[...]
    xn = x * lax.rsqrt(jnp.mean(x * x, axis=-1, keepdims=True) + EPS) * n1_ref[...]
    h = (xn * (1.0 + mod[:, D:2 * D]) + mod[:, 0:D]).astype(BF16)

    oc = ocf_ref[...] + ocb_ref[...]
    oc = oc * lax.rsqrt(_group_mean(oc * oc, ones_ref[...], DV_C) + EPS) * con_ref[...]
    oc = (oc * sg_ref[...].astype(F32)).astype(BF16)
    branches = (oa_ref[...], ob_ref[...], oc)
    mix = None
    for jb in range(3):
        gate = jax.nn.sigmoid(jnp.dot(h, wg_ref[:, jb * D:(jb + 1) * D], preferred_element_type=F32))
        t = gate * jnp.dot(branches[jb], wbr_ref[jb], preferred_element_type=F32)
        mix = t if mix is None else mix + t
    out = jnp.dot(mix.astype(BF16), wout_ref[...], preferred_element_type=F32)
    x1 = x + mod[:, 2 * D:3 * D] * out
    x1_o[...] = x1

    x1n = x1 * lax.rsqrt(jnp.mean(x1 * x1, axis=-1, keepdims=True) + EPS) * n2_ref[...]
    h2 = x1n * (1.0 + mod[:, 4 * D:5 * D]) + mod[:, 3 * D:4 * D]
    h2_o[...] = h2.astype(BF16)

    h2h, h2l = _split_hi_lo(h2)
    logits = (jnp.dot(h2h, wrh_ref[...], preferred_element_type=F32)
              + jnp.dot(h2l, wrh_ref[...], preferred_element_type=F32)
              + jnp.dot(h2h, wrl_ref[...], preferred_element_type=F32)) + br_ref[...]
    lane = lax.broadcasted_iota(jnp.int32, logits.shape, 1)
    neg = -jnp.inf
    is_g = lane < N_GROUPS
    gl = jnp.where(is_g, logits, neg)
    gmax = gl.max(axis=-1, keepdims=True)
    gidx = jnp.min(jnp.where(gl == gmax, lane, LANE), axis=-1, keepdims=True)
    gp = 1.0 / jnp.sum(jnp.where(is_g, jnp.exp(gl - gmax), 0.0), axis=-1, keepdims=True)
    eid = lane - N_GROUPS
    in_grp = (eid >= 0) & (eid < N_EXPERTS) & ((eid >> 2) == gidx)
    el = jnp.where(in_grp, logits, neg)
    v1 = el.max(axis=-1, keepdims=True)
    i1 = jnp.min(jnp.where(el == v1, lane, LANE), axis=-1, keepdims=True)
    el2 = jnp.where(lane == i1, neg, el)
    v2 = el2.max(axis=-1, keepdims=True)
    i2 = jnp.min(jnp.where(el2 == v2, lane, LANE), axis=-1, keepdims=True)
    e2 = jnp.exp(v2 - v1)
    w1 = gp / (1.0 + e2)
    w2 = gp * e2 / (1.0 + e2)
    comb = jnp.where(lane == i1, w1, 0.0) + jnp.where(lane == i2, w2, 0.0)
    comb = pltpu.roll(comb, LANE - N_GROUPS, 1)
    comb_o[...] = comb

    step = pl.program_id(0) % MOE_STEPS

    @pl.when(step == 0)
    def _():
        carry_scr[...] = jnp.zeros_like(carry_scr)

    for sb in range(TMG // SBK):
        sub = comb[sb * SBK:(sb + 1) * SBK, :]
        routed = sub > 0.0
        ind = jnp.where(routed, 1.0, 0.0)
        carry = carry_scr[...]
        rank = jnp.dot(tri_ref[...], ind.astype(BF16), preferred_element_type=F32) + carry
        rank_o[sb * SBK:(sb + 1) * SBK, :] = jnp.where(routed, rank, -1.0)
        carry_o[sb] = carry
        carry_scr[...] = carry + jnp.sum(ind, axis=0, keepdims=True)

    @pl.when(step == MOE_STEPS - 1)
    def _():
        count = carry_scr[...]
        seg = jnp.floor((count + (SEG_ALIGN - 1.0)) * (1.0 / SEG_ALIGN)) * SEG_ALIGN
        off = jnp.dot(jnp.broadcast_to(seg, (8, LANE)), upper_ref[...], preferred_element_type=F32,
                      precision=lax.Precision.HIGHEST)
        tab_o[0:1, :] = count
        tab_o[1:2, :] = off[0:1, :]


TMG = 512


def _merge(layer, x, mod, n1, wgate, oa, ob, ocf, ocb, sg, con, ones, wbr, wout, n2, wr_hi, wr_lo, br):
    ctx_tiles, per_seq = T_CTX // TMG, DEC_SEQ // TMG
    mrow = lambda i: jnp.where(i < ctx_tiles, CTX_MOD_ROW, (i - ctx_tiles) // per_seq)
    tspec = lambda w: pl.BlockSpec((TMG, w), lambda i: (i, 0))
    wspec = lambda tail: pl.BlockSpec((None,) + tail, lambda i: (layer,) + (0,) * len(tail),
                                      pipeline_mode=pl.Buffered(1))
    return pl.pallas_call(
        _merge_kernel,
        grid=(T_ALL // TMG,),
        in_specs=_x_specs(x, TMG) + [
            _mod_spec(layer, mrow),
            _layer_spec((1, D), layer), wspec((D, 3 * D)),
            tspec(512), tspec(512), tspec(512), tspec(512), tspec(512),
            _layer_spec((1, 512), layer), _const_spec((256, 256)),
            wspec((3, BRANCH_W, D)), wspec((D, D)), _layer_spec((1, D), layer),
            _layer_spec((D, LANE), layer), _layer_spec((D, LANE), layer), _layer_spec((1, LANE), layer),
            _const_spec((SBK, SBK)), _const_spec((LANE, LANE)),
        ],
        out_specs=[
            tspec(D), tspec(D), tspec(LANE), tspec(LANE),
            pl.BlockSpec((None, TMG // SBK, 1, LANE), lambda i: (i // MOE_STEPS, i % MOE_STEPS, 0, 0)),
            pl.BlockSpec((None, 2, LANE), lambda i: (i // MOE_STEPS, 0, 0)),
        ],
        out_shape=[jax.ShapeDtypeStruct((T_ALL, D), F32), jax.ShapeDtypeStruct((T_ALL, D), BF16),
                   jax.ShapeDtypeStruct((T_ALL, LANE), F32), jax.ShapeDtypeStruct((T_ALL, LANE), F32),
                   jax.ShapeDtypeStruct((N_BLK, N_SB, 1, LANE), F32), jax.ShapeDtypeStruct((N_BLK, 2, LANE), F32)],
        scratch_shapes=[pltpu.VMEM((1, LANE), F32)],
        compiler_params=_cparams(("arbitrary",)),
        name="merge",
    )(*_x_pair(x)[:2], mod, n1, wgate, oa, ob, ocf, ocb, sg, con, ones, wbr, wout, n2, wr_hi, wr_lo, br,
      jnp.asarray(np.tril(np.ones((SBK, SBK)), -1), BF16), jnp.asarray(np.triu(np.ones((LANE, LANE)), 1), F32))


NB = 2048
N_BLK = T_ALL // NB
SBK = 256
N_SB = NB // SBK
WIN_SHIFT, FT_SHIFT, SEG_SHIFT = 6, 7, 4
WIN = 1 << WIN_SHIFT
FT = 1 << FT_SHIFT
SEG_ALIGN = 1 << SEG_SHIFT
STG = 2 * NB + N_EXPERTS * SEG_ALIGN + 256
QUAD = 4
E_STEP = 2
MOE_STEPS = NB // TMG


def _moe_kernel(cnt_s, off_s, car_s, h2_ref, rank_ref, comb_ref, offv_ref, eg_ref, eu_ref, ed_ref, o_ref,
                stg_ref, acc_ref):
    blk = pl.program_id(0)
    step = pl.program_id(1)
    srow = lax.broadcasted_iota(jnp.int32, (WIN, SBK), 0).astype(F32)

    def windows(s):
        out = []
        for ex in range(N_EXPERTS):
            start = off_s[blk, ex] + car_s[blk, s, ex]
            length = car_s[blk, s + 1, ex] - car_s[blk, s, ex]
            ws = (start >> SEG_SHIFT) << SEG_SHIFT
            out.append((ws, (start - ws + length + (WIN - 1)) >> WIN_SHIFT))
        return out, functools.reduce(jnp.maximum, [w[1] for w in out])

    def positions(s):
        rows = pl.ds(pl.multiple_of(s * SBK, SBK), SBK)
        rank = rank_ref[rows, :]
        return rows, jnp.where(rank >= 0.0, rank + offv_ref[1:2, :], -1.0e6)

    @pl.when(step == 0)
    def _():
        stg_ref[...] = jnp.zeros_like(stg_ref)

        def sub_body(s, carry):
            rows, pos = positions(s)
            pos_t = pos.T
            h2 = h2_ref[rows, :]
            wins, nmax = windows(s)

            def chunk_body(c, carry2):
                for quad in range(N_EXPERTS // QUAD):
                    blocks = []
                    for ex in range(quad * QUAD, (quad + 1) * QUAD):
                        base = (wins[ex][0] + c * WIN).astype(F32)
                        hit = (pos_t[ex:ex + 1, :] - base) == srow
                        blocks.append(jnp.where(hit, 1.0, 0.0).astype(BF16))
                    moved = jnp.dot(jnp.concatenate(blocks, axis=0), h2, preferred_element_type=F32).astype(BF16)
                    for i in range(QUAD):
                        first = jnp.minimum(wins[quad * QUAD + i][0] + c * WIN, STG - WIN)
                        dst = pl.ds(pl.multiple_of(first, SEG_ALIGN), WIN)
                        stg_ref[dst, :] = stg_ref[dst, :] + moved[i * WIN:(i + 1) * WIN, :]
                return carry2

            lax.fori_loop(0, nmax, chunk_body, 0)
            return carry

        lax.fori_loop(0, N_SB, sub_body, 0)

    def ffn_tile(j, first, n_rows, n_valid):
        rows = pl.ds(pl.multiple_of(first, SEG_ALIGN), n_rows)
        xs = stg_ref[rows, :]
        hg = jnp.dot(xs, eg_ref[j], preferred_element_type=F32)
        hu = jnp.dot(xs, eu_ref[j], preferred_element_type=F32)
        act = (hg * jax.nn.sigmoid(hg) * hu).astype(BF16)
        y = jnp.dot(act, ed_ref[j], preferred_element_type=F32).astype(BF16)
        rid = lax.broadcasted_iota(jnp.int32, (n_rows, 1), 0)
        stg_ref[rows, :] = jnp.where(rid < n_valid, y, xs)

    for j in range(E_STEP):
        ex = step * E_STEP + j
        count = cnt_s[blk, ex]
        seg0 = off_s[blk, ex]
        n_big = (count + (FT - 1)) >> (FT_SHIFT + 1)

        def ffn_body(t, carry, j=j, count=count, seg0=seg0):
            ffn_tile(j, seg0 + t * (2 * FT), 2 * FT, count - t * (2 * FT))
            return carry

        lax.fori_loop(0, n_big, ffn_body, 0)

        @pl.when(count > n_big * (2 * FT))
        def _(j=j, count=count, seg0=seg0, n_big=n_big):
            ffn_tile(j, seg0 + n_big * (2 * FT), FT, count - n_big * (2 * FT))

    @pl.when(step == N_EXPERTS // E_STEP - 1)
    def _():
        def sub_body(s, carry):
            rows, pos = positions(s)
            pos_t = pos.T
            wts_t = comb_ref[rows, :].T
            wins, nmax = windows(s)
            acc_ref[...] = jnp.zeros_like(acc_ref)

            def chunk_body(c, carry2):
                sel_t, srcs = [], []
                for ex in range(N_EXPERTS):
                    base = (wins[ex][0] + c * WIN).astype(F32)
                    hit = (pos_t[ex:ex + 1, :] - base) == srow
                    sel_t.append(jnp.where(hit, wts_t[ex:ex + 1, :], 0.0).astype(BF16))
                    first = jnp.minimum(wins[ex][0] + c * WIN, STG - WIN)
                    srcs.append(stg_ref[pl.ds(pl.multiple_of(first, SEG_ALIGN), WIN), :])
                acc_ref[...] += lax.dot_general(jnp.concatenate(sel_t, axis=0), jnp.concatenate(srcs, axis=0),
                                                (((0,), (0,)), ((), ())), preferred_element_type=F32)
                return carry2

            lax.fori_loop(0, nmax, chunk_body, 0)
            o_ref[rows, :] = acc_ref[...].astype(BF16)
            return carry

        lax.fori_loop(0, N_SB, sub_body, 0)


def _moe(layer, h2, comb, rank, carry, tab, eg, eu, ed):
    cnt_i = tab[:, 0, :N_EXPERTS].astype(jnp.int32)
    off_i = tab[:, 1, :N_EXPERTS].astype(jnp.int32)
    car_i = jnp.concatenate([carry[:, :, 0, :N_EXPERTS].astype(jnp.int32), cnt_i[:, None, :]], axis=1)
    bspec = lambda w: pl.BlockSpec((NB, w), lambda b, e, *_: (b, 0))
    grid_spec = pltpu.PrefetchScalarGridSpec(
        num_scalar_prefetch=3,
        grid=(N_BLK, N_EXPERTS // E_STEP),
        in_specs=[
            bspec(D), bspec(LANE), bspec(LANE),
            pl.BlockSpec((None, 2, LANE), lambda b, e, *_: (b, 0, 0)),
            pl.BlockSpec((None, E_STEP, D, D_EXPERT), lambda b, e, *_: (layer, e, 0, 0)),
            pl.BlockSpec((None, E_STEP, D, D_EXPERT), lambda b, e, *_: (layer, e, 0, 0)),
            pl.BlockSpec((None, E_STEP, D_EXPERT, D), lambda b, e, *_: (layer, e, 0, 0)),
        ],
        out_specs=bspec(D),
        scratch_shapes=[pltpu.VMEM((STG, D), BF16), pltpu.VMEM((SBK, D), F32)],
    )
    return pl.pallas_call(
        _moe_kernel,
        grid_spec=grid_spec,
        out_shape=jax.ShapeDtypeStruct((T_ALL, D), BF16),
        compiler_params=_cparams(("parallel", "arbitrary")),
        name="moe",
    )(cnt_i, off_i, car_i, h2, rank, comb, tab, eg, eu, ed)


def _residual_kernel(final, x1_ref, moe_ref, mod_ref, fg_ref, o_ref):
    x2 = x1_ref[...] + mod_ref[:, 5 * D:6 * D] * moe_ref[...].astype(F32)
    if final:
        x2 = x2 * lax.rsqrt(jnp.mean(x2 * x2, axis=-1, keepdims=True) + EPS) * fg_ref[...]
    o_ref[...] = x2


TR = DEC_SEQ


def _residual(layer, final, x1, moe, mod, fg, tile0=0, n_tiles=T_ALL // TR):
    ctx_tiles = T_CTX // TR
    mrow = lambda i: jnp.where(tile0 + i < ctx_tiles, CTX_MOD_ROW, tile0 + i - ctx_tiles)
    src = lambda: pl.BlockSpec((TR, D), lambda i: (tile0 + i, 0))
    return pl.pallas_call(
        functools.partial(_residual_kernel, final),
        grid=(n_tiles,),
        in_specs=[src(), src(), _mod_spec(layer, mrow), _const_spec((1, D))],
        out_specs=pl.BlockSpec((TR, D), lambda i: (i, 0)),
        out_shape=jax.ShapeDtypeStruct((n_tiles * TR, D), F32),
        compiler_params=_cparams(("parallel",)),
        name="residual",
    )(x1, moe, mod, fg)


def _state_to_blockdiag(s):
    lead = s.shape[:-3]
    st = jnp.swapaxes(s, -1, -2).reshape(lead + (N_PAIRS, 2, DV_C, DK_C))
    z = jnp.zeros_like(st[..., 0, :, :])
    top = jnp.concatenate([st[..., 0, :, :], z], axis=-1)
    bot = jnp.concatenate([z, st[..., 1, :, :]], axis=-1)
    return jnp.concatenate([top, bot], axis=-2)


def _blockdiag_to_state(sb):
    lead = sb.shape[:-3]
    even = sb[..., :DV_C, :DK_C]
    odd = sb[..., DV_C:, DK_C:]
    st = jnp.stack([even, odd], axis=-3).reshape(lead + (H_C, DV_C, DK_C))
    return jnp.swapaxes(st, -1, -2)


def kernel(x_prompt, x_sample, cache_gqa_k, cache_gqa_v, cache_mla_ckv, cache_mla_krope, state_hgrn, c, c_ctx,
           w_mod, b_mod, norm1_g, norm2_g, w_in, a_qnorm, a_knorm, b_qnorm, b_wq, b_kvnorm, b_wkv, c_lb_logits,
           c_onorm, w_branch, w_out, r_group_w, r_group_b, r_expert_w, r_expert_b, e_gate, e_up, e_down, final_g):
    x = (x_prompt.reshape(T_CTX, D), x_sample.reshape(T_LAT, D))
    cvec = jnp.concatenate([c, c_ctx[None, :], jnp.zeros((MOD_ROWS - DEC_BATCH - 1, D), F32)], axis=0)
    mod = _mod_table(cvec, w_mod, b_mod).reshape(DEPTH * MOD_ROWS, 1, 6 * D)
    taba, tabb, tabk = _rope_tables()
    ones = _ones_block(256, 64)
    lbl = c_lb_logits.reshape(DEPTH, 2, H_C * DK_C)

    vec = lambda g, reps=1: jnp.tile(g, (1, reps))[:, None, :]
    n1, n2 = vec(norm1_g), vec(norm2_g)
    aq, ak, con = vec(a_qnorm, H_A), vec(a_knorm, KV_A), vec(c_onorm, H_C)
    bq, bkv = vec(b_qnorm), vec(b_kvnorm)
    w_in_p, w_gate = _pack_w_in(w_in)
    wq_p, wkv_p = _pack_wq(b_wq), _pack_wkv(b_wkv)
    wbr, wout = w_branch.astype(BF16), w_out.astype(BF16)
    n_pad = LANE - N_GROUPS - N_EXPERTS
    wr_hi, wr_lo = _split_hi_lo(jnp.concatenate([r_group_w, r_expert_w, jnp.zeros((DEPTH, D, n_pad), F32)], axis=-1))
    br = jnp.concatenate([r_group_b, r_expert_b, jnp.zeros((DEPTH, n_pad), F32)], axis=-1)[:, None, :]
    eg, eu, ed = e_gate.astype(BF16), e_up.astype(BF16), e_down.astype(BF16)

    ck = cache_gqa_k.reshape(DEC_BATCH, DEPTH, PAST, KV_A * HD_A)
    cv = cache_gqa_v.reshape(DEC_BATCH, DEPTH, PAST, KV_A * HD_A)
    ckvb = _ctx_kv(cache_mla_ckv.reshape(DEC_BATCH * DEPTH * PAST, KV_RANK), wkv_p)
    ckr = cache_mla_krope.reshape(DEC_BATCH * DEPTH * PAST, ROPE_B)
    zpad = jnp.zeros_like(ckr)
    ckre = jnp.concatenate([ckr, zpad, ckr, zpad], axis=1)
    s0 = jnp.swapaxes(_state_to_blockdiag(state_hgrn), 2, 3)

    new_k, new_v, new_ckv, new_kr, new_s = [], [], [], [], []
    for l in range(DEPTH):
        (qa, kan, ka, va, qb, ckv, kvb, kr, kre, lff, lfb, kff, kfb, qc, vc, sg) = _inproj(
            l, x, mod, n1, w_in_p, aq, ak, bq, wq_p, bkv, wkv_p, lbl, ones, taba, tabb, tabk)
        new_k.append(kan[:T_CTX])
        new_v.append(va[:T_CTX])
        new_ckv.append(ckv[:T_CTX])
        new_kr.append(kr[:T_CTX, :ROPE_B])

        oa, ob = _attention(qa, qb, ka, va, kvb, kre)
        oa, ob = _attention(qa, qb, ka, va, kvb, kre, prev=(oa, ob), cache=(ck, cv, ckvb, ckre), layer=l)

        ocf, ocb, s_ctx = _hgrn(qc, vc, kff, lff, kfb, lfb)
        ocf, ocb, _ = _hgrn(qc, vc, kff, lff, kfb, lfb, s0=s0, prev=(ocf, ocb), layer=l)
        new_s.append(s_ctx)

        x1, h2, comb, rank, carry, tab = _merge(l, x, mod, n1, w_gate, oa, ob, ocf, ocb, sg, con, ones, wbr, wout,
                                                n2, wr_hi, wr_lo, br)
        moe = _moe(l, h2, comb, rank, carry, tab, eg, eu, ed)
        if l < DEPTH - 1:
            x = _residual(l, False, x1, moe, mod, final_g[None, :])

    last = DEPTH - 1
    y_prompt = _residual(last, True, x1, moe, mod, final_g[None, :], 0, T_CTX // TR)
    y_sample = _residual(last, True, x1, moe, mod, final_g[None, :], T_CTX // TR, T_LAT // TR)
    stack = lambda parts, tail: jnp.stack([p.reshape(BATCH, SEQ, -1) for p in parts], axis=1).reshape(
        (BATCH, DEPTH, SEQ) + tail)
    states = _blockdiag_to_state(jnp.swapaxes(jnp.stack(new_s, axis=1), 2, 3))
    return (y_prompt.reshape(BATCH, SEQ, D), y_sample.reshape(DEC_BATCH, DEC_SEQ, D),
            stack(new_k, (KV_A, HD_A)), stack(new_v, (KV_A, HD_A)), stack(new_ckv, (KV_RANK,)),
            stack(new_kr, (ROPE_B,)), states)
```

```python
import functools

import numpy as np
import jax
import jax.numpy as jnp
from jax import lax
from jax.experimental import pallas as pl
from jax.experimental.pallas import tpu as pltpu

D = 1024
BATCH, SEQ = 32, 256
DEC_BATCH, DEC_SEQ = 8, 1024
PAST = 256
DEPTH = 2
GRID_W = 64
THETA = 10000.0
EPS = 1e-6
F_FLOOR = 1e-30
H_A, KV_A, HD_A = 8, 2, 64
H_B, Q_RANK, KV_RANK, NOPE_B, ROPE_B, V_B = 8, 384, 256, 64, 32, 64
H_C, DK_C, DV_C = 8, 64, 64
BRANCH_W = 512
N_GROUPS, E_PER_GROUP, N_EXPERTS, D_EXPERT = 4, 4, 16, 512

T_CTX = BATCH * SEQ
T_LAT = DEC_BATCH * DEC_SEQ
T_ALL = T_CTX + T_LAT
TMI = 512
MOD_ROWS = 16
CTX_MOD_ROW = DEC_BATCH
LANE = 128
VMEM_LIMIT = 56 * 1024 * 1024

C_QA, C_KA, C_VA, C_QRA, C_KVA, C_KR = 0, 512, 640, 768, 1152, 1408
C_FF, C_FB, C_QC, C_IC, C_GC, C_END = 1536, 2048, 2560, 3072, 3584, 4096
R_QA, R_KA, R_VA, R_QRA, R_KVA, R_KR = 0, 512, 640, 768, 1152, 1408
R_FF, R_FB, R_QC, R_IC, R_GC, R_GATE, R_END = 1440, 1952, 2464, 2976, 3488, 4000, 7072

F32 = jnp.float32
BF16 = jnp.bfloat16


def _cparams(sem):
    return pltpu.CompilerParams(dimension_semantics=sem, vmem_limit_bytes=VMEM_LIMIT)


def _split_hi_lo(x):
    hi = x.astype(BF16)
    lo = (x - hi.astype(F32)).astype(BF16)
    return hi, lo


def _group_mean(x2, ones_blk, width):
    n = ones_blk.shape[0]
    outs = []
    for j in range(x2.shape[-1] // n):
        blk = x2[:, j * n:(j + 1) * n]
        hi, lo = _split_hi_lo(blk)
        s = jnp.dot(hi, ones_blk, preferred_element_type=F32) + jnp.dot(lo, ones_blk, preferred_element_type=F32)
        outs.append(s)
    s = outs[0] if len(outs) == 1 else jnp.concatenate(outs, axis=-1)
    return s * (1.0 / width)


def _rope(x, tab_ref, shift, period):
    c, s1, s2 = tab_ref[0], tab_ref[1], tab_ref[2]
    outs = []
    for j in range(x.shape[-1] // period):
        blk = x[:, j * period:(j + 1) * period]
        outs.append(blk * c + pltpu.roll(blk, shift, 1) * s1 + pltpu.roll(blk, period - shift, 1) * s2)
    return outs[0] if len(outs) == 1 else jnp.concatenate(outs, axis=-1)


def _mod_kernel(c_ref, w_ref, b_ref, o_ref):
    c = c_ref[...]
    a = c * jax.nn.sigmoid(c)
    o_ref[...] = jnp.dot(a, w_ref[...], preferred_element_type=F32, precision=lax.Precision.HIGHEST) + b_ref[...]


def _mod_table(cvec, w_mod, b_mod):
    nt = 1024
    return pl.pallas_call(
        _mod_kernel,
        grid=(DEPTH, 6 * D // nt),
        in_specs=[
            pl.BlockSpec((MOD_ROWS, D), lambda l, j: (0, 0)),
            pl.BlockSpec((None, D, nt), lambda l, j: (l, 0, j)),
            pl.BlockSpec((None, 1, nt), lambda l, j: (l, 0, j)),
        ],
        out_specs=pl.BlockSpec((None, MOD_ROWS, nt), lambda l, j: (l, 0, j)),
        out_shape=jax.ShapeDtypeStruct((DEPTH, MOD_ROWS, 6 * D), F32),
        compiler_params=_cparams(("arbitrary", "arbitrary")),
        name="mod_table",
    )(cvec, w_mod, b_mod.reshape(DEPTH, 1, 6 * D))


def _x_pair(x):
    if isinstance(x, tuple):
        return x[0], x[1], 0
    return x, x, T_CTX


def _x_specs(x, tile):
    _, _, lat_off = _x_pair(x)
    ctx = T_CTX // tile
    return [pl.BlockSpec((tile, D), lambda i: (jnp.minimum(i, ctx - 1), 0)),
            pl.BlockSpec((tile, D), lambda i: (jnp.maximum(i - ctx, 0) + lat_off // tile, 0))]


def _x_tile(xc_ref, xl_ref, tile):
    return jnp.where(pl.program_id(0) < T_CTX // tile, xc_ref[...], xl_ref[...])


def _inproj_kernel(layer, fused, *refs):
    if fused:
        x1_ref, moe_ref, gmod_ref = refs[:3]
        refs, x_o = refs[3:-1], refs[-1]
        x = x1_ref[...] + gmod_ref[:, 5 * D:6 * D] * moe_ref[...].astype(F32)
        x_o[...] = x
    else:
        x = _x_tile(refs[0], refs[1], TMI)
        refs = refs[2:]
    (mod_ref, n1_ref, w_ref, aq_ref, ak_ref, bq_ref, wq_ref, bkv_ref, wkv_ref,
     lbl_ref, ones_ref, taba_ref, tabb_ref, tabk_ref,
     qa_o, kan_o, ka_o, va_o, qb_o, ckv_o, kvb_o, kr_o, kre_o,
     lff_o, lfb_o, kff_o, kfb_o, qc_o, vc_o, sg_o) = refs
    mod = mod_ref[...]
    xn = x * lax.rsqrt(jnp.mean(x * x, axis=-1, keepdims=True) + EPS) * n1_ref[...]
    h = (xn * (1.0 + mod[:, D:2 * D]) + mod[:, 0:D]).astype(BF16)
    y_all = jnp.dot(h, w_ref[...], preferred_element_type=F32)

    def y(c0, c1):
        return y_all[:, c0:c1]

    ones = ones_ref[...]

    qa = y(C_QA, C_KA)
    qa = qa * lax.rsqrt(_group_mean(qa * qa, ones, HD_A) + EPS) * aq_ref[...]
    qa_o[...] = (_rope(qa, taba_ref, 16, LANE) * (HD_A ** -0.5)).astype(BF16)
    ka = y(C_KA, C_VA)
    ka = ka * lax.rsqrt(_group_mean(ka * ka, ones[:LANE, :LANE], HD_A) + EPS) * ak_ref[...]
    kan_o[...] = ka
    ka_o[...] = _rope(ka, taba_ref, 16, LANE).astype(BF16)
    va_o[...] = y(C_VA, C_QRA)

    qr = y(C_QRA, C_KVA)
    qr = qr * lax.rsqrt(jnp.mean(qr * qr, axis=-1, keepdims=True) + EPS) * bq_ref[...]
    qb = jnp.dot(qr.astype(BF16), wq_ref[...], preferred_element_type=F32)
    qb_o[...] = (_rope(qb, tabb_ref, 8, 2 * LANE) * ((NOPE_B + ROPE_B) ** -0.5)).astype(BF16)
    kv = y(C_KVA, C_KR)
    ckv = kv * lax.rsqrt(jnp.mean(kv * kv, axis=-1, keepdims=True) + EPS) * bkv_ref[...]
    ckv_o[...] = ckv
    kvb_o[...] = jnp.dot(ckv.astype(BF16), wkv_ref[...], preferred_element_type=F32).astype(BF16)
    kr = y(C_KR, C_FF)
    kr_o[...] = kr
    kre_o[...] = _rope(kr, tabk_ref, 8, LANE).astype(BF16)

    lbl = lbl_ref[...]
    e = jnp.exp(lbl - jnp.max(lbl, axis=0, keepdims=True))
    p = e / jnp.sum(e, axis=0, keepdims=True)
    lb = p[0] * 0.0
    for i in range(1, layer + 1):
        lb = lb + p[i]
    for d, (c0, lf_o, kf_o) in enumerate(((C_FF, lff_o, kff_o), (C_FB, lfb_o, kfb_o))):
        pre = y(c0, c0 + 512)
        lbd = lb[d:d + 1, :]
        f = jnp.maximum(lbd + (1.0 - lbd) * jax.nn.sigmoid(pre), F_FLOOR)
        lf_o[...] = jnp.log(f)
        kf_o[...] = 1.0 - f
    qc_o[...] = y(C_QC, C_IC).astype(BF16)
    vc_o[...] = y(C_IC, C_GC).astype(BF16)
    gc = y(C_GC, C_END)
    sg_o[...] = (gc * jax.nn.sigmoid(gc)).astype(BF16)


def _const_spec(shape):
    nd = len(shape)
    return pl.BlockSpec(shape, lambda i: (0,) * nd)


def _layer_spec(tail, layer):
    return pl.BlockSpec((None,) + tuple(tail), lambda *_: (layer,) + (0,) * len(tail))


def _mod_spec(layer, row_of_tile):
    return pl.BlockSpec((None, 1, 6 * D), lambda i, *_: (layer * MOD_ROWS + row_of_tile(i), 0, 0))


def _inproj(layer, x, mod, n1, w_in_p, aq, ak, bq, wq_p, bkv, wkv_p, lbl, ones, taba, tabb, tabk, pending=None):
    outs = [
        (512, BF16), (128, F32), (128, BF16), (128, F32), (1024, BF16), (256, F32), (1024, BF16),
        (128, F32), (128, BF16), (512, F32), (512, F32), (512, F32), (512, F32), (512, BF16), (512, BF16),
        (512, BF16),
    ]
    ctx_tiles, per_seq = T_CTX // TMI, DEC_SEQ // TMI
    mrow = lambda i: jnp.where(i < ctx_tiles, CTX_MOD_ROW, (i - ctx_tiles) // per_seq)
    pos_blk = lambda i: jnp.where(i < ctx_tiles, per_seq, (i - ctx_tiles) % per_seq)
    tab_spec = lambda w: pl.BlockSpec((3, TMI, w), lambda i: (0, pos_blk(i), 0))
    tile_spec = lambda w: pl.BlockSpec((TMI, w), lambda i: (i, 0))
    fused = pending is not None
    if fused:
        x_specs = [tile_spec(D), tile_spec(D), _mod_spec(layer - 1, mrow)]
        x_args = [pending[0], pending[1], mod]
        outs = outs + [(D, F32)]
    else:
        x_specs, x_args = _x_specs(x, TMI), list(_x_pair(x)[:2])
    return pl.pallas_call(
        functools.partial(_inproj_kernel, layer, fused),
        grid=(T_ALL // TMI,),
        in_specs=x_specs + [
            _mod_spec(layer, mrow),
            _layer_spec((1, D), layer),
            _layer_spec((D, C_END), layer),
            _layer_spec((1, 512), layer), _layer_spec((1, 128), layer), _layer_spec((1, Q_RANK), layer),
            _layer_spec((Q_RANK, 1024), layer), _layer_spec((1, KV_RANK), layer), _layer_spec((KV_RANK, 1024), layer),
            _const_spec((DEPTH, 2, 512)), _const_spec((256, 256)),
            tab_spec(LANE), tab_spec(2 * LANE), tab_spec(LANE),
        ],
        out_specs=[tile_spec(w) for w, _ in outs],
        out_shape=[jax.ShapeDtypeStruct((T_ALL, w), dt) for w, dt in outs],
        compiler_params=_cparams(("parallel",)),
        name="inproj",
    )(*x_args, mod, n1, w_in_p, aq, ak, bq, wq_p, bkv, wkv_p, lbl, ones, taba, tabb, tabk)


def _pack_w_in(w):
    w = w.astype(BF16)
    z = jnp.zeros((DEPTH, D, 32), BF16)
    kr = w[..., R_KR:R_FF]
    main = jnp.concatenate([w[..., :R_KR], kr, z, kr, z, w[..., R_FF:R_GATE]], axis=-1)
    return main, w[..., R_GATE:]


def _pack_wq(wq):
    w = wq.reshape(DEPTH, Q_RANK, H_B, NOPE_B + ROPE_B)
    nope, rope = w[..., :NOPE_B], w[..., NOPE_B:]
    z = jnp.zeros((DEPTH, Q_RANK, H_B, 32), wq.dtype)
    even = jnp.concatenate([rope, z, nope], axis=-1)
    odd = jnp.concatenate([nope, rope, z], axis=-1)
    is_even = (jnp.arange(H_B) % 2 == 0)[None, None, :, None]
    return jnp.where(is_even, even, odd).reshape(DEPTH, Q_RANK, H_B * LANE).astype(BF16)


def _pack_wkv(wkv):
    w = wkv.reshape(DEPTH, KV_RANK, H_B, NOPE_B + V_B)
    nope, v = w[..., :NOPE_B], w[..., NOPE_B:]
    is_even = (jnp.arange(H_B) % 2 == 0)[None, None, :, None]
    return jnp.where(is_even, jnp.concatenate([v, nope], -1), jnp.concatenate([nope, v], -1)).reshape(
        DEPTH, KV_RANK, H_B * LANE).astype(BF16)


def _rope_tables():
    pos = np.arange(DEC_SEQ)
    row, col = pos // GRID_W, pos % GRID_W

    def pattern(half):
        quarter = half // 2
        inv = THETA ** (-np.arange(0, half, 2, dtype=np.float64) / half)
        ang = np.concatenate([row[:, None] * inv, row[:, None] * inv, col[:, None] * inv, col[:, None] * inv], 1)
        is_x2 = np.tile(np.concatenate([np.zeros(quarter), np.ones(quarter)]), 2)[None, :]
        c = np.cos(ang)
        s1 = np.sin(ang) * is_x2
        s2 = -np.sin(ang) * (1 - is_x2)
        return c, s1, s2

    def assemble(width, spans, half):
        c, s1, s2 = pattern(half)
        tc = np.ones((DEC_SEQ + TMI, width))
        t1 = np.zeros((DEC_SEQ + TMI, width))
        t2 = np.zeros((DEC_SEQ + TMI, width))
        for start in spans:
            tc[:DEC_SEQ, start:start + 2 * half] = c
            t1[:DEC_SEQ, start:start + 2 * half] = s1
            t2[:DEC_SEQ, start:start + 2 * half] = s2
        return jnp.asarray(np.stack([tc, t1, t2]), F32)

    taba = assemble(LANE, (0, 64), 32)
    tabb = assemble(2 * LANE, (0, 128 + 64), 16)
    tabk = assemble(LANE, (0, 64), 16)
    return taba, tabb, tabk


def _ones_block(n, width):
    g = np.arange(n) // width
    return jnp.asarray(g[:, None] == g[None, :], BF16)


def _ctxkv_kernel(c_ref, w_ref, o_ref):
    o_ref[...] = jnp.dot(c_ref[...].astype(BF16), w_ref[...], preferred_element_type=F32).astype(BF16)


def _ctx_kv(ckv_cache, wkv_p):
    rows = ckv_cache.shape[0]
    return pl.pallas_call(
        _ctxkv_kernel,
        grid=(rows // PAST,),
        in_specs=[pl.BlockSpec((PAST, KV_RANK), lambda i: (i, 0)),
                  pl.BlockSpec((None, KV_RANK, 1024), lambda i: (i % DEPTH, 0, 0))],
        out_specs=pl.BlockSpec((PAST, 1024), lambda i: (i, 0)),
        out_shape=jax.ShapeDtypeStruct((rows, 1024), BF16),
        compiler_params=_cparams(("parallel",)),
        name="ctx_kv",
    )(ckv_cache, wkv_p)


_NT = (((1,), (1,)), ((), ()))


def _den_lane(parity):
    return 64 if parity == 0 else 0


def _softmax_pv(s, v, parity, mxu_den):
    m = s.max(axis=-1, keepdims=True)
    if not mxu_den:
        p = jnp.exp(s - m)
        return jnp.dot(p.astype(BF16), v, preferred_element_type=F32) / p.sum(axis=-1, keepdims=True)
    lane = lax.broadcasted_iota(jnp.int32, (1, LANE), 1)
    keep = (lane < 64) if parity == 0 else (lane >= 64)
    o = jnp.dot(jnp.exp((s - m).astype(BF16)), v, preferred_element_type=F32)
    return jnp.where(keep, o / o[:, _den_lane(parity):_den_lane(parity) + 1], 0.0)


def _attn_kernel(n_pieces, qa_ref, qb_ref, *refs):
    kv_refs = refs[:4 * n_pieces]
    oa_ref, ob_ref = refs[4 * n_pieces:4 * n_pieces + 2]
    mxu_den = n_pieces == 2
    lane = lax.broadcasted_iota(jnp.int32, (1, LANE), 1)
    lo = lane < 64
    hi = jnp.logical_not(lo)

    def rows(parts):
        return parts[0] if len(parts) == 1 else jnp.concatenate(parts, axis=0)

    def with_den(v, parity):
        return jnp.where(lane == _den_lane(parity), jnp.ones_like(v), v) if mxu_den else v

    def slabs_a():
        ka = [kv_refs[4 * i][...].astype(F32) for i in range(n_pieces)]
        va = [kv_refs[4 * i + 1][...].astype(F32) for i in range(n_pieces)]

        def place(x, g, parity):
            if g != parity:
                x = pltpu.roll(x, 64, 1)
            return jnp.where(lo if parity == 0 else hi, x, 0.0).astype(BF16)

        return {2 * g + parity: (rows([place(k, g, parity) for k in ka]),
                                 with_den(rows([place(v, g, parity) for v in va]), parity))
                for g in range(KV_A) for parity in range(2)}

    def slabs_b():
        out = {}
        kre = rows([kv_refs[4 * i + 3][...].astype(BF16) for i in range(n_pieces)])
        for h in range(H_B):
            parity = h % 2
            nope = hi if parity == 0 else lo
            kvb = rows([kv_refs[4 * i + 2][:, h * LANE:(h + 1) * LANE] for i in range(n_pieces)])
            out[h] = (jnp.where(nope, kvb, kre), with_den(jnp.where(nope, jnp.zeros_like(kvb), kvb), parity))
        return out

    sa, sb = slabs_a(), slabs_b()
    get_a, get_b = sa.__getitem__, sb.__getitem__

    for pair in range(H_A // 2):
        g = (2 * pair) // (H_A // KV_A)
        q = qa_ref[:, pair * LANE:(pair + 1) * LANE]
        acc = None
        for parity in range(2):
            k, v = get_a(2 * g + parity)
            o = _softmax_pv(lax.dot_general(q, k, _NT, preferred_element_type=F32), v, parity, mxu_den)
            acc = o if acc is None else acc + o
        oa_ref[:, pair * LANE:(pair + 1) * LANE] = acc.astype(BF16)

    for pair in range(H_B // 2):
        acc = None
        for parity in range(2):
            h = 2 * pair + parity
            k, v = get_b(h)
            q = qb_ref[:, h * LANE:(h + 1) * LANE]
            o = _softmax_pv(lax.dot_general(q, k, _NT, preferred_element_type=F32), v, parity, mxu_den)
            acc = o if acc is None else acc + o
        ob_ref[:, pair * LANE:(pair + 1) * LANE] = acc.astype(BF16)


TQ_LAT = 512


def _attention(qa, qb, ka, va, kvb, kre, prev=None, cache=None, layer=0):
    if cache is None:
        tq, nb, nqt, nk, q_blk0, k_blk0 = SEQ, BATCH, 1, SEQ, 0, 0
    else:
        tq = TQ_LAT
        nb, nqt, nk, q_blk0, k_blk0 = DEC_BATCH, DEC_SEQ // tq, DEC_SEQ, T_CTX // tq, T_CTX // DEC_SEQ
    qspec = lambda w: pl.BlockSpec((tq, w), lambda b, j: (q_blk0 + b * nqt + j, 0))
    kspec = lambda w: pl.BlockSpec((nk, w), lambda b, j: (k_blk0 + b, 0))
    in_specs = [qspec(512), qspec(1024), kspec(128), kspec(128), kspec(1024), kspec(128)]
    args = [qa, qb, ka, va, kvb, kre]
    n_pieces = 1
    aliases = {}
    if cache is not None:
        cspec4 = pl.BlockSpec((None, None, PAST, 128), lambda b, j: (b, layer, 0, 0))
        cspec = lambda w: pl.BlockSpec((PAST, w), lambda b, j: (b * DEPTH + layer, 0))
        in_specs += [cspec4, cspec4, cspec(1024), cspec(128)]
        args += list(cache)
        n_pieces = 2
        in_specs += [pl.BlockSpec(memory_space=pl.ANY)] * 2
        args += list(prev)
        aliases = {len(args) - 2: 0, len(args) - 1: 1}

    def body(*refs):
        if cache is not None:
            n_in = 2 + 4 * n_pieces
            refs = refs[:n_in] + refs[n_in + 2:]
        _attn_kernel(n_pieces, *refs)

    return pl.pallas_call(
        body,
        grid=(nb, nqt),
        in_specs=in_specs,
        out_specs=[qspec(512), qspec(512)],
        out_shape=[jax.ShapeDtypeStruct((T_ALL, 512), BF16)] * 2,
        input_output_aliases=aliases,
        compiler_params=_cparams(("parallel", "arbitrary")),
        name="attention_ctx" if cache is None else "attention_lat",
    )(*args)


HL = 256
HS = 128
HG = 64
N_PAIRS = H_C // 2
FAST_DECAY_LIMIT = 80.0


def _hgrn_bottom_exact(q, k, c, lo, rev):
    row = lax.broadcasted_iota(jnp.int32, (HS, LANE), 0)
    srow = lax.broadcasted_iota(jnp.int32, (HS, HS), 0)
    scol = lax.broadcasted_iota(jnp.int32, (HS, HS), 1)
    out = []
    for parity in range(2):
        def dup(x):
            xs = pltpu.roll(x, 64, 1)
            return jnp.where(lo, x, xs) if parity == 0 else jnp.where(lo, xs, x)
        qd, kd, bd = dup(q), dup(k), dup(c)

        dg = row & 3
        if rev:
            dg = 3 - dg
        e = [None]
        for delta in range(1, 4):
            shifted = pltpu.roll(bd, delta if rev else HS - delta, 0)
            e.append(jnp.exp(jnp.minimum(shifted - bd, 0.0)))
        qp, kp = [], []
        for c1, c2 in ((0, 1), (2, 3)):
            cv = jnp.where(lo, c1, c2)
            dl = cv - dg
            fac = jnp.where(dl == 0, 1.0, jnp.where(dl == 1, e[1], jnp.where(dl == 2, e[2],
                            jnp.where(dl == 3, e[3], 0.0))))
            kp.append((kd * fac).astype(BF16))
            qp.append(jnp.where(dg == cv, qd, 0.0).astype(BF16))
        s = lax.dot_general(jnp.concatenate(qp, axis=1), jnp.concatenate(kp, axis=1), _NT,
                            preferred_element_type=F32)
        tot = jnp.where((srow >> 2) == (scol >> 2), s, 0.0)

        for lev in range(1, 3):
            g = 4 ** lev
            par = 4 * g
            shape3 = (HS // par, par, LANE)
            rid = lax.broadcasted_iota(jnp.int32, shape3, 1)
            dg3 = rid >> (2 * lev)
            if rev:
                dg3 = 3 - dg3
            b3, q3, k3 = bd.reshape(shape3), qd.reshape(shape3), kd.reshape(shape3)
            lo3 = lo.reshape(1, 1, LANE)
            qp, kp = [], []
            for c1, c2 in ((1, 2), (3, None)):
                idx = lambda cc: (4 - cc) * g if rev else cc * g - 1
                i1 = idx(c1)
                i2 = idx(c2) if c2 is not None else i1
                ridx = jnp.where(lo3, i1, i2)
                ref = jnp.sum(jnp.where(rid == ridx, b3, 0.0), axis=1, keepdims=True)
                cvk = jnp.where(lo3, c1, c2 if c2 is not None else 0)
                cvq = jnp.where(lo3, c1, c2 if c2 is not None else -1)
                kk = jnp.where(dg3 < cvk, k3 * jnp.exp(jnp.minimum(ref - b3, 0.0)), 0.0)
                qq = jnp.where(dg3 == cvq, q3 * jnp.exp(jnp.minimum(b3 - ref, 0.0)), 0.0)
                kp.append(kk.reshape(HS, LANE).astype(BF16))
                qp.append(qq.reshape(HS, LANE).astype(BF16))
            s = lax.dot_general(jnp.concatenate(qp, axis=1), jnp.concatenate(kp, axis=1), _NT,
                                preferred_element_type=F32)
            sh = 2 * lev + 2
            tot = tot + jnp.where((srow >> sh) == (scol >> sh), s, 0.0)
        out.append(tot)
    return out


SUBLANES = 8


def _group_cumsum(lf, rev):
    nv, per_group = HS // SUBLANES, HG // SUBLANES
    x = lf.reshape(nv, SUBLANES, LANE)
    sub = lax.broadcasted_iota(jnp.int32, (nv, SUBLANES, LANE), 1)
    d = 1
    while d < SUBLANES:
        if rev:
            x = x + jnp.where(sub < SUBLANES - d, pltpu.roll(x, SUBLANES - d, 1), 0.0)
        else:
            x = x + jnp.where(sub >= d, pltpu.roll(x, d, 1), 0.0)
        d *= 2
    edge = x[:, 0:1, :] if rev else x[:, SUBLANES - 1:SUBLANES, :]
    offs = [None] * nv
    for g in range(HS // HG):
        order = range(g * per_group, (g + 1) * per_group)
        acc = None
        for vi in (reversed(order) if rev else order):
            offs[vi] = jnp.zeros((1, 1, LANE), F32) if acc is None else acc
            acc = edge[vi:vi + 1] if acc is None else acc + edge[vi:vi + 1]
    return (x + jnp.concatenate(offs, axis=0)).reshape(HS, LANE)


def _hgrn_head(q, k, lf, v, st_ref, rev):
    row = lax.broadcasted_iota(jnp.int32, (HS, LANE), 0)
    lane = lax.broadcasted_iota(jnp.int32, (1, LANE), 1)
    lo = lane < 64
    hi = jnp.logical_not(lo)
    in_g1 = row >= HG

    c = _group_cumsum(lf, rev)
    if rev:
        t0, t1 = c[0:1, :], c[HG:HG + 1, :]
    else:
        t0, t1 = c[HG - 1:HG, :], c[HS - 1:HS, :]
    et0, et1 = jnp.exp(t0), jnp.exp(t1)
    qe = q * jnp.exp(c)
    e_out = jnp.exp(jnp.where(in_g1, t1, t0) - c)
    ke = k * e_out

    if rev:
        qb = qe * jnp.where(in_g1, 1.0, et1)
        kh = ke * jnp.where(in_g1, et0, 1.0)
    else:
        qb = qe * jnp.where(in_g1, et0, 1.0)
        kh = ke * jnp.where(in_g1, 1.0, et1)
    st = st_ref[...]
    o_int = lax.dot_general(qb.astype(BF16), st.astype(BF16), _NT, preferred_element_type=F32)
    upd = lax.dot_general(v, kh.astype(BF16), (((0,), (0,)), ((), ())), preferred_element_type=F32)
    r128 = lax.broadcasted_iota(jnp.int32, (LANE, LANE), 0)
    c128 = lax.broadcasted_iota(jnp.int32, (LANE, LANE), 1)
    st_ref[...] = st * (et0 * et1) + jnp.where((r128 >> 6) == (c128 >> 6), upd, 0.0)

    q_late = in_g1 if not rev else jnp.logical_not(in_g1)
    q_top = jnp.where(q_late, qe, 0.0)
    k_top = jnp.where(q_late, 0.0, ke).astype(BF16)
    top = [lax.dot_general(jnp.where(m, q_top, 0.0).astype(BF16), k_top, _NT, preferred_element_type=F32)
           for m in (lo, hi)]
    mid = HG // 2 if rev else HG // 2 - 1
    cm = c - jnp.where(in_g1, c[HG + mid:HG + mid + 1, :], c[mid:mid + 1, :])
    return c, top, o_int, cm, jnp.max(jnp.abs(cm))


def _hgrn_steps(jobs, bot_ref):
    lane = lax.broadcasted_iota(jnp.int32, (1, LANE), 1)
    lo = lane < 64
    hi = jnp.logical_not(lo)
    heads = [_hgrn_head(*job) for job in jobs]
    fast = functools.reduce(jnp.maximum, [h[4] for h in heads]) <= FAST_DECAY_LIMIT

    @pl.when(fast)
    def _():
        srow = lax.broadcasted_iota(jnp.int32, (HS, HS), 0)
        scol = lax.broadcasted_iota(jnp.int32, (HS, HS), 1)
        same = (srow >> 6) == (scol >> 6)
        for ji, (job, (_, _, _, cm, _)) in enumerate(zip(jobs, heads)):
            keep = same & ((scol >= srow) if job[5] else (scol <= srow))
            qf = job[0] * jnp.exp(cm)
            kf = (job[1] * jnp.exp(-cm)).astype(BF16)
            for parity, m in enumerate((lo, hi)):
                s = lax.dot_general(jnp.where(m, qf, 0.0).astype(BF16), kf, _NT, preferred_element_type=F32)
                bot_ref[ji, parity] = jnp.where(keep, s, 0.0)

    @pl.when(jnp.logical_not(fast))
    def _():
        for ji, (job, (c, _, _, _, _)) in enumerate(zip(jobs, heads)):
            for parity, s in enumerate(_hgrn_bottom_exact(job[0], job[1], c, lo, job[5])):
                bot_ref[ji, parity] = s

    outs = []
    for ji, (job, (_, top, o_int, _, _)) in enumerate(zip(jobs, heads)):
        v = job[3]
        probs = jnp.concatenate([(bot_ref[ji, 0] + top[0]).astype(BF16), (bot_ref[ji, 1] + top[1]).astype(BF16)],
                                axis=1)
        vv = jnp.concatenate([jnp.where(lo, v, jnp.zeros_like(v)), jnp.where(hi, v, jnp.zeros_like(v))], axis=0)
        outs.append(jnp.dot(probs, vv, preferred_element_type=F32) + o_int)
    return outs


PAIRS_PER_ITER = 4


def _hgrn_kernel(has_s0, nt, *refs):
    if has_s0:
        (qf_ref, vf_ref, kf_ref, lf_ref, qb_ref, vb_ref, kb_ref, lb_ref, s0_ref,
         of_ref, ob_ref, so_ref, st_scr, bot_scr) = refs
    else:
        (qf_ref, vf_ref, kf_ref, lf_ref, qb_ref, vb_ref, kb_ref, lb_ref,
         of_ref, ob_ref, so_ref, st_scr, bot_scr) = refs
    j = pl.program_id(1)

    @pl.when(j == 0)
    def _():
        if has_s0:
            st_scr[...] = s0_ref[...]
        else:
            st_scr[...] = jnp.zeros_like(st_scr)

    n_sub = HL // HS

    def pair_body(it, carry):
        for step in range(n_sub):
            jobs, dests = [], []
            for pp in range(PAIRS_PER_ITER):
                p = it * PAIRS_PER_ITER + pp
                cols = pl.ds(pl.multiple_of(p * LANE, LANE), LANE)
                rf = pl.ds(step * HS, HS)
                rb = pl.ds((n_sub - 1 - step) * HS, HS)
                jobs.append((qf_ref[rf, cols].astype(F32), kf_ref[rf, cols], lf_ref[rf, cols], vf_ref[rf, cols],
                             st_scr.at[p, 0], False))
                dests.append((of_ref, rf, cols))
                jobs.append((qb_ref[rb, cols].astype(F32), kb_ref[rb, cols], lb_ref[rb, cols], vb_ref[rb, cols],
                             st_scr.at[p, 1], True))
                dests.append((ob_ref, rb, cols))
            for (ref, rows, cols), o in zip(dests, _hgrn_steps(jobs, bot_scr)):
                ref[rows, cols] = o
        return carry

    lax.fori_loop(0, N_PAIRS // PAIRS_PER_ITER, pair_body, 0)

    @pl.when(j == nt - 1)
    def _():
        so_ref[...] = st_scr[...]


def _hgrn(qc, vc, kff, lff, kfb, lfb, s0=None, prev=None, layer=0):
    if s0 is None:
        nb, nt, blk0 = BATCH, SEQ // HL, 0
    else:
        nb, nt, blk0 = DEC_BATCH, DEC_SEQ // HL, T_CTX // HL
    fspec = pl.BlockSpec((HL, 512), lambda b, j: (blk0 + b * nt + j, 0))
    bspec = pl.BlockSpec((HL, 512), lambda b, j: (blk0 + b * nt + nt - 1 - j, 0))
    sspec = pl.BlockSpec((None, N_PAIRS, 2, LANE, LANE), lambda b, j: (b, 0, 0, 0, 0))
    in_specs = [fspec] * 4 + [bspec] * 4
    args = [qc, vc, kff, lff, qc, vc, kfb, lfb]
    aliases = {}
    if s0 is not None:
        in_specs += [pl.BlockSpec((None, None, N_PAIRS, 2, LANE, LANE), lambda b, j: (b, layer, 0, 0, 0, 0))]
        args += [s0]
        in_specs += [pl.BlockSpec(memory_space=pl.ANY)] * 2
        args += list(prev)
        aliases = {len(args) - 2: 0, len(args) - 1: 1}

    def body(*refs):
        if s0 is not None:
            refs = refs[:9] + refs[11:]
        _hgrn_kernel(s0 is not None, nt, *refs)

    return pl.pallas_call(
        body,
        grid=(nb, nt),
        in_specs=in_specs,
        out_specs=[fspec, bspec, sspec],
        out_shape=[jax.ShapeDtypeStruct((T_ALL, 512), F32), jax.ShapeDtypeStruct((T_ALL, 512), F32),
                   jax.ShapeDtypeStruct((nb, N_PAIRS, 2, LANE, LANE), F32)],
        scratch_shapes=[pltpu.VMEM((N_PAIRS, 2, LANE, LANE), F32),
                        pltpu.VMEM((2 * PAIRS_PER_ITER, 2, HS, HS), F32)],
        input_output_aliases=aliases,
        compiler_params=_cparams(("parallel", "arbitrary")),
        name="hgrn_ctx" if s0 is None else "hgrn_lat",
    )(*args)


def _merge_kernel(xc_ref, xl_ref, mod_ref, n1_ref, wg_ref, oa_ref, ob_ref, ocf_ref, ocb_ref, sg_ref, con_ref,
                  ones_ref, wbr_ref, wout_ref, n2_ref, wrh_ref, wrl_ref, br_ref, tri_ref, upper_ref,
                  x1_o, h2_o, comb_o, rank_o, carry_o, tab_o, carry_scr):
    x = _x_tile(xc_ref, xl_ref, TMG)
    mod = mod_ref[...]
    xn = x * lax.rsqrt(jnp.mean(x * x, axis=-1, keepdims=True) + EPS) * n1_ref[...]
    h = (xn * (1.0 + mod[:, D:2 * D]) + mod[:, 0:D]).astype(BF16)

    oc = ocf_ref[...] + ocb_ref[...]
    oc = oc * lax.rsqrt(_group_mean(oc * oc, ones_ref[...], DV_C) + EPS) * con_ref[...]
    oc = (oc * sg_ref[...].astype(F32)).astype(BF16)
    branches = (oa_ref[...], ob_ref[...], oc)
    mix = None
    for jb in range(3):
        gate = jax.nn.sigmoid(jnp.dot(h, wg_ref[:, jb * D:(jb + 1) * D], preferred_element_type=F32))
        t = gate * jnp.dot(branches[jb], wbr_ref[jb], preferred_element_type=F32)
        mix = t if mix is None else mix + t
    out = jnp.dot(mix.astype(BF16), wout_ref[...], preferred_element_type=F32)
    x1 = x + mod[:, 2 * D:3 * D] * out
    x1_o[...] = x1

    x1n = x1 * lax.rsqrt(jnp.mean(x1 * x1, axis=-1, keepdims=True) + EPS) * n2_ref[...]
    h2 = x1n * (1.0 + mod[:, 4 * D:5 * D]) + mod[:, 3 * D:4 * D]
    h2_o[...] = h2.astype(BF16)

    h2h, h2l = _split_hi_lo(h2)
    logits = (jnp.dot(h2h, wrh_ref[...], preferred_element_type=F32)
              + jnp.dot(h2l, wrh_ref[...], preferred_element_type=F32)
              + jnp.dot(h2h, wrl_ref[...], preferred_element_type=F32)) + br_ref[...]
    lane = lax.broadcasted_iota(jnp.int32, logits.shape, 1)
    neg = -jnp.inf
    is_g = lane < N_GROUPS
    gl = jnp.where(is_g, logits, neg)
    gmax = gl.max(axis=-1, keepdims=True)
    gidx = jnp.min(jnp.where(gl == gmax, lane, LANE), axis=-1, keepdims=True)
    gp = 1.0 / jnp.sum(jnp.where(is_g, jnp.exp(gl - gmax), 0.0), axis=-1, keepdims=True)
    eid = lane - N_GROUPS
    in_grp = (eid >= 0) & (eid < N_EXPERTS) & ((eid >> 2) == gidx)
    el = jnp.where(in_grp, logits, neg)
    v1 = el.max(axis=-1, keepdims=True)
    i1 = jnp.min(jnp.where(el == v1, lane, LANE), axis=-1, keepdims=True)
    el2 = jnp.where(lane == i1, neg, el)
    v2 = el2.max(axis=-1, keepdims=True)
    i2 = jnp.min(jnp.where(el2 == v2, lane, LANE), axis=-1, keepdims=True)
    e2 = jnp.exp(v2 - v1)
    w1 = gp / (1.0 + e2)
    w2 = gp * e2 / (1.0 + e2)
    comb = jnp.where(lane == i1, w1, 0.0) + jnp.where(lane == i2, w2, 0.0)
    comb = pltpu.roll(comb, LANE - N_GROUPS, 1)
    comb_o[...] = comb

    step = pl.program_id(0) % MOE_STEPS

    @pl.when(step == 0)
    def _():
        carry_scr[...] = jnp.zeros_like(carry_scr)

    for sb in range(TMG // SBK):
        sub = comb[sb * SBK:(sb + 1) * SBK, :]
        routed = sub > 0.0
        ind = jnp.where(routed, 1.0, 0.0)
        carry = carry_scr[...]
        rank = jnp.dot(tri_ref[...], ind.astype(BF16), preferred_element_type=F32) + carry
        rank_o[sb * SBK:(sb + 1) * SBK, :] = jnp.where(routed, rank, -1.0)
        carry_o[sb] = carry
        carry_scr[...] = carry + jnp.sum(ind, axis=0, keepdims=True)

    @pl.when(step == MOE_STEPS - 1)
    def _():
        count = carry_scr[...]
        seg = jnp.floor((count + (SEG_ALIGN - 1.0)) * (1.0 / SEG_ALIGN)) * SEG_ALIGN
        off = jnp.dot(jnp.broadcast_to(seg, (8, LANE)), upper_ref[...], preferred_element_type=F32,
                      precision=lax.Precision.HIGHEST)
        tab_o[0:1, :] = count
        tab_o[1:2, :] = off[0:1, :]


TMG = 512


def _merge(layer, x, mod, n1, wgate, oa, ob, ocf, ocb, sg, con, ones, wbr, wout, n2, wr_hi, wr_lo, br):
    ctx_tiles, per_seq = T_CTX // TMG, DEC_SEQ // TMG
    mrow = lambda i: jnp.where(i < ctx_tiles, CTX_MOD_ROW, (i - ctx_tiles) // per_seq)
    tspec = lambda w: pl.BlockSpec((TMG, w), lambda i: (i, 0))
    wspec = lambda tail: pl.BlockSpec((None,) + tail, lambda i: (layer,) + (0,) * len(tail),
                                      pipeline_mode=pl.Buffered(1))
    return pl.pallas_call(
        _merge_kernel,
        grid=(T_ALL // TMG,),
        in_specs=_x_specs(x, TMG) + [
            _mod_spec(layer, mrow),
            _layer_spec((1, D), layer), wspec((D, 3 * D)),
            tspec(512), tspec(512), tspec(512), tspec(512), tspec(512),
            _layer_spec((1, 512), layer), _const_spec((256, 256)),
            wspec((3, BRANCH_W, D)), wspec((D, D)), _layer_spec((1, D), layer),
            _layer_spec((D, LANE), layer), _layer_spec((D, LANE), layer), _layer_spec((1, LANE), layer),
            _const_spec((SBK, SBK)), _const_spec((LANE, LANE)),
        ],
        out_specs=[
            tspec(D), tspec(D), tspec(LANE), tspec(LANE),
            pl.BlockSpec((None, TMG // SBK, 1, LANE), lambda i: (i // MOE_STEPS, i % MOE_STEPS, 0, 0)),
            pl.BlockSpec((None, 2, LANE), lambda i: (i // MOE_STEPS, 0, 0)),
        ],
        out_shape=[jax.ShapeDtypeStruct((T_ALL, D), F32), jax.ShapeDtypeStruct((T_ALL, D), BF16),
                   jax.ShapeDtypeStruct((T_ALL, LANE), F32), jax.ShapeDtypeStruct((T_ALL, LANE), F32),
                   jax.ShapeDtypeStruct((N_BLK, N_SB, 1, LANE), F32), jax.ShapeDtypeStruct((N_BLK, 2, LANE), F32)],
        scratch_shapes=[pltpu.VMEM((1, LANE), F32)],
        compiler_params=_cparams(("arbitrary",)),
        name="merge",
    )(*_x_pair(x)[:2], mod, n1, wgate, oa, ob, ocf, ocb, sg, con, ones, wbr, wout, n2, wr_hi, wr_lo, br,
      jnp.asarray(np.tril(np.ones((SBK, SBK)), -1), BF16), jnp.asarray(np.triu(np.ones((LANE, LANE)), 1), F32))


NB = 2048
N_BLK = T_ALL // NB
SBK = 256
N_SB = NB // SBK
WIN_SHIFT, FT_SHIFT, SEG_SHIFT = 6, 7, 4
WIN = 1 << WIN_SHIFT
FT = 1 << FT_SHIFT
SEG_ALIGN = 1 << SEG_SHIFT
STG = 2 * NB + N_EXPERTS * SEG_ALIGN + 256
QUAD = 4
E_STEP = 2
MOE_STEPS = NB // TMG


def _moe_kernel(cnt_s, off_s, car_s, h2_ref, rank_ref, comb_ref, offv_ref, eg_ref, eu_ref, ed_ref, o_ref,
                stg_ref, acc_ref):
    blk = pl.program_id(0)
    step = pl.program_id(1)
    srow = lax.broadcasted_iota(jnp.int32, (WIN, SBK), 0).astype(F32)

    def windows(s):
        out = []
        for ex in range(N_EXPERTS):
            start = off_s[blk, ex] + car_s[blk, s, ex]
            length = car_s[blk, s + 1, ex] - car_s[blk, s, ex]
            ws = (start >> SEG_SHIFT) << SEG_SHIFT
            out.append((ws, (start - ws + length + (WIN - 1)) >> WIN_SHIFT))
        return out, functools.reduce(jnp.maximum, [w[1] for w in out])

    def positions(s):
        rows = pl.ds(pl.multiple_of(s * SBK, SBK), SBK)
        rank = rank_ref[rows, :]
        return rows, jnp.where(rank >= 0.0, rank + offv_ref[1:2, :], -1.0e6)

    @pl.when(step == 0)
    def _():
        stg_ref[...] = jnp.zeros_like(stg_ref)

        def sub_body(s, carry):
            rows, pos = positions(s)
            pos_t = pos.T
            h2 = h2_ref[rows, :]
            wins, nmax = windows(s)

            def chunk_body(c, carry2):
                for quad in range(N_EXPERTS // QUAD):
                    blocks = []
                    for ex in range(quad * QUAD, (quad + 1) * QUAD):
                        base = (wins[ex][0] + c * WIN).astype(F32)
                        hit = (pos_t[ex:ex + 1, :] - base) == srow
                        blocks.append(jnp.where(hit, 1.0, 0.0).astype(BF16))
                    moved = jnp.dot(jnp.concatenate(blocks, axis=0), h2, preferred_element_type=F32).astype(BF16)
                    for i in range(QUAD):
                        first = jnp.minimum(wins[quad * QUAD + i][0] + c * WIN, STG - WIN)
                        dst = pl.ds(pl.multiple_of(first, SEG_ALIGN), WIN)
                        stg_ref[dst, :] = stg_ref[dst, :] + moved[i * WIN:(i + 1) * WIN, :]
                return carry2

            lax.fori_loop(0, nmax, chunk_body, 0)
            return carry

        lax.fori_loop(0, N_SB, sub_body, 0)

    def ffn_tile(j, first, n_rows, n_valid):
        rows = pl.ds(pl.multiple_of(first, SEG_ALIGN), n_rows)
        xs = stg_ref[rows, :]
        hg = jnp.dot(xs, eg_ref[j], preferred_element_type=F32)
        hu = jnp.dot(xs, eu_ref[j], preferred_element_type=F32)
        act = (hg * jax.nn.sigmoid(hg) * hu).astype(BF16)
        y = jnp.dot(act, ed_ref[j], preferred_element_type=F32).astype(BF16)
        rid = lax.broadcasted_iota(jnp.int32, (n_rows, 1), 0)
        stg_ref[rows, :] = jnp.where(rid < n_valid, y, xs)

    for j in range(E_STEP):
        ex = step * E_STEP + j
        count = cnt_s[blk, ex]
        seg0 = off_s[blk, ex]
        n_big = (count + (FT - 1)) >> (FT_SHIFT + 1)

        def ffn_body(t, carry, j=j, count=count, seg0=seg0):
            ffn_tile(j, seg0 + t * (2 * FT), 2 * FT, count - t * (2 * FT))
            return carry

        lax.fori_loop(0, n_big, ffn_body, 0)

        @pl.when(count > n_big * (2 * FT))
        def _(j=j, count=count, seg0=seg0, n_big=n_big):
            ffn_tile(j, seg0 + n_big * (2 * FT), FT, count - n_big * (2 * FT))

    @pl.when(step == N_EXPERTS // E_STEP - 1)
    def _():
        def sub_body(s, carry):
            rows, pos = positions(s)
            pos_t = pos.T
            wts_t = comb_ref[rows, :].T
            wins, nmax = windows(s)
            acc_ref[...] = jnp.zeros_like(acc_ref)

            def chunk_body(c, carry2):
                sel_t, srcs = [], []
                for ex in range(N_EXPERTS):
                    base = (wins[ex][0] + c * WIN).astype(F32)
                    hit = (pos_t[ex:ex + 1, :] - base) == srow
                    sel_t.append(jnp.where(hit, wts_t[ex:ex + 1, :], 0.0).astype(BF16))
                    first = jnp.minimum(wins[ex][0] + c * WIN, STG - WIN)
                    srcs.append(stg_ref[pl.ds(pl.multiple_of(first, SEG_ALIGN), WIN), :])
                acc_ref[...] += lax.dot_general(jnp.concatenate(sel_t, axis=0), jnp.concatenate(srcs, axis=0),
                                                (((0,), (0,)), ((), ())), preferred_element_type=F32)
                return carry2

            lax.fori_loop(0, nmax, chunk_body, 0)
            o_ref[rows, :] = acc_ref[...].astype(BF16)
            return carry

        lax.fori_loop(0, N_SB, sub_body, 0)


def _moe(layer, h2, comb, rank, carry, tab, eg, eu, ed):
    cnt_i = tab[:, 0, :N_EXPERTS].astype(jnp.int32)
    off_i = tab[:, 1, :N_EXPERTS].astype(jnp.int32)
    car_i = jnp.concatenate([carry[:, :, 0, :N_EXPERTS].astype(jnp.int32), cnt_i[:, None, :]], axis=1)
    bspec = lambda w: pl.BlockSpec((NB, w), lambda b, e, *_: (b, 0))
    grid_spec = pltpu.PrefetchScalarGridSpec(
        num_scalar_prefetch=3,
        grid=(N_BLK, N_EXPERTS // E_STEP),
        in_specs=[
            bspec(D), bspec(LANE), bspec(LANE),
            pl.BlockSpec((None, 2, LANE), lambda b, e, *_: (b, 0, 0)),
            pl.BlockSpec((None, E_STEP, D, D_EXPERT), lambda b, e, *_: (layer, e, 0, 0)),
            pl.BlockSpec((None, E_STEP, D, D_EXPERT), lambda b, e, *_: (layer, e, 0, 0)),
            pl.BlockSpec((None, E_STEP, D_EXPERT, D), lambda b, e, *_: (layer, e, 0, 0)),
        ],
        out_specs=bspec(D),
        scratch_shapes=[pltpu.VMEM((STG, D), BF16), pltpu.VMEM((SBK, D), F32)],
    )
    return pl.pallas_call(
        _moe_kernel,
        grid_spec=grid_spec,
        out_shape=jax.ShapeDtypeStruct((T_ALL, D), BF16),
        compiler_params=_cparams(("parallel", "arbitrary")),
        name="moe",
    )(cnt_i, off_i, car_i, h2, rank, comb, tab, eg, eu, ed)


def _residual_kernel(final, x1_ref, moe_ref, mod_ref, fg_ref, o_ref):
    x2 = x1_ref[...] + mod_ref[:, 5 * D:6 * D] * moe_ref[...].astype(F32)
    if final:
        x2 = x2 * lax.rsqrt(jnp.mean(x2 * x2, axis=-1, keepdims=True) + EPS) * fg_ref[...]
    o_ref[...] = x2


TR = DEC_SEQ


def _residual(layer, final, x1, moe, mod, fg, tile0=0, n_tiles=T_ALL // TR):
    ctx_tiles = T_CTX // TR
    mrow = lambda i: jnp.where(tile0 + i < ctx_tiles, CTX_MOD_ROW, tile0 + i - ctx_tiles)
    src = lambda: pl.BlockSpec((TR, D), lambda i: (tile0 + i, 0))
    return pl.pallas_call(
        functools.partial(_residual_kernel, final),
        grid=(n_tiles,),
        in_specs=[src(), src(), _mod_spec(layer, mrow), _const_spec((1, D))],
        out_specs=pl.BlockSpec((TR, D), lambda i: (i, 0)),
        out_shape=jax.ShapeDtypeStruct((n_tiles * TR, D), F32),
        compiler_params=_cparams(("parallel",)),
        name="residual",
    )(x1, moe, mod, fg)


def _state_to_blockdiag(s):
    lead = s.shape[:-3]
    st = jnp.swapaxes(s, -1, -2).reshape(lead + (N_PAIRS, 2, DV_C, DK_C))
    z = jnp.zeros_like(st[..., 0, :, :])
    top = jnp.concatenate([st[..., 0, :, :], z], axis=-1)
    bot = jnp.concatenate([z, st[..., 1, :, :]], axis=-1)
    return jnp.concatenate([top, bot], axis=-2)


def _blockdiag_to_state(sb):
    lead = sb.shape[:-3]
    even = sb[..., :DV_C, :DK_C]
    odd = sb[..., DV_C:, DK_C:]
    st = jnp.stack([even, odd], axis=-3).reshape(lead + (H_C, DV_C, DK_C))
    return jnp.swapaxes(st, -1, -2)


def kernel(x_prompt, x_sample, cache_gqa_k, cache_gqa_v, cache_mla_ckv, cache_mla_krope, state_hgrn, c, c_ctx,
           w_mod, b_mod, norm1_g, norm2_g, w_in, a_qnorm, a_knorm, b_qnorm, b_wq, b_kvnorm, b_wkv, c_lb_logits,
           c_onorm, w_branch, w_out, r_group_w, r_group_b, r_expert_w, r_expert_b, e_gate, e_up, e_down, final_g):
    x = (x_prompt.reshape(T_CTX, D), x_sample.reshape(T_LAT, D))
    cvec = jnp.concatenate([c, c_ctx[None, :], jnp.zeros((MOD_ROWS - DEC_BATCH - 1, D), F32)], axis=0)
    mod = _mod_table(cvec, w_mod, b_mod).reshape(DEPTH * MOD_ROWS, 1, 6 * D)
    taba, tabb, tabk = _rope_tables()
    ones = _ones_block(256, 64)
    lbl = c_lb_logits.reshape(DEPTH, 2, H_C * DK_C)

    vec = lambda g, reps=1: jnp.tile(g, (1, reps))[:, None, :]
    n1, n2 = vec(norm1_g), vec(norm2_g)
    aq, ak, con = vec(a_qnorm, H_A), vec(a_knorm, KV_A), vec(c_onorm, H_C)
    bq, bkv = vec(b_qnorm), vec(b_kvnorm)
    w_in_p, w_gate = _pack_w_in(w_in)
    wq_p, wkv_p = _pack_wq(b_wq), _pack_wkv(b_wkv)
    wbr, wout = w_branch.astype(BF16), w_out.astype(BF16)
    n_pad = LANE - N_GROUPS - N_EXPERTS
    wr_hi, wr_lo = _split_hi_lo(jnp.concatenate([r_group_w, r_expert_w, jnp.zeros((DEPTH, D, n_pad), F32)], axis=-1))
    br = jnp.concatenate([r_group_b, r_expert_b, jnp.zeros((DEPTH, n_pad), F32)], axis=-1)[:, None, :]
    eg, eu, ed = e_gate.astype(BF16), e_up.astype(BF16), e_down.astype(BF16)

    ck = cache_gqa_k.reshape(DEC_BATCH, DEPTH, PAST, KV_A * HD_A)
    cv = cache_gqa_v.reshape(DEC_BATCH, DEPTH, PAST, KV_A * HD_A)
    ckvb = _ctx_kv(cache_mla_ckv.reshape(DEC_BATCH * DEPTH * PAST, KV_RANK), wkv_p)
    ckr = cache_mla_krope.reshape(DEC_BATCH * DEPTH * PAST, ROPE_B)
    zpad = jnp.zeros_like(ckr)
    ckre = jnp.concatenate([ckr, zpad, ckr, zpad], axis=1)
    s0 = jnp.swapaxes(_state_to_blockdiag(state_hgrn), 2, 3)

    new_k, new_v, new_ckv, new_kr, new_s = [], [], [], [], []
    pending = None
    for l in range(DEPTH):
        outs = _inproj(l, x, mod, n1, w_in_p, aq, ak, bq, wq_p, bkv, wkv_p, lbl, ones, taba, tabb, tabk,
                       pending=pending)
        if pending is not None:
            x = outs[-1]
        (qa, kan, ka, va, qb, ckv, kvb, kr, kre, lff, lfb, kff, kfb, qc, vc, sg) = outs[:16]
        new_k.append(kan[:T_CTX])
        new_v.append(va[:T_CTX])
        new_ckv.append(ckv[:T_CTX])
        new_kr.append(kr[:T_CTX, :ROPE_B])

        oa, ob = _attention(qa, qb, ka, va, kvb, kre)
        oa, ob = _attention(qa, qb, ka, va, kvb, kre, prev=(oa, ob), cache=(ck, cv, ckvb, ckre), layer=l)

        ocf, ocb, s_ctx = _hgrn(qc, vc, kff, lff, kfb, lfb)
        ocf, ocb, _ = _hgrn(qc, vc, kff, lff, kfb, lfb, s0=s0, prev=(ocf, ocb), layer=l)
        new_s.append(s_ctx)

        x1, h2, comb, rank, carry, tab = _merge(l, x, mod, n1, w_gate, oa, ob, ocf, ocb, sg, con, ones, wbr, wout,
                                                n2, wr_hi, wr_lo, br)
        moe = _moe(l, h2, comb, rank, carry, tab, eg, eu, ed)
        pending = (x1, moe)

    last = DEPTH - 1
    y_prompt = _residual(last, True, x1, moe, mod, final_g[None, :], 0, T_CTX // TR)
    y_sample = _residual(last, True, x1, moe, mod, final_g[None, :], T_CTX // TR, T_LAT // TR)
    stack = lambda parts, tail: jnp.stack([p.reshape(BATCH, SEQ, -1) for p in parts], axis=1).reshape(
        (BATCH, DEPTH, SEQ) + tail)
    states = _blockdiag_to_state(jnp.swapaxes(jnp.stack(new_s, axis=1), 2, 3))
    return (y_prompt.reshape(BATCH, SEQ, D), y_sample.reshape(DEC_BATCH, DEC_SEQ, D),
            stack(new_k, (KV_A, HD_A)), stack(new_v, (KV_A, HD_A)), stack(new_ckv, (KV_RANK,)),
            stack(new_kr, (ROPE_B,)), states)
```

```python
import functools

import numpy as np
import jax
import jax.numpy as jnp
from jax import lax
from jax.experimental import pallas as pl
from jax.experimental.pallas import tpu as pltpu

D = 1024
BATCH, SEQ = 32, 256
DEC_BATCH, DEC_SEQ = 8, 1024
PAST = 256
DEPTH = 2
GRID_W = 64
THETA = 10000.0
EPS = 1e-6
F_FLOOR = 1e-30
H_A, KV_A, HD_A = 8, 2, 64
H_B, Q_RANK, KV_RANK, NOPE_B, ROPE_B, V_B = 8, 384, 256, 64, 32, 64
H_C, DK_C, DV_C = 8, 64, 64
BRANCH_W = 512
N_GROUPS, E_PER_GROUP, N_EXPERTS, D_EXPERT = 4, 4, 16, 512

T_CTX = BATCH * SEQ
T_LAT = DEC_BATCH * DEC_SEQ
T_ALL = T_CTX + T_LAT
TMI = 512
MOD_ROWS = 16
CTX_MOD_ROW = DEC_BATCH
LANE = 128
VMEM_LIMIT = 56 * 1024 * 1024

C_QA, C_KA, C_VA, C_QRA, C_KVA, C_KR = 0, 512, 640, 768, 1152, 1408
C_FF, C_FB, C_QC, C_IC, C_GC, C_END = 1536, 2048, 2560, 3072, 3584, 4096
R_QA, R_KA, R_VA, R_QRA, R_KVA, R_KR = 0, 512, 640, 768, 1152, 1408
R_FF, R_FB, R_QC, R_IC, R_GC, R_GATE, R_END = 1440, 1952, 2464, 2976, 3488, 4000, 7072

F32 = jnp.float32
BF16 = jnp.bfloat16


def _cparams(sem):
    return pltpu.CompilerParams(dimension_semantics=sem, vmem_limit_bytes=VMEM_LIMIT)


def _split_hi_lo(x):
    hi = x.astype(BF16)
    lo = (x - hi.astype(F32)).astype(BF16)
    return hi, lo


def _group_mean(x2, ones_blk, width):
    n = ones_blk.shape[0]
    outs = []
    for j in range(x2.shape[-1] // n):
        blk = x2[:, j * n:(j + 1) * n]
        hi, lo = _split_hi_lo(blk)
        s = jnp.dot(hi, ones_blk, preferred_element_type=F32) + jnp.dot(lo, ones_blk, preferred_element_type=F32)
        outs.append(s)
    s = outs[0] if len(outs) == 1 else jnp.concatenate(outs, axis=-1)
    return s * (1.0 / width)


def _rope(x, tab_ref, shift, period):
    c, s1, s2 = tab_ref[0], tab_ref[1], tab_ref[2]
    outs = []
    for j in range(x.shape[-1] // period):
        blk = x[:, j * period:(j + 1) * period]
        outs.append(blk * c + pltpu.roll(blk, shift, 1) * s1 + pltpu.roll(blk, period - shift, 1) * s2)
    return outs[0] if len(outs) == 1 else jnp.concatenate(outs, axis=-1)


def _mod_kernel(c_ref, w_ref, b_ref, o_ref):
    c = c_ref[...]
    a = c * jax.nn.sigmoid(c)
    o_ref[...] = jnp.dot(a, w_ref[...], preferred_element_type=F32, precision=lax.Precision.HIGHEST) + b_ref[...]


def _mod_table(cvec, w_mod, b_mod):
    nt = 1024
    return pl.pallas_call(
        _mod_kernel,
        grid=(DEPTH, 6 * D // nt),
        in_specs=[
            pl.BlockSpec((MOD_ROWS, D), lambda l, j: (0, 0)),
            pl.BlockSpec((None, D, nt), lambda l, j: (l, 0, j)),
            pl.BlockSpec((None, 1, nt), lambda l, j: (l, 0, j)),
        ],
        out_specs=pl.BlockSpec((None, MOD_ROWS, nt), lambda l, j: (l, 0, j)),
        out_shape=jax.ShapeDtypeStruct((DEPTH, MOD_ROWS, 6 * D), F32),
        compiler_params=_cparams(("arbitrary", "arbitrary")),
        name="mod_table",
    )(cvec, w_mod, b_mod.reshape(DEPTH, 1, 6 * D))


def _x_pair(x):
    if isinstance(x, tuple):
        return x[0], x[1], 0
    return x, x, T_CTX


def _x_specs(x, tile):
    _, _, lat_off = _x_pair(x)
    ctx = T_CTX // tile
    return [pl.BlockSpec((tile, D), lambda i: (jnp.minimum(i, ctx - 1), 0)),
            pl.BlockSpec((tile, D), lambda i: (jnp.maximum(i - ctx, 0) + lat_off // tile, 0))]


def _x_tile(xc_ref, xl_ref, tile):
    return jnp.where(pl.program_id(0) < T_CTX // tile, xc_ref[...], xl_ref[...])


def _inproj_kernel(layer, fused, *refs):
    if fused:
        x1_ref, moe_ref, gmod_ref = refs[:3]
        refs, x_o = refs[3:-1], refs[-1]
        x = x1_ref[...] + gmod_ref[:, 5 * D:6 * D] * moe_ref[...].astype(F32)
        x_o[...] = x
    else:
        x = _x_tile(refs[0], refs[1], TMI)
        refs = refs[2:]
    (mod_ref, n1_ref, w_ref, aq_ref, ak_ref, bq_ref, wq_ref, bkv_ref, wkv_ref,
     lbl_ref, ones_ref, taba_ref, tabb_ref, tabk_ref,
     qa_o, kan_o, ka_o, va_o, qb_o, ckv_o, kvb_o, kr_o, kre_o,
     lff_o, lfb_o, kff_o, kfb_o, qc_o, vc_o, sg_o) = refs
    mod = mod_ref[...]
    xn = x * lax.rsqrt(jnp.mean(x * x, axis=-1, keepdims=True) + EPS) * n1_ref[...]
    h = (xn * (1.0 + mod[:, D:2 * D]) + mod[:, 0:D]).astype(BF16)
    y_all = jnp.dot(h, w_ref[...], preferred_element_type=F32)

    def y(c0, c1):
        return y_all[:, c0:c1]

    ones = ones_ref[...]

    qa = y(C_QA, C_KA)
    qa = qa * lax.rsqrt(_group_mean(qa * qa, ones, HD_A) + EPS) * aq_ref[...]
    qa_o[...] = (_rope(qa, taba_ref, 16, LANE) * (HD_A ** -0.5)).astype(BF16)
    ka = y(C_KA, C_VA)
    ka = ka * lax.rsqrt(_group_mean(ka * ka, ones[:LANE, :LANE], HD_A) + EPS) * ak_ref[...]
    kan_o[...] = ka
    ka_o[...] = _rope(ka, taba_ref, 16, LANE).astype(BF16)
    va_o[...] = y(C_VA, C_QRA)

    qr = y(C_QRA, C_KVA)
    qr = qr * lax.rsqrt(jnp.mean(qr * qr, axis=-1, keepdims=True) + EPS) * bq_ref[...]
    qb = jnp.dot(qr.astype(BF16), wq_ref[...], preferred_element_type=F32)
    qb_o[...] = (_rope(qb, tabb_ref, 8, 2 * LANE) * ((NOPE_B + ROPE_B) ** -0.5)).astype(BF16)
    kv = y(C_KVA, C_KR)
    ckv = kv * lax.rsqrt(jnp.mean(kv * kv, axis=-1, keepdims=True) + EPS) * bkv_ref[...]
    ckv_o[...] = ckv
    kvb_o[...] = jnp.dot(ckv.astype(BF16), wkv_ref[...], preferred_element_type=F32).astype(BF16)
    kr = y(C_KR, C_FF)
    kr_o[...] = kr
    kre_o[...] = _rope(kr, tabk_ref, 8, LANE).astype(BF16)

    lbl = lbl_ref[...]
    e = jnp.exp(lbl - jnp.max(lbl, axis=0, keepdims=True))
    p = e / jnp.sum(e, axis=0, keepdims=True)
    lb = p[0] * 0.0
    for i in range(1, layer + 1):
        lb = lb + p[i]
    for d, (c0, lf_o, kf_o) in enumerate(((C_FF, lff_o, kff_o), (C_FB, lfb_o, kfb_o))):
        pre = y(c0, c0 + 512)
        lbd = lb[d:d + 1, :]
        f = jnp.maximum(lbd + (1.0 - lbd) * jax.nn.sigmoid(pre), F_FLOOR)
        lf_o[...] = jnp.log(f)
        kf_o[...] = 1.0 - f
    qc_o[...] = y(C_QC, C_IC).astype(BF16)
    vc_o[...] = y(C_IC, C_GC).astype(BF16)
    gc = y(C_GC, C_END)
    sg_o[...] = (gc * jax.nn.sigmoid(gc)).astype(BF16)


def _const_spec(shape):
    nd = len(shape)
    return pl.BlockSpec(shape, lambda i: (0,) * nd)


def _layer_spec(tail, layer):
    return pl.BlockSpec((None,) + tuple(tail), lambda *_: (layer,) + (0,) * len(tail))


def _mod_spec(layer, row_of_tile):
    return pl.BlockSpec((None, 1, 6 * D), lambda i, *_: (layer * MOD_ROWS + row_of_tile(i), 0, 0))


def _inproj(layer, x, mod, n1, w_in_p, aq, ak, bq, wq_p, bkv, wkv_p, lbl, ones, taba, tabb, tabk, pending=None):
    outs = [
        (512, BF16), (128, F32), (128, BF16), (128, F32), (1024, BF16), (256, F32), (1024, BF16),
        (128, F32), (128, BF16), (512, F32), (512, F32), (512, F32), (512, F32), (512, BF16), (512, BF16),
        (512, BF16),
    ]
    ctx_tiles, per_seq = T_CTX // TMI, DEC_SEQ // TMI
    mrow = lambda i: jnp.where(i < ctx_tiles, CTX_MOD_ROW, (i - ctx_tiles) // per_seq)
    pos_blk = lambda i: jnp.where(i < ctx_tiles, per_seq, (i - ctx_tiles) % per_seq)
    tab_spec = lambda w: pl.BlockSpec((3, TMI, w), lambda i: (0, pos_blk(i), 0))
    tile_spec = lambda w: pl.BlockSpec((TMI, w), lambda i: (i, 0))
    fused = pending is not None
    if fused:
        x_specs = [tile_spec(D), tile_spec(D), _mod_spec(layer - 1, mrow)]
        x_args = [pending[0], pending[1], mod]
        outs = outs + [(D, F32)]
    else:
        x_specs, x_args = _x_specs(x, TMI), list(_x_pair(x)[:2])
    return pl.pallas_call(
        functools.partial(_inproj_kernel, layer, fused),
        grid=(T_ALL // TMI,),
        in_specs=x_specs + [
            _mod_spec(layer, mrow),
            _layer_spec((1, D), layer),
            _layer_spec((D, C_END), layer),
            _layer_spec((1, 512), layer), _layer_spec((1, 128), layer), _layer_spec((1, Q_RANK), layer),
            _layer_spec((Q_RANK, 1024), layer), _layer_spec((1, KV_RANK), layer), _layer_spec((KV_RANK, 1024), layer),
            _const_spec((DEPTH, 2, 512)), _const_spec((256, 256)),
            tab_spec(LANE), tab_spec(2 * LANE), tab_spec(LANE),
        ],
        out_specs=[tile_spec(w) for w, _ in outs],
        out_shape=[jax.ShapeDtypeStruct((T_ALL, w), dt) for w, dt in outs],
        compiler_params=_cparams(("parallel",)),
        name="inproj",
    )(*x_args, mod, n1, w_in_p, aq, ak, bq, wq_p, bkv, wkv_p, lbl, ones, taba, tabb, tabk)


def _pack_w_in(w):
    w = w.astype(BF16)
    z = jnp.zeros((DEPTH, D, 32), BF16)
    kr = w[..., R_KR:R_FF]
    main = jnp.concatenate([w[..., :R_KR], kr, z, kr, z, w[..., R_FF:R_GATE]], axis=-1)
    return main, w[..., R_GATE:]


def _pack_wq(wq):
    w = wq.reshape(DEPTH, Q_RANK, H_B, NOPE_B + ROPE_B)
    nope, rope = w[..., :NOPE_B], w[..., NOPE_B:]
    z = jnp.zeros((DEPTH, Q_RANK, H_B, 32), wq.dtype)
    even = jnp.concatenate([rope, z, nope], axis=-1)
    odd = jnp.concatenate([nope, rope, z], axis=-1)
    is_even = (jnp.arange(H_B) % 2 == 0)[None, None, :, None]
    return jnp.where(is_even, even, odd).reshape(DEPTH, Q_RANK, H_B * LANE).astype(BF16)


def _pack_wkv(wkv):
    w = wkv.reshape(DEPTH, KV_RANK, H_B, NOPE_B + V_B)
    nope, v = w[..., :NOPE_B], w[..., NOPE_B:]
    is_even = (jnp.arange(H_B) % 2 == 0)[None, None, :, None]
    return jnp.where(is_even, jnp.concatenate([v, nope], -1), jnp.concatenate([nope, v], -1)).reshape(
        DEPTH, KV_RANK, H_B * LANE).astype(BF16)


def _rope_tables():
    pos = np.arange(DEC_SEQ)
    row, col = pos // GRID_W, pos % GRID_W

    def pattern(half):
        quarter = half // 2
        inv = THETA ** (-np.arange(0, half, 2, dtype=np.float64) / half)
        ang = np.concatenate([row[:, None] * inv, row[:, None] * inv, col[:, None] * inv, col[:, None] * inv], 1)
        is_x2 = np.tile(np.concatenate([np.zeros(quarter), np.ones(quarter)]), 2)[None, :]
        c = np.cos(ang)
        s1 = np.sin(ang) * is_x2
        s2 = -np.sin(ang) * (1 - is_x2)
        return c, s1, s2

    def assemble(width, spans, half):
        c, s1, s2 = pattern(half)
        tc = np.ones((DEC_SEQ + TMI, width))
        t1 = np.zeros((DEC_SEQ + TMI, width))
        t2 = np.zeros((DEC_SEQ + TMI, width))
        for start in spans:
            tc[:DEC_SEQ, start:start + 2 * half] = c
            t1[:DEC_SEQ, start:start + 2 * half] = s1
            t2[:DEC_SEQ, start:start + 2 * half] = s2
        return jnp.asarray(np.stack([tc, t1, t2]), F32)

    taba = assemble(LANE, (0, 64), 32)
    tabb = assemble(2 * LANE, (0, 128 + 64), 16)
    tabk = assemble(LANE, (0, 64), 16)
    return taba, tabb, tabk


def _ones_block(n, width):
    g = np.arange(n) // width
    return jnp.asarray(g[:, None] == g[None, :], BF16)


def _ctxkv_kernel(c_ref, w_ref, o_ref):
    o_ref[...] = jnp.dot(c_ref[...].astype(BF16), w_ref[...], preferred_element_type=F32).astype(BF16)


def _ctx_kv(ckv_cache, wkv_p):
    rows = ckv_cache.shape[0]
    return pl.pallas_call(
        _ctxkv_kernel,
        grid=(rows // PAST,),
        in_specs=[pl.BlockSpec((PAST, KV_RANK), lambda i: (i, 0)),
                  pl.BlockSpec((None, KV_RANK, 1024), lambda i: (i % DEPTH, 0, 0))],
        out_specs=pl.BlockSpec((PAST, 1024), lambda i: (i, 0)),
        out_shape=jax.ShapeDtypeStruct((rows, 1024), BF16),
        compiler_params=_cparams(("parallel",)),
        name="ctx_kv",
    )(ckv_cache, wkv_p)


_NT = (((1,), (1,)), ((), ()))


def _den_lane(parity):
    return 64 if parity == 0 else 0


def _softmax_pv(s, v, parity, mxu_den):
    m = s.max(axis=-1, keepdims=True)
    if not mxu_den:
        p = jnp.exp(s - m)
        return jnp.dot(p.astype(BF16), v, preferred_element_type=F32) / p.sum(axis=-1, keepdims=True)
    lane = lax.broadcasted_iota(jnp.int32, (1, LANE), 1)
    keep = (lane < 64) if parity == 0 else (lane >= 64)
    o = jnp.dot(jnp.exp((s - m).astype(BF16)), v, preferred_element_type=F32)
    return jnp.where(keep, o / o[:, _den_lane(parity):_den_lane(parity) + 1], 0.0)


def _attn_kernel(n_pieces, qa_ref, qb_ref, *refs):
    kv_refs = refs[:4 * n_pieces]
    mxu_den = n_pieces == 2
    if n_pieces == 2:
        wkv_ref = refs[4 * n_pieces]
        oa_ref, ob_ref = refs[4 * n_pieces + 1:4 * n_pieces + 3]
    else:
        oa_ref, ob_ref = refs[4 * n_pieces:4 * n_pieces + 2]
    lane = lax.broadcasted_iota(jnp.int32, (1, LANE), 1)
    lo = lane < 64
    hi = jnp.logical_not(lo)

    def rows(parts):
        return parts[0] if len(parts) == 1 else jnp.concatenate(parts, axis=0)

    def with_den(v, parity):
        return jnp.where(lane == _den_lane(parity), jnp.ones_like(v), v) if mxu_den else v

    def slabs_a():
        ka = [kv_refs[4 * i][...].astype(F32) for i in range(n_pieces)]
        va = [kv_refs[4 * i + 1][...].astype(F32) for i in range(n_pieces)]

        def place(x, g, parity):
            if g != parity:
                x = pltpu.roll(x, 64, 1)
            return jnp.where(lo if parity == 0 else hi, x, 0.0).astype(BF16)

        return {2 * g + parity: (rows([place(k, g, parity) for k in ka]),
                                 with_den(rows([place(v, g, parity) for v in va]), parity))
                for g in range(KV_A) for parity in range(2)}

    def slabs_b():
        out = {}
        kre = rows([kv_refs[4 * i + 3][...].astype(BF16) for i in range(n_pieces)])
        kvbs = [kv_refs[2][...]]
        if n_pieces == 2:
            kvbs.append(jnp.dot(kv_refs[6][...].astype(BF16), wkv_ref[...], preferred_element_type=F32).astype(BF16))
        for h in range(H_B):
            parity = h % 2
            nope = hi if parity == 0 else lo
            kvb = rows([x[:, h * LANE:(h + 1) * LANE] for x in kvbs])
            out[h] = (jnp.where(nope, kvb, kre), with_den(jnp.where(nope, jnp.zeros_like(kvb), kvb), parity))
        return out

    sa, sb = slabs_a(), slabs_b()
    get_a, get_b = sa.__getitem__, sb.__getitem__

    for pair in range(H_A // 2):
        g = (2 * pair) // (H_A // KV_A)
        q = qa_ref[:, pair * LANE:(pair + 1) * LANE]
        acc = None
        for parity in range(2):
            k, v = get_a(2 * g + parity)
            o = _softmax_pv(lax.dot_general(q, k, _NT, preferred_element_type=F32), v, parity, mxu_den)
            acc = o if acc is None else acc + o
        oa_ref[:, pair * LANE:(pair + 1) * LANE] = acc.astype(BF16)

    for pair in range(H_B // 2):
        acc = None
        for parity in range(2):
            h = 2 * pair + parity
            k, v = get_b(h)
            q = qb_ref[:, h * LANE:(h + 1) * LANE]
            o = _softmax_pv(lax.dot_general(q, k, _NT, preferred_element_type=F32), v, parity, mxu_den)
            acc = o if acc is None else acc + o
        ob_ref[:, pair * LANE:(pair + 1) * LANE] = acc.astype(BF16)


TQ_LAT = 512


def _attention(qa, qb, ka, va, kvb, kre, prev=None, cache=None, layer=0):
    if cache is None:
        tq, nb, nqt, nk, q_blk0, k_blk0 = SEQ, BATCH, 1, SEQ, 0, 0
    else:
        tq = TQ_LAT
        nb, nqt, nk, q_blk0, k_blk0 = DEC_BATCH, DEC_SEQ // tq, DEC_SEQ, T_CTX // tq, T_CTX // DEC_SEQ
    qspec = lambda w: pl.BlockSpec((tq, w), lambda b, j: (q_blk0 + b * nqt + j, 0))
    kspec = lambda w: pl.BlockSpec((nk, w), lambda b, j: (k_blk0 + b, 0))
    in_specs = [qspec(512), qspec(1024), kspec(128), kspec(128), kspec(1024), kspec(128)]
    args = [qa, qb, ka, va, kvb, kre]
    n_pieces = 1
    aliases = {}
    if cache is not None:
        cspec4 = pl.BlockSpec((None, None, PAST, 128), lambda b, j: (b, layer, 0, 0))
        cspec = lambda w: pl.BlockSpec((PAST, w), lambda b, j: (b * DEPTH + layer, 0))
        in_specs += [cspec4, cspec4, pl.BlockSpec((None, None, PAST, KV_RANK), lambda b, j: (b, layer, 0, 0)),
                     cspec(128), pl.BlockSpec((None, KV_RANK, 1024), lambda b, j: (layer, 0, 0))]
        args += list(cache)
        n_pieces = 2
        in_specs += [pl.BlockSpec(memory_space=pl.ANY)] * 2
        args += list(prev)
        aliases = {len(args) - 2: 0, len(args) - 1: 1}

    def body(*refs):
        if cache is not None:
            n_in = 2 + 4 * n_pieces + 1
            refs = refs[:n_in] + refs[n_in + 2:]
        _attn_kernel(n_pieces, *refs)

    return pl.pallas_call(
        body,
        grid=(nb, nqt),
        in_specs=in_specs,
        out_specs=[qspec(512), qspec(512)],
        out_shape=[jax.ShapeDtypeStruct((T_ALL, 512), BF16)] * 2,
        input_output_aliases=aliases,
        compiler_params=_cparams(("parallel", "arbitrary")),
        name="attention_ctx" if cache is None else "attention_lat",
    )(*args)


HL = 256
HS = 128
HG = 64
N_PAIRS = H_C // 2
FAST_DECAY_LIMIT = 80.0


def _hgrn_bottom_exact(q, k, c, lo, rev):
    row = lax.broadcasted_iota(jnp.int32, (HS, LANE), 0)
    srow = lax.broadcasted_iota(jnp.int32, (HS, HS), 0)
    scol = lax.broadcasted_iota(jnp.int32, (HS, HS), 1)
    out = []
    for parity in range(2):
        def dup(x):
            xs = pltpu.roll(x, 64, 1)
            return jnp.where(lo, x, xs) if parity == 0 else jnp.where(lo, xs, x)
        qd, kd, bd = dup(q), dup(k), dup(c)

        dg = row & 3
        if rev:
            dg = 3 - dg
        e = [None]
        for delta in range(1, 4):
            shifted = pltpu.roll(bd, delta if rev else HS - delta, 0)
            e.append(jnp.exp(jnp.minimum(shifted - bd, 0.0)))
        qp, kp = [], []
        for c1, c2 in ((0, 1), (2, 3)):
            cv = jnp.where(lo, c1, c2)
            dl = cv - dg
            fac = jnp.where(dl == 0, 1.0, jnp.where(dl == 1, e[1], jnp.where(dl == 2, e[2],
                            jnp.where(dl == 3, e[3], 0.0))))
            kp.append((kd * fac).astype(BF16))
            qp.append(jnp.where(dg == cv, qd, 0.0).astype(BF16))
        s = lax.dot_general(jnp.concatenate(qp, axis=1), jnp.concatenate(kp, axis=1), _NT,
                            preferred_element_type=F32)
        tot = jnp.where((srow >> 2) == (scol >> 2), s, 0.0)

        for lev in range(1, 3):
            g = 4 ** lev
            par = 4 * g
            shape3 = (HS // par, par, LANE)
            rid = lax.broadcasted_iota(jnp.int32, shape3, 1)
            dg3 = rid >> (2 * lev)
            if rev:
                dg3 = 3 - dg3
            b3, q3, k3 = bd.reshape(shape3), qd.reshape(shape3), kd.reshape(shape3)
            lo3 = lo.reshape(1, 1, LANE)
            qp, kp = [], []
            for c1, c2 in ((1, 2), (3, None)):
                idx = lambda cc: (4 - cc) * g if rev else cc * g - 1
                i1 = idx(c1)
                i2 = idx(c2) if c2 is not None else i1
                ridx = jnp.where(lo3, i1, i2)
                ref = jnp.sum(jnp.where(rid == ridx, b3, 0.0), axis=1, keepdims=True)
                cvk = jnp.where(lo3, c1, c2 if c2 is not None else 0)
                cvq = jnp.where(lo3, c1, c2 if c2 is not None else -1)
                kk = jnp.where(dg3 < cvk, k3 * jnp.exp(jnp.minimum(ref - b3, 0.0)), 0.0)
                qq = jnp.where(dg3 == cvq, q3 * jnp.exp(jnp.minimum(b3 - ref, 0.0)), 0.0)
                kp.append(kk.reshape(HS, LANE).astype(BF16))
                qp.append(qq.reshape(HS, LANE).astype(BF16))
            s = lax.dot_general(jnp.concatenate(qp, axis=1), jnp.concatenate(kp, axis=1), _NT,
                                preferred_element_type=F32)
            sh = 2 * lev + 2
            tot = tot + jnp.where((srow >> sh) == (scol >> sh), s, 0.0)
        out.append(tot)
    return out


SUBLANES = 8


def _group_cumsum(lf, rev):
    nv, per_group = HS // SUBLANES, HG // SUBLANES
    x = lf.reshape(nv, SUBLANES, LANE)
    sub = lax.broadcasted_iota(jnp.int32, (nv, SUBLANES, LANE), 1)
    d = 1
    while d < SUBLANES:
        if rev:
            x = x + jnp.where(sub < SUBLANES - d, pltpu.roll(x, SUBLANES - d, 1), 0.0)
        else:
            x = x + jnp.where(sub >= d, pltpu.roll(x, d, 1), 0.0)
        d *= 2
    edge = x[:, 0:1, :] if rev else x[:, SUBLANES - 1:SUBLANES, :]
    offs = [None] * nv
    for g in range(HS // HG):
        order = range(g * per_group, (g + 1) * per_group)
        acc = None
        for vi in (reversed(order) if rev else order):
            offs[vi] = jnp.zeros((1, 1, LANE), F32) if acc is None else acc
            acc = edge[vi:vi + 1] if acc is None else acc + edge[vi:vi + 1]
    return (x + jnp.concatenate(offs, axis=0)).reshape(HS, LANE)


def _hgrn_head(q, k, lf, v, st_ref, rev):
    row = lax.broadcasted_iota(jnp.int32, (HS, LANE), 0)
    lane = lax.broadcasted_iota(jnp.int32, (1, LANE), 1)
    lo = lane < 64
    hi = jnp.logical_not(lo)
    in_g1 = row >= HG

    c = _group_cumsum(lf, rev)
    if rev:
        t0, t1 = c[0:1, :], c[HG:HG + 1, :]
    else:
        t0, t1 = c[HG - 1:HG, :], c[HS - 1:HS, :]
    et0, et1 = jnp.exp(t0), jnp.exp(t1)
    qe = q * jnp.exp(c)
    e_out = jnp.exp(jnp.where(in_g1, t1, t0) - c)
    ke = k * e_out

    if rev:
        qb = qe * jnp.where(in_g1, 1.0, et1)
        kh = ke * jnp.where(in_g1, et0, 1.0)
    else:
        qb = qe * jnp.where(in_g1, et0, 1.0)
        kh = ke * jnp.where(in_g1, 1.0, et1)
    st = st_ref[...]
    o_int = lax.dot_general(qb.astype(BF16), st.astype(BF16), _NT, preferred_element_type=F32)
    upd = lax.dot_general(v, kh.astype(BF16), (((0,), (0,)), ((), ())), preferred_element_type=F32)
    r128 = lax.broadcasted_iota(jnp.int32, (LANE, LANE), 0)
    c128 = lax.broadcasted_iota(jnp.int32, (LANE, LANE), 1)
    st_ref[...] = st * (et0 * et1) + jnp.where((r128 >> 6) == (c128 >> 6), upd, 0.0)

    q_late = in_g1 if not rev else jnp.logical_not(in_g1)
    q_top = jnp.where(q_late, qe, 0.0)
    k_top = jnp.where(q_late, 0.0, ke).astype(BF16)
    top = [lax.dot_general(jnp.where(m, q_top, 0.0).astype(BF16), k_top, _NT, preferred_element_type=F32)
           for m in (lo, hi)]
    mid = HG // 2 if rev else HG // 2 - 1
    cm = c - jnp.where(in_g1, c[HG + mid:HG + mid + 1, :], c[mid:mid + 1, :])
    return c, top, o_int, cm, jnp.max(jnp.abs(cm))


def _hgrn_steps(jobs, bot_ref):
    lane = lax.broadcasted_iota(jnp.int32, (1, LANE), 1)
    lo = lane < 64
    hi = jnp.logical_not(lo)
    heads = [_hgrn_head(*job) for job in jobs]
    fast = functools.reduce(jnp.maximum, [h[4] for h in heads]) <= FAST_DECAY_LIMIT

    @pl.when(fast)
    def _():
        srow = lax.broadcasted_iota(jnp.int32, (HS, HS), 0)
        scol = lax.broadcasted_iota(jnp.int32, (HS, HS), 1)
        same = (srow >> 6) == (scol >> 6)
        for ji, (job, (_, _, _, cm, _)) in enumerate(zip(jobs, heads)):
            keep = same & ((scol >= srow) if job[5] else (scol <= srow))
            qf = job[0] * jnp.exp(cm)
            kf = (job[1] * jnp.exp(-cm)).astype(BF16)
            for parity, m in enumerate((lo, hi)):
                s = lax.dot_general(jnp.where(m, qf, 0.0).astype(BF16), kf, _NT, preferred_element_type=F32)
                bot_ref[ji, parity] = jnp.where(keep, s, 0.0)

    @pl.when(jnp.logical_not(fast))
    def _():
        for ji, (job, (c, _, _, _, _)) in enumerate(zip(jobs, heads)):
            for parity, s in enumerate(_hgrn_bottom_exact(job[0], job[1], c, lo, job[5])):
                bot_ref[ji, parity] = s

    outs = []
    for ji, (job, (_, top, o_int, _, _)) in enumerate(zip(jobs, heads)):
        v = job[3]
        probs = jnp.concatenate([(bot_ref[ji, 0] + top[0]).astype(BF16), (bot_ref[ji, 1] + top[1]).astype(BF16)],
                                axis=1)
        vv = jnp.concatenate([jnp.where(lo, v, jnp.zeros_like(v)), jnp.where(hi, v, jnp.zeros_like(v))], axis=0)
        outs.append(jnp.dot(probs, vv, preferred_element_type=F32) + o_int)
    return outs


PAIRS_PER_ITER = 4


def _hgrn_kernel(has_s0, nt, *refs):
    if has_s0:
        (qf_ref, vf_ref, kf_ref, lf_ref, qb_ref, vb_ref, kb_ref, lb_ref, s0_ref,
         of_ref, ob_ref, so_ref, st_scr, bot_scr) = refs
    else:
        (qf_ref, vf_ref, kf_ref, lf_ref, qb_ref, vb_ref, kb_ref, lb_ref,
         of_ref, ob_ref, so_ref, st_scr, bot_scr) = refs
    j = pl.program_id(1)

    @pl.when(j == 0)
    def _():
        if has_s0:
            st_scr[...] = s0_ref[...]
        else:
            st_scr[...] = jnp.zeros_like(st_scr)

    n_sub = HL // HS

    def pair_body(it, carry):
        for step in range(n_sub):
            jobs, dests = [], []
            for pp in range(PAIRS_PER_ITER):
                p = it * PAIRS_PER_ITER + pp
                cols = pl.ds(pl.multiple_of(p * LANE, LANE), LANE)
                rf = pl.ds(step * HS, HS)
                rb = pl.ds((n_sub - 1 - step) * HS, HS)
                jobs.append((qf_ref[rf, cols].astype(F32), kf_ref[rf, cols], lf_ref[rf, cols], vf_ref[rf, cols],
                             st_scr.at[p, 0], False))
                dests.append((of_ref, rf, cols))
                jobs.append((qb_ref[rb, cols].astype(F32), kb_ref[rb, cols], lb_ref[rb, cols], vb_ref[rb, cols],
                             st_scr.at[p, 1], True))
                dests.append((ob_ref, rb, cols))
            for (ref, rows, cols), o in zip(dests, _hgrn_steps(jobs, bot_scr)):
                ref[rows, cols] = o
        return carry

    lax.fori_loop(0, N_PAIRS // PAIRS_PER_ITER, pair_body, 0)

    @pl.when(j == nt - 1)
    def _():
        so_ref[...] = st_scr[...]


def _hgrn(qc, vc, kff, lff, kfb, lfb, s0=None, prev=None, layer=0):
    if s0 is None:
        nb, nt, blk0 = BATCH, SEQ // HL, 0
    else:
        nb, nt, blk0 = DEC_BATCH, DEC_SEQ // HL, T_CTX // HL
    fspec = pl.BlockSpec((HL, 512), lambda b, j: (blk0 + b * nt + j, 0))
    bspec = pl.BlockSpec((HL, 512), lambda b, j: (blk0 + b * nt + nt - 1 - j, 0))
    sspec = pl.BlockSpec((None, N_PAIRS, 2, LANE, LANE), lambda b, j: (b, 0, 0, 0, 0))
    in_specs = [fspec] * 4 + [bspec] * 4
    args = [qc, vc, kff, lff, qc, vc, kfb, lfb]
    aliases = {}
    if s0 is not None:
        in_specs += [pl.BlockSpec((None, None, N_PAIRS, 2, LANE, LANE), lambda b, j: (b, layer, 0, 0, 0, 0))]
        args += [s0]
        in_specs += [pl.BlockSpec(memory_space=pl.ANY)] * 2
        args += list(prev)
        aliases = {len(args) - 2: 0, len(args) - 1: 1}

    def body(*refs):
        if s0 is not None:
            refs = refs[:9] + refs[11:]
        _hgrn_kernel(s0 is not None, nt, *refs)

    return pl.pallas_call(
        body,
        grid=(nb, nt),
        in_specs=in_specs,
        out_specs=[fspec, bspec, sspec],
        out_shape=[jax.ShapeDtypeStruct((T_ALL, 512), F32), jax.ShapeDtypeStruct((T_ALL, 512), F32),
                   jax.ShapeDtypeStruct((nb, N_PAIRS, 2, LANE, LANE), F32)],
        scratch_shapes=[pltpu.VMEM((N_PAIRS, 2, LANE, LANE), F32),
                        pltpu.VMEM((2 * PAIRS_PER_ITER, 2, HS, HS), F32)],
        input_output_aliases=aliases,
        compiler_params=_cparams(("parallel", "arbitrary")),
        name="hgrn_ctx" if s0 is None else "hgrn_lat",
    )(*args)


def _merge_kernel(xc_ref, xl_ref, mod_ref, n1_ref, wg_ref, oa_ref, ob_ref, ocf_ref, ocb_ref, sg_ref, con_ref,
                  ones_ref, wbr_ref, wout_ref, n2_ref, wrh_ref, wrl_ref, br_ref, tri_ref, upper_ref,
                  x1_o, h2_o, comb_o, rank_o, carry_o, tab_o, carry_scr):
    x = _x_tile(xc_ref, xl_ref, TMG)
    mod = mod_ref[...]
    xn = x * lax.rsqrt(jnp.mean(x * x, axis=-1, keepdims=True) + EPS) * n1_ref[...]
    h = (xn * (1.0 + mod[:, D:2 * D]) + mod[:, 0:D]).astype(BF16)

    oc = ocf_ref[...] + ocb_ref[...]
    oc = oc * lax.rsqrt(_group_mean(oc * oc, ones_ref[...], DV_C) + EPS) * con_ref[...]
    oc = (oc * sg_ref[...].astype(F32)).astype(BF16)
    branches = (oa_ref[...], ob_ref[...], oc)
    mix = None
    for jb in range(3):
        gate = jax.nn.sigmoid(jnp.dot(h, wg_ref[:, jb * D:(jb + 1) * D], preferred_element_type=F32))
        t = gate * jnp.dot(branches[jb], wbr_ref[jb], preferred_element_type=F32)
        mix = t if mix is None else mix + t
    out = jnp.dot(mix.astype(BF16), wout_ref[...], preferred_element_type=F32)
    x1 = x + mod[:, 2 * D:3 * D] * out
    x1_o[...] = x1

    x1n = x1 * lax.rsqrt(jnp.mean(x1 * x1, axis=-1, keepdims=True) + EPS) * n2_ref[...]
    h2 = x1n * (1.0 + mod[:, 4 * D:5 * D]) + mod[:, 3 * D:4 * D]
    h2_o[...] = h2.astype(BF16)

    h2h, h2l = _split_hi_lo(h2)
    logits = (jnp.dot(h2h, wrh_ref[...], preferred_element_type=F32)
              + jnp.dot(h2l, wrh_ref[...], preferred_element_type=F32)
              + jnp.dot(h2h, wrl_ref[...], preferred_element_type=F32)) + br_ref[...]
    lane = lax.broadcasted_iota(jnp.int32, logits.shape, 1)
    neg = -jnp.inf
    is_g = lane < N_GROUPS
    gl = jnp.where(is_g, logits, neg)
    gmax = gl.max(axis=-1, keepdims=True)
    gidx = jnp.min(jnp.where(gl == gmax, lane, LANE), axis=-1, keepdims=True)
    gp = 1.0 / jnp.sum(jnp.where(is_g, jnp.exp(gl - gmax), 0.0), axis=-1, keepdims=True)
    eid = lane - N_GROUPS
    in_grp = (eid >= 0) & (eid < N_EXPERTS) & ((eid >> 2) == gidx)
    el = jnp.where(in_grp, logits, neg)
    v1 = el.max(axis=-1, keepdims=True)
    i1 = jnp.min(jnp.where(el == v1, lane, LANE), axis=-1, keepdims=True)
    el2 = jnp.where(lane == i1, neg, el)
    v2 = el2.max(axis=-1, keepdims=True)
    i2 = jnp.min(jnp.where(el2 == v2, lane, LANE), axis=-1, keepdims=True)
    e2 = jnp.exp(v2 - v1)
    w1 = gp / (1.0 + e2)
    w2 = gp * e2 / (1.0 + e2)
    comb = jnp.where(lane == i1, w1, 0.0) + jnp.where(lane == i2, w2, 0.0)
    comb = pltpu.roll(comb, LANE - N_GROUPS, 1)
    comb_o[...] = comb

    step = pl.program_id(0) % MOE_STEPS

    @pl.when(step == 0)
    def _():
        carry_scr[...] = jnp.zeros_like(carry_scr)

    for sb in range(TMG // SBK):
        sub = comb[sb * SBK:(sb + 1) * SBK, :]
        routed = sub > 0.0
        ind = jnp.where(routed, 1.0, 0.0)
        carry = carry_scr[...]
        rank = jnp.dot(tri_ref[...], ind.astype(BF16), preferred_element_type=F32) + carry
        rank_o[sb * SBK:(sb + 1) * SBK, :] = jnp.where(routed, rank, -1.0)
        carry_o[sb] = carry
        carry_scr[...] = carry + jnp.sum(ind, axis=0, keepdims=True)

    @pl.when(step == MOE_STEPS - 1)
    def _():
        count = carry_scr[...]
        seg = jnp.floor((count + (SEG_ALIGN - 1.0)) * (1.0 / SEG_ALIGN)) * SEG_ALIGN
        off = jnp.dot(jnp.broadcast_to(seg, (8, LANE)), upper_ref[...], preferred_element_type=F32,
                      precision=lax.Precision.HIGHEST)
        tab_o[0:1, :] = count
        tab_o[1:2, :] = off[0:1, :]


TMG = 512


def _merge(layer, x, mod, n1, wgate, oa, ob, ocf, ocb, sg, con, ones, wbr, wout, n2, wr_hi, wr_lo, br):
    ctx_tiles, per_seq = T_CTX // TMG, DEC_SEQ // TMG
    mrow = lambda i: jnp.where(i < ctx_tiles, CTX_MOD_ROW, (i - ctx_tiles) // per_seq)
    tspec = lambda w: pl.BlockSpec((TMG, w), lambda i: (i, 0))
    wspec = lambda tail: pl.BlockSpec((None,) + tail, lambda i: (layer,) + (0,) * len(tail),
                                      pipeline_mode=pl.Buffered(1))
    return pl.pallas_call(
        _merge_kernel,
        grid=(T_ALL // TMG,),
        in_specs=_x_specs(x, TMG) + [
            _mod_spec(layer, mrow),
            _layer_spec((1, D), layer), wspec((D, 3 * D)),
            tspec(512), tspec(512), tspec(512), tspec(512), tspec(512),
            _layer_spec((1, 512), layer), _const_spec((256, 256)),
            wspec((3, BRANCH_W, D)), wspec((D, D)), _layer_spec((1, D), layer),
            _layer_spec((D, LANE), layer), _layer_spec((D, LANE), layer), _layer_spec((1, LANE), layer),
            _const_spec((SBK, SBK)), _const_spec((LANE, LANE)),
        ],
        out_specs=[
            tspec(D), tspec(D), tspec(LANE), tspec(LANE),
            pl.BlockSpec((None, TMG // SBK, 1, LANE), lambda i: (i // MOE_STEPS, i % MOE_STEPS, 0, 0)),
            pl.BlockSpec((None, 2, LANE), lambda i: (i // MOE_STEPS, 0, 0)),
        ],
        out_shape=[jax.ShapeDtypeStruct((T_ALL, D), F32), jax.ShapeDtypeStruct((T_ALL, D), BF16),
                   jax.ShapeDtypeStruct((T_ALL, LANE), F32), jax.ShapeDtypeStruct((T_ALL, LANE), F32),
                   jax.ShapeDtypeStruct((N_BLK, N_SB, 1, LANE), F32), jax.ShapeDtypeStruct((N_BLK, 2, LANE), F32)],
        scratch_shapes=[pltpu.VMEM((1, LANE), F32)],
        compiler_params=_cparams(("arbitrary",)),
        name="merge",
    )(*_x_pair(x)[:2], mod, n1, wgate, oa, ob, ocf, ocb, sg, con, ones, wbr, wout, n2, wr_hi, wr_lo, br,
      jnp.asarray(np.tril(np.ones((SBK, SBK)), -1), BF16), jnp.asarray(np.triu(np.ones((LANE, LANE)), 1), F32))


NB = 2048
N_BLK = T_ALL // NB
SBK = 256
N_SB = NB // SBK
WIN_SHIFT, FT_SHIFT, SEG_SHIFT = 6, 7, 4
WIN = 1 << WIN_SHIFT
FT = 1 << FT_SHIFT
SEG_ALIGN = 1 << SEG_SHIFT
STG = 2 * NB + N_EXPERTS * SEG_ALIGN + 256
QUAD = 4
E_STEP = 2
MOE_STEPS = NB // TMG


def _moe_kernel(cnt_s, off_s, car_s, h2_ref, rank_ref, comb_ref, offv_ref, eg_ref, eu_ref, ed_ref, o_ref,
                stg_ref, acc_ref):
    blk = pl.program_id(0)
    step = pl.program_id(1)
    srow = lax.broadcasted_iota(jnp.int32, (WIN, SBK), 0).astype(F32)

    def windows(s):
        out = []
        for ex in range(N_EXPERTS):
            start = off_s[blk, ex] + car_s[blk, s, ex]
            length = car_s[blk, s + 1, ex] - car_s[blk, s, ex]
            ws = (start >> SEG_SHIFT) << SEG_SHIFT
            out.append((ws, (start - ws + length + (WIN - 1)) >> WIN_SHIFT))
        return out, functools.reduce(jnp.maximum, [w[1] for w in out])

    def positions(s):
        rows = pl.ds(pl.multiple_of(s * SBK, SBK), SBK)
        rank = rank_ref[rows, :]
        return rows, jnp.where(rank >= 0.0, rank + offv_ref[1:2, :], -1.0e6)

    @pl.when(step == 0)
    def _():
        stg_ref[...] = jnp.zeros_like(stg_ref)

        def sub_body(s, carry):
            rows, pos = positions(s)
            pos_t = pos.T
            h2 = h2_ref[rows, :]
            wins, nmax = windows(s)

            def chunk_body(c, carry2):
                for quad in range(N_EXPERTS // QUAD):
                    blocks = []
                    for ex in range(quad * QUAD, (quad + 1) * QUAD):
                        base = (wins[ex][0] + c * WIN).astype(F32)
                        hit = (pos_t[ex:ex + 1, :] - base) == srow
                        blocks.append(jnp.where(hit, 1.0, 0.0).astype(BF16))
                    moved = jnp.dot(jnp.concatenate(blocks, axis=0), h2, preferred_element_type=F32).astype(BF16)
                    for i in range(QUAD):
                        first = jnp.minimum(wins[quad * QUAD + i][0] + c * WIN, STG - WIN)
                        dst = pl.ds(pl.multiple_of(first, SEG_ALIGN), WIN)
                        stg_ref[dst, :] = stg_ref[dst, :] + moved[i * WIN:(i + 1) * WIN, :]
                return carry2

            lax.fori_loop(0, nmax, chunk_body, 0)
            return carry

        lax.fori_loop(0, N_SB, sub_body, 0)

    def ffn_tile(j, first, n_rows, n_valid):
        rows = pl.ds(pl.multiple_of(first, SEG_ALIGN), n_rows)
        xs = stg_ref[rows, :]
        hg = jnp.dot(xs, eg_ref[j], preferred_element_type=F32)
        hu = jnp.dot(xs, eu_ref[j], preferred_element_type=F32)
        act = (hg * jax.nn.sigmoid(hg) * hu).astype(BF16)
        y = jnp.dot(act, ed_ref[j], preferred_element_type=F32).astype(BF16)
        rid = lax.broadcasted_iota(jnp.int32, (n_rows, 1), 0)
        stg_ref[rows, :] = jnp.where(rid < n_valid, y, xs)

    for j in range(E_STEP):
        ex = step * E_STEP + j
        count = cnt_s[blk, ex]
        seg0 = off_s[blk, ex]
        n_big = (count + (FT - 1)) >> (FT_SHIFT + 1)

        def ffn_body(t, carry, j=j, count=count, seg0=seg0):
            ffn_tile(j, seg0 + t * (2 * FT), 2 * FT, count - t * (2 * FT))
            return carry

        lax.fori_loop(0, n_big, ffn_body, 0)

        @pl.when(count > n_big * (2 * FT))
        def _(j=j, count=count, seg0=seg0, n_big=n_big):
            ffn_tile(j, seg0 + n_big * (2 * FT), FT, count - n_big * (2 * FT))

    @pl.when(step == N_EXPERTS // E_STEP - 1)
    def _():
        def sub_body(s, carry):
            rows, pos = positions(s)
            pos_t = pos.T
            wts_t = comb_ref[rows, :].T
            wins, nmax = windows(s)
            acc_ref[...] = jnp.zeros_like(acc_ref)

            def chunk_body(c, carry2):
                sel_t, srcs = [], []
                for ex in range(N_EXPERTS):
                    base = (wins[ex][0] + c * WIN).astype(F32)
                    hit = (pos_t[ex:ex + 1, :] - base) == srow
                    sel_t.append(jnp.where(hit, wts_t[ex:ex + 1, :], 0.0).astype(BF16))
                    first = jnp.minimum(wins[ex][0] + c * WIN, STG - WIN)
                    srcs.append(stg_ref[pl.ds(pl.multiple_of(first, SEG_ALIGN), WIN), :])
                acc_ref[...] += lax.dot_general(jnp.concatenate(sel_t, axis=0), jnp.concatenate(srcs, axis=0),
                                                (((0,), (0,)), ((), ())), preferred_element_type=F32)
                return carry2

            lax.fori_loop(0, nmax, chunk_body, 0)
            o_ref[rows, :] = acc_ref[...].astype(BF16)
            return carry

        lax.fori_loop(0, N_SB, sub_body, 0)


def _moe(layer, h2, comb, rank, carry, tab, eg, eu, ed):
    cnt_i = tab[:, 0, :N_EXPERTS].astype(jnp.int32)
    off_i = tab[:, 1, :N_EXPERTS].astype(jnp.int32)
    car_i = jnp.concatenate([carry[:, :, 0, :N_EXPERTS].astype(jnp.int32), cnt_i[:, None, :]], axis=1)
    bspec = lambda w: pl.BlockSpec((NB, w), lambda b, e, *_: (b, 0))
    grid_spec = pltpu.PrefetchScalarGridSpec(
        num_scalar_prefetch=3,
        grid=(N_BLK, N_EXPERTS // E_STEP),
        in_specs=[
            bspec(D), bspec(LANE), bspec(LANE),
            pl.BlockSpec((None, 2, LANE), lambda b, e, *_: (b, 0, 0)),
            pl.BlockSpec((None, E_STEP, D, D_EXPERT), lambda b, e, *_: (layer, e, 0, 0)),
            pl.BlockSpec((None, E_STEP, D, D_EXPERT), lambda b, e, *_: (layer, e, 0, 0)),
            pl.BlockSpec((None, E_STEP, D_EXPERT, D), lambda b, e, *_: (layer, e, 0, 0)),
        ],
        out_specs=bspec(D),
        scratch_shapes=[pltpu.VMEM((STG, D), BF16), pltpu.VMEM((SBK, D), F32)],
    )
    return pl.pallas_call(
        _moe_kernel,
        grid_spec=grid_spec,
        out_shape=jax.ShapeDtypeStruct((T_ALL, D), BF16),
        compiler_params=_cparams(("parallel", "arbitrary")),
        name="moe",
    )(cnt_i, off_i, car_i, h2, rank, comb, tab, eg, eu, ed)


def _residual_kernel(final, x1_ref, moe_ref, mod_ref, fg_ref, o_ref):
    x2 = x1_ref[...] + mod_ref[:, 5 * D:6 * D] * moe_ref[...].astype(F32)
    if final:
        x2 = x2 * lax.rsqrt(jnp.mean(x2 * x2, axis=-1, keepdims=True) + EPS) * fg_ref[...]
    o_ref[...] = x2


TR = DEC_SEQ


def _residual(layer, final, x1, moe, mod, fg, tile0=0, n_tiles=T_ALL // TR):
    ctx_tiles = T_CTX // TR
    mrow = lambda i: jnp.where(tile0 + i < ctx_tiles, CTX_MOD_ROW, tile0 + i - ctx_tiles)
    src = lambda: pl.BlockSpec((TR, D), lambda i: (tile0 + i, 0))
    return pl.pallas_call(
        functools.partial(_residual_kernel, final),
        grid=(n_tiles,),
        in_specs=[src(), src(), _mod_spec(layer, mrow), _const_spec((1, D))],
        out_specs=pl.BlockSpec((TR, D), lambda i: (i, 0)),
        out_shape=jax.ShapeDtypeStruct((n_tiles * TR, D), F32),
        compiler_params=_cparams(("parallel",)),
        name="residual",
    )(x1, moe, mod, fg)


def _state_to_blockdiag(s):
    lead = s.shape[:-3]
    st = jnp.swapaxes(s, -1, -2).reshape(lead + (N_PAIRS, 2, DV_C, DK_C))
    z = jnp.zeros_like(st[..., 0, :, :])
    top = jnp.concatenate([st[..., 0, :, :], z], axis=-1)
    bot = jnp.concatenate([z, st[..., 1, :, :]], axis=-1)
    return jnp.concatenate([top, bot], axis=-2)


def _blockdiag_to_state(sb):
    lead = sb.shape[:-3]
    even = sb[..., :DV_C, :DK_C]
    odd = sb[..., DV_C:, DK_C:]
    st = jnp.stack([even, odd], axis=-3).reshape(lead + (H_C, DV_C, DK_C))
    return jnp.swapaxes(st, -1, -2)


def kernel(x_prompt, x_sample, cache_gqa_k, cache_gqa_v, cache_mla_ckv, cache_mla_krope, state_hgrn, c, c_ctx,
           w_mod, b_mod, norm1_g, norm2_g, w_in, a_qnorm, a_knorm, b_qnorm, b_wq, b_kvnorm, b_wkv, c_lb_logits,
           c_onorm, w_branch, w_out, r_group_w, r_group_b, r_expert_w, r_expert_b, e_gate, e_up, e_down, final_g):
    x = (x_prompt.reshape(T_CTX, D), x_sample.reshape(T_LAT, D))
    cvec = jnp.concatenate([c, c_ctx[None, :], jnp.zeros((MOD_ROWS - DEC_BATCH - 1, D), F32)], axis=0)
    mod = _mod_table(cvec, w_mod, b_mod).reshape(DEPTH * MOD_ROWS, 1, 6 * D)
    taba, tabb, tabk = _rope_tables()
    ones = _ones_block(256, 64)
    lbl = c_lb_logits.reshape(DEPTH, 2, H_C * DK_C)

    vec = lambda g, reps=1: jnp.tile(g, (1, reps))[:, None, :]
    n1, n2 = vec(norm1_g), vec(norm2_g)
    aq, ak, con = vec(a_qnorm, H_A), vec(a_knorm, KV_A), vec(c_onorm, H_C)
    bq, bkv = vec(b_qnorm), vec(b_kvnorm)
    w_in_p, w_gate = _pack_w_in(w_in)
    wq_p, wkv_p = _pack_wq(b_wq), _pack_wkv(b_wkv)
    wbr, wout = w_branch.astype(BF16), w_out.astype(BF16)
    n_pad = LANE - N_GROUPS - N_EXPERTS
    wr_hi, wr_lo = _split_hi_lo(jnp.concatenate([r_group_w, r_expert_w, jnp.zeros((DEPTH, D, n_pad), F32)], axis=-1))
    br = jnp.concatenate([r_group_b, r_expert_b, jnp.zeros((DEPTH, n_pad), F32)], axis=-1)[:, None, :]
    eg, eu, ed = e_gate.astype(BF16), e_up.astype(BF16), e_down.astype(BF16)

    ck = cache_gqa_k.reshape(DEC_BATCH, DEPTH, PAST, KV_A * HD_A)
    cv = cache_gqa_v.reshape(DEC_BATCH, DEPTH, PAST, KV_A * HD_A)
    cckv = cache_mla_ckv.reshape(DEC_BATCH, DEPTH, PAST, KV_RANK)
    ckr = cache_mla_krope.reshape(DEC_BATCH * DEPTH * PAST, ROPE_B)
    zpad = jnp.zeros_like(ckr)
    ckre = jnp.concatenate([ckr, zpad, ckr, zpad], axis=1)
    s0 = jnp.swapaxes(_state_to_blockdiag(state_hgrn), 2, 3)

    new_k, new_v, new_ckv, new_kr, new_s = [], [], [], [], []
    pending = None
    for l in range(DEPTH):
        outs = _inproj(l, x, mod, n1, w_in_p, aq, ak, bq, wq_p, bkv, wkv_p, lbl, ones, taba, tabb, tabk,
                       pending=pending)
        if pending is not None:
            x = outs[-1]
        (qa, kan, ka, va, qb, ckv, kvb, kr, kre, lff, lfb, kff, kfb, qc, vc, sg) = outs[:16]
        new_k.append(kan[:T_CTX])
        new_v.append(va[:T_CTX])
        new_ckv.append(ckv[:T_CTX])
        new_kr.append(kr[:T_CTX, :ROPE_B])

        oa, ob = _attention(qa, qb, ka, va, kvb, kre)
        oa, ob = _attention(qa, qb, ka, va, kvb, kre, prev=(oa, ob), cache=(ck, cv, cckv, ckre, wkv_p), layer=l)

        ocf, ocb, s_ctx = _hgrn(qc, vc, kff, lff, kfb, lfb)
        ocf, ocb, _ = _hgrn(qc, vc, kff, lff, kfb, lfb, s0=s0, prev=(ocf, ocb), layer=l)
        new_s.append(s_ctx)

        x1, h2, comb, rank, carry, tab = _merge(l, x, mod, n1, w_gate, oa, ob, ocf, ocb, sg, con, ones, wbr, wout,
                                                n2, wr_hi, wr_lo, br)
        moe = _moe(l, h2, comb, rank, carry, tab, eg, eu, ed)
        pending = (x1, moe)

    last = DEPTH - 1
    y_prompt = _residual(last, True, x1, moe, mod, final_g[None, :], 0, T_CTX // TR)
    y_sample = _residual(last, True, x1, moe, mod, final_g[None, :], T_CTX // TR, T_LAT // TR)
    stack = lambda parts, tail: jnp.stack([p.reshape(BATCH, SEQ, -1) for p in parts], axis=1).reshape(
        (BATCH, DEPTH, SEQ) + tail)
    states = _blockdiag_to_state(jnp.swapaxes(jnp.stack(new_s, axis=1), 2, 3))
    return (y_prompt.reshape(BATCH, SEQ, D), y_sample.reshape(DEC_BATCH, DEC_SEQ, D),
            stack(new_k, (KV_A, HD_A)), stack(new_v, (KV_A, HD_A)), stack(new_ckv, (KV_RANK,)),
            stack(new_kr, (ROPE_B,)), states)
```
